```python
import jax, jax.numpy as jnp
from jax import lax
import numpy as np

D_MODEL = 1024
BATCH = 16
SEQ = 256
DEPTH = 1
DEC_BATCH = 8
DEC_SEQ = 2048
PAST_LEN = 256

GRID_W = 64
HEAD_DIM = 64
ATTN_HEADS = 8
KV_HEADS = 2
GQA_GROUP = ATTN_HEADS // KV_HEADS
ATTN_WIDTH = ATTN_HEADS * HEAD_DIM
KV_WIDTH = KV_HEADS * HEAD_DIM
ATTN_IN = ATTN_WIDTH + 2 * KV_WIDTH
ATTN_SCALE = HEAD_DIM ** -0.5
Q_BLOCK = 128
ROPE_THETA = 10000.0
ROPE_HALF = HEAD_DIM // 2
ROPE_FREQS = ROPE_HALF // 2
RWKV_HEAD = 64
RWKV_HEADS = 8
RWKV_WIDTH = RWKV_HEADS * RWKV_HEAD
DECAY_RANK = 64
ICLR_RANK = 64
GATE_RANK = 128
RWKV_SIZES = (RWKV_WIDTH, RWKV_WIDTH, RWKV_WIDTH, DECAY_RANK, DECAY_RANK, ICLR_RANK, ICLR_RANK, GATE_RANK)
RWKV_IN = 3 * RWKV_WIDTH + 2 * DECAY_RANK + 2 * ICLR_RANK + GATE_RANK
D_IN = ATTN_IN + RWKV_IN
MIX_WIDTH = ATTN_WIDTH + RWKV_WIDTH
DECAY_SCALE = 0.606531
GN_EPS = 64e-5
N_EXPERTS = 64
TOP_K = 6
N_GROUPS = 8
TOPK_GROUPS = 4
EXPERT_FF = 256
SHARED_FF = 256
ROUTED_SCALE = 2.5
EPS = 1e-6

kernel_name = 'hybrid_attn_rwkv7_moe_diffusion_step'

f32 = jnp.float32


def _split_cols(z, sizes):
    offs = np.cumsum(sizes)[:-1].tolist()
    return jnp.split(z, offs, axis=-1)


def rmsnorm(x, g):
    xf = x.astype(f32)
    y = xf * lax.rsqrt(jnp.mean(xf * xf, axis=-1, keepdims=True) + EPS) * g.astype(f32)
    return y.astype(x.dtype)


def axial_rope(row_idx, col_idx):
    inv = ROPE_THETA ** (-jnp.arange(0, ROPE_HALF, 2, dtype=f32) / ROPE_HALF)
    ang = jnp.stack([row_idx.astype(f32)[:, None] * inv, col_idx.astype(f32)[:, None] * inv], axis=1)
    ang = ang[:, None, :, None, :]
    return jnp.cos(ang), jnp.sin(ang)


def apply_rope(x, cos, sin):
    xs = x.reshape(x.shape[:-1] + (2, 2, ROPE_FREQS))
    rot = jnp.stack([-xs[..., 1, :], xs[..., 0, :]], axis=-2)
    out = xs * cos.astype(x.dtype) + rot * sin.astype(x.dtype)
    return out.reshape(x.shape)


def block_attention(q, k, v):
    B, T, H, D = q.shape
    nb = T // Q_BLOCK
    qb = q.reshape(B, nb, Q_BLOCK, KV_HEADS, GQA_GROUP, D).transpose(1, 0, 2, 3, 4, 5)

    def one(qblk):
        s = jnp.einsum('bqkgd,blkd->bkgql', qblk, k).astype(f32) * ATTN_SCALE
        p = jax.nn.softmax(s, axis=-1).astype(v.dtype)
        return jnp.einsum('bkgql,blkd->bqkgd', p, v)

    o = lax.map(one, qb)
    return o.transpose(1, 0, 2, 3, 4, 5).reshape(B, T, H * D)


def token_shift(z, mu_prev, mu_next):
    prev = jnp.pad(z[:, :-1], ((0, 0), (1, 0), (0, 0)))
    nxt = jnp.pad(z[:, 1:], ((0, 0), (0, 1), (0, 0)))
    return z + mu_prev * (prev - z) + mu_next * (nxt - z)


def _wkv_step(S, inp):
    w, kk, b, k, r, v = inp
    sa = jnp.einsum('dbhvk,dbhk->dbhv', S, -kk)
    S = S * w[..., None, :] + sa[..., :, None] * b[..., None, :] + v[..., :, None] * k[..., None, :]
    return S, jnp.einsum('dbhvk,dbhk->dbhv', S, r)


def rwkv_mix(z_rwkv, p, s0):
    zs = token_shift(z_rwkv, p['mu_prev'], p['mu_next'])
    r, k, v, dw_f, dw_b, da_f, da_b, dg = _split_cols(zs, RWKV_SIZES)
    B, T, _ = r.shape
    dw = jnp.stack([dw_f, dw_b])
    da = jnp.stack([da_f, da_b])
    w_raw = p['decay_w0'][:, None, None, :] + jnp.einsum('dbtr,drc->dbtc', jnp.tanh(dw), p['decay_w2'])
    decay = jnp.exp(-DECAY_SCALE * jax.nn.sigmoid(w_raw.astype(f32)))
    a = jax.nn.sigmoid((p['iclr_a0'][:, None, None, :] + jnp.einsum('dbtr,drc->dbtc', da, p['iclr_a2'])).astype(f32))
    g = (jax.nn.sigmoid(dg) @ p['gate_g2']).astype(f32)

    def heads(t):
        return t.reshape(t.shape[:-1] + (RWKV_HEADS, RWKV_HEAD))

    rf, kf, vf = r.astype(f32), k.astype(f32), v.astype(f32)
    kk = heads(kf * p['k_k'].astype(f32))
    kk = kk * lax.rsqrt(jnp.sum(kk * kk, axis=-1, keepdims=True) + 1e-12)
    k_eff = heads(kf[None] * (1.0 + (a - 1.0) * p['k_a'].astype(f32)))
    b = kk[None] * heads(a)
    rh, vh = heads(rf), heads(vf)

    def to_scan(t):
        t = jnp.stack([t[0], jnp.flip(t[1], axis=1)])
        return jnp.moveaxis(t, 2, 0)

    def both(t):
        return jnp.broadcast_to(t[None], (2,) + t.shape)

    xs = (to_scan(heads(decay)), to_scan(both(kk)), to_scan(b), to_scan(k_eff), to_scan(both(rh)), to_scan(both(vh)))
    s_fin, o = lax.scan(_wkv_step, s0.astype(f32), xs)
    o = jnp.moveaxis(o, 0, 2)
    o = o[0] + jnp.flip(o[1], axis=1)
    mu = jnp.mean(o, axis=-1, keepdims=True)
    var = jnp.mean(jnp.square(o - mu), axis=-1, keepdims=True)
    on = ((o - mu) * lax.rsqrt(var + GN_EPS)).reshape(B, T, RWKV_WIDTH)
    on = on * p['lnx_g'].astype(f32) + p['lnx_b'].astype(f32)
    bonus = jnp.sum(jnp.sum(rh[None] * k_eff * p['r_k'].astype(f32), axis=-1, keepdims=True), axis=0) * vh
    out = (on + bonus.reshape(B, T, RWKV_WIDTH)) * g
    return out.astype(z_rwkv.dtype), s_fin.astype(z_rwkv.dtype)


def mixer(h, p, rope, past_k, past_v, s0):
    B, T, _ = h.shape
    z = h @ p['w_in']
    z_attn, z_rwkv = z[..., :ATTN_IN], z[..., ATTN_IN:]
    q, k, v = _split_cols(z_attn, (ATTN_WIDTH, KV_WIDTH, KV_WIDTH))
    q = rmsnorm(q.reshape(B, T, ATTN_HEADS, HEAD_DIM), p['q_gain'])
    k = rmsnorm(k.reshape(B, T, KV_HEADS, HEAD_DIM), p['k_gain'])
    v = v.reshape(B, T, KV_HEADS, HEAD_DIM)
    k_own, v_own = k, v
    if rope is not None:
        q = apply_rope(q, *rope)
        k = jnp.concatenate([past_k.astype(k.dtype), apply_rope(k, *rope)], axis=1)
        v = jnp.concatenate([past_v.astype(v.dtype), v], axis=1)
    attn = block_attention(q, k, v)
    rw, s_fin = rwkv_mix(z_rwkv, p, s0)
    y = jnp.concatenate([attn, rw], axis=-1) @ p['w_out']
    return y, k_own, v_own, s_fin


def moe(h, p):
    x2 = h.reshape(-1, h.shape[-1])
    n = x2.shape[0]
    scores = jax.nn.sigmoid((x2 @ p['router_w']).astype(f32))
    biased = scores + p['router_b'].astype(f32)
    grp = biased.reshape(n, N_GROUPS, N_EXPERTS // N_GROUPS)
    grp_score = jnp.sum(lax.top_k(grp, 2)[0], axis=-1)
    _, gidx = lax.top_k(grp_score, TOPK_GROUPS)
    gmask = jnp.sum(jax.nn.one_hot(gidx, N_GROUPS, dtype=f32), axis=-2)
    emask = jnp.repeat(gmask, N_EXPERTS // N_GROUPS, axis=-1)
    _, eidx = lax.top_k(jnp.where(emask > 0, biased, -jnp.inf), TOP_K)
    wsel = jnp.take_along_axis(scores, eidx, axis=-1)
    wsel = wsel / jnp.sum(wsel, axis=-1, keepdims=True) * ROUTED_SCALE
    gates = jnp.sum(jax.nn.one_hot(eidx, N_EXPERTS, dtype=f32) * wsel[..., None], axis=-2).astype(h.dtype)

    def expert(acc, e):
        wg, wu, wd, ge = e
        y = (jax.nn.silu(x2 @ wg) * (x2 @ wu)) @ wd
        return acc + ge[:, None] * y, None

    routed, _ = lax.scan(expert, jnp.zeros_like(x2), (p['expert_wg'], p['expert_wu'], p['expert_wd'], gates.T))
    shared = (jax.nn.silu(x2 @ p['shared_wg']) * (x2 @ p['shared_wu'])) @ p['shared_wd']
    return (routed + shared).reshape(h.shape)


def trunk_layer(x, mod, p, rope, past_k, past_v, s0):
    sh_a, sc_a, gt_a, sh_f, sc_f, gt_f = jnp.split(mod, 6, axis=-1)
    h = rmsnorm(x, p['g_pre_mix']) * (1 + sc_a) + sh_a
    y, kc, vc, st = mixer(h, p, rope, past_k, past_v, s0)
    x = x + gt_a * rmsnorm(y, p['g_post_mix'])
    h = rmsnorm(x, p['g_pre_ffn']) * (1 + sc_f) + sh_f
    x = x + gt_f * rmsnorm(moe(h, p), p['g_post_ffn'])
    return x, kc, vc, st


def setup_inputs(seed: int = 0) -> dict:
    key = jax.random.key(seed)
    ks = iter(jax.random.split(key, 48))
    nrm = lambda shape, s=1.0: jax.random.normal(next(ks), shape, f32) * s
    gain = lambda shape: 1.0 + 0.02 * jax.random.normal(next(ks), shape, f32)
    L = DEPTH
    return {
        'x_prompt': nrm((BATCH, SEQ, D_MODEL)),
        'x_sample': nrm((DEC_BATCH, DEC_SEQ, D_MODEL)),
        'c': nrm((DEC_BATCH, D_MODEL)),
        'cache_k': nrm((DEC_BATCH, L, PAST_LEN, KV_HEADS, HEAD_DIM)),
        'cache_v': nrm((DEC_BATCH, L, PAST_LEN, KV_HEADS, HEAD_DIM)),
        'state_rwkv': nrm((DEC_BATCH, L, 2, RWKV_HEADS, RWKV_HEAD, RWKV_HEAD)),
        'c_ctx': nrm((D_MODEL,)),
        'w_ada': nrm((L, D_MODEL, 6 * D_MODEL), 0.5 * D_MODEL ** -0.5),
        'b_ada': nrm((L, 6 * D_MODEL), 0.01),
        'g_pre_mix': gain((L, D_MODEL)),
        'g_post_mix': gain((L, D_MODEL)),
        'g_pre_ffn': gain((L, D_MODEL)),
        'g_post_ffn': gain((L, D_MODEL)),
        'w_in': nrm((L, D_MODEL, D_IN), D_MODEL ** -0.5),
        'w_out': nrm((L, MIX_WIDTH, D_MODEL), MIX_WIDTH ** -0.5),
        'q_gain': gain((L, HEAD_DIM)),
        'k_gain': gain((L, HEAD_DIM)),
        'mu_prev': jax.random.uniform(next(ks), (L, RWKV_IN), f32, 0.0, 0.5),
        'mu_next': jax.random.uniform(next(ks), (L, RWKV_IN), f32, 0.0, 0.5),
        'decay_w0': nrm((L, 2, RWKV_WIDTH)),
        'decay_w2': nrm((L, 2, DECAY_RANK, RWKV_WIDTH), DECAY_RANK ** -0.5),
        'iclr_a0': nrm((L, 2, RWKV_WIDTH), 0.5),
        'iclr_a2': nrm((L, 2, ICLR_RANK, RWKV_WIDTH), ICLR_RANK ** -0.5),
        'gate_g2': nrm((L, GATE_RANK, RWKV_WIDTH), GATE_RANK ** -0.5),
        'k_k': 0.85 + nrm((L, RWKV_WIDTH), 0.02),
        'k_a': gain((L, RWKV_WIDTH)),
        'r_k': nrm((L, RWKV_HEADS, RWKV_HEAD), 0.1),
        'lnx_g': gain((L, RWKV_WIDTH)),
        'lnx_b': nrm((L, RWKV_WIDTH), 0.01),
        'router_w': nrm((L, D_MODEL, N_EXPERTS), D_MODEL ** -0.5),
        'router_b': nrm((L, N_EXPERTS), 0.01),
        'expert_wg': nrm((L, N_EXPERTS, D_MODEL, EXPERT_FF), D_MODEL ** -0.5),
        'expert_wu': nrm((L, N_EXPERTS, D_MODEL, EXPERT_FF), D_MODEL ** -0.5),
        'expert_wd': nrm((L, N_EXPERTS, EXPERT_FF, D_MODEL), EXPERT_FF ** -0.5),
        'shared_wg': nrm((L, D_MODEL, SHARED_FF), D_MODEL ** -0.5),
        'shared_wu': nrm((L, D_MODEL, SHARED_FF), D_MODEL ** -0.5),
        'shared_wd': nrm((L, SHARED_FF, D_MODEL), SHARED_FF ** -0.5),
    }


def reference(x_prompt, x_sample, c, cache_k, cache_v, state_rwkv, c_ctx, w_ada, b_ada,
              g_pre_mix, g_post_mix, g_pre_ffn, g_post_ffn, w_in, w_out, q_gain, k_gain,
              mu_prev, mu_next, decay_w0, decay_w2, iclr_a0, iclr_a2, gate_g2, k_k, k_a, r_k,
              lnx_g, lnx_b, router_w, router_b, expert_wg, expert_wu, expert_wd,
              shared_wg, shared_wu, shared_wd):
    rows = x_sample.shape[1] // GRID_W
    row_idx = jnp.repeat(jnp.arange(rows, dtype=jnp.int32), GRID_W)
    col_idx = jnp.tile(jnp.arange(GRID_W, dtype=jnp.int32), rows)
    rope = axial_rope(row_idx, col_idx)
    n_ctx = x_prompt.shape[0]
    y_p, y_s = x_prompt, x_sample
    ks_out, vs_out, st_out = [], [], []
    for l in range(DEPTH):
        p = {
            'g_pre_mix': g_pre_mix[l], 'g_post_mix': g_post_mix[l],
            'g_pre_ffn': g_pre_ffn[l], 'g_post_ffn': g_post_ffn[l],
            'w_in': w_in[l], 'w_out': w_out[l], 'q_gain': q_gain[l], 'k_gain': k_gain[l],
            'mu_prev': mu_prev[l], 'mu_next': mu_next[l],
            'decay_w0': decay_w0[l], 'decay_w2': decay_w2[l],
            'iclr_a0': iclr_a0[l], 'iclr_a2': iclr_a2[l], 'gate_g2': gate_g2[l],
            'k_k': k_k[l], 'k_a': k_a[l], 'r_k': r_k[l], 'lnx_g': lnx_g[l], 'lnx_b': lnx_b[l],
            'router_w': router_w[l], 'router_b': router_b[l],
            'expert_wg': expert_wg[l], 'expert_wu': expert_wu[l], 'expert_wd': expert_wd[l],
            'shared_wg': shared_wg[l], 'shared_wu': shared_wu[l], 'shared_wd': shared_wd[l],
        }
        mod_ctx = (jax.nn.silu(c_ctx)[None] @ w_ada[l] + b_ada[l])[:, None, :]
        mod_lat = (jax.nn.silu(c) @ w_ada[l] + b_ada[l])[:, None, :]
        s0_ctx = jnp.zeros((2, n_ctx, RWKV_HEADS, RWKV_HEAD, RWKV_HEAD), f32)
        y_p, kc, vc, st = trunk_layer(y_p, mod_ctx, p, None, None, None, s0_ctx)
        ks_out.append(kc)
        vs_out.append(vc)
        st_out.append(jnp.moveaxis(st, 0, 1))
        s0_lat = jnp.moveaxis(state_rwkv[:, l], 1, 0)
        y_s, _, _, _ = trunk_layer(y_s, mod_lat, p, rope, cache_k[:, l], cache_v[:, l], s0_lat)
    new_cache_k = jnp.stack(ks_out, axis=1)
    new_cache_v = jnp.stack(vs_out, axis=1)
    new_state_rwkv = jnp.stack(st_out, axis=1)
    return (y_prompt := y_p, y_s, new_cache_k, new_cache_v, new_state_rwkv)
```

```python
import functools

import numpy as np
import jax
import jax.numpy as jnp
from jax import lax
from jax.experimental import pallas as pl
from jax.experimental.pallas import tpu as pltpu

f32 = jnp.float32
bf16 = jnp.bfloat16

D_MODEL = 1024
GRID_W = 64
HEAD_DIM = 64
ATTN_HEADS = 8
KV_HEADS = 2
GQA_GROUP = ATTN_HEADS // KV_HEADS
ATTN_SCALE = HEAD_DIM ** -0.5
ROPE_THETA = 10000.0
ROPE_HALF = HEAD_DIM // 2
ROPE_FREQS = ROPE_HALF // 2
RWKV_HEAD = 64
RWKV_HEADS = 8
RWKV_WIDTH = RWKV_HEADS * RWKV_HEAD
DECAY_RANK = 64
ICLR_RANK = 64
GATE_RANK = 128
RWKV_IN = 3 * RWKV_WIDTH + 2 * DECAY_RANK + 2 * ICLR_RANK + GATE_RANK
DECAY_SCALE = 0.606531
GN_EPS = 64e-5
N_EXPERTS = 64
TOP_K = 6
N_GROUPS = 8
GROUP_SIZE = N_EXPERTS // N_GROUPS
TOPK_GROUPS = 4
EXPERT_FF = 256
ROUTED_SCALE = 2.5
EPS = 1e-6

LANE = 128
HEAD_PAD = LANE
CHUNK = 64
PAIR = 2 * RWKV_HEAD
N_PAIRS = RWKV_HEADS // 2
VMEM_LIMIT = 56 * 1024 * 1024
SCAN_PASSES = 3


def _cparams(*sem):
    return pltpu.CompilerParams(dimension_semantics=sem, vmem_limit_bytes=VMEM_LIMIT)


def _dot(a, b, nt=False):
    dims = (((1,), (1,)), ((), ())) if nt else (((1,), (0,)), ((), ()))
    return lax.dot_general(a, b, dims, preferred_element_type=f32)


def _split2(x):
    hi = x.astype(bf16)
    lo = (x - hi.astype(f32)).astype(bf16)
    return hi, lo


def _split3(x):
    h1 = x.astype(bf16)
    r1 = x - h1.astype(f32)
    h2 = r1.astype(bf16)
    h3 = (r1 - h2.astype(f32)).astype(bf16)
    return h1, h2, h3


def _mm(a, b, nt=False, passes=3):
    if passes == 1:
        return _dot(a.astype(bf16), b.astype(bf16), nt)
    ah, al = _split2(a)
    bh, bl = _split2(b)
    return _dot(ah, bh, nt) + (_dot(al, bh, nt) + _dot(ah, bl, nt))


def _mm_exact_rhs(a, b_bf16, nt=False):
    ah, al = _split2(a)
    return _dot(ah, b_bf16, nt) + _dot(al, b_bf16, nt)


def _rms(x, g):
    return x * lax.rsqrt(jnp.mean(x * x, axis=-1, keepdims=True) + EPS) * g


def _sigmoid(x):
    return jax.nn.sigmoid(x)


def _full_spec(a, grid_rank=1):
    zeros = (0,) * a.ndim
    if grid_rank == 1:
        return pl.BlockSpec(a.shape, lambda i: zeros)
    if grid_rank == 2:
        return pl.BlockSpec(a.shape, lambda i, j: zeros)
    return pl.BlockSpec(a.shape, lambda i, j, k: zeros)


def _mod_kernel(c_ref, w_ref, b_ref, o_ref):
    c = c_ref[...]
    s = c * _sigmoid(c)
    o_ref[...] = _mm(s, w_ref[...]) + b_ref[...]


def _modulation(cc, w_ada, b_ada):
    rows, n = cc.shape[0], w_ada.shape[1]
    tn = 512
    return pl.pallas_call(
        _mod_kernel,
        grid=(n // tn,),
        in_specs=[pl.BlockSpec((rows, D_MODEL), lambda j: (0, 0)),
                  pl.BlockSpec((D_MODEL, tn), lambda j: (0, j)),
                  pl.BlockSpec((1, tn), lambda j: (0, j))],
        out_specs=pl.BlockSpec((rows, tn), lambda j: (0, j)),
        out_shape=jax.ShapeDtypeStruct((rows, n), f32),
        compiler_params=_cparams("parallel"),
        name="modulation",
    )(cc, w_ada, b_ada.reshape(1, n))


def _in_kernel(*refs, rope):
    if rope:
        (x_ref, mod_ref, g_ref, wq_ref, wk_ref, wv_ref, wr_ref, qg_ref, kg_ref, cos_ref, sin_ref,
         q_ref, ko_ref, vo_ref, kp_ref, vp_ref, zr_ref) = refs
    else:
        (x_ref, mod_ref, g_ref, wq_ref, wk_ref, wv_ref, wr_ref, qg_ref, kg_ref,
         q_ref, ko_ref, vo_ref, kp_ref, vp_ref, zr_ref) = refs
    x = x_ref[...]
    m = mod_ref[...]
    h = _rms(x, g_ref[...])
    h = h * (1.0 + m[:, D_MODEL:2 * D_MODEL]) + m[:, 0:D_MODEL]
    hb = h.astype(bf16)
    zr_ref[...] = _dot(hb, wr_ref[...])
    zq = _dot(hb, wq_ref[...])
    zk = _dot(hb, wk_ref[...])
    zv = _dot(hb, wv_ref[...])

    lane = lax.broadcasted_iota(jnp.int32, (1, LANE), 1)
    first_half = (lane & (ROPE_HALF - 1)) < ROPE_FREQS

    def head_norm(z, gain):
        ms = jnp.sum(z * z, axis=-1, keepdims=True) * (1.0 / HEAD_DIM)
        return z * lax.rsqrt(ms + EPS) * gain

    def rotate(y):
        if not rope:
            return y
        partner = jnp.where(first_half, pltpu.roll(y, LANE - ROPE_FREQS, 1), pltpu.roll(y, ROPE_FREQS, 1))
        return y * cos_ref[...] + partner * sin_ref[...]

    for hh in range(ATTN_HEADS):
        sl = slice(hh * HEAD_PAD, (hh + 1) * HEAD_PAD)
        q_ref[:, sl] = rotate(head_norm(zq[:, sl], qg_ref[...])).astype(bf16)
    kn = [head_norm(zk[:, hh * HEAD_PAD:(hh + 1) * HEAD_PAD], kg_ref[...]) for hh in range(KV_HEADS)]
    ko_ref[...] = kn[0] + pltpu.roll(kn[1], HEAD_DIM, 1)
    vo_ref[...] = zv[:, 0:HEAD_PAD] + pltpu.roll(zv[:, HEAD_PAD:2 * HEAD_PAD], HEAD_DIM, 1)
    for hh in range(KV_HEADS):
        sl = slice(hh * HEAD_PAD, (hh + 1) * HEAD_PAD)
        kp_ref[:, sl] = rotate(kn[hh]).astype(bf16)
    vp_ref[...] = zv.astype(bf16)


def _in_proj(x2, mod3, mod_row, g_pre, wq, wk, wv, wr, qg, kg, rope_tabs, seq_len, tm):
    n = x2.shape[0]
    rope = rope_tabs is not None
    in_specs = [pl.BlockSpec((tm, D_MODEL), lambda i: (i, 0)),
                pl.BlockSpec((None, 1, 6 * D_MODEL), lambda i: (mod_row(i, tm), 0, 0)),
                _full_spec(g_pre), _full_spec(wq), _full_spec(wk), _full_spec(wv), _full_spec(wr),
                _full_spec(qg), _full_spec(kg)]
    args = [x2, mod3, g_pre, wq, wk, wv, wr, qg, kg]
    if rope:
        blocks_per_seq = seq_len // tm
        in_specs += [pl.BlockSpec((tm, LANE), lambda i: (i % blocks_per_seq, 0))] * 2
        args += list(rope_tabs)
    out_shape = [jax.ShapeDtypeStruct((n, ATTN_HEADS * HEAD_PAD), bf16),
                 jax.ShapeDtypeStruct((n, KV_HEADS * HEAD_DIM), f32),
                 jax.ShapeDtypeStruct((n, KV_HEADS * HEAD_DIM), f32),
                 jax.ShapeDtypeStruct((n, KV_HEADS * HEAD_PAD), bf16),
                 jax.ShapeDtypeStruct((n, KV_HEADS * HEAD_PAD), bf16),
                 jax.ShapeDtypeStruct((n, RWKV_IN), f32)]
    out_specs = [pl.BlockSpec((tm, s.shape[1]), lambda i: (i, 0)) for s in out_shape]
    return pl.pallas_call(
        functools.partial(_in_kernel, rope=rope),
        grid=(n // tm,), in_specs=in_specs, out_specs=out_specs, out_shape=out_shape,
        compiler_params=_cparams("parallel"), name="in_proj",
    )(*args)


def _attn_kernel(q_ref, k_ref, v_ref, o_ref):
    tq = q_ref.shape[0]
    for kv in range(KV_HEADS):
        kh = k_ref[:, kv * HEAD_PAD:(kv + 1) * HEAD_PAD]
        vh = v_ref[:, kv * HEAD_PAD:(kv + 1) * HEAD_PAD]
        q4 = jnp.concatenate(
            [q_ref[:, (GQA_GROUP * kv + g) * HEAD_PAD:(GQA_GROUP * kv + g + 1) * HEAD_PAD]
             for g in range(GQA_GROUP)], axis=0)
        s = _dot(q4, kh, nt=True)
        mx = jnp.max(s, axis=-1, keepdims=True)
        p = jnp.exp(s - mx)
        den = jnp.sum(p, axis=-1, keepdims=True)
        o = _dot(p.astype(bf16), vh) / den
        for j in range(GQA_GROUP // 2):
            lo = o[(2 * j) * tq:(2 * j + 1) * tq]
            hi = o[(2 * j + 1) * tq:(2 * j + 2) * tq]
            col = (kv * (GQA_GROUP // 2) + j) * LANE
            o_ref[:, col:col + LANE] = lo + pltpu.roll(hi, HEAD_DIM, 1)


def _attention(q_pad, k_full, v_full, batch, seq_len, tq):
    n = q_pad.shape[0]
    kv_len = k_full.shape[1]
    nq = seq_len // tq
    return pl.pallas_call(
        _attn_kernel,
        grid=(batch, nq),
        in_specs=[pl.BlockSpec((tq, ATTN_HEADS * HEAD_PAD), lambda b, j: (b * nq + j, 0)),
                  pl.BlockSpec((None, kv_len, KV_HEADS * HEAD_PAD), lambda b, j: (b, 0, 0)),
                  pl.BlockSpec((None, kv_len, KV_HEADS * HEAD_PAD), lambda b, j: (b, 0, 0))],
        out_specs=pl.BlockSpec((tq, ATTN_HEADS * HEAD_DIM), lambda b, j: (b * nq + j, 0)),
        out_shape=jax.ShapeDtypeStruct((n, ATTN_HEADS * HEAD_DIM), f32),
        compiler_params=_cparams("parallel", "parallel"), name="attention",
    )(q_pad, k_full, v_full)


def _prep_kernel(z_ref, zp_ref, zn_ref, mup_ref, mun_ref, w0_ref, w2_ref, a0_ref, a2_ref, g2_ref,
                 kkw_ref, ka_ref, rk_ref, seg_ref,
                 r_ref, k_ref, v_ref, kk_ref, lw_ref, a_ref, g_ref, bonus_ref, *, seq_len, tm):
    i = pl.program_id(0)
    z = z_ref[...]
    row = lax.broadcasted_iota(jnp.int32, (tm, 1), 0)
    seq_start = ((i * tm) % seq_len) == 0
    seq_end = (((i + 1) * tm) % seq_len) == 0
    prow = jnp.where(seq_start, 0.0, zp_ref[7:8, :])
    nrow = jnp.where(seq_end, 0.0, zn_ref[0:1, :])
    prev = jnp.where(row == 0, prow, pltpu.roll(z, 1, 0))
    nxt = jnp.where(row == tm - 1, nrow, pltpu.roll(z, tm - 1, 0))
    zs = z + mup_ref[...] * (prev - z) + mun_ref[...] * (nxt - z)
    w = RWKV_WIDTH
    r = zs[:, 0:w]
    k = zs[:, w:2 * w]
    v = zs[:, 2 * w:3 * w]
    dw = zs[:, 3 * w:3 * w + 2 * DECAY_RANK]
    da = zs[:, 3 * w + 2 * DECAY_RANK:3 * w + 2 * DECAY_RANK + 2 * ICLR_RANK]
    dg = zs[:, 3 * w + 2 * DECAY_RANK + 2 * ICLR_RANK:]
    lw = -DECAY_SCALE * _sigmoid(w0_ref[...] + _mm(jnp.tanh(dw), w2_ref[...]))
    a = _sigmoid(a0_ref[...] + _mm(da, a2_ref[...]))
    g = _mm(_sigmoid(dg), g2_ref[...])
    kk = k * kkw_ref[...]
    seg = seg_ref[...]
    kk = kk * lax.rsqrt(_mm_exact_rhs(kk * kk, seg) + 1e-12)
    ka = ka_ref[...]
    ke_sum = k * (2.0 + (a[:, 0:w] + a[:, w:2 * w] - 2.0) * ka)
    bonus = _mm_exact_rhs(r * ke_sum * rk_ref[...], seg) * v
    r_ref[...] = r
    k_ref[...] = k
    v_ref[...] = v
    kk_ref[...] = kk
    lw_ref[...] = lw
    a_ref[...] = a
    g_ref[...] = g
    bonus_ref[...] = bonus


def _rwkv_prep(zr, p, seq_len, tm):
    n = zr.shape[0]
    w = RWKV_WIDTH
    nb8 = n // 8
    consts = [p['mu_prev'], p['mu_next'], p['w0'], p['w2bd'], p['a0'], p['a2bd'], p['g2'],
              p['k_k'], p['k_a'], p['r_k'], p['seg']]
    in_specs = [pl.BlockSpec((tm, RWKV_IN), lambda i: (i, 0)),
                pl.BlockSpec((8, RWKV_IN), lambda i: (jnp.maximum(i * (tm // 8) - 1, 0), 0)),
                pl.BlockSpec((8, RWKV_IN), lambda i: (jnp.minimum((i + 1) * (tm // 8), nb8 - 1), 0))]
    in_specs += [_full_spec(c) for c in consts]
    widths = [w, w, w, w, 2 * w, 2 * w, w, w]
    out_shape = [jax.ShapeDtypeStruct((n, wd), f32) for wd in widths]
    out_specs = [pl.BlockSpec((tm, wd), lambda i: (i, 0)) for wd in widths]
    return pl.pallas_call(
        functools.partial(_prep_kernel, seq_len=seq_len, tm=tm),
        grid=(n // tm,), in_specs=in_specs, out_specs=out_specs, out_shape=out_shape,
        compiler_params=_cparams("parallel"), name="rwkv_prep",
    )(zr, zr, zr, *consts)


def _scan_kernel(r_ref, k_ref, v_ref, kk_ref, lw_ref, a_ref, ka_ref, s0_ref,
                 o_ref, sfin_ref, st_ref, lhs_ref, rhs_ref, rhsw_ref, vv_ref, wc_ref):
    d = pl.program_id(0)
    c = pl.program_id(2)

    @pl.when(c == 0)
    def _():
        st_ref[...] = s0_ref[...]

    sgn = 1 - 2 * d
    row = lax.broadcasted_iota(jnp.int32, (CHUNK, CHUNK), 0)
    col = lax.broadcasted_iota(jnp.int32, (CHUNK, CHUNK), 1)
    tri = jnp.where((row - col) * sgn >= 0, 1.0, 0.0).astype(bf16)
    lw = lw_ref[...]
    l1, l2, l3 = _split3(lw)
    cum = _dot(tri, l1) + (_dot(tri, l2) + _dot(tri, l3))
    e_incl = jnp.exp(cum)
    e_excl = jnp.exp(cum - lw)
    e_neg = jnp.exp(-cum)
    wc = jnp.exp(jnp.sum(lw, axis=0, keepdims=True))
    kk = kk_ref[...]
    a = a_ref[...]
    at = -kk * e_excl
    rt = r_ref[...] * e_incl
    bh = kk * a * e_neg
    kh = k_ref[...] * (1.0 + (a - 1.0) * ka_ref[...]) * e_neg
    lane = lax.broadcasted_iota(jnp.int32, (1, RWKV_WIDTH), 1)
    m0 = ((lane & (PAIR - 1)) < RWKV_HEAD).astype(f32)
    m1 = 1.0 - m0
    lhs_ref[0 * CHUNK:1 * CHUNK, :] = at * m0
    lhs_ref[1 * CHUNK:2 * CHUNK, :] = at * m1
    lhs_ref[2 * CHUNK:3 * CHUNK, :] = rt * m0
    lhs_ref[3 * CHUNK:4 * CHUNK, :] = rt * m1
    rhs_ref[0 * CHUNK:1 * CHUNK, :] = bh * m0
    rhs_ref[1 * CHUNK:2 * CHUNK, :] = bh * m1
    rhs_ref[2 * CHUNK:3 * CHUNK, :] = kh * m0
    rhs_ref[3 * CHUNK:4 * CHUNK, :] = kh * m1
    rhsw_ref[...] = rhs_ref[...] * wc
    v = v_ref[...]
    vv_ref[0 * CHUNK:1 * CHUNK, :] = v * m0
    vv_ref[1 * CHUNK:2 * CHUNK, :] = v * m1
    wc_ref[...] = jnp.broadcast_to(wc, wc_ref.shape)

    r2 = lax.broadcasted_iota(jnp.int32, (PAIR, PAIR), 0)
    c2 = lax.broadcasted_iota(jnp.int32, (PAIR, PAIR), 1)
    dist = ((r2 & (CHUNK - 1)) - (c2 & (CHUNK - 1))) * sgn
    dist = jnp.where((r2 & CHUNK) == (c2 & CHUNK), dist, -1)
    strict = dist > 0
    incl = dist >= 0
    eye = jnp.where(r2 == c2, 1.0, 0.0).astype(f32)
    mm = functools.partial(_mm, passes=SCAN_PASSES)

    def pair_body(p, carry):
        ls = pl.ds(pl.multiple_of(p * PAIR, PAIR), PAIR)
        lhs = lhs_ref[:, ls]
        rhs = rhs_ref[:, ls]
        amat = mm(lhs, rhs, nt=True)
        nmat = jnp.where(strict, amat[0:PAIR, 0:PAIR], 0.0)
        a_ak = jnp.where(strict, amat[0:PAIR, PAIR:2 * PAIR], 0.0)
        a_rb = jnp.where(incl, amat[PAIR:2 * PAIR, 0:PAIR], 0.0)
        a_rk = jnp.where(incl, amat[PAIR:2 * PAIR, PAIR:2 * PAIR], 0.0)
        tinv = eye + nmat
        npow = nmat
        for _ in range(5):
            npow = mm(npow, npow)
            tinv = tinv + mm(tinv, npow)
        st = st_ref[p]
        vbd = vv_ref[:, ls]
        u = mm(tinv, mm(lhs[0:PAIR], st, nt=True) + mm(a_ak, vbd))
        obd = mm(lhs[PAIR:2 * PAIR], st, nt=True) + mm(a_rb, u) + mm(a_rk, vbd)
        o_ref[:, ls] = obd[0:CHUNK] + obd[CHUNK:2 * CHUNK]
        rw = rhsw_ref[:, ls]
        st_ref[p] = st * wc_ref[0:1, ls] + mm(u.T, rw[0:PAIR]) + mm(vbd.T, rw[PAIR:2 * PAIR])
        return carry

    lax.fori_loop(0, N_PAIRS, pair_body, 0)

    @pl.when(c == pl.num_programs(2) - 1)
    def _():
        sfin_ref[...] = st_ref[...]


def _rwkv_scan(r, k, v, kk, lw, a, k_a, s0, batch, seq_len):
    n = r.shape[0]
    w = RWKV_WIDTH
    nc = seq_len // CHUNK

    def tok(d, b, c):
        return (b * nc + c + d * (nc - 1 - 2 * c), 0)

    def tok_dir(d, b, c):
        return (b * nc + c + d * (nc - 1 - 2 * c), d)

    shared = pl.BlockSpec((CHUNK, w), tok)
    per_dir = pl.BlockSpec((CHUNK, w), tok_dir)
    state_spec = pl.BlockSpec((None, None, N_PAIRS, PAIR, PAIR), lambda d, b, c: (b, d, 0, 0, 0))
    return pl.pallas_call(
        _scan_kernel,
        grid=(2, batch, nc),
        in_specs=[shared, shared, shared, shared, per_dir, per_dir,
                  pl.BlockSpec((1, w), lambda d, b, c: (0, 0)), state_spec],
        out_specs=[pl.BlockSpec((None, CHUNK, w), lambda d, b, c: (d,) + tok(d, b, c)),
                   state_spec],
        out_shape=[jax.ShapeDtypeStruct((2, n, w), f32),
                   jax.ShapeDtypeStruct((batch, 2, N_PAIRS, PAIR, PAIR), f32)],
        scratch_shapes=[pltpu.VMEM((N_PAIRS, PAIR, PAIR), f32),
                        pltpu.VMEM((4 * CHUNK, w), f32),
                        pltpu.VMEM((4 * CHUNK, w), f32),
                        pltpu.VMEM((4 * CHUNK, w), f32),
                        pltpu.VMEM((2 * CHUNK, w), f32),
                        pltpu.VMEM((8, w), f32)],
        compiler_params=_cparams("parallel", "parallel", "arbitrary"), name="rwkv_scan",
    )(r, k, v, kk, lw, a, k_a, s0)


def _out_kernel(attn_ref, o_ref, bonus_ref, g_ref, x_ref, mod_ref, lng_ref, lnb_ref, seg_ref,
                wo_ref, gpm_ref, gpf_ref, rwt_ref, rb_ref,
                x1_ref, h2_ref, gates_ref):
    tm = x_ref.shape[0]
    seg = seg_ref[...]
    inv = 1.0 / RWKV_HEAD
    o = o_ref[0] + o_ref[1]
    mu = _mm_exact_rhs(o, seg) * inv
    dl = o - mu
    var = _mm_exact_rhs(dl * dl, seg) * inv
    on = dl * lax.rsqrt(var + GN_EPS) * lng_ref[...] + lnb_ref[...]
    rw = (on + bonus_ref[...]) * g_ref[...]
    half = ATTN_HEADS * HEAD_DIM
    y = _dot(attn_ref[...].astype(bf16), wo_ref[0:half, :]) + _dot(rw.astype(bf16), wo_ref[half:, :])
    m = mod_ref[...]
    gt_a = m[:, 2 * D_MODEL:3 * D_MODEL]
    sh_f = m[:, 3 * D_MODEL:4 * D_MODEL]
    sc_f = m[:, 4 * D_MODEL:5 * D_MODEL]
    x1 = x_ref[...] + gt_a * _rms(y, gpm_ref[...])
    x1_ref[...] = x1
    h2 = _rms(x1, gpf_ref[...]) * (1.0 + sc_f) + sh_f
    h2_ref[...] = h2.astype(bf16)

    logits = _mm(rwt_ref[...], h2, nt=True)
    scores = _sigmoid(logits)
    biased = scores + rb_ref[...]
    neg = -jnp.inf
    shape3 = (N_GROUPS, GROUP_SIZE, tm)
    b3 = biased.reshape(shape3)
    s3 = scores.reshape(shape3)
    in_grp = lax.broadcasted_iota(jnp.int32, shape3, 1)
    grp = lax.broadcasted_iota(jnp.int32, shape3, 0)
    m1 = jnp.max(b3, axis=1, keepdims=True)
    i1 = jnp.min(jnp.where(b3 == m1, in_grp, GROUP_SIZE), axis=1, keepdims=True)
    m2 = jnp.max(jnp.where(in_grp == i1, neg, b3), axis=1, keepdims=True)
    gscore = m1 + m2
    gidx = lax.broadcasted_iota(jnp.int32, (N_GROUPS, 1, tm), 0)
    gsel = jnp.zeros((N_GROUPS, 1, tm), f32)
    cur = gscore
    for _ in range(TOPK_GROUPS):
        mx = jnp.max(cur, axis=0, keepdims=True)
        ii = jnp.min(jnp.where(cur == mx, gidx, N_GROUPS), axis=0, keepdims=True)
        hit = gidx == ii
        gsel = jnp.where(hit, 1.0, gsel)
        cur = jnp.where(hit, neg, cur)
    cand = jnp.where(jnp.broadcast_to(gsel, shape3) > 0.0, b3, neg)
    eidx = grp * GROUP_SIZE + in_grp
    wsel = jnp.zeros(shape3, f32)
    for _ in range(TOP_K):
        mx = jnp.max(jnp.max(cand, axis=1, keepdims=True), axis=0, keepdims=True)
        ii = jnp.min(jnp.min(jnp.where(cand == mx, eidx, N_EXPERTS), axis=1, keepdims=True),
                     axis=0, keepdims=True)
        hit = eidx == ii
        wsel = jnp.where(hit, s3, wsel)
        cand = jnp.where(hit, neg, cand)
    den = jnp.sum(jnp.sum(wsel, axis=1, keepdims=True), axis=0, keepdims=True)
    gates_t = (wsel / den * ROUTED_SCALE).reshape(N_EXPERTS, tm)
    extra_row = lax.broadcasted_iota(jnp.int32, (LANE - N_EXPERTS, tm), 0)
    extra = jnp.where(extra_row == 0, 1.0, 0.0).astype(f32)
    gates_ref[...] = jnp.concatenate([gates_t, extra], axis=0).T


def _mix_out(attn, o2, bonus, g, x2, mod3, mod_row, p, tm):
    n = x2.shape[0]
    w = RWKV_WIDTH
    consts = [p['lnx_g'], p['lnx_b'], p['seg'], p['w_out'], p['g_post_mix'], p['g_pre_ffn'],
              p['router_wt'], p['router_b']]
    row = lambda wd: pl.BlockSpec((tm, wd), lambda i: (i, 0))
    in_specs = [row(ATTN_HEADS * HEAD_DIM),
                pl.BlockSpec((2, tm, w), lambda i: (0, i, 0)),
                row(w), row(w), row(D_MODEL),
                pl.BlockSpec((None, 1, 6 * D_MODEL), lambda i: (mod_row(i, tm), 0, 0))]
    in_specs += [_full_spec(c) for c in consts]
    out_shape = [jax.ShapeDtypeStruct((n, D_MODEL), f32),
                 jax.ShapeDtypeStruct((n, D_MODEL), bf16),
                 jax.ShapeDtypeStruct((n, LANE), f32)]
    out_specs = [row(D_MODEL), row(D_MODEL), row(LANE)]
    return pl.pallas_call(
        _out_kernel, grid=(n // tm,), in_specs=in_specs, out_specs=out_specs, out_shape=out_shape,
        compiler_params=_cparams("parallel"), name="mix_out",
    )(attn, o2, bonus, g, x2, mod3, *consts)


def _moe_kernel(h_ref, gates_ref, wg_ref, wu_ref, wd_ref, x1_ref, mod_ref, gpost_ref, y_ref, acc_ref):
    e = pl.program_id(1)

    @pl.when(e == 0)
    def _():
        acc_ref[...] = jnp.zeros_like(acc_ref)

    h = h_ref[...]
    gg = _dot(h, wg_ref[...])
    uu = _dot(h, wu_ref[...])
    pick = (lax.broadcasted_iota(jnp.int32, (LANE, LANE), 0) == e).astype(bf16)
    gcol = _mm_exact_rhs(gates_ref[...], pick)
    hm = gg * _sigmoid(gg) * uu * jnp.concatenate([gcol] * (EXPERT_FF // LANE), axis=1)
    acc_ref[...] += _dot(hm.astype(bf16), wd_ref[...])

    @pl.when(e == pl.num_programs(1) - 1)
    def _():
        gt_f = mod_ref[...][:, 5 * D_MODEL:6 * D_MODEL]
        y_ref[...] = x1_ref[...] + gt_f * _rms(acc_ref[...], gpost_ref[...])


def _moe(h2, gates, x1, mod3, mod_row, wg, wu, wd, g_post, tm):
    n = h2.shape[0]
    ne = wg.shape[0]
    return pl.pallas_call(
        _moe_kernel,
        grid=(n // tm, ne),
        in_specs=[pl.BlockSpec((tm, D_MODEL), lambda i, e: (i, 0)),
                  pl.BlockSpec((tm, LANE), lambda i, e: (i, 0)),
                  pl.BlockSpec((None, D_MODEL, EXPERT_FF), lambda i, e: (e, 0, 0)),
                  pl.BlockSpec((None, D_MODEL, EXPERT_FF), lambda i, e: (e, 0, 0)),
                  pl.BlockSpec((None, EXPERT_FF, D_MODEL), lambda i, e: (e, 0, 0)),
                  pl.BlockSpec((tm, D_MODEL), lambda i, e: (i, 0)),
                  pl.BlockSpec((None, 1, 6 * D_MODEL), lambda i, e: (mod_row(i, tm), 0, 0)),
                  pl.BlockSpec((1, D_MODEL), lambda i, e: (0, 0))],
        out_specs=pl.BlockSpec((tm, D_MODEL), lambda i, e: (i, 0)),
        out_shape=jax.ShapeDtypeStruct((n, D_MODEL), f32),
        scratch_shapes=[pltpu.VMEM((tm, D_MODEL), f32)],
        compiler_params=_cparams("parallel", "arbitrary"), name="moe",
    )(h2, gates, wg, wu, wd, x1, mod3, g_post)


def _pad_heads(w, heads):
    rows = w.shape[0]
    w = w.reshape(rows, heads, HEAD_DIM)
    return jnp.pad(w, ((0, 0), (0, 0), (0, HEAD_PAD - HEAD_DIM))).reshape(rows, heads * HEAD_PAD)


def _block_diag2(m):
    z = jnp.zeros_like(m[0])
    return jnp.concatenate([jnp.concatenate([m[0], z], axis=1), jnp.concatenate([z, m[1]], axis=1)], axis=0)


def _prepare_params(w):
    aw, kw = ATTN_HEADS * HEAD_DIM, KV_HEADS * HEAD_DIM
    w_in = w['w_in']
    lane_row = lambda v: v.reshape(1, -1).astype(f32)
    head_id = np.arange(RWKV_WIDTH) // RWKV_HEAD
    seg = jnp.asarray(head_id[:, None] == head_id[None, :], dtype=bf16)
    gain_pad = lambda gvec: jnp.pad(gvec, (0, HEAD_PAD - HEAD_DIM)).reshape(1, HEAD_PAD)
    return dict(
        g_pre_mix=lane_row(w['g_pre_mix']), g_post_mix=lane_row(w['g_post_mix']),
        g_pre_ffn=lane_row(w['g_pre_ffn']), g_post_ffn=lane_row(w['g_post_ffn']),
        wq=_pad_heads(w_in[:, :aw], ATTN_HEADS).astype(bf16),
        wk=_pad_heads(w_in[:, aw:aw + kw], KV_HEADS).astype(bf16),
        wv=_pad_heads(w_in[:, aw + kw:aw + 2 * kw], KV_HEADS).astype(bf16),
        wr=w_in[:, aw + 2 * kw:].astype(bf16),
        qg=gain_pad(w['q_gain'] * ATTN_SCALE), kg=gain_pad(w['k_gain']),
        mu_prev=lane_row(w['mu_prev']), mu_next=lane_row(w['mu_next']),
        w0=lane_row(w['decay_w0']), w2bd=_block_diag2(w['decay_w2']),
        a0=lane_row(w['iclr_a0']), a2bd=_block_diag2(w['iclr_a2']),
        g2=w['gate_g2'], k_k=lane_row(w['k_k']), k_a=lane_row(w['k_a']), r_k=lane_row(w['r_k']),
        lnx_g=lane_row(w['lnx_g']), lnx_b=lane_row(w['lnx_b']), seg=seg,
        w_out=w['w_out'].astype(bf16),
        router_wt=w['router_w'].T, router_b=w['router_b'].reshape(N_EXPERTS, 1),
        wg=jnp.concatenate([w['expert_wg'], w['shared_wg'][None]], axis=0).astype(bf16),
        wu=jnp.concatenate([w['expert_wu'], w['shared_wu'][None]], axis=0).astype(bf16),
        wd=jnp.concatenate([w['expert_wd'], w['shared_wd'][None]], axis=0).astype(bf16),
    )


def _rope_tables(seq_len):
    rows = seq_len // GRID_W
    row_idx = jnp.repeat(jnp.arange(rows, dtype=jnp.int32), GRID_W)
    col_idx = jnp.tile(jnp.arange(GRID_W, dtype=jnp.int32), rows)
    inv = ROPE_THETA ** (-jnp.arange(0, ROPE_HALF, 2, dtype=f32) / ROPE_HALF)
    ang = jnp.stack([row_idx.astype(f32)[:, None] * inv, col_idx.astype(f32)[:, None] * inv], axis=1)
    cos, sin = jnp.cos(ang), jnp.sin(ang)
    cos_l = jnp.stack([cos, cos], axis=2).reshape(seq_len, HEAD_DIM)
    sin_l = jnp.stack([-sin, sin], axis=2).reshape(seq_len, HEAD_DIM)
    pad = ((0, 0), (0, HEAD_PAD - HEAD_DIM))
    return jnp.pad(cos_l, pad), jnp.pad(sin_l, pad)


def _states_to_pairs(s):
    b = s.shape[0]
    s = s.reshape(b, 2, N_PAIRS, 2, RWKV_HEAD, RWKV_HEAD)
    eye = jnp.eye(2, dtype=s.dtype)
    return jnp.einsum('bdphvk,hg->bdphvgk', s, eye).reshape(b, 2, N_PAIRS, PAIR, PAIR)


def _pairs_to_states(s):
    b = s.shape[0]
    s = s.reshape(b, 2, N_PAIRS, 2, RWKV_HEAD, 2, RWKV_HEAD)
    eye = jnp.eye(2, dtype=s.dtype)
    return jnp.einsum('bdphvgk,hg->bdphvk', s, eye).reshape(b, 2, RWKV_HEADS, RWKV_HEAD, RWKV_HEAD)


def _trunk(x, mod3, mod_row, p, rope_tabs, past_k, past_v, s0_pairs, tm_in, tq, tm_prep, tm_out, tm_moe):
    b, t, _ = x.shape
    n = b * t
    x2 = x.reshape(n, D_MODEL)
    q_pad, k_own, v_own, k_pad, v_pad, zr = _in_proj(
        x2, mod3, mod_row, p['g_pre_mix'], p['wq'], p['wk'], p['wv'], p['wr'], p['qg'], p['kg'],
        rope_tabs, t, tm_in)
    k_full = k_pad.reshape(b, t, KV_HEADS * HEAD_PAD)
    v_full = v_pad.reshape(b, t, KV_HEADS * HEAD_PAD)
    if past_k is not None:
        past = lambda c: _pad_heads(c.reshape(-1, KV_HEADS * HEAD_DIM), KV_HEADS).astype(bf16).reshape(
            b, -1, KV_HEADS * HEAD_PAD)
        k_full = jnp.concatenate([past(past_k), k_full], axis=1)
        v_full = jnp.concatenate([past(past_v), v_full], axis=1)
    attn = _attention(q_pad, k_full, v_full, b, t, tq)
    r, k, v, kk, lw, a, g, bonus = _rwkv_prep(zr, p, t, tm_prep)
    o2, s_fin = _rwkv_scan(r, k, v, kk, lw, a, p['k_a'], s0_pairs, b, t)
    x1, h2, gates = _mix_out(attn, o2, bonus, g, x2, mod3, mod_row, p, tm_out)
    y = _moe(h2, gates, x1, mod3, mod_row, p['wg'], p['wu'], p['wd'], p['g_post_ffn'], tm_moe)
    return y.reshape(b, t, D_MODEL), k_own, v_own, s_fin


def kernel(x_prompt, x_sample, c, cache_k, cache_v, state_rwkv, c_ctx, w_ada, b_ada, g_pre_mix, g_post_mix,
           g_pre_ffn, g_post_ffn, w_in, w_out, q_gain, k_gain, mu_prev, mu_next, decay_w0, decay_w2, iclr_a0,
           iclr_a2, gate_g2, k_k, k_a, r_k, lnx_g, lnx_b, router_w, router_b, expert_wg, expert_wu, expert_wd,
           shared_wg, shared_wu, shared_wd):
    layer = 0
    names = ('g_pre_mix g_post_mix g_pre_ffn g_post_ffn w_in w_out q_gain k_gain mu_prev mu_next decay_w0 '
             'decay_w2 iclr_a0 iclr_a2 gate_g2 k_k k_a r_k lnx_g lnx_b router_w router_b expert_wg expert_wu '
             'expert_wd shared_wg shared_wu shared_wd').split()
    vals = (g_pre_mix, g_post_mix, g_pre_ffn, g_post_ffn, w_in, w_out, q_gain, k_gain, mu_prev, mu_next,
            decay_w0, decay_w2, iclr_a0, iclr_a2, gate_g2, k_k, k_a, r_k, lnx_g, lnx_b, router_w, router_b,
            expert_wg, expert_wu, expert_wd, shared_wg, shared_wu, shared_wd)
    p = _prepare_params({nm: v[layer] for nm, v in zip(names, vals)})

    nb, ts, _ = x_sample.shape
    npb, tp, _ = x_prompt.shape
    mod_rows = 16
    cc = jnp.zeros((mod_rows, D_MODEL), f32).at[0].set(c_ctx).at[1:1 + nb].set(c)
    mod3 = _modulation(cc, w_ada[layer], b_ada[layer]).reshape(mod_rows, 1, 6 * D_MODEL)

    zeros_state = jnp.zeros((npb, 2, N_PAIRS, PAIR, PAIR), f32)
    y_p, kc, vc, st = _trunk(x_prompt, mod3, lambda i, tm: 0, p, None, None, None, zeros_state,
                             tm_in=256, tq=tp, tm_prep=256, tm_out=256, tm_moe=1024)
    s0_lat = _states_to_pairs(state_rwkv[:, layer])
    y_s, _, _, _ = _trunk(x_sample, mod3, lambda i, tm: 1 + (i * tm) // ts, p, _rope_tables(ts),
                          cache_k[:, layer], cache_v[:, layer], s0_lat,
                          tm_in=512, tq=128, tm_prep=256, tm_out=512, tm_moe=1024)

    new_cache_k = kc.reshape(npb, 1, tp, KV_HEADS, HEAD_DIM)
    new_cache_v = vc.reshape(npb, 1, tp, KV_HEADS, HEAD_DIM)
    new_state = _pairs_to_states(st)[:, None]
    return (y_p, y_s, new_cache_k, new_cache_v, new_state)
```

```python
import functools

import numpy as np
import jax
import jax.numpy as jnp
from jax import lax
from jax.experimental import pallas as pl
from jax.experimental.pallas import tpu as pltpu

f32 = jnp.float32
bf16 = jnp.bfloat16

D_MODEL = 1024
GRID_W = 64
HEAD_DIM = 64
ATTN_HEADS = 8
KV_HEADS = 2
GQA_GROUP = ATTN_HEADS // KV_HEADS
ATTN_SCALE = HEAD_DIM ** -0.5
ROPE_THETA = 10000.0
ROPE_HALF = HEAD_DIM // 2
ROPE_FREQS = ROPE_HALF // 2
RWKV_HEAD = 64
RWKV_HEADS = 8
RWKV_WIDTH = RWKV_HEADS * RWKV_HEAD
DECAY_RANK = 64
ICLR_RANK = 64
GATE_RANK = 128
RWKV_IN = 3 * RWKV_WIDTH + 2 * DECAY_RANK + 2 * ICLR_RANK + GATE_RANK
DECAY_SCALE = 0.606531
GN_EPS = 64e-5
N_EXPERTS = 64
TOP_K = 6
N_GROUPS = 8
GROUP_SIZE = N_EXPERTS // N_GROUPS
TOPK_GROUPS = 4
EXPERT_FF = 256
ROUTED_SCALE = 2.5
EPS = 1e-6

LANE = 128
HEAD_PAD = LANE
CHUNK = 64
PAIR = 2 * RWKV_HEAD
N_PAIRS = RWKV_HEADS // 2
VMEM_LIMIT = 56 * 1024 * 1024
SCAN_PASSES = 1


def _cparams(*sem):
    return pltpu.CompilerParams(dimension_semantics=sem, vmem_limit_bytes=VMEM_LIMIT)


def _dot(a, b, nt=False):
    dims = (((1,), (1,)), ((), ())) if nt else (((1,), (0,)), ((), ()))
    return lax.dot_general(a, b, dims, preferred_element_type=f32)


def _split2(x):
    hi = x.astype(bf16)
    lo = (x - hi.astype(f32)).astype(bf16)
    return hi, lo


def _split3(x):
    h1 = x.astype(bf16)
    r1 = x - h1.astype(f32)
    h2 = r1.astype(bf16)
    h3 = (r1 - h2.astype(f32)).astype(bf16)
    return h1, h2, h3


def _mm(a, b, nt=False, passes=3):
    if passes == 1:
        return _dot(a.astype(bf16), b.astype(bf16), nt)
    ah, al = _split2(a)
    bh, bl = _split2(b)
    return _dot(ah, bh, nt) + (_dot(al, bh, nt) + _dot(ah, bl, nt))


def _mm_exact_rhs(a, b_bf16, nt=False):
    ah, al = _split2(a)
    return _dot(ah, b_bf16, nt) + _dot(al, b_bf16, nt)


def _rms(x, g):
    return x * lax.rsqrt(jnp.mean(x * x, axis=-1, keepdims=True) + EPS) * g


def _sigmoid(x):
    return jax.nn.sigmoid(x)


def _full_spec(a, grid_rank=1):
    zeros = (0,) * a.ndim
    if grid_rank == 1:
        return pl.BlockSpec(a.shape, lambda i: zeros)
    if grid_rank == 2:
        return pl.BlockSpec(a.shape, lambda i, j: zeros)
    return pl.BlockSpec(a.shape, lambda i, j, k: zeros)


def _mod_kernel(c_ref, w_ref, b_ref, o_ref):
    c = c_ref[...]
    s = c * _sigmoid(c)
    o_ref[...] = _mm(s, w_ref[...]) + b_ref[...]


def _modulation(cc, w_ada, b_ada):
    rows, n = cc.shape[0], w_ada.shape[1]
    tn = 512
    return pl.pallas_call(
        _mod_kernel,
        grid=(n // tn,),
        in_specs=[pl.BlockSpec((rows, D_MODEL), lambda j: (0, 0)),
                  pl.BlockSpec((D_MODEL, tn), lambda j: (0, j)),
                  pl.BlockSpec((1, tn), lambda j: (0, j))],
        out_specs=pl.BlockSpec((rows, tn), lambda j: (0, j)),
        out_shape=jax.ShapeDtypeStruct((rows, n), f32),
        compiler_params=_cparams("parallel"),
        name="modulation",
    )(cc, w_ada, b_ada.reshape(1, n))


def _in_kernel(*refs, rope):
    if rope:
        (x_ref, mod_ref, g_ref, wq_ref, wk_ref, wv_ref, wr_ref, qg_ref, kg_ref, cos_ref, sin_ref,
         q_ref, ko_ref, vo_ref, kp_ref, vp_ref, zr_ref) = refs
    else:
        (x_ref, mod_ref, g_ref, wq_ref, wk_ref, wv_ref, wr_ref, qg_ref, kg_ref,
         q_ref, ko_ref, vo_ref, kp_ref, vp_ref, zr_ref) = refs
    x = x_ref[...]
    m = mod_ref[...]
    h = _rms(x, g_ref[...])
    h = h * (1.0 + m[:, D_MODEL:2 * D_MODEL]) + m[:, 0:D_MODEL]
    hb = h.astype(bf16)
    zr_ref[...] = _dot(hb, wr_ref[...])
    zq = _dot(hb, wq_ref[...])
    zk = _dot(hb, wk_ref[...])
    zv = _dot(hb, wv_ref[...])

    lane = lax.broadcasted_iota(jnp.int32, (1, LANE), 1)
    first_half = (lane & (ROPE_HALF - 1)) < ROPE_FREQS

    def head_norm(z, gain):
        ms = jnp.sum(z * z, axis=-1, keepdims=True) * (1.0 / HEAD_DIM)
        return z * lax.rsqrt(ms + EPS) * gain

    def rotate(y):
        if not rope:
            return y
        partner = jnp.where(first_half, pltpu.roll(y, LANE - ROPE_FREQS, 1), pltpu.roll(y, ROPE_FREQS, 1))
        return y * cos_ref[...] + partner * sin_ref[...]

    for hh in range(ATTN_HEADS):
        sl = slice(hh * HEAD_PAD, (hh + 1) * HEAD_PAD)
        q_ref[:, sl] = rotate(head_norm(zq[:, sl], qg_ref[...])).astype(bf16)
    kn = [head_norm(zk[:, hh * HEAD_PAD:(hh + 1) * HEAD_PAD], kg_ref[...]) for hh in range(KV_HEADS)]
    ko_ref[...] = kn[0] + pltpu.roll(kn[1], HEAD_DIM, 1)
    vo_ref[...] = zv[:, 0:HEAD_PAD] + pltpu.roll(zv[:, HEAD_PAD:2 * HEAD_PAD], HEAD_DIM, 1)
    for hh in range(KV_HEADS):
        sl = slice(hh * HEAD_PAD, (hh + 1) * HEAD_PAD)
        kp_ref[:, sl] = rotate(kn[hh]).astype(bf16)
    vp_ref[...] = zv.astype(bf16)


def _in_proj(x2, mod3, mod_row, g_pre, wq, wk, wv, wr, qg, kg, rope_tabs, seq_len, tm):
    n = x2.shape[0]
    rope = rope_tabs is not None
    in_specs = [pl.BlockSpec((tm, D_MODEL), lambda i: (i, 0)),
                pl.BlockSpec((None, 1, 6 * D_MODEL), lambda i: (mod_row(i, tm), 0, 0)),
                _full_spec(g_pre), _full_spec(wq), _full_spec(wk), _full_spec(wv), _full_spec(wr),
                _full_spec(qg), _full_spec(kg)]
    args = [x2, mod3, g_pre, wq, wk, wv, wr, qg, kg]
    if rope:
        blocks_per_seq = seq_len // tm
        in_specs += [pl.BlockSpec((tm, LANE), lambda i: (i % blocks_per_seq, 0))] * 2
        args += list(rope_tabs)
    out_shape = [jax.ShapeDtypeStruct((n, ATTN_HEADS * HEAD_PAD), bf16),
                 jax.ShapeDtypeStruct((n, KV_HEADS * HEAD_DIM), f32),
                 jax.ShapeDtypeStruct((n, KV_HEADS * HEAD_DIM), f32),
                 jax.ShapeDtypeStruct((n, KV_HEADS * HEAD_PAD), bf16),
                 jax.ShapeDtypeStruct((n, KV_HEADS * HEAD_PAD), bf16),
                 jax.ShapeDtypeStruct((n, RWKV_IN), f32)]
    out_specs = [pl.BlockSpec((tm, s.shape[1]), lambda i: (i, 0)) for s in out_shape]
    return pl.pallas_call(
        functools.partial(_in_kernel, rope=rope),
        grid=(n // tm,), in_specs=in_specs, out_specs=out_specs, out_shape=out_shape,
        compiler_params=_cparams("parallel"), name="in_proj",
    )(*args)


def _attn_kernel(q_ref, k_ref, v_ref, o_ref):
    tq = q_ref.shape[0]
    for kv in range(KV_HEADS):
        kh = k_ref[:, kv * HEAD_PAD:(kv + 1) * HEAD_PAD]
        vh = v_ref[:, kv * HEAD_PAD:(kv + 1) * HEAD_PAD]
        q4 = jnp.concatenate(
            [q_ref[:, (GQA_GROUP * kv + g) * HEAD_PAD:(GQA_GROUP * kv + g + 1) * HEAD_PAD]
             for g in range(GQA_GROUP)], axis=0)
        s = _dot(q4, kh, nt=True)
        mx = jnp.max(s, axis=-1, keepdims=True)
        p = jnp.exp(s - mx)
        den = jnp.sum(p, axis=-1, keepdims=True)
        o = _dot(p.astype(bf16), vh) / den
        for j in range(GQA_GROUP // 2):
            lo = o[(2 * j) * tq:(2 * j + 1) * tq]
            hi = o[(2 * j + 1) * tq:(2 * j + 2) * tq]
            col = (kv * (GQA_GROUP // 2) + j) * LANE
            o_ref[:, col:col + LANE] = lo + pltpu.roll(hi, HEAD_DIM, 1)


def _attention(q_pad, k_full, v_full, batch, seq_len, tq):
    n = q_pad.shape[0]
    kv_len = k_full.shape[1]
    nq = seq_len // tq
    return pl.pallas_call(
        _attn_kernel,
        grid=(batch, nq),
        in_specs=[pl.BlockSpec((tq, ATTN_HEADS * HEAD_PAD), lambda b, j: (b * nq + j, 0)),
                  pl.BlockSpec((None, kv_len, KV_HEADS * HEAD_PAD), lambda b, j: (b, 0, 0)),
                  pl.BlockSpec((None, kv_len, KV_HEADS * HEAD_PAD), lambda b, j: (b, 0, 0))],
        out_specs=pl.BlockSpec((tq, ATTN_HEADS * HEAD_DIM), lambda b, j: (b * nq + j, 0)),
        out_shape=jax.ShapeDtypeStruct((n, ATTN_HEADS * HEAD_DIM), f32),
        compiler_params=_cparams("parallel", "parallel"), name="attention",
    )(q_pad, k_full, v_full)


def _prep_kernel(z_ref, zp_ref, zn_ref, mup_ref, mun_ref, w0_ref, w2_ref, a0_ref, a2_ref, g2_ref,
                 kkw_ref, ka_ref, rk_ref, seg_ref,
                 r_ref, k_ref, v_ref, kk_ref, lw_ref, a_ref, g_ref, bonus_ref, *, seq_len, tm):
    i = pl.program_id(0)
    z = z_ref[...]
    row = lax.broadcasted_iota(jnp.int32, (tm, 1), 0)
    seq_start = ((i * tm) % seq_len) == 0
    seq_end = (((i + 1) * tm) % seq_len) == 0
    prow = jnp.where(seq_start, 0.0, zp_ref[7:8, :])
    nrow = jnp.where(seq_end, 0.0, zn_ref[0:1, :])
    prev = jnp.where(row == 0, prow, pltpu.roll(z, 1, 0))
    nxt = jnp.where(row == tm - 1, nrow, pltpu.roll(z, tm - 1, 0))
    zs = z + mup_ref[...] * (prev - z) + mun_ref[...] * (nxt - z)
    w = RWKV_WIDTH
    r = zs[:, 0:w]
    k = zs[:, w:2 * w]
    v = zs[:, 2 * w:3 * w]
    dw = zs[:, 3 * w:3 * w + 2 * DECAY_RANK]
    da = zs[:, 3 * w + 2 * DECAY_RANK:3 * w + 2 * DECAY_RANK + 2 * ICLR_RANK]
    dg = zs[:, 3 * w + 2 * DECAY_RANK + 2 * ICLR_RANK:]
    lw = -DECAY_SCALE * _sigmoid(w0_ref[...] + _mm(jnp.tanh(dw), w2_ref[...]))
    a = _sigmoid(a0_ref[...] + _mm(da, a2_ref[...]))
    g = _mm(_sigmoid(dg), g2_ref[...])
    kk = k * kkw_ref[...]
    seg = seg_ref[...]
    kk = kk * lax.rsqrt(_mm_exact_rhs(kk * kk, seg) + 1e-12)
    ka = ka_ref[...]
    ke_sum = k * (2.0 + (a[:, 0:w] + a[:, w:2 * w] - 2.0) * ka)
    bonus = _mm_exact_rhs(r * ke_sum * rk_ref[...], seg) * v
    r_ref[...] = r
    k_ref[...] = k
    v_ref[...] = v
    kk_ref[...] = kk
    lw_ref[...] = lw
    a_ref[...] = a
    g_ref[...] = g
    bonus_ref[...] = bonus


def _rwkv_prep(zr, p, seq_len, tm):
    n = zr.shape[0]
    w = RWKV_WIDTH
    nb8 = n // 8
    consts = [p['mu_prev'], p['mu_next'], p['w0'], p['w2bd'], p['a0'], p['a2bd'], p['g2'],
              p['k_k'], p['k_a'], p['r_k'], p['seg']]
    in_specs = [pl.BlockSpec((tm, RWKV_IN), lambda i: (i, 0)),
                pl.BlockSpec((8, RWKV_IN), lambda i: (jnp.maximum(i * (tm // 8) - 1, 0), 0)),
                pl.BlockSpec((8, RWKV_IN), lambda i: (jnp.minimum((i + 1) * (tm // 8), nb8 - 1), 0))]
    in_specs += [_full_spec(c) for c in consts]
    widths = [w, w, w, w, 2 * w, 2 * w, w, w]
    out_shape = [jax.ShapeDtypeStruct((n, wd), f32) for wd in widths]
    out_specs = [pl.BlockSpec((tm, wd), lambda i: (i, 0)) for wd in widths]
    return pl.pallas_call(
        functools.partial(_prep_kernel, seq_len=seq_len, tm=tm),
        grid=(n // tm,), in_specs=in_specs, out_specs=out_specs, out_shape=out_shape,
        compiler_params=_cparams("parallel"), name="rwkv_prep",
    )(zr, zr, zr, *consts)


def _scan_kernel(rf_ref, kf_ref, vf_ref, kkf_ref, lwf_ref, af_ref,
                 rb_ref, kb_ref, vb_ref, kkb_ref, lwb_ref, ab_ref, ka_ref, s0_ref,
                 of_ref, ob_ref, sfin_ref, st_ref):
    c = pl.program_id(1)

    @pl.when(c == 0)
    def _():
        st_ref[...] = s0_ref[...]

    cast = (lambda x: x.astype(bf16)) if SCAN_PASSES == 1 else (lambda x: x)
    mm = functools.partial(_mm, passes=SCAN_PASSES)
    row = lax.broadcasted_iota(jnp.int32, (CHUNK, CHUNK), 0)
    col = lax.broadcasted_iota(jnp.int32, (CHUNK, CHUNK), 1)
    lane = lax.broadcasted_iota(jnp.int32, (1, RWKV_WIDTH), 1)
    m0 = ((lane & (PAIR - 1)) < RWKV_HEAD).astype(f32)
    m1 = 1.0 - m0
    r2 = lax.broadcasted_iota(jnp.int32, (PAIR, PAIR), 0)
    c2 = lax.broadcasted_iota(jnp.int32, (PAIR, PAIR), 1)
    same_head = (r2 & CHUNK) == (c2 & CHUNK)
    eye = jnp.where(r2 == c2, 1.0, 0.0).astype(f32)
    ka = ka_ref[...]
    sl = [slice(p * PAIR, (p + 1) * PAIR) for p in range(N_PAIRS)]

    def stack(x):
        return jnp.concatenate([x * m0, x * m1], axis=0)

    chains = []
    for d, (r_ref, k_ref, v_ref, kk_ref, lw_ref, a_ref) in enumerate(
            ((rf_ref, kf_ref, vf_ref, kkf_ref, lwf_ref, af_ref),
             (rb_ref, kb_ref, vb_ref, kkb_ref, lwb_ref, ab_ref))):
        sgn = 1 - 2 * d
        tri = jnp.where((row - col) * sgn >= 0, 1.0, 0.0).astype(bf16)
        lw = lw_ref[...]
        l1, l2, l3 = _split3(lw)
        cum = _dot(tri, l1) + (_dot(tri, l2) + _dot(tri, l3))
        e_incl = jnp.exp(cum)
        e_excl = jnp.exp(cum - lw)
        e_neg = jnp.exp(-cum)
        wc = jnp.exp(jnp.sum(lw, axis=0, keepdims=True))
        kk = kk_ref[...]
        a = a_ref[...]
        bh_s = stack(kk * a * e_neg)
        kh_s = stack(k_ref[...] * (1.0 + (a - 1.0) * ka) * e_neg)
        v_t = stack(v_ref[...])
        ops = dict(at=cast(stack(-kk * e_excl)), rt=cast(stack(r_ref[...] * e_incl)),
                   bh=cast(bh_s), kh=cast(kh_s), bw=cast(bh_s * wc), kw=cast(kh_s * wc),
                   vt=v_t, vs=cast(v_t), wc=wc)
        dist = jnp.where(same_head, ((r2 & (CHUNK - 1)) - (c2 & (CHUNK - 1))) * sgn, -1)
        for p in range(N_PAIRS):
            ch = {key: val[:, sl[p]] for key, val in ops.items()}
            ch.update(d=d, p=p, strict=dist > 0, incl=dist >= 0, st=st_ref[d, p])
            chains.append(ch)

    for ch in chains:
        ch['stc'] = cast(ch['st'])
        amat = mm(jnp.concatenate([ch['at'], ch['rt']], axis=0),
                  jnp.concatenate([ch['bh'], ch['kh']], axis=0), nt=True)
        ch['n'] = jnp.where(ch['strict'], amat[0:PAIR, 0:PAIR], 0.0)
        ch['a_ak'] = cast(jnp.where(ch['strict'], amat[0:PAIR, PAIR:2 * PAIR], 0.0))
        ch['a_rb'] = cast(jnp.where(ch['incl'], amat[PAIR:2 * PAIR, 0:PAIR], 0.0))
        ch['a_rk'] = cast(jnp.where(ch['incl'], amat[PAIR:2 * PAIR, PAIR:2 * PAIR], 0.0))
        ch['tinv'] = eye + ch['n']
    for _ in range(5):
        for ch in chains:
            ch['n'] = mm(ch['n'], ch['n'])
        for ch in chains:
            ch['tinv'] = ch['tinv'] + mm(ch['tinv'], ch['n'])
    for ch in chains:
        ch['rhs'] = mm(ch['at'], ch['stc'], nt=True) + mm(ch['a_ak'], ch['vs'])
    for ch in chains:
        ch['u'] = mm(ch['tinv'], ch['rhs'])
    for ch in chains:
        ch['obd'] = mm(ch['rt'], ch['stc'], nt=True) + mm(ch['a_rb'], ch['u']) + mm(ch['a_rk'], ch['vs'])
        ch['st_new'] = ch['st'] * ch['wc'] + mm(ch['u'].T, ch['bw']) + mm(ch['vt'].T, ch['kw'])
    for ch in chains:
        o_ref = ob_ref if ch['d'] else of_ref
        o_ref[:, sl[ch['p']]] = ch['obd'][0:CHUNK] + ch['obd'][CHUNK:2 * CHUNK]
        st_ref[ch['d'], ch['p']] = ch['st_new']

    @pl.when(c == pl.num_programs(1) - 1)
    def _():
        sfin_ref[...] = st_ref[...]


def _rwkv_scan(r, k, v, kk, lw, a, k_a, s0, batch, seq_len):
    n = r.shape[0]
    w = RWKV_WIDTH
    nc = seq_len // CHUNK
    fwd = lambda b, c: (b * nc + c, 0)
    bwd = lambda b, c: (b * nc + nc - 1 - c, 0)
    bwd_dir = lambda b, c: (b * nc + nc - 1 - c, 1)
    blk = lambda index_map: pl.BlockSpec((CHUNK, w), index_map)
    state_spec = pl.BlockSpec((None, 2, N_PAIRS, PAIR, PAIR), lambda b, c: (b, 0, 0, 0, 0))
    return pl.pallas_call(
        _scan_kernel,
        grid=(batch, nc),
        in_specs=[blk(fwd)] * 6 + [blk(bwd)] * 4 + [blk(bwd_dir)] * 2
                 + [pl.BlockSpec((1, w), lambda b, c: (0, 0)), state_spec],
        out_specs=[blk(fwd), blk(bwd), state_spec],
        out_shape=[jax.ShapeDtypeStruct((n, w), f32), jax.ShapeDtypeStruct((n, w), f32),
                   jax.ShapeDtypeStruct((batch, 2, N_PAIRS, PAIR, PAIR), f32)],
        scratch_shapes=[pltpu.VMEM((2, N_PAIRS, PAIR, PAIR), f32)],
        compiler_params=_cparams("parallel", "arbitrary"), name="rwkv_scan",
    )(r, k, v, kk, lw, a, r, k, v, kk, lw, a, k_a, s0)


def _out_kernel(attn_ref, of_ref, ob_ref, bonus_ref, g_ref, x_ref, mod_ref, lng_ref, lnb_ref, seg_ref,
                wo_ref, gpm_ref, gpf_ref, rwt_ref, rb_ref,
                x1_ref, h2_ref, gates_ref):
    tm = x_ref.shape[0]
    seg = seg_ref[...]
    inv = 1.0 / RWKV_HEAD
    o = of_ref[...] + ob_ref[...]
    mu = _mm_exact_rhs(o, seg) * inv
    dl = o - mu
    var = _mm_exact_rhs(dl * dl, seg) * inv
    on = dl * lax.rsqrt(var + GN_EPS) * lng_ref[...] + lnb_ref[...]
    rw = (on + bonus_ref[...]) * g_ref[...]
    half = ATTN_HEADS * HEAD_DIM
    y = _dot(attn_ref[...].astype(bf16), wo_ref[0:half, :]) + _dot(rw.astype(bf16), wo_ref[half:, :])
    m = mod_ref[...]
    gt_a = m[:, 2 * D_MODEL:3 * D_MODEL]
    sh_f = m[:, 3 * D_MODEL:4 * D_MODEL]
    sc_f = m[:, 4 * D_MODEL:5 * D_MODEL]
    x1 = x_ref[...] + gt_a * _rms(y, gpm_ref[...])
    x1_ref[...] = x1
    h2 = _rms(x1, gpf_ref[...]) * (1.0 + sc_f) + sh_f
    h2_ref[...] = h2.astype(bf16)

    logits = _mm(rwt_ref[...], h2, nt=True)
    scores = _sigmoid(logits)
    biased = scores + rb_ref[...]
    neg = -jnp.inf
    shape3 = (N_GROUPS, GROUP_SIZE, tm)
    b3 = biased.reshape(shape3)
    s3 = scores.reshape(shape3)
    in_grp = lax.broadcasted_iota(jnp.int32, shape3, 1)
    grp = lax.broadcasted_iota(jnp.int32, shape3, 0)
    m1 = jnp.max(b3, axis=1, keepdims=True)
    i1 = jnp.min(jnp.where(b3 == m1, in_grp, GROUP_SIZE), axis=1, keepdims=True)
    m2 = jnp.max(jnp.where(in_grp == i1, neg, b3), axis=1, keepdims=True)
    gscore = m1 + m2
    gidx = lax.broadcasted_iota(jnp.int32, (N_GROUPS, 1, tm), 0)
    gsel = jnp.zeros((N_GROUPS, 1, tm), f32)
    cur = gscore
    for _ in range(TOPK_GROUPS):
        mx = jnp.max(cur, axis=0, keepdims=True)
        ii = jnp.min(jnp.where(cur == mx, gidx, N_GROUPS), axis=0, keepdims=True)
        hit = gidx == ii
        gsel = jnp.where(hit, 1.0, gsel)
        cur = jnp.where(hit, neg, cur)
    cand = jnp.where(jnp.broadcast_to(gsel, shape3) > 0.0, b3, neg)
    eidx = grp * GROUP_SIZE + in_grp
    wsel = jnp.zeros(shape3, f32)
    for _ in range(TOP_K):
        mx = jnp.max(jnp.max(cand, axis=1, keepdims=True), axis=0, keepdims=True)
        ii = jnp.min(jnp.min(jnp.where(cand == mx, eidx, N_EXPERTS), axis=1, keepdims=True),
                     axis=0, keepdims=True)
        hit = eidx == ii
        wsel = jnp.where(hit, s3, wsel)
        cand = jnp.where(hit, neg, cand)
    den = jnp.sum(jnp.sum(wsel, axis=1, keepdims=True), axis=0, keepdims=True)
    gates_t = (wsel / den * ROUTED_SCALE).reshape(N_EXPERTS, tm)
    extra_row = lax.broadcasted_iota(jnp.int32, (LANE - N_EXPERTS, tm), 0)
    extra = jnp.where(extra_row == 0, 1.0, 0.0).astype(f32)
    gates_ref[...] = jnp.concatenate([gates_t, extra], axis=0).T


def _mix_out(attn, o_fw, o_bw, bonus, g, x2, mod3, mod_row, p, tm):
    n = x2.shape[0]
    w = RWKV_WIDTH
    consts = [p['lnx_g'], p['lnx_b'], p['seg'], p['w_out'], p['g_post_mix'], p['g_pre_ffn'],
              p['router_wt'], p['router_b']]
    row = lambda wd: pl.BlockSpec((tm, wd), lambda i: (i, 0))
    in_specs = [row(ATTN_HEADS * HEAD_DIM), row(w), row(w),
                row(w), row(w), row(D_MODEL),
                pl.BlockSpec((None, 1, 6 * D_MODEL), lambda i: (mod_row(i, tm), 0, 0))]
    in_specs += [_full_spec(c) for c in consts]
    out_shape = [jax.ShapeDtypeStruct((n, D_MODEL), f32),
                 jax.ShapeDtypeStruct((n, D_MODEL), bf16),
                 jax.ShapeDtypeStruct((n, LANE), f32)]
    out_specs = [row(D_MODEL), row(D_MODEL), row(LANE)]
    return pl.pallas_call(
        _out_kernel, grid=(n // tm,), in_specs=in_specs, out_specs=out_specs, out_shape=out_shape,
        compiler_params=_cparams("parallel"), name="mix_out",
    )(attn, o_fw, o_bw, bonus, g, x2, mod3, *consts)


def _moe_kernel(h_ref, gates_ref, wg_ref, wu_ref, wd_ref, x1_ref, mod_ref, gpost_ref, y_ref, acc_ref):
    e = pl.program_id(1)

    @pl.when(e == 0)
    def _():
        acc_ref[...] = jnp.zeros_like(acc_ref)

    h = h_ref[...]
    gg = _dot(h, wg_ref[...])
    uu = _dot(h, wu_ref[...])
    pick = (lax.broadcasted_iota(jnp.int32, (LANE, LANE), 0) == e).astype(bf16)
    gcol = _mm_exact_rhs(gates_ref[...], pick)
    hm = gg * _sigmoid(gg) * uu * jnp.concatenate([gcol] * (EXPERT_FF // LANE), axis=1)
    acc_ref[...] += _dot(hm.astype(bf16), wd_ref[...])

    @pl.when(e == pl.num_programs(1) - 1)
    def _():
        gt_f = mod_ref[...][:, 5 * D_MODEL:6 * D_MODEL]
        y_ref[...] = x1_ref[...] + gt_f * _rms(acc_ref[...], gpost_ref[...])


def _moe(h2, gates, x1, mod3, mod_row, wg, wu, wd, g_post, tm):
    n = h2.shape[0]
    ne = wg.shape[0]
    return pl.pallas_call(
        _moe_kernel,
        grid=(n // tm, ne),
        in_specs=[pl.BlockSpec((tm, D_MODEL), lambda i, e: (i, 0)),
                  pl.BlockSpec((tm, LANE), lambda i, e: (i, 0)),
                  pl.BlockSpec((None, D_MODEL, EXPERT_FF), lambda i, e: (e, 0, 0)),
                  pl.BlockSpec((None, D_MODEL, EXPERT_FF), lambda i, e: (e, 0, 0)),
                  pl.BlockSpec((None, EXPERT_FF, D_MODEL), lambda i, e: (e, 0, 0)),
                  pl.BlockSpec((tm, D_MODEL), lambda i, e: (i, 0)),
                  pl.BlockSpec((None, 1, 6 * D_MODEL), lambda i, e: (mod_row(i, tm), 0, 0)),
                  pl.BlockSpec((1, D_MODEL), lambda i, e: (0, 0))],
        out_specs=pl.BlockSpec((tm, D_MODEL), lambda i, e: (i, 0)),
        out_shape=jax.ShapeDtypeStruct((n, D_MODEL), f32),
        scratch_shapes=[pltpu.VMEM((tm, D_MODEL), f32)],
        compiler_params=_cparams("parallel", "arbitrary"), name="moe",
    )(h2, gates, wg, wu, wd, x1, mod3, g_post)


def _pad_heads(w, heads):
    rows = w.shape[0]
    w = w.reshape(rows, heads, HEAD_DIM)
    return jnp.pad(w, ((0, 0), (0, 0), (0, HEAD_PAD - HEAD_DIM))).reshape(rows, heads * HEAD_PAD)


def _block_diag2(m):
    z = jnp.zeros_like(m[0])
    return jnp.concatenate([jnp.concatenate([m[0], z], axis=1), jnp.concatenate([z, m[1]], axis=1)], axis=0)


def _prepare_params(w):
    aw, kw = ATTN_HEADS * HEAD_DIM, KV_HEADS * HEAD_DIM
    w_in = w['w_in']
    lane_row = lambda v: v.reshape(1, -1).astype(f32)
    head_id = np.arange(RWKV_WIDTH) // RWKV_HEAD
    seg = jnp.asarray(head_id[:, None] == head_id[None, :], dtype=bf16)
    gain_pad = lambda gvec: jnp.pad(gvec, (0, HEAD_PAD - HEAD_DIM)).reshape(1, HEAD_PAD)
    return dict(
        g_pre_mix=lane_row(w['g_pre_mix']), g_post_mix=lane_row(w['g_post_mix']),
        g_pre_ffn=lane_row(w['g_pre_ffn']), g_post_ffn=lane_row(w['g_post_ffn']),
        wq=_pad_heads(w_in[:, :aw], ATTN_HEADS).astype(bf16),
        wk=_pad_heads(w_in[:, aw:aw + kw], KV_HEADS).astype(bf16),
        wv=_pad_heads(w_in[:, aw + kw:aw + 2 * kw], KV_HEADS).astype(bf16),
        wr=w_in[:, aw + 2 * kw:].astype(bf16),
        qg=gain_pad(w['q_gain'] * ATTN_SCALE), kg=gain_pad(w['k_gain']),
        mu_prev=lane_row(w['mu_prev']), mu_next=lane_row(w['mu_next']),
        w0=lane_row(w['decay_w0']), w2bd=_block_diag2(w['decay_w2']),
        a0=lane_row(w['iclr_a0']), a2bd=_block_diag2(w['iclr_a2']),
        g2=w['gate_g2'], k_k=lane_row(w['k_k']), k_a=lane_row(w['k_a']), r_k=lane_row(w['r_k']),
        lnx_g=lane_row(w['lnx_g']), lnx_b=lane_row(w['lnx_b']), seg=seg,
        w_out=w['w_out'].astype(bf16),
        router_wt=w['router_w'].T, router_b=w['router_b'].reshape(N_EXPERTS, 1),
        wg=jnp.concatenate([w['expert_wg'], w['shared_wg'][None]], axis=0).astype(bf16),
        wu=jnp.concatenate([w['expert_wu'], w['shared_wu'][None]], axis=0).astype(bf16),
        wd=jnp.concatenate([w['expert_wd'], w['shared_wd'][None]], axis=0).astype(bf16),
    )


def _rope_tables(seq_len):
    rows = seq_len // GRID_W
    row_idx = jnp.repeat(jnp.arange(rows, dtype=jnp.int32), GRID_W)
    col_idx = jnp.tile(jnp.arange(GRID_W, dtype=jnp.int32), rows)
    inv = ROPE_THETA ** (-jnp.arange(0, ROPE_HALF, 2, dtype=f32) / ROPE_HALF)
    ang = jnp.stack([row_idx.astype(f32)[:, None] * inv, col_idx.astype(f32)[:, None] * inv], axis=1)
    cos, sin = jnp.cos(ang), jnp.sin(ang)
    cos_l = jnp.stack([cos, cos], axis=2).reshape(seq_len, HEAD_DIM)
    sin_l = jnp.stack([-sin, sin], axis=2).reshape(seq_len, HEAD_DIM)
    pad = ((0, 0), (0, HEAD_PAD - HEAD_DIM))
    return jnp.pad(cos_l, pad), jnp.pad(sin_l, pad)


def _states_to_pairs(s):
    b = s.shape[0]
    s = s.reshape(b, 2, N_PAIRS, 2, RWKV_HEAD, RWKV_HEAD)
    eye = jnp.eye(2, dtype=s.dtype)
    return jnp.einsum('bdphvk,hg->bdphvgk', s, eye).reshape(b, 2, N_PAIRS, PAIR, PAIR)


def _pairs_to_states(s):
    b = s.shape[0]
    s = s.reshape(b, 2, N_PAIRS, 2, RWKV_HEAD, 2, RWKV_HEAD)
    eye = jnp.eye(2, dtype=s.dtype)
    return jnp.einsum('bdphvgk,hg->bdphvk', s, eye).reshape(b, 2, RWKV_HEADS, RWKV_HEAD, RWKV_HEAD)


def _trunk(x, mod3, mod_row, p, rope_tabs, past_k, past_v, s0_pairs, tm_in, tq, tm_prep, tm_out, tm_moe):
    b, t, _ = x.shape
    n = b * t
    x2 = x.reshape(n, D_MODEL)
    q_pad, k_own, v_own, k_pad, v_pad, zr = _in_proj(
        x2, mod3, mod_row, p['g_pre_mix'], p['wq'], p['wk'], p['wv'], p['wr'], p['qg'], p['kg'],
        rope_tabs, t, tm_in)
    k_full = k_pad.reshape(b, t, KV_HEADS * HEAD_PAD)
    v_full = v_pad.reshape(b, t, KV_HEADS * HEAD_PAD)
    if past_k is not None:
        past = lambda c: _pad_heads(c.reshape(-1, KV_HEADS * HEAD_DIM), KV_HEADS).astype(bf16).reshape(
            b, -1, KV_HEADS * HEAD_PAD)
        k_full = jnp.concatenate([past(past_k), k_full], axis=1)
        v_full = jnp.concatenate([past(past_v), v_full], axis=1)
    attn = _attention(q_pad, k_full, v_full, b, t, tq)
    r, k, v, kk, lw, a, g, bonus = _rwkv_prep(zr, p, t, tm_prep)
    o_fw, o_bw, s_fin = _rwkv_scan(r, k, v, kk, lw, a, p['k_a'], s0_pairs, b, t)
    x1, h2, gates = _mix_out(attn, o_fw, o_bw, bonus, g, x2, mod3, mod_row, p, tm_out)
    y = _moe(h2, gates, x1, mod3, mod_row, p['wg'], p['wu'], p['wd'], p['g_post_ffn'], tm_moe)
    return y.reshape(b, t, D_MODEL), k_own, v_own, s_fin


def kernel(x_prompt, x_sample, c, cache_k, cache_v, state_rwkv, c_ctx, w_ada, b_ada, g_pre_mix, g_post_mix,
           g_pre_ffn, g_post_ffn, w_in, w_out, q_gain, k_gain, mu_prev, mu_next, decay_w0, decay_w2, iclr_a0,
           iclr_a2, gate_g2, k_k, k_a, r_k, lnx_g, lnx_b, router_w, router_b, expert_wg, expert_wu, expert_wd,
           shared_wg, shared_wu, shared_wd):
    layer = 0
    names = ('g_pre_mix g_post_mix g_pre_ffn g_post_ffn w_in w_out q_gain k_gain mu_prev mu_next decay_w0 '
             'decay_w2 iclr_a0 iclr_a2 gate_g2 k_k k_a r_k lnx_g lnx_b router_w router_b expert_wg expert_wu '
             'expert_wd shared_wg shared_wu shared_wd').split()
    vals = (g_pre_mix, g_post_mix, g_pre_ffn, g_post_ffn, w_in, w_out, q_gain, k_gain, mu_prev, mu_next,
            decay_w0, decay_w2, iclr_a0, iclr_a2, gate_g2, k_k, k_a, r_k, lnx_g, lnx_b, router_w, router_b,
            expert_wg, expert_wu, expert_wd, shared_wg, shared_wu, shared_wd)
    p = _prepare_params({nm: v[layer] for nm, v in zip(names, vals)})

    nb, ts, _ = x_sample.shape
    npb, tp, _ = x_prompt.shape
    mod_rows = 16
    cc = jnp.zeros((mod_rows, D_MODEL), f32).at[0].set(c_ctx).at[1:1 + nb].set(c)
    mod3 = _modulation(cc, w_ada[layer], b_ada[layer]).reshape(mod_rows, 1, 6 * D_MODEL)

    zeros_state = jnp.zeros((npb, 2, N_PAIRS, PAIR, PAIR), f32)
    y_p, kc, vc, st = _trunk(x_prompt, mod3, lambda i, tm: 0, p, None, None, None, zeros_state,
                             tm_in=256, tq=tp, tm_prep=256, tm_out=256, tm_moe=1024)
    s0_lat = _states_to_pairs(state_rwkv[:, layer])
    y_s, _, _, _ = _trunk(x_sample, mod3, lambda i, tm: 1 + (i * tm) // ts, p, _rope_tables(ts),
                          cache_k[:, layer], cache_v[:, layer], s0_lat,
                          tm_in=512, tq=128, tm_prep=256, tm_out=512, tm_moe=1024)

    new_cache_k = kc.reshape(npb, 1, tp, KV_HEADS, HEAD_DIM)
    new_cache_v = vc.reshape(npb, 1, tp, KV_HEADS, HEAD_DIM)
    new_state = _pairs_to_states(st)[:, None]
    return (y_p, y_s, new_cache_k, new_cache_v, new_state)
```

```python
import functools

import numpy as np
import jax
import jax.numpy as jnp
from jax import lax
from jax.experimental import pallas as pl
from jax.experimental.pallas import tpu as pltpu

f32 = jnp.float32
bf16 = jnp.bfloat16

D_MODEL = 1024
GRID_W = 64
HEAD_DIM = 64
ATTN_HEADS = 8
KV_HEADS = 2
GQA_GROUP = ATTN_HEADS // KV_HEADS
ATTN_SCALE = HEAD_DIM ** -0.5
ROPE_THETA = 10000.0
ROPE_HALF = HEAD_DIM // 2
ROPE_FREQS = ROPE_HALF // 2
RWKV_HEAD = 64
RWKV_HEADS = 8
RWKV_WIDTH = RWKV_HEADS * RWKV_HEAD
DECAY_RANK = 64
ICLR_RANK = 64
GATE_RANK = 128
RWKV_IN = 3 * RWKV_WIDTH + 2 * DECAY_RANK + 2 * ICLR_RANK + GATE_RANK
DECAY_SCALE = 0.606531
GN_EPS = 64e-5
N_EXPERTS = 64
TOP_K = 6
N_GROUPS = 8
GROUP_SIZE = N_EXPERTS // N_GROUPS
TOPK_GROUPS = 4
EXPERT_FF = 256
ROUTED_SCALE = 2.5
EPS = 1e-6

LANE = 128
HEAD_PAD = LANE
CHUNK = 64
PAIR = 2 * RWKV_HEAD
N_PAIRS = RWKV_HEADS // 2
VMEM_LIMIT = 56 * 1024 * 1024
SCAN_PASSES = 1
TM_MOE = 256
SEG_SHIFT = 4
SEG = 1 << SEG_SHIFT
SLOT_ROWS = 2560
FFN_ROWS = 512


def _cparams(*sem):
    return pltpu.CompilerParams(dimension_semantics=sem, vmem_limit_bytes=VMEM_LIMIT)


def _dot(a, b, nt=False):
    dims = (((1,), (1,)), ((), ())) if nt else (((1,), (0,)), ((), ()))
    return lax.dot_general(a, b, dims, preferred_element_type=f32)


def _split2(x):
    hi = x.astype(bf16)
    lo = (x - hi.astype(f32)).astype(bf16)
    return hi, lo


def _split3(x):
    h1 = x.astype(bf16)
    r1 = x - h1.astype(f32)
    h2 = r1.astype(bf16)
    h3 = (r1 - h2.astype(f32)).astype(bf16)
    return h1, h2, h3


def _mm(a, b, nt=False, passes=3):
    if passes == 1:
        return _dot(a.astype(bf16), b.astype(bf16), nt)
    ah, al = _split2(a)
    bh, bl = _split2(b)
    return _dot(ah, bh, nt) + (_dot(al, bh, nt) + _dot(ah, bl, nt))


def _mm_exact_rhs(a, b_bf16, nt=False):
    ah, al = _split2(a)
    return _dot(ah, b_bf16, nt) + _dot(al, b_bf16, nt)


def _rms(x, g):
    return x * lax.rsqrt(jnp.mean(x * x, axis=-1, keepdims=True) + EPS) * g


def _sigmoid(x):
    return jax.nn.sigmoid(x)


def _full_spec(a, grid_rank=1):
    zeros = (0,) * a.ndim
    if grid_rank == 1:
        return pl.BlockSpec(a.shape, lambda i: zeros)
    if grid_rank == 2:
        return pl.BlockSpec(a.shape, lambda i, j: zeros)
    return pl.BlockSpec(a.shape, lambda i, j, k: zeros)


def _mod_kernel(c_ref, w_ref, b_ref, o_ref):
    c = c_ref[...]
    s = c * _sigmoid(c)
    o_ref[...] = _mm(s, w_ref[...]) + b_ref[...]


def _modulation(cc, w_ada, b_ada):
    rows, n = cc.shape[0], w_ada.shape[1]
    tn = 512
    return pl.pallas_call(
        _mod_kernel,
        grid=(n // tn,),
        in_specs=[pl.BlockSpec((rows, D_MODEL), lambda j: (0, 0)),
                  pl.BlockSpec((D_MODEL, tn), lambda j: (0, j)),
                  pl.BlockSpec((1, tn), lambda j: (0, j))],
        out_specs=pl.BlockSpec((rows, tn), lambda j: (0, j)),
        out_shape=jax.ShapeDtypeStruct((rows, n), f32),
        compiler_params=_cparams("parallel"),
        name="modulation",
    )(cc, w_ada, b_ada.reshape(1, n))


def _in_kernel(*refs, rope):
    if rope:
        (x_ref, mod_ref, g_ref, wq_ref, wk_ref, wv_ref, wr_ref, qg_ref, kg_ref, cos_ref, sin_ref,
         q_ref, ko_ref, vo_ref, kp_ref, vp_ref, zr_ref) = refs
    else:
        (x_ref, mod_ref, g_ref, wq_ref, wk_ref, wv_ref, wr_ref, qg_ref, kg_ref,
         q_ref, ko_ref, vo_ref, kp_ref, vp_ref, zr_ref) = refs
    x = x_ref[...]
    m = mod_ref[...]
    h = _rms(x, g_ref[...])
    h = h * (1.0 + m[:, D_MODEL:2 * D_MODEL]) + m[:, 0:D_MODEL]
    hb = h.astype(bf16)
    zr_ref[...] = _dot(hb, wr_ref[...])
    zq = _dot(hb, wq_ref[...])
    zk = _dot(hb, wk_ref[...])
    zv = _dot(hb, wv_ref[...])

    lane = lax.broadcasted_iota(jnp.int32, (1, LANE), 1)
    first_half = (lane & (ROPE_HALF - 1)) < ROPE_FREQS

    def head_norm(z, gain):
        ms = jnp.sum(z * z, axis=-1, keepdims=True) * (1.0 / HEAD_DIM)
        return z * lax.rsqrt(ms + EPS) * gain

    def rotate(y):
        if not rope:
            return y
        partner = jnp.where(first_half, pltpu.roll(y, LANE - ROPE_FREQS, 1), pltpu.roll(y, ROPE_FREQS, 1))
        return y * cos_ref[...] + partner * sin_ref[...]

    for hh in range(ATTN_HEADS):
        sl = slice(hh * HEAD_PAD, (hh + 1) * HEAD_PAD)
        q_ref[:, sl] = rotate(head_norm(zq[:, sl], qg_ref[...])).astype(bf16)
    kn = [head_norm(zk[:, hh * HEAD_PAD:(hh + 1) * HEAD_PAD], kg_ref[...]) for hh in range(KV_HEADS)]
    ko_ref[...] = kn[0] + pltpu.roll(kn[1], HEAD_DIM, 1)
    vo_ref[...] = zv[:, 0:HEAD_PAD] + pltpu.roll(zv[:, HEAD_PAD:2 * HEAD_PAD], HEAD_DIM, 1)
    for hh in range(KV_HEADS):
        sl = slice(hh * HEAD_PAD, (hh + 1) * HEAD_PAD)
        kp_ref[:, sl] = rotate(kn[hh]).astype(bf16)
    vp_ref[...] = zv.astype(bf16)


def _in_proj(x2, mod3, mod_row, g_pre, wq, wk, wv, wr, qg, kg, rope_tabs, seq_len, tm):
    n = x2.shape[0]
    rope = rope_tabs is not None
    in_specs = [pl.BlockSpec((tm, D_MODEL), lambda i: (i, 0)),
                pl.BlockSpec((None, 1, 6 * D_MODEL), lambda i: (mod_row(i, tm), 0, 0)),
                _full_spec(g_pre), _full_spec(wq), _full_spec(wk), _full_spec(wv), _full_spec(wr),
                _full_spec(qg), _full_spec(kg)]
    args = [x2, mod3, g_pre, wq, wk, wv, wr, qg, kg]
    if rope:
        blocks_per_seq = seq_len // tm
        in_specs += [pl.BlockSpec((tm, LANE), lambda i: (i % blocks_per_seq, 0))] * 2
        args += list(rope_tabs)
    out_shape = [jax.ShapeDtypeStruct((n, ATTN_HEADS * HEAD_PAD), bf16),
                 jax.ShapeDtypeStruct((n, KV_HEADS * HEAD_DIM), f32),
                 jax.ShapeDtypeStruct((n, KV_HEADS * HEAD_DIM), f32),
                 jax.ShapeDtypeStruct((n, KV_HEADS * HEAD_PAD), bf16),
                 jax.ShapeDtypeStruct((n, KV_HEADS * HEAD_PAD), bf16),
                 jax.ShapeDtypeStruct((n, RWKV_IN), f32)]
    out_specs = [pl.BlockSpec((tm, s.shape[1]), lambda i: (i, 0)) for s in out_shape]
    return pl.pallas_call(
        functools.partial(_in_kernel, rope=rope),
        grid=(n // tm,), in_specs=in_specs, out_specs=out_specs, out_shape=out_shape,
        compiler_params=_cparams("parallel"), name="in_proj",
    )(*args)


def _attn_kernel(q_ref, k_ref, v_ref, o_ref):
    tq = q_ref.shape[0]
    for kv in range(KV_HEADS):
        kh = k_ref[:, kv * HEAD_PAD:(kv + 1) * HEAD_PAD]
        vh = v_ref[:, kv * HEAD_PAD:(kv + 1) * HEAD_PAD]
        q4 = jnp.concatenate(
            [q_ref[:, (GQA_GROUP * kv + g) * HEAD_PAD:(GQA_GROUP * kv + g + 1) * HEAD_PAD]
             for g in range(GQA_GROUP)], axis=0)
        s = _dot(q4, kh, nt=True)
        mx = jnp.max(s, axis=-1, keepdims=True)
        p = jnp.exp(s - mx)
        den = jnp.sum(p, axis=-1, keepdims=True)
        o = _dot(p.astype(bf16), vh) / den
        for j in range(GQA_GROUP // 2):
            lo = o[(2 * j) * tq:(2 * j + 1) * tq]
            hi = o[(2 * j + 1) * tq:(2 * j + 2) * tq]
            col = (kv * (GQA_GROUP // 2) + j) * LANE
            o_ref[:, col:col + LANE] = lo + pltpu.roll(hi, HEAD_DIM, 1)


def _attention(q_pad, k_full, v_full, batch, seq_len, tq):
    n = q_pad.shape[0]
    kv_len = k_full.shape[1]
    nq = seq_len // tq
    return pl.pallas_call(
        _attn_kernel,
        grid=(batch, nq),
        in_specs=[pl.BlockSpec((tq, ATTN_HEADS * HEAD_PAD), lambda b, j: (b * nq + j, 0)),
                  pl.BlockSpec((None, kv_len, KV_HEADS * HEAD_PAD), lambda b, j: (b, 0, 0)),
                  pl.BlockSpec((None, kv_len, KV_HEADS * HEAD_PAD), lambda b, j: (b, 0, 0))],
        out_specs=pl.BlockSpec((tq, ATTN_HEADS * HEAD_DIM), lambda b, j: (b * nq + j, 0)),
        out_shape=jax.ShapeDtypeStruct((n, ATTN_HEADS * HEAD_DIM), f32),
        compiler_params=_cparams("parallel", "parallel"), name="attention",
    )(q_pad, k_full, v_full)


def _prep_kernel(z_ref, zp_ref, zn_ref, mup_ref, mun_ref, w0_ref, w2_ref, a0_ref, a2_ref, g2_ref,
                 kkw_ref, ka_ref, rk_ref, seg_ref,
                 r_ref, k_ref, v_ref, kk_ref, lw_ref, a_ref, g_ref, bonus_ref, *, seq_len, tm):
    i = pl.program_id(0)
    z = z_ref[...]
    row = lax.broadcasted_iota(jnp.int32, (tm, 1), 0)
    seq_start = ((i * tm) % seq_len) == 0
    seq_end = (((i + 1) * tm) % seq_len) == 0
    prow = jnp.where(seq_start, 0.0, zp_ref[7:8, :])
    nrow = jnp.where(seq_end, 0.0, zn_ref[0:1, :])
    prev = jnp.where(row == 0, prow, pltpu.roll(z, 1, 0))
    nxt = jnp.where(row == tm - 1, nrow, pltpu.roll(z, tm - 1, 0))
    zs = z + mup_ref[...] * (prev - z) + mun_ref[...] * (nxt - z)
    w = RWKV_WIDTH
    r = zs[:, 0:w]
    k = zs[:, w:2 * w]
    v = zs[:, 2 * w:3 * w]
    dw = zs[:, 3 * w:3 * w + 2 * DECAY_RANK]
    da = zs[:, 3 * w + 2 * DECAY_RANK:3 * w + 2 * DECAY_RANK + 2 * ICLR_RANK]
    dg = zs[:, 3 * w + 2 * DECAY_RANK + 2 * ICLR_RANK:]
    lw = -DECAY_SCALE * _sigmoid(w0_ref[...] + _mm(jnp.tanh(dw), w2_ref[...]))
    a = _sigmoid(a0_ref[...] + _mm(da, a2_ref[...]))
    g = _mm(_sigmoid(dg), g2_ref[...])
    kk = k * kkw_ref[...]
    seg = seg_ref[...]
    kk = kk * lax.rsqrt(_mm_exact_rhs(kk * kk, seg) + 1e-12)
    ka = ka_ref[...]
    ke_sum = k * (2.0 + (a[:, 0:w] + a[:, w:2 * w] - 2.0) * ka)
    bonus = _mm_exact_rhs(r * ke_sum * rk_ref[...], seg) * v
    r_ref[...] = r
    k_ref[...] = k
    v_ref[...] = v
    kk_ref[...] = kk
    lw_ref[...] = lw
    a_ref[...] = a
    g_ref[...] = g
    bonus_ref[...] = bonus


def _rwkv_prep(zr, p, seq_len, tm):
    n = zr.shape[0]
    w = RWKV_WIDTH
    nb8 = n // 8
    consts = [p['mu_prev'], p['mu_next'], p['w0'], p['w2bd'], p['a0'], p['a2bd'], p['g2'],
              p['k_k'], p['k_a'], p['r_k'], p['seg']]
    in_specs = [pl.BlockSpec((tm, RWKV_IN), lambda i: (i, 0)),
                pl.BlockSpec((8, RWKV_IN), lambda i: (jnp.maximum(i * (tm // 8) - 1, 0), 0)),
                pl.BlockSpec((8, RWKV_IN), lambda i: (jnp.minimum((i + 1) * (tm // 8), nb8 - 1), 0))]
    in_specs += [_full_spec(c) for c in consts]
    widths = [w, w, w, w, 2 * w, 2 * w, w, w]
    out_shape = [jax.ShapeDtypeStruct((n, wd), f32) for wd in widths]
    out_specs = [pl.BlockSpec((tm, wd), lambda i: (i, 0)) for wd in widths]
    return pl.pallas_call(
        functools.partial(_prep_kernel, seq_len=seq_len, tm=tm),
        grid=(n // tm,), in_specs=in_specs, out_specs=out_specs, out_shape=out_shape,
        compiler_params=_cparams("parallel"), name="rwkv_prep",
    )(zr, zr, zr, *consts)


def _scan_kernel(rf_ref, kf_ref, vf_ref, kkf_ref, lwf_ref, af_ref,
                 rb_ref, kb_ref, vb_ref, kkb_ref, lwb_ref, ab_ref, ka_ref, s0_ref,
                 of_ref, ob_ref, sfin_ref, st_ref):
    c = pl.program_id(1)

    @pl.when(c == 0)
    def _():
        st_ref[...] = s0_ref[...]

    cast = (lambda x: x.astype(bf16)) if SCAN_PASSES == 1 else (lambda x: x)
    mm = functools.partial(_mm, passes=SCAN_PASSES)
    row = lax.broadcasted_iota(jnp.int32, (CHUNK, CHUNK), 0)
    col = lax.broadcasted_iota(jnp.int32, (CHUNK, CHUNK), 1)
    lane = lax.broadcasted_iota(jnp.int32, (1, RWKV_WIDTH), 1)
    m0 = ((lane & (PAIR - 1)) < RWKV_HEAD).astype(f32)
    m1 = 1.0 - m0
    r2 = lax.broadcasted_iota(jnp.int32, (PAIR, PAIR), 0)
    c2 = lax.broadcasted_iota(jnp.int32, (PAIR, PAIR), 1)
    same_head = (r2 & CHUNK) == (c2 & CHUNK)
    eye = jnp.where(r2 == c2, 1.0, 0.0).astype(f32)
    ka = ka_ref[...]
    sl = [slice(p * PAIR, (p + 1) * PAIR) for p in range(N_PAIRS)]

    def stack(x):
        return jnp.concatenate([x * m0, x * m1], axis=0)

    chains = []
    for d, (r_ref, k_ref, v_ref, kk_ref, lw_ref, a_ref) in enumerate(
            ((rf_ref, kf_ref, vf_ref, kkf_ref, lwf_ref, af_ref),
             (rb_ref, kb_ref, vb_ref, kkb_ref, lwb_ref, ab_ref))):
        sgn = 1 - 2 * d
        tri = jnp.where((row - col) * sgn >= 0, 1.0, 0.0).astype(bf16)
        lw = lw_ref[...]
        l1, l2, l3 = _split3(lw)
        cum = _dot(tri, l1) + (_dot(tri, l2) + _dot(tri, l3))
        e_incl = jnp.exp(cum)
        e_excl = jnp.exp(cum - lw)
        e_neg = jnp.exp(-cum)
        wc = jnp.exp(jnp.sum(lw, axis=0, keepdims=True))
        kk = kk_ref[...]
        a = a_ref[...]
        bh_s = stack(kk * a * e_neg)
        kh_s = stack(k_ref[...] * (1.0 + (a - 1.0) * ka) * e_neg)
        v_t = stack(v_ref[...])
        ops = dict(at=cast(stack(-kk * e_excl)), rt=cast(stack(r_ref[...] * e_incl)),
                   bh=cast(bh_s), kh=cast(kh_s), bw=cast(bh_s * wc), kw=cast(kh_s * wc),
                   vt=v_t, vs=cast(v_t), wc=wc)
        dist = jnp.where(same_head, ((r2 & (CHUNK - 1)) - (c2 & (CHUNK - 1))) * sgn, -1)
        for p in range(N_PAIRS):
            ch = {key: val[:, sl[p]] for key, val in ops.items()}
            ch.update(d=d, p=p, strict=dist > 0, incl=dist >= 0, st=st_ref[d, p])
            chains.append(ch)

    for ch in chains:
        ch['stc'] = cast(ch['st'])
        amat = mm(jnp.concatenate([ch['at'], ch['rt']], axis=0),
                  jnp.concatenate([ch['bh'], ch['kh']], axis=0), nt=True)
        ch['n'] = jnp.where(ch['strict'], amat[0:PAIR, 0:PAIR], 0.0)
        ch['a_ak'] = cast(jnp.where(ch['strict'], amat[0:PAIR, PAIR:2 * PAIR], 0.0))
        ch['a_rb'] = cast(jnp.where(ch['incl'], amat[PAIR:2 * PAIR, 0:PAIR], 0.0))
        ch['a_rk'] = cast(jnp.where(ch['incl'], amat[PAIR:2 * PAIR, PAIR:2 * PAIR], 0.0))
        ch['tinv'] = eye + ch['n']
    for _ in range(5):
        for ch in chains:
            ch['n'] = mm(ch['n'], ch['n'])
        for ch in chains:
            ch['tinv'] = ch['tinv'] + mm(ch['tinv'], ch['n'])
    for ch in chains:
        ch['rhs'] = mm(ch['at'], ch['stc'], nt=True) + mm(ch['a_ak'], ch['vs'])
    for ch in chains:
        ch['u'] = mm(ch['tinv'], ch['rhs'])
    for ch in chains:
        ch['obd'] = mm(ch['rt'], ch['stc'], nt=True) + mm(ch['a_rb'], ch['u']) + mm(ch['a_rk'], ch['vs'])
        ch['st_new'] = ch['st'] * ch['wc'] + mm(ch['u'].T, ch['bw']) + mm(ch['vt'].T, ch['kw'])
    for ch in chains:
        o_ref = ob_ref if ch['d'] else of_ref
        o_ref[:, sl[ch['p']]] = ch['obd'][0:CHUNK] + ch['obd'][CHUNK:2 * CHUNK]
        st_ref[ch['d'], ch['p']] = ch['st_new']

    @pl.when(c == pl.num_programs(1) - 1)
    def _():
        sfin_ref[...] = st_ref[...]


def _rwkv_scan(r, k, v, kk, lw, a, k_a, s0, batch, seq_len):
    n = r.shape[0]
    w = RWKV_WIDTH
    nc = seq_len // CHUNK
    fwd = lambda b, c: (b * nc + c, 0)
    bwd = lambda b, c: (b * nc + nc - 1 - c, 0)
    bwd_dir = lambda b, c: (b * nc + nc - 1 - c, 1)
    blk = lambda index_map: pl.BlockSpec((CHUNK, w), index_map)
    state_spec = pl.BlockSpec((None, 2, N_PAIRS, PAIR, PAIR), lambda b, c: (b, 0, 0, 0, 0))
    return pl.pallas_call(
        _scan_kernel,
        grid=(batch, nc),
        in_specs=[blk(fwd)] * 6 + [blk(bwd)] * 4 + [blk(bwd_dir)] * 2
                 + [pl.BlockSpec((1, w), lambda b, c: (0, 0)), state_spec],
        out_specs=[blk(fwd), blk(bwd), state_spec],
        out_shape=[jax.ShapeDtypeStruct((n, w), f32), jax.ShapeDtypeStruct((n, w), f32),
                   jax.ShapeDtypeStruct((batch, 2, N_PAIRS, PAIR, PAIR), f32)],
        scratch_shapes=[pltpu.VMEM((2, N_PAIRS, PAIR, PAIR), f32)],
        compiler_params=_cparams("parallel", "arbitrary"), name="rwkv_scan",
    )(r, k, v, kk, lw, a, r, k, v, kk, lw, a, k_a, s0)


def _out_kernel(attn_ref, of_ref, ob_ref, bonus_ref, g_ref, x_ref, mod_ref, lng_ref, lnb_ref, seg_ref,
                wo_ref, gpm_ref, gpf_ref, rwt_ref, rb_ref,
                x1_ref, h2_ref, slot_ref, sw_ref, cnt_ref):
    tm = x_ref.shape[0]
    seg = seg_ref[...]
    inv = 1.0 / RWKV_HEAD
    o = of_ref[...] + ob_ref[...]
    mu = _mm_exact_rhs(o, seg) * inv
    dl = o - mu
    var = _mm_exact_rhs(dl * dl, seg) * inv
    on = dl * lax.rsqrt(var + GN_EPS) * lng_ref[...] + lnb_ref[...]
    rw = (on + bonus_ref[...]) * g_ref[...]
    half = ATTN_HEADS * HEAD_DIM
    y = _dot(attn_ref[...].astype(bf16), wo_ref[0:half, :]) + _dot(rw.astype(bf16), wo_ref[half:, :])
    m = mod_ref[...]
    gt_a = m[:, 2 * D_MODEL:3 * D_MODEL]
    sh_f = m[:, 3 * D_MODEL:4 * D_MODEL]
    sc_f = m[:, 4 * D_MODEL:5 * D_MODEL]
    x1 = x_ref[...] + gt_a * _rms(y, gpm_ref[...])
    x1_ref[...] = x1
    h2 = _rms(x1, gpf_ref[...]) * (1.0 + sc_f) + sh_f
    h2_ref[...] = h2.astype(bf16)

    logits = _mm(rwt_ref[...], h2, nt=True)
    scores = _sigmoid(logits)
    biased = scores + rb_ref[...]
    neg = -jnp.inf
    shape3 = (N_GROUPS, GROUP_SIZE, tm)
    b3 = biased.reshape(shape3)
    s3 = scores.reshape(shape3)
    in_grp = lax.broadcasted_iota(jnp.int32, shape3, 1)
    grp = lax.broadcasted_iota(jnp.int32, shape3, 0)
    m1 = jnp.max(b3, axis=1, keepdims=True)
    i1 = jnp.min(jnp.where(b3 == m1, in_grp, GROUP_SIZE), axis=1, keepdims=True)
    m2 = jnp.max(jnp.where(in_grp == i1, neg, b3), axis=1, keepdims=True)
    gscore = m1 + m2
    gidx = lax.broadcasted_iota(jnp.int32, (N_GROUPS, 1, tm), 0)
    gsel = jnp.zeros((N_GROUPS, 1, tm), f32)
    cur = gscore
    for _ in range(TOPK_GROUPS):
        mx = jnp.max(cur, axis=0, keepdims=True)
        ii = jnp.min(jnp.where(cur == mx, gidx, N_GROUPS), axis=0, keepdims=True)
        hit = gidx == ii
        gsel = jnp.where(hit, 1.0, gsel)
        cur = jnp.where(hit, neg, cur)
    cand = jnp.where(jnp.broadcast_to(gsel, shape3) > 0.0, b3, neg)
    eidx = grp * GROUP_SIZE + in_grp
    wsel = jnp.zeros(shape3, f32)
    mask = jnp.zeros(shape3, f32)
    hits = []
    for _ in range(TOP_K):
        mx = jnp.max(jnp.max(cand, axis=1, keepdims=True), axis=0, keepdims=True)
        ii = jnp.min(jnp.min(jnp.where(cand == mx, eidx, N_EXPERTS), axis=1, keepdims=True),
                     axis=0, keepdims=True)
        hit = eidx == ii
        hits.append(hit)
        wsel = jnp.where(hit, s3, wsel)
        mask = jnp.where(hit, 1.0, mask)
        cand = jnp.where(hit, neg, cand)
    den = jnp.sum(jnp.sum(wsel, axis=1, keepdims=True), axis=0, keepdims=True)
    gates3 = wsel / den * ROUTED_SCALE

    mask2 = mask.reshape(N_EXPERTS, tm)
    cnt = jnp.sum(mask2, axis=1, keepdims=True)
    cnt_pad = jnp.floor((cnt + (SEG - 1)) * (1.0 / SEG)) * SEG
    er = lax.broadcasted_iota(jnp.int32, (N_EXPERTS, N_EXPERTS), 0)
    ec = lax.broadcasted_iota(jnp.int32, (N_EXPERTS, N_EXPERTS), 1)
    before = jnp.where(ec < er, 1.0, 0.0).astype(bf16)
    cnt_pad_l = jnp.broadcast_to(cnt_pad, (N_EXPERTS, LANE))
    seg_start = _dot(before, cnt_pad_l.astype(bf16))
    tr = lax.broadcasted_iota(jnp.int32, (tm, tm), 0)
    tc = lax.broadcasted_iota(jnp.int32, (tm, tm), 1)
    earlier = jnp.where(tr < tc, 1.0, 0.0).astype(bf16)
    rank = _dot(mask2.astype(bf16), earlier)
    slot3 = (rank + seg_start[:, 0:1]).reshape(shape3)

    def pick(hit, val3):
        return jnp.sum(jnp.sum(jnp.where(hit, val3, 0.0), axis=1, keepdims=True), axis=0)

    slots = [pick(h, slot3) for h in hits]
    wts = [pick(h, gates3) for h in hits]
    pad_rows = lambda k, val: [jnp.full((k, tm), val, f32)]
    slot_ref[...] = jnp.concatenate(slots + pad_rows(8 - TOP_K, -1.0), axis=0).astype(jnp.int32)
    sw_t = jnp.concatenate(slots + pad_rows(8 - TOP_K, -1.0) + wts + pad_rows(LANE - 8 - TOP_K, 0.0), axis=0)
    sw_ref[...] = sw_t.T
    cnt_ref[...] = cnt_pad_l


def _mix_out(attn, o_fw, o_bw, bonus, g, x2, mod3, mod_row, p, tm):
    n = x2.shape[0]
    w = RWKV_WIDTH
    consts = [p['lnx_g'], p['lnx_b'], p['seg'], p['w_out'], p['g_post_mix'], p['g_pre_ffn'],
              p['router_wt'], p['router_b']]
    row = lambda wd: pl.BlockSpec((tm, wd), lambda i: (i, 0))
    in_specs = [row(ATTN_HEADS * HEAD_DIM), row(w), row(w),
                row(w), row(w), row(D_MODEL),
                pl.BlockSpec((None, 1, 6 * D_MODEL), lambda i: (mod_row(i, tm), 0, 0))]
    in_specs += [_full_spec(c) for c in consts]
    out_shape = [jax.ShapeDtypeStruct((n, D_MODEL), f32),
                 jax.ShapeDtypeStruct((n, D_MODEL), bf16),
                 jax.ShapeDtypeStruct((8, n), jnp.int32),
                 jax.ShapeDtypeStruct((n, LANE), f32),
                 jax.ShapeDtypeStruct((n // tm, N_EXPERTS, LANE), f32)]
    out_specs = [row(D_MODEL), row(D_MODEL), pl.BlockSpec((8, tm), lambda i: (0, i)), row(LANE),
                 pl.BlockSpec((None, N_EXPERTS, LANE), lambda i: (i, 0, 0))]
    return pl.pallas_call(
        _out_kernel, grid=(n // tm,), in_specs=in_specs, out_specs=out_specs, out_shape=out_shape,
        compiler_params=_cparams("parallel"), name="mix_out",
    )(attn, o_fw, o_bw, bonus, g, x2, mod3, *consts)


def _segment_copies(tile, base_sm, cnt_sm, make_copy, sem_wait_copy):
    def per_expert(e, loc):
        n = cnt_sm[tile * N_EXPERTS + e]
        dst = base_sm[tile * N_EXPERTS + e]

        def per_chunk(c, carry):
            off = c * SEG
            make_copy(pl.multiple_of(loc + off, SEG), pl.multiple_of(dst + off, SEG)).start()
            return carry

        lax.fori_loop(0, lax.shift_right_logical(n, SEG_SHIFT), per_chunk, 0)
        return loc + n

    total = lax.fori_loop(0, N_EXPERTS, per_expert, 0)

    def drain(c, carry):
        sem_wait_copy().wait()
        return carry

    lax.fori_loop(0, lax.shift_right_logical(total, SEG_SHIFT), drain, 0)
    return total


def _dispatch_kernel(base_sm, cnt_sm, fill_sm, h_ref, slot_ref, xs_hbm, xc_ref, zero_ref, sem, zsem):
    i = pl.program_id(0)

    @pl.when(i == 0)
    def _():
        zero_ref[...] = jnp.zeros_like(zero_ref)
        n_tiles = xs_hbm.shape[0] // FFN_ROWS

        def tail_copy(row):
            return pltpu.make_async_copy(zero_ref.at[pl.ds(0, SEG)], xs_hbm.at[pl.ds(row, SEG)], zsem)

        def tile_copy(t):
            return pltpu.make_async_copy(zero_ref, xs_hbm.at[pl.ds(pl.multiple_of(t * FFN_ROWS, FFN_ROWS), FFN_ROWS)],
                                         zsem)

        def per_expert(e, carry):
            start = fill_sm[1 + e]
            chunks = lax.shift_right_logical(fill_sm[1 + N_EXPERTS + e], SEG_SHIFT)
            lax.fori_loop(0, chunks, lambda c, z: (tail_copy(pl.multiple_of(start + c * SEG, SEG)).start(), z)[1], 0)
            lax.fori_loop(0, chunks, lambda c, z: (tail_copy(0).wait(), z)[1], 0)
            return carry

        lax.fori_loop(0, N_EXPERTS, per_expert, 0)
        lax.fori_loop(fill_sm[0], n_tiles, lambda t, z: (tile_copy(t).start(), z)[1], 0)
        lax.fori_loop(fill_sm[0], n_tiles, lambda t, z: (tile_copy(0).wait(), z)[1], 0)

    slot = slot_ref[...]
    h = h_ref[...]
    blk = 512
    for rb in range(SLOT_ROWS // blk):
        rows = lax.broadcasted_iota(jnp.int32, (blk, TM_MOE), 0) + rb * blk
        sel = jnp.zeros((blk, TM_MOE), f32)
        for j in range(TOP_K):
            sel = jnp.where(rows == slot[j:j + 1, :], 1.0, sel)
        xc_ref[rb * blk:(rb + 1) * blk, :] = _dot(sel.astype(bf16), h).astype(bf16)

    def copy(local_row, global_row):
        return pltpu.make_async_copy(xc_ref.at[pl.ds(local_row, SEG)], xs_hbm.at[pl.ds(global_row, SEG)], sem)

    _segment_copies(i, base_sm, cnt_sm, copy, lambda: copy(0, 0))


def _dispatch(h2, slots, base, cnt, fill, total_rows):
    n = h2.shape[0]
    grid_spec = pltpu.PrefetchScalarGridSpec(
        num_scalar_prefetch=3, grid=(n // TM_MOE,),
        in_specs=[pl.BlockSpec((TM_MOE, D_MODEL), lambda i, b, c, f: (i, 0)),
                  pl.BlockSpec((8, TM_MOE), lambda i, b, c, f: (0, i))],
        out_specs=pl.BlockSpec(memory_space=pl.ANY),
        scratch_shapes=[pltpu.VMEM((SLOT_ROWS, D_MODEL), bf16), pltpu.VMEM((FFN_ROWS, D_MODEL), bf16),
                        pltpu.SemaphoreType.DMA(()), pltpu.SemaphoreType.DMA(())])
    return pl.pallas_call(
        _dispatch_kernel, grid_spec=grid_spec,
        out_shape=jax.ShapeDtypeStruct((total_rows, D_MODEL), bf16),
        compiler_params=_cparams("arbitrary"), name="moe_dispatch",
    )(base, cnt, fill, h2, slots)


def _ffn_kernel(te_sm, nv_sm, xs_ref, wg_ref, wu_ref, wd_ref, ys_ref):
    r = pl.program_id(0)

    @pl.when(r < nv_sm[0])
    def _():
        x = xs_ref[...]
        gg = _dot(x, wg_ref[...].astype(bf16))
        uu = _dot(x, wu_ref[...].astype(bf16))
        hm = gg * _sigmoid(gg) * uu
        ys_ref[...] = _dot(hm.astype(bf16), wd_ref[...].astype(bf16)).astype(bf16)

    @pl.when(r >= nv_sm[0])
    def _():
        ys_ref[...] = jnp.zeros_like(ys_ref)


def _expert_ffn(xs, tile_expert, n_valid, wg, wu, wd):
    total_rows = xs.shape[0]
    last = lambda r, nv: jnp.minimum(r, nv[0] - 1)
    grid_spec = pltpu.PrefetchScalarGridSpec(
        num_scalar_prefetch=2, grid=(total_rows // FFN_ROWS,),
        in_specs=[pl.BlockSpec((FFN_ROWS, D_MODEL), lambda r, te, nv: (last(r, nv), 0)),
                  pl.BlockSpec((None, D_MODEL, EXPERT_FF), lambda r, te, nv: (te[last(r, nv)], 0, 0)),
                  pl.BlockSpec((None, D_MODEL, EXPERT_FF), lambda r, te, nv: (te[last(r, nv)], 0, 0)),
                  pl.BlockSpec((None, EXPERT_FF, D_MODEL), lambda r, te, nv: (te[last(r, nv)], 0, 0))],
        out_specs=pl.BlockSpec((FFN_ROWS, D_MODEL), lambda r, te, nv: (r, 0)))
    return pl.pallas_call(
        _ffn_kernel, grid_spec=grid_spec,
        out_shape=jax.ShapeDtypeStruct((total_rows, D_MODEL), bf16),
        compiler_params=_cparams("arbitrary"), name="moe_ffn",
    )(tile_expert, n_valid, xs, wg, wu, wd)


def _combine_kernel(base_sm, cnt_sm, sw_ref, h_ref, x1_ref, mod_ref, gpost_ref, swg_ref, swu_ref, swd_ref,
                    ys_hbm, y_ref, yc_ref, sem, *, tile_offset):
    i = pl.program_id(0) + tile_offset

    def copy(local_row, global_row):
        return pltpu.make_async_copy(ys_hbm.at[pl.ds(global_row, SEG)], yc_ref.at[pl.ds(local_row, SEG)], sem)

    used = _segment_copies(i, base_sm, cnt_sm, copy, lambda: copy(0, 0))
    h = h_ref[...]
    gg = _dot(h, swg_ref[...])
    uu = _dot(h, swu_ref[...])
    moe = _dot((gg * _sigmoid(gg) * uu).astype(bf16), swd_ref[...])
    sw = sw_ref[...]
    blk = 512
    for rb in range(SLOT_ROWS // blk):
        cols = lax.broadcasted_iota(jnp.int32, (TM_MOE, blk), 1) + rb * blk
        colsf = cols.astype(f32)
        wmat = jnp.zeros((TM_MOE, blk), f32)
        for j in range(TOP_K):
            wmat = jnp.where(colsf == sw[:, j:j + 1], sw[:, 8 + j:9 + j], wmat)
        rows = lax.broadcasted_iota(jnp.int32, (blk, 1), 0) + rb * blk
        yc = jnp.where(rows < used, yc_ref[rb * blk:(rb + 1) * blk, :], jnp.zeros((), bf16))
        moe = moe + _mm_exact_rhs(wmat, yc)
    gt_f = mod_ref[...][:, 5 * D_MODEL:6 * D_MODEL]
    y_ref[...] = x1_ref[...] + gt_f * _rms(moe, gpost_ref[...])


def _combine(ys, base, cnt, sw, h2, x1, mod3, mod_row, g_post, swg, swu, swd, tile_offset):
    n = h2.shape[0]
    row = lambda wd: pl.BlockSpec((TM_MOE, wd), lambda i, b, c: (i, 0))
    const = lambda a: pl.BlockSpec(a.shape, lambda i, b, c: (0,) * a.ndim)
    grid_spec = pltpu.PrefetchScalarGridSpec(
        num_scalar_prefetch=2, grid=(n // TM_MOE,),
        in_specs=[row(LANE), row(D_MODEL), row(D_MODEL),
                  pl.BlockSpec((None, 1, 6 * D_MODEL), lambda i, b, c: (mod_row(i, TM_MOE), 0, 0)),
                  const(g_post), const(swg), const(swu), const(swd),
                  pl.BlockSpec(memory_space=pl.ANY)],
        out_specs=row(D_MODEL),
        scratch_shapes=[pltpu.VMEM((SLOT_ROWS, D_MODEL), bf16), pltpu.SemaphoreType.DMA(())])
    return pl.pallas_call(
        functools.partial(_combine_kernel, tile_offset=tile_offset), grid_spec=grid_spec,
        out_shape=jax.ShapeDtypeStruct((n, D_MODEL), f32),
        compiler_params=_cparams("arbitrary"), name="moe_combine",
    )(base, cnt, sw, h2, x1, mod3, g_post, swg, swu, swd, ys)


def _sparse_moe(paths, mod3, p):
    cnt = jnp.concatenate([q['cnt'][:, :, 0] for q in paths], axis=0).astype(jnp.int32)
    n_tiles = cnt.shape[0]
    before = jnp.cumsum(cnt, axis=0) - cnt
    rows_e = jnp.sum(cnt, axis=0)
    region = (rows_e + FFN_ROWS - 1) // FFN_ROWS * FFN_ROWS
    region_end = jnp.cumsum(region)
    region_start = region_end - region
    base = (region_start[None, :] + before).reshape(-1)
    worst = n_tiles * (TOP_K * TM_MOE + N_EXPERTS * (SEG - 1)) + N_EXPERTS * (FFN_ROWS - SEG)
    total_rows = -(-worst // FFN_ROWS) * FFN_ROWS
    first_row = jnp.arange(total_rows // FFN_ROWS, dtype=jnp.int32) * FFN_ROWS
    ended = jnp.sum((region_end[None, :] <= first_row[:, None]).astype(jnp.int32), axis=1)
    tile_expert = jnp.minimum(ended, N_EXPERTS - 1).astype(jnp.int32)
    n_valid = (region_end[-1:] // FFN_ROWS).astype(jnp.int32)

    h2_all = jnp.concatenate([q['h2'] for q in paths], axis=0)
    slots_all = jnp.concatenate([q['slots'] for q in paths], axis=1)
    fill = jnp.concatenate([n_valid, region_start + rows_e, region - rows_e]).astype(jnp.int32)
    xs = _dispatch(h2_all, slots_all, base, cnt.reshape(-1), fill, total_rows)
    ys = _expert_ffn(xs, tile_expert, n_valid, p['expert_wg'], p['expert_wu'], p['expert_wd'])
    outs, tile_offset = [], 0
    for q in paths:
        outs.append(_combine(ys, base, cnt.reshape(-1), q['sw'], q['h2'], q['x1'], mod3, q['mod_row'],
                             p['g_post_ffn'], p['shared_wg'], p['shared_wu'], p['shared_wd'], tile_offset))
        tile_offset += q['h2'].shape[0] // TM_MOE
    return outs


def _pad_heads(w, heads):
    rows = w.shape[0]
    w = w.reshape(rows, heads, HEAD_DIM)
    return jnp.pad(w, ((0, 0), (0, 0), (0, HEAD_PAD - HEAD_DIM))).reshape(rows, heads * HEAD_PAD)


def _block_diag2(m):
    z = jnp.zeros_like(m[0])
    return jnp.concatenate([jnp.concatenate([m[0], z], axis=1), jnp.concatenate([z, m[1]], axis=1)], axis=0)


def _prepare_params(w):
    aw, kw = ATTN_HEADS * HEAD_DIM, KV_HEADS * HEAD_DIM
    w_in = w['w_in']
    lane_row = lambda v: v.reshape(1, -1).astype(f32)
    head_id = np.arange(RWKV_WIDTH) // RWKV_HEAD
    seg = jnp.asarray(head_id[:, None] == head_id[None, :], dtype=bf16)
    gain_pad = lambda gvec: jnp.pad(gvec, (0, HEAD_PAD - HEAD_DIM)).reshape(1, HEAD_PAD)
    return dict(
        g_pre_mix=lane_row(w['g_pre_mix']), g_post_mix=lane_row(w['g_post_mix']),
        g_pre_ffn=lane_row(w['g_pre_ffn']), g_post_ffn=lane_row(w['g_post_ffn']),
        wq=_pad_heads(w_in[:, :aw], ATTN_HEADS).astype(bf16),
        wk=_pad_heads(w_in[:, aw:aw + kw], KV_HEADS).astype(bf16),
        wv=_pad_heads(w_in[:, aw + kw:aw + 2 * kw], KV_HEADS).astype(bf16),
        wr=w_in[:, aw + 2 * kw:].astype(bf16),
        qg=gain_pad(w['q_gain'] * ATTN_SCALE), kg=gain_pad(w['k_gain']),
        mu_prev=lane_row(w['mu_prev']), mu_next=lane_row(w['mu_next']),
        w0=lane_row(w['decay_w0']), w2bd=_block_diag2(w['decay_w2']),
        a0=lane_row(w['iclr_a0']), a2bd=_block_diag2(w['iclr_a2']),
        g2=w['gate_g2'], k_k=lane_row(w['k_k']), k_a=lane_row(w['k_a']), r_k=lane_row(w['r_k']),
        lnx_g=lane_row(w['lnx_g']), lnx_b=lane_row(w['lnx_b']), seg=seg,
        w_out=w['w_out'].astype(bf16),
        router_wt=w['router_w'].T, router_b=w['router_b'].reshape(N_EXPERTS, 1),
        expert_wg=w['expert_wg'], expert_wu=w['expert_wu'], expert_wd=w['expert_wd'],
        shared_wg=w['shared_wg'].astype(bf16), shared_wu=w['shared_wu'].astype(bf16),
        shared_wd=w['shared_wd'].astype(bf16),
    )


def _rope_tables(seq_len):
    rows = seq_len // GRID_W
    row_idx = jnp.repeat(jnp.arange(rows, dtype=jnp.int32), GRID_W)
    col_idx = jnp.tile(jnp.arange(GRID_W, dtype=jnp.int32), rows)
    inv = ROPE_THETA ** (-jnp.arange(0, ROPE_HALF, 2, dtype=f32) / ROPE_HALF)
    ang = jnp.stack([row_idx.astype(f32)[:, None] * inv, col_idx.astype(f32)[:, None] * inv], axis=1)
    cos, sin = jnp.cos(ang), jnp.sin(ang)
    cos_l = jnp.stack([cos, cos], axis=2).reshape(seq_len, HEAD_DIM)
    sin_l = jnp.stack([-sin, sin], axis=2).reshape(seq_len, HEAD_DIM)
    pad = ((0, 0), (0, HEAD_PAD - HEAD_DIM))
    return jnp.pad(cos_l, pad), jnp.pad(sin_l, pad)


def _states_to_pairs(s):
    b = s.shape[0]
    s = s.reshape(b, 2, N_PAIRS, 2, RWKV_HEAD, RWKV_HEAD)
    eye = jnp.eye(2, dtype=s.dtype)
    return jnp.einsum('bdphvk,hg->bdphvgk', s, eye).reshape(b, 2, N_PAIRS, PAIR, PAIR)


def _pairs_to_states(s):
    b = s.shape[0]
    s = s.reshape(b, 2, N_PAIRS, 2, RWKV_HEAD, 2, RWKV_HEAD)
    eye = jnp.eye(2, dtype=s.dtype)
    return jnp.einsum('bdphvgk,hg->bdphvk', s, eye).reshape(b, 2, RWKV_HEADS, RWKV_HEAD, RWKV_HEAD)


def _trunk(x, mod3, mod_row, p, rope_tabs, past_k, past_v, s0_pairs, tm_in, tq, tm_prep):
    b, t, _ = x.shape
    n = b * t
    x2 = x.reshape(n, D_MODEL)
    q_pad, k_own, v_own, k_pad, v_pad, zr = _in_proj(
        x2, mod3, mod_row, p['g_pre_mix'], p['wq'], p['wk'], p['wv'], p['wr'], p['qg'], p['kg'],
        rope_tabs, t, tm_in)
    k_full = k_pad.reshape(b, t, KV_HEADS * HEAD_PAD)
    v_full = v_pad.reshape(b, t, KV_HEADS * HEAD_PAD)
    if past_k is not None:
        past = lambda c: _pad_heads(c.reshape(-1, KV_HEADS * HEAD_DIM), KV_HEADS).astype(bf16).reshape(
            b, -1, KV_HEADS * HEAD_PAD)
        k_full = jnp.concatenate([past(past_k), k_full], axis=1)
        v_full = jnp.concatenate([past(past_v), v_full], axis=1)
    attn = _attention(q_pad, k_full, v_full, b, t, tq)
    r, k, v, kk, lw, a, g, bonus = _rwkv_prep(zr, p, t, tm_prep)
    o_fw, o_bw, s_fin = _rwkv_scan(r, k, v, kk, lw, a, p['k_a'], s0_pairs, b, t)
    x1, h2, slots, sw, cnt = _mix_out(attn, o_fw, o_bw, bonus, g, x2, mod3, mod_row, p, TM_MOE)
    return dict(h2=h2, slots=slots, sw=sw, cnt=cnt, x1=x1, mod_row=mod_row), k_own, v_own, s_fin


def kernel(x_prompt, x_sample, c, cache_k, cache_v, state_rwkv, c_ctx, w_ada, b_ada, g_pre_mix, g_post_mix,
           g_pre_ffn, g_post_ffn, w_in, w_out, q_gain, k_gain, mu_prev, mu_next, decay_w0, decay_w2, iclr_a0,
           iclr_a2, gate_g2, k_k, k_a, r_k, lnx_g, lnx_b, router_w, router_b, expert_wg, expert_wu, expert_wd,
           shared_wg, shared_wu, shared_wd):
    layer = 0
    names = ('g_pre_mix g_post_mix g_pre_ffn g_post_ffn w_in w_out q_gain k_gain mu_prev mu_next decay_w0 '
             'decay_w2 iclr_a0 iclr_a2 gate_g2 k_k k_a r_k lnx_g lnx_b router_w router_b expert_wg expert_wu '
             'expert_wd shared_wg shared_wu shared_wd').split()
    vals = (g_pre_mix, g_post_mix, g_pre_ffn, g_post_ffn, w_in, w_out, q_gain, k_gain, mu_prev, mu_next,
            decay_w0, decay_w2, iclr_a0, iclr_a2, gate_g2, k_k, k_a, r_k, lnx_g, lnx_b, router_w, router_b,
            expert_wg, expert_wu, expert_wd, shared_wg, shared_wu, shared_wd)
    p = _prepare_params({nm: v[layer] for nm, v in zip(names, vals)})

    nb, ts, _ = x_sample.shape
    npb, tp, _ = x_prompt.shape
    mod_rows = 16
    cc = jnp.zeros((mod_rows, D_MODEL), f32).at[0].set(c_ctx).at[1:1 + nb].set(c)
    mod3 = _modulation(cc, w_ada[layer], b_ada[layer]).reshape(mod_rows, 1, 6 * D_MODEL)

    zeros_state = jnp.zeros((npb, 2, N_PAIRS, PAIR, PAIR), f32)
    moe_p, kc, vc, st = _trunk(x_prompt, mod3, lambda i, tm: 0, p, None, None, None, zeros_state,
                               tm_in=256, tq=tp, tm_prep=256)
    s0_lat = _states_to_pairs(state_rwkv[:, layer])
    moe_s, _, _, _ = _trunk(x_sample, mod3, lambda i, tm: 1 + (i * tm) // ts, p, _rope_tables(ts),
                            cache_k[:, layer], cache_v[:, layer], s0_lat,
                            tm_in=512, tq=128, tm_prep=256)
    y_p, y_s = _sparse_moe([moe_p, moe_s], mod3, p)
    y_p = y_p.reshape(x_prompt.shape)
    y_s = y_s.reshape(x_sample.shape)

    new_cache_k = kc.reshape(npb, 1, tp, KV_HEADS, HEAD_DIM)
    new_cache_v = vc.reshape(npb, 1, tp, KV_HEADS, HEAD_DIM)
    new_state = _pairs_to_states(st)[:, None]
    return (y_p, y_s, new_cache_k, new_cache_v, new_state)
```

```python
import functools

import numpy as np
import jax
import jax.numpy as jnp
from jax import lax
from jax.experimental import pallas as pl
from jax.experimental.pallas import tpu as pltpu

f32 = jnp.float32
bf16 = jnp.bfloat16

D_MODEL = 1024
GRID_W = 64
HEAD_DIM = 64
ATTN_HEADS = 8
KV_HEADS = 2
GQA_GROUP = ATTN_HEADS // KV_HEADS
ATTN_SCALE = HEAD_DIM ** -0.5
ROPE_THETA = 10000.0
ROPE_HALF = HEAD_DIM // 2
ROPE_FREQS = ROPE_HALF // 2
RWKV_HEAD = 64
RWKV_HEADS = 8
RWKV_WIDTH = RWKV_HEADS * RWKV_HEAD
DECAY_RANK = 64
ICLR_RANK = 64
GATE_RANK = 128
RWKV_IN = 3 * RWKV_WIDTH + 2 * DECAY_RANK + 2 * ICLR_RANK + GATE_RANK
DECAY_SCALE = 0.606531
GN_EPS = 64e-5
N_EXPERTS = 64
TOP_K = 6
N_GROUPS = 8
GROUP_SIZE = N_EXPERTS // N_GROUPS
TOPK_GROUPS = 4
EXPERT_FF = 256
ROUTED_SCALE = 2.5
EPS = 1e-6

LANE = 128
HEAD_PAD = LANE
CHUNK = 64
PAIR = 2 * RWKV_HEAD
N_PAIRS = RWKV_HEADS // 2
VMEM_LIMIT = 56 * 1024 * 1024
SCAN_PASSES = 1
TM_MOE = 256
SEG_SHIFT = 4
SEG = 1 << SEG_SHIFT
SLOT_ROWS = 2560
MAX_CHUNKS = SLOT_ROWS // SEG
CHUNK_GROUP_SHIFT = 3
CHUNK_GROUP = 1 << CHUNK_GROUP_SHIFT
SEL_BLOCK = 512
COMMON_BLOCKS = 4
FFN_ROWS = 512


def _cparams(*sem):
    return pltpu.CompilerParams(dimension_semantics=sem, vmem_limit_bytes=VMEM_LIMIT)


def _dot(a, b, nt=False):
    dims = (((1,), (1,)), ((), ())) if nt else (((1,), (0,)), ((), ()))
    return lax.dot_general(a, b, dims, preferred_element_type=f32)


def _split2(x):
    hi = x.astype(bf16)
    lo = (x - hi.astype(f32)).astype(bf16)
    return hi, lo


def _split3(x):
    h1 = x.astype(bf16)
    r1 = x - h1.astype(f32)
    h2 = r1.astype(bf16)
    h3 = (r1 - h2.astype(f32)).astype(bf16)
    return h1, h2, h3


def _mm(a, b, nt=False, passes=3):
    if passes == 1:
        return _dot(a.astype(bf16), b.astype(bf16), nt)
    ah, al = _split2(a)
    bh, bl = _split2(b)
    return _dot(ah, bh, nt) + (_dot(al, bh, nt) + _dot(ah, bl, nt))


def _mm_exact_rhs(a, b_bf16, nt=False):
    ah, al = _split2(a)
    return _dot(ah, b_bf16, nt) + _dot(al, b_bf16, nt)


def _rms(x, g):
    return x * lax.rsqrt(jnp.mean(x * x, axis=-1, keepdims=True) + EPS) * g


def _sigmoid(x):
    return jax.nn.sigmoid(x)


def _full_spec(a, grid_rank=1):
    zeros = (0,) * a.ndim
    if grid_rank == 1:
        return pl.BlockSpec(a.shape, lambda i: zeros)
    if grid_rank == 2:
        return pl.BlockSpec(a.shape, lambda i, j: zeros)
    return pl.BlockSpec(a.shape, lambda i, j, k: zeros)


def _mod_kernel(c_ref, w_ref, b_ref, o_ref):
    c = c_ref[...]
    s = c * _sigmoid(c)
    o_ref[...] = _mm(s, w_ref[...]) + b_ref[...]


def _modulation(cc, w_ada, b_ada):
    rows, n = cc.shape[0], w_ada.shape[1]
    tn = 512
    return pl.pallas_call(
        _mod_kernel,
        grid=(n // tn,),
        in_specs=[pl.BlockSpec((rows, D_MODEL), lambda j: (0, 0)),
                  pl.BlockSpec((D_MODEL, tn), lambda j: (0, j)),
                  pl.BlockSpec((1, tn), lambda j: (0, j))],
        out_specs=pl.BlockSpec((rows, tn), lambda j: (0, j)),
        out_shape=jax.ShapeDtypeStruct((rows, n), f32),
        compiler_params=_cparams("parallel"),
        name="modulation",
    )(cc, w_ada, b_ada.reshape(1, n))


def _in_kernel(*refs, rope):
    if rope:
        (x_ref, mod_ref, g_ref, wq_ref, wk_ref, wv_ref, wr_ref, qg_ref, kg_ref, cos_ref, sin_ref,
         q_ref, ko_ref, vo_ref, kp_ref, zr_ref) = refs
    else:
        (x_ref, mod_ref, g_ref, wq_ref, wk_ref, wv_ref, wr_ref, qg_ref, kg_ref,
         q_ref, ko_ref, vo_ref, kp_ref, zr_ref) = refs
    x = x_ref[...]
    m = mod_ref[...]
    h = _rms(x, g_ref[...])
    h = h * (1.0 + m[:, D_MODEL:2 * D_MODEL]) + m[:, 0:D_MODEL]
    hb = h.astype(bf16)
    zr_ref[...] = _dot(hb, wr_ref[...])
    zq = _dot(hb, wq_ref[...])
    zk = _dot(hb, wk_ref[...])
    zv = _dot(hb, wv_ref[...])

    lane = lax.broadcasted_iota(jnp.int32, (1, LANE), 1)
    first_half = (lane & (ROPE_HALF - 1)) < ROPE_FREQS

    def head_norm(z, gain):
        ms = jnp.sum(z * z, axis=-1, keepdims=True) * (1.0 / HEAD_DIM)
        return z * lax.rsqrt(ms + EPS) * gain

    def rotate(y):
        if not rope:
            return y
        partner = jnp.where(first_half, pltpu.roll(y, LANE - ROPE_FREQS, 1), pltpu.roll(y, ROPE_FREQS, 1))
        return y * cos_ref[...] + partner * sin_ref[...]

    for hh in range(ATTN_HEADS):
        sl = slice(hh * HEAD_PAD, (hh + 1) * HEAD_PAD)
        q_ref[:, sl] = rotate(head_norm(zq[:, sl], qg_ref[...])).astype(bf16)
    kn = [head_norm(zk[:, hh * HEAD_PAD:(hh + 1) * HEAD_PAD], kg_ref[...]) for hh in range(KV_HEADS)]
    ko_ref[...] = kn[0] + pltpu.roll(kn[1], HEAD_DIM, 1)
    vo_ref[...] = zv
    for hh in range(KV_HEADS):
        sl = slice(hh * HEAD_PAD, (hh + 1) * HEAD_PAD)
        kp_ref[:, sl] = rotate(kn[hh]).astype(bf16)


def _in_proj(x2, mod3, mod_row, g_pre, wq, wk, wv, wr, qg, kg, rope_tabs, seq_len, tm):
    n = x2.shape[0]
    rope = rope_tabs is not None
    in_specs = [pl.BlockSpec((tm, D_MODEL), lambda i: (i, 0)),
                pl.BlockSpec((None, 1, 6 * D_MODEL), lambda i: (mod_row(i, tm), 0, 0)),
                _full_spec(g_pre), _full_spec(wq), _full_spec(wk), _full_spec(wv), _full_spec(wr),
                _full_spec(qg), _full_spec(kg)]
    args = [x2, mod3, g_pre, wq, wk, wv, wr, qg, kg]
    if rope:
        blocks_per_seq = seq_len // tm
        in_specs += [pl.BlockSpec((tm, LANE), lambda i: (i % blocks_per_seq, 0))] * 2
        args += list(rope_tabs)
    out_shape = [jax.ShapeDtypeStruct((n, ATTN_HEADS * HEAD_PAD), bf16),
                 jax.ShapeDtypeStruct((n, KV_HEADS * HEAD_DIM), f32),
                 jax.ShapeDtypeStruct((n, KV_HEADS * HEAD_DIM), f32),
                 jax.ShapeDtypeStruct((n, KV_HEADS * HEAD_PAD), bf16),
                 jax.ShapeDtypeStruct((n, RWKV_IN), f32)]
    out_specs = [pl.BlockSpec((tm, s.shape[1]), lambda i: (i, 0)) for s in out_shape]
    return pl.pallas_call(
        functools.partial(_in_kernel, rope=rope),
        grid=(n // tm,), in_specs=in_specs, out_specs=out_specs, out_shape=out_shape,
        compiler_params=_cparams("parallel"), name="in_proj",
    )(*args)


def _attn_kernel(q_ref, k_ref, vt_ref, o_ref):
    tq = q_ref.shape[0]
    heads_t = []
    for kv in range(KV_HEADS):
        kh = k_ref[:, kv * HEAD_PAD:(kv + 1) * HEAD_PAD]
        vt = vt_ref[kv * HEAD_DIM:(kv + 1) * HEAD_DIM, :]
        q4 = jnp.concatenate(
            [q_ref[:, (GQA_GROUP * kv + g) * HEAD_PAD:(GQA_GROUP * kv + g + 1) * HEAD_PAD]
             for g in range(GQA_GROUP)], axis=0)
        st = _dot(kh, q4, nt=True)
        mx = jnp.max(st, axis=0, keepdims=True)
        p = jnp.exp(st - mx)
        den = jnp.sum(p, axis=0, keepdims=True)
        ot = _dot(vt, p.astype(bf16)) / den
        heads_t += [ot[:, g * tq:(g + 1) * tq] for g in range(GQA_GROUP)]
    o_ref[...] = jnp.concatenate(heads_t, axis=0).T


def _attention(q_pad, k_full, vt_full, batch, seq_len, tq):
    n = q_pad.shape[0]
    kv_len = k_full.shape[1]
    nq = seq_len // tq
    return pl.pallas_call(
        _attn_kernel,
        grid=(batch, nq),
        in_specs=[pl.BlockSpec((tq, ATTN_HEADS * HEAD_PAD), lambda b, j: (b * nq + j, 0)),
                  pl.BlockSpec((None, kv_len, KV_HEADS * HEAD_PAD), lambda b, j: (b, 0, 0)),
                  pl.BlockSpec((None, KV_HEADS * HEAD_DIM, kv_len), lambda b, j: (b, 0, 0))],
        out_specs=pl.BlockSpec((tq, ATTN_HEADS * HEAD_DIM), lambda b, j: (b * nq + j, 0)),
        out_shape=jax.ShapeDtypeStruct((n, ATTN_HEADS * HEAD_DIM), f32),
        compiler_params=_cparams("parallel", "parallel"), name="attention",
    )(q_pad, k_full, vt_full)


def _prep_kernel(z_ref, zp_ref, zn_ref, mup_ref, mun_ref, w0_ref, w2_ref, a0_ref, a2_ref, g2_ref,
                 kkw_ref, ka_ref, rk_ref, seg_ref,
                 r_ref, k_ref, v_ref, kk_ref, lw_ref, a_ref, g_ref, bonus_ref, *, seq_len, tm):
    i = pl.program_id(0)
    z = z_ref[...]
    row = lax.broadcasted_iota(jnp.int32, (tm, 1), 0)
    seq_start = ((i * tm) % seq_len) == 0
    seq_end = (((i + 1) * tm) % seq_len) == 0
    prow = jnp.where(seq_start, 0.0, zp_ref[7:8, :])
    nrow = jnp.where(seq_end, 0.0, zn_ref[0:1, :])
    prev = jnp.where(row == 0, prow, pltpu.roll(z, 1, 0))
    nxt = jnp.where(row == tm - 1, nrow, pltpu.roll(z, tm - 1, 0))
    zs = z + mup_ref[...] * (prev - z) + mun_ref[...] * (nxt - z)
    w = RWKV_WIDTH
    r = zs[:, 0:w]
    k = zs[:, w:2 * w]
    v = zs[:, 2 * w:3 * w]
    dw = zs[:, 3 * w:3 * w + 2 * DECAY_RANK]
    da = zs[:, 3 * w + 2 * DECAY_RANK:3 * w + 2 * DECAY_RANK + 2 * ICLR_RANK]
    dg = zs[:, 3 * w + 2 * DECAY_RANK + 2 * ICLR_RANK:]
    lw = -DECAY_SCALE * _sigmoid(w0_ref[...] + _mm(jnp.tanh(dw), w2_ref[...]))
    a = _sigmoid(a0_ref[...] + _mm(da, a2_ref[...]))
    g = _mm(_sigmoid(dg), g2_ref[...])
    kk = k * kkw_ref[...]
    seg = seg_ref[...]
    kk = kk * lax.rsqrt(_mm_exact_rhs(kk * kk, seg) + 1e-12)
    ka = ka_ref[...]
    ke_sum = k * (2.0 + (a[:, 0:w] + a[:, w:2 * w] - 2.0) * ka)
    bonus = _mm_exact_rhs(r * ke_sum * rk_ref[...], seg) * v
    r_ref[...] = r
    k_ref[...] = k
    v_ref[...] = v
    kk_ref[...] = kk
    lw_ref[...] = lw
    a_ref[...] = a
    g_ref[...] = g
    bonus_ref[...] = bonus


def _rwkv_prep(zr, p, seq_len, tm):
    n = zr.shape[0]
    w = RWKV_WIDTH
    nb8 = n // 8
    consts = [p['mu_prev'], p['mu_next'], p['w0'], p['w2bd'], p['a0'], p['a2bd'], p['g2'],
              p['k_k'], p['k_a'], p['r_k'], p['seg']]
    in_specs = [pl.BlockSpec((tm, RWKV_IN), lambda i: (i, 0)),
                pl.BlockSpec((8, RWKV_IN), lambda i: (jnp.maximum(i * (tm // 8) - 1, 0), 0)),
                pl.BlockSpec((8, RWKV_IN), lambda i: (jnp.minimum((i + 1) * (tm // 8), nb8 - 1), 0))]
    in_specs += [_full_spec(c) for c in consts]
    widths = [w, w, w, w, 2 * w, 2 * w, w, w]
    out_shape = [jax.ShapeDtypeStruct((n, wd), f32) for wd in widths]
    out_specs = [pl.BlockSpec((tm, wd), lambda i: (i, 0)) for wd in widths]
    return pl.pallas_call(
        functools.partial(_prep_kernel, seq_len=seq_len, tm=tm),
        grid=(n // tm,), in_specs=in_specs, out_specs=out_specs, out_shape=out_shape,
        compiler_params=_cparams("parallel"), name="rwkv_prep",
    )(zr, zr, zr, *consts)


def _scan_kernel(rf_ref, kf_ref, vf_ref, kkf_ref, lwf_ref, af_ref,
                 rb_ref, kb_ref, vb_ref, kkb_ref, lwb_ref, ab_ref, ka_ref, s0_ref,
                 of_ref, ob_ref, sfin_ref, st_ref):
    c = pl.program_id(1)

    @pl.when(c == 0)
    def _():
        st_ref[...] = s0_ref[...]

    cast = (lambda x: x.astype(bf16)) if SCAN_PASSES == 1 else (lambda x: x)
    mm = functools.partial(_mm, passes=SCAN_PASSES)
    row = lax.broadcasted_iota(jnp.int32, (CHUNK, CHUNK), 0)
    col = lax.broadcasted_iota(jnp.int32, (CHUNK, CHUNK), 1)
    lane = lax.broadcasted_iota(jnp.int32, (1, RWKV_WIDTH), 1)
    m0 = ((lane & (PAIR - 1)) < RWKV_HEAD).astype(f32)
    m1 = 1.0 - m0
    r2 = lax.broadcasted_iota(jnp.int32, (PAIR, PAIR), 0)
    c2 = lax.broadcasted_iota(jnp.int32, (PAIR, PAIR), 1)
    same_head = (r2 & CHUNK) == (c2 & CHUNK)
    eye = jnp.where(r2 == c2, 1.0, 0.0).astype(f32)
    ka = ka_ref[...]
    sl = [slice(p * PAIR, (p + 1) * PAIR) for p in range(N_PAIRS)]

    def stack(x):
        return jnp.concatenate([x * m0, x * m1], axis=0)

    chains = []
    for d, (r_ref, k_ref, v_ref, kk_ref, lw_ref, a_ref) in enumerate(
            ((rf_ref, kf_ref, vf_ref, kkf_ref, lwf_ref, af_ref),
             (rb_ref, kb_ref, vb_ref, kkb_ref, lwb_ref, ab_ref))):
        sgn = 1 - 2 * d
        tri = jnp.where((row - col) * sgn >= 0, 1.0, 0.0).astype(bf16)
        lw = lw_ref[...]
        l1, l2, l3 = _split3(lw)
        cum = _dot(tri, l1) + (_dot(tri, l2) + _dot(tri, l3))
        e_incl = jnp.exp(cum)
        e_excl = jnp.exp(cum - lw)
        e_neg = jnp.exp(-cum)
        wc = jnp.exp(jnp.sum(lw, axis=0, keepdims=True))
        kk = kk_ref[...]
        a = a_ref[...]
        bh_s = stack(kk * a * e_neg)
        kh_s = stack(k_ref[...] * (1.0 + (a - 1.0) * ka) * e_neg)
        v_t = stack(v_ref[...])
        ops = dict(at=cast(stack(-kk * e_excl)), rt=cast(stack(r_ref[...] * e_incl)),
                   bh=cast(bh_s), kh=cast(kh_s), bw=cast(bh_s * wc), kw=cast(kh_s * wc),
                   vt=v_t, vs=cast(v_t), wc=wc)
        dist = jnp.where(same_head, ((r2 & (CHUNK - 1)) - (c2 & (CHUNK - 1))) * sgn, -1)
        for p in range(N_PAIRS):
            ch = {key: val[:, sl[p]] for key, val in ops.items()}
            ch.update(d=d, p=p, strict=dist > 0, incl=dist >= 0, st=st_ref[d, p])
            chains.append(ch)

    for ch in chains:
        ch['stc'] = cast(ch['st'])
        amat = mm(jnp.concatenate([ch['at'], ch['rt']], axis=0),
                  jnp.concatenate([ch['bh'], ch['kh']], axis=0), nt=True)
        ch['n'] = jnp.where(ch['strict'], amat[0:PAIR, 0:PAIR], 0.0)
        ch['a_ak'] = cast(jnp.where(ch['strict'], amat[0:PAIR, PAIR:2 * PAIR], 0.0))
        ch['a_rb'] = cast(jnp.where(ch['incl'], amat[PAIR:2 * PAIR, 0:PAIR], 0.0))
        ch['a_rk'] = cast(jnp.where(ch['incl'], amat[PAIR:2 * PAIR, PAIR:2 * PAIR], 0.0))
        ch['tinv'] = eye + ch['n']
    for _ in range(5):
        for ch in chains:
            ch['n'] = mm(ch['n'], ch['n'])
        for ch in chains:
            ch['tinv'] = ch['tinv'] + mm(ch['tinv'], ch['n'])
    for ch in chains:
        ch['rhs'] = mm(ch['at'], ch['stc'], nt=True) + mm(ch['a_ak'], ch['vs'])
    for ch in chains:
        ch['u'] = mm(ch['tinv'], ch['rhs'])
    for ch in chains:
        ch['obd'] = mm(ch['rt'], ch['stc'], nt=True) + mm(ch['a_rb'], ch['u']) + mm(ch['a_rk'], ch['vs'])
        ch['st_new'] = ch['st'] * ch['wc'] + mm(ch['u'].T, ch['bw']) + mm(ch['vt'].T, ch['kw'])
    for ch in chains:
        o_ref = ob_ref if ch['d'] else of_ref
        o_ref[:, sl[ch['p']]] = ch['obd'][0:CHUNK] + ch['obd'][CHUNK:2 * CHUNK]
        st_ref[ch['d'], ch['p']] = ch['st_new']

    @pl.when(c == pl.num_programs(1) - 1)
    def _():
        sfin_ref[...] = st_ref[...]


def _rwkv_scan(r, k, v, kk, lw, a, k_a, s0, batch, seq_len):
    n = r.shape[0]
    w = RWKV_WIDTH
    nc = seq_len // CHUNK
    fwd = lambda b, c: (b * nc + c, 0)
    bwd = lambda b, c: (b * nc + nc - 1 - c, 0)
    bwd_dir = lambda b, c: (b * nc + nc - 1 - c, 1)
    blk = lambda index_map: pl.BlockSpec((CHUNK, w), index_map)
    state_spec = pl.BlockSpec((None, 2, N_PAIRS, PAIR, PAIR), lambda b, c: (b, 0, 0, 0, 0))
    return pl.pallas_call(
        _scan_kernel,
        grid=(batch, nc),
        in_specs=[blk(fwd)] * 6 + [blk(bwd)] * 4 + [blk(bwd_dir)] * 2
                 + [pl.BlockSpec((1, w), lambda b, c: (0, 0)), state_spec],
        out_specs=[blk(fwd), blk(bwd), state_spec],
        out_shape=[jax.ShapeDtypeStruct((n, w), f32), jax.ShapeDtypeStruct((n, w), f32),
                   jax.ShapeDtypeStruct((batch, 2, N_PAIRS, PAIR, PAIR), f32)],
        scratch_shapes=[pltpu.VMEM((2, N_PAIRS, PAIR, PAIR), f32)],
        compiler_params=_cparams("parallel", "arbitrary"), name="rwkv_scan",
    )(r, k, v, kk, lw, a, r, k, v, kk, lw, a, k_a, s0)


def _out_kernel(attn_ref, of_ref, ob_ref, bonus_ref, g_ref, x_ref, mod_ref, lng_ref, lnb_ref, seg_ref,
                wo_ref, gpm_ref, gpf_ref, rwt_ref, rb_ref,
                x1_ref, h2_ref, slot_ref, sw_ref, cnt_ref):
    tm = x_ref.shape[0]
    seg = seg_ref[...]
    inv = 1.0 / RWKV_HEAD
    o = of_ref[...] + ob_ref[...]
    mu = _mm_exact_rhs(o, seg) * inv
    dl = o - mu
    var = _mm_exact_rhs(dl * dl, seg) * inv
    on = dl * lax.rsqrt(var + GN_EPS) * lng_ref[...] + lnb_ref[...]
    rw = (on + bonus_ref[...]) * g_ref[...]
    half = ATTN_HEADS * HEAD_DIM
    y = _dot(attn_ref[...].astype(bf16), wo_ref[0:half, :]) + _dot(rw.astype(bf16), wo_ref[half:, :])
    m = mod_ref[...]
    gt_a = m[:, 2 * D_MODEL:3 * D_MODEL]
    sh_f = m[:, 3 * D_MODEL:4 * D_MODEL]
    sc_f = m[:, 4 * D_MODEL:5 * D_MODEL]
    x1 = x_ref[...] + gt_a * _rms(y, gpm_ref[...])
    x1_ref[...] = x1
    h2 = _rms(x1, gpf_ref[...]) * (1.0 + sc_f) + sh_f
    h2_ref[...] = h2.astype(bf16)

    logits = _mm(rwt_ref[...], h2, nt=True)
    scores = _sigmoid(logits)
    biased = scores + rb_ref[...]
    neg = -jnp.inf
    shape3 = (N_GROUPS, GROUP_SIZE, tm)
    b3 = biased.reshape(shape3)
    s3 = scores.reshape(shape3)
    in_grp = lax.broadcasted_iota(jnp.int32, shape3, 1)
    grp = lax.broadcasted_iota(jnp.int32, shape3, 0)
    m1 = jnp.max(b3, axis=1, keepdims=True)
    i1 = jnp.min(jnp.where(b3 == m1, in_grp, GROUP_SIZE), axis=1, keepdims=True)
    m2 = jnp.max(jnp.where(in_grp == i1, neg, b3), axis=1, keepdims=True)
    gscore = m1 + m2
    gidx = lax.broadcasted_iota(jnp.int32, (N_GROUPS, 1, tm), 0)
    gsel = jnp.zeros((N_GROUPS, 1, tm), f32)
    cur = gscore
    for _ in range(TOPK_GROUPS):
        mx = jnp.max(cur, axis=0, keepdims=True)
        ii = jnp.min(jnp.where(cur == mx, gidx, N_GROUPS), axis=0, keepdims=True)
        hit = gidx == ii
        gsel = jnp.where(hit, 1.0, gsel)
        cur = jnp.where(hit, neg, cur)
    cand = jnp.where(jnp.broadcast_to(gsel, shape3) > 0.0, b3, neg)
    eidx = grp * GROUP_SIZE + in_grp
    wsel = jnp.zeros(shape3, f32)
    mask = jnp.zeros(shape3, f32)
    hits = []
    for _ in range(TOP_K):
        mx = jnp.max(jnp.max(cand, axis=1, keepdims=True), axis=0, keepdims=True)
        ii = jnp.min(jnp.min(jnp.where(cand == mx, eidx, N_EXPERTS), axis=1, keepdims=True),
                     axis=0, keepdims=True)
        hit = eidx == ii
        hits.append(hit)
        wsel = jnp.where(hit, s3, wsel)
        mask = jnp.where(hit, 1.0, mask)
        cand = jnp.where(hit, neg, cand)
    den = jnp.sum(jnp.sum(wsel, axis=1, keepdims=True), axis=0, keepdims=True)
    gates3 = wsel / den * ROUTED_SCALE

    mask2 = mask.reshape(N_EXPERTS, tm)
    cnt = jnp.sum(mask2, axis=1, keepdims=True)
    cnt_pad = jnp.floor((cnt + (SEG - 1)) * (1.0 / SEG)) * SEG
    er = lax.broadcasted_iota(jnp.int32, (N_EXPERTS, N_EXPERTS), 0)
    ec = lax.broadcasted_iota(jnp.int32, (N_EXPERTS, N_EXPERTS), 1)
    before = jnp.where(ec < er, 1.0, 0.0).astype(bf16)
    cnt_pad_l = jnp.broadcast_to(cnt_pad, (N_EXPERTS, LANE))
    seg_start = _dot(before, cnt_pad_l.astype(bf16))
    tr = lax.broadcasted_iota(jnp.int32, (tm, tm), 0)
    tc = lax.broadcasted_iota(jnp.int32, (tm, tm), 1)
    earlier = jnp.where(tr < tc, 1.0, 0.0).astype(bf16)
    rank = _dot(mask2.astype(bf16), earlier)
    slot3 = (rank + seg_start[:, 0:1]).reshape(shape3)

    def pick(hit, val3):
        return jnp.sum(jnp.sum(jnp.where(hit, val3, 0.0), axis=1, keepdims=True), axis=0)

    slots = [pick(h, slot3) for h in hits]
    wts = [pick(h, gates3) for h in hits]
    pad_rows = lambda k, val: [jnp.full((k, tm), val, f32)]
    slot_ref[...] = jnp.concatenate(slots + pad_rows(8 - TOP_K, -1.0), axis=0).astype(jnp.int32)
    sw_t = jnp.concatenate(slots + pad_rows(8 - TOP_K, -1.0) + wts + pad_rows(LANE - 8 - TOP_K, 0.0), axis=0)
    sw_ref[...] = sw_t.T
    cnt_ref[...] = cnt_pad_l


def _mix_out(attn, o_fw, o_bw, bonus, g, x2, mod3, mod_row, p, tm):
    n = x2.shape[0]
    w = RWKV_WIDTH
    consts = [p['lnx_g'], p['lnx_b'], p['seg'], p['w_out'], p['g_post_mix'], p['g_pre_ffn'],
              p['router_wt'], p['router_b']]
    row = lambda wd: pl.BlockSpec((tm, wd), lambda i: (i, 0))
    in_specs = [row(ATTN_HEADS * HEAD_DIM), row(w), row(w),
                row(w), row(w), row(D_MODEL),
                pl.BlockSpec((None, 1, 6 * D_MODEL), lambda i: (mod_row(i, tm), 0, 0))]
    in_specs += [_full_spec(c) for c in consts]
    out_shape = [jax.ShapeDtypeStruct((n, D_MODEL), f32),
                 jax.ShapeDtypeStruct((n, D_MODEL), bf16),
                 jax.ShapeDtypeStruct((8, n), jnp.int32),
                 jax.ShapeDtypeStruct((n, LANE), f32),
                 jax.ShapeDtypeStruct((n // tm, N_EXPERTS, LANE), f32)]
    out_specs = [row(D_MODEL), row(D_MODEL), pl.BlockSpec((8, tm), lambda i: (0, i)), row(LANE),
                 pl.BlockSpec((None, N_EXPERTS, LANE), lambda i: (i, 0, 0))]
    return pl.pallas_call(
        _out_kernel, grid=(n // tm,), in_specs=in_specs, out_specs=out_specs, out_shape=out_shape,
        compiler_params=_cparams("parallel"), name="mix_out",
    )(attn, o_fw, o_bw, bonus, g, x2, mod3, *consts)


def _chunk_groups(tile, nch_sm):
    return lax.shift_right_logical(nch_sm[tile] + (CHUNK_GROUP - 1), CHUNK_GROUP_SHIFT)


def _start_chunks(tile, dst_sm, nch_sm, make_copy):
    def body(g, carry):
        for u in range(CHUNK_GROUP):
            c = g * CHUNK_GROUP + u
            make_copy(c, pl.multiple_of(dst_sm[tile * MAX_CHUNKS + c], SEG)).start()
        return carry

    lax.fori_loop(0, _chunk_groups(tile, nch_sm), body, 0)


def _wait_chunks(tile, nch_sm, make_copy):
    def body(g, carry):
        for _ in range(CHUNK_GROUP):
            make_copy(0, 0).wait()
        return carry

    lax.fori_loop(0, _chunk_groups(tile, nch_sm), body, 0)


def _dispatch_kernel(dst_sm, nch_sm, fill_sm, h_ref, slot_ref, xs_hbm, xc_ref, zero_ref, sems, zsem):
    i = pl.program_id(0)
    last = pl.num_programs(0) - 1
    buf = lax.rem(i, 2)

    def copy_from(b):
        def make(chunk, global_row):
            src = xc_ref.at[b, pl.ds(pl.multiple_of(chunk * SEG, SEG), SEG)]
            return pltpu.make_async_copy(src, xs_hbm.at[pl.ds(global_row, SEG)], sems.at[b])
        return make

    @pl.when(i == 0)
    def _():
        zero_ref[...] = jnp.zeros_like(zero_ref)
        n_tiles = xs_hbm.shape[0] // FFN_ROWS

        def tail_copy(row):
            return pltpu.make_async_copy(zero_ref.at[pl.ds(0, SEG)], xs_hbm.at[pl.ds(row, SEG)], zsem)

        def tile_copy(t):
            return pltpu.make_async_copy(zero_ref, xs_hbm.at[pl.ds(pl.multiple_of(t * FFN_ROWS, FFN_ROWS), FFN_ROWS)],
                                         zsem)

        def per_expert(e, carry):
            start = fill_sm[1 + e]
            chunks = lax.shift_right_logical(fill_sm[1 + N_EXPERTS + e], SEG_SHIFT)
            lax.fori_loop(0, chunks, lambda c, z: (tail_copy(pl.multiple_of(start + c * SEG, SEG)).start(), z)[1], 0)
            lax.fori_loop(0, chunks, lambda c, z: (tail_copy(0).wait(), z)[1], 0)
            return carry

        lax.fori_loop(0, N_EXPERTS, per_expert, 0)
        lax.fori_loop(fill_sm[0], n_tiles, lambda t, z: (tile_copy(t).start(), z)[1], 0)
        lax.fori_loop(fill_sm[0], n_tiles, lambda t, z: (tile_copy(0).wait(), z)[1], 0)

    @pl.when(i >= 2)
    def _():
        _wait_chunks(i - 2, nch_sm, copy_from(buf))

    slot = slot_ref[...]
    h = h_ref[...]
    used = nch_sm[i] * SEG
    def build(rb):
        rows = lax.broadcasted_iota(jnp.int32, (SEL_BLOCK, TM_MOE), 0) + rb * SEL_BLOCK
        sel = jnp.zeros((SEL_BLOCK, TM_MOE), f32)
        for j in range(TOP_K):
            sel = jnp.where(rows == slot[j:j + 1, :], 1.0, sel)
        xc_ref[buf, rb * SEL_BLOCK:(rb + 1) * SEL_BLOCK, :] = _dot(sel.astype(bf16), h).astype(bf16)

    for rb in range(COMMON_BLOCKS):
        build(rb)
    for rb in range(COMMON_BLOCKS, SLOT_ROWS // SEL_BLOCK):
        pl.when(rb * SEL_BLOCK < used)(functools.partial(build, rb))

    _start_chunks(i, dst_sm, nch_sm, copy_from(buf))

    @pl.when(i == last)
    def _():
        @pl.when(i >= 1)
        def _():
            _wait_chunks(i - 1, nch_sm, copy_from(1 - buf))
        _wait_chunks(i, nch_sm, copy_from(buf))


def _dispatch(h2, slots, dst, nch, fill, total_rows):
    n = h2.shape[0]
    grid_spec = pltpu.PrefetchScalarGridSpec(
        num_scalar_prefetch=3, grid=(n // TM_MOE,),
        in_specs=[pl.BlockSpec((TM_MOE, D_MODEL), lambda i, d, c, f: (i, 0)),
                  pl.BlockSpec((8, TM_MOE), lambda i, d, c, f: (0, i))],
        out_specs=pl.BlockSpec(memory_space=pl.ANY),
        scratch_shapes=[pltpu.VMEM((2, SLOT_ROWS, D_MODEL), bf16), pltpu.VMEM((FFN_ROWS, D_MODEL), bf16),
                        pltpu.SemaphoreType.DMA((2,)), pltpu.SemaphoreType.DMA(())])
    return pl.pallas_call(
        _dispatch_kernel, grid_spec=grid_spec,
        out_shape=jax.ShapeDtypeStruct((total_rows, D_MODEL), bf16),
        compiler_params=_cparams("arbitrary"), name="moe_dispatch",
    )(dst, nch, fill, h2, slots)


def _ffn_kernel(te_sm, nv_sm, xs_ref, wg_ref, wu_ref, wd_ref, ys_ref):
    r = pl.program_id(0)

    @pl.when(r < nv_sm[0])
    def _():
        x = xs_ref[...]
        gg = _dot(x, wg_ref[...].astype(bf16))
        uu = _dot(x, wu_ref[...].astype(bf16))
        hm = gg * _sigmoid(gg) * uu
        ys_ref[...] = _dot(hm.astype(bf16), wd_ref[...].astype(bf16)).astype(bf16)

    @pl.when(r >= nv_sm[0])
    def _():
        ys_ref[...] = jnp.zeros_like(ys_ref)


def _expert_ffn(xs, tile_expert, n_valid, wg, wu, wd):
    total_rows = xs.shape[0]
    last = lambda r, nv: jnp.minimum(r, nv[0] - 1)
    grid_spec = pltpu.PrefetchScalarGridSpec(
        num_scalar_prefetch=2, grid=(total_rows // FFN_ROWS,),
        in_specs=[pl.BlockSpec((FFN_ROWS, D_MODEL), lambda r, te, nv: (last(r, nv), 0)),
                  pl.BlockSpec((None, D_MODEL, EXPERT_FF), lambda r, te, nv: (te[last(r, nv)], 0, 0)),
                  pl.BlockSpec((None, D_MODEL, EXPERT_FF), lambda r, te, nv: (te[last(r, nv)], 0, 0)),
                  pl.BlockSpec((None, EXPERT_FF, D_MODEL), lambda r, te, nv: (te[last(r, nv)], 0, 0))],
        out_specs=pl.BlockSpec((FFN_ROWS, D_MODEL), lambda r, te, nv: (r, 0)))
    return pl.pallas_call(
        _ffn_kernel, grid_spec=grid_spec,
        out_shape=jax.ShapeDtypeStruct((total_rows, D_MODEL), bf16),
        compiler_params=_cparams("arbitrary"), name="moe_ffn",
    )(tile_expert, n_valid, xs, wg, wu, wd)


def _combine_kernel(dst_sm, nch_sm, sw_ref, h_ref, x1_ref, mod_ref, gpost_ref, swg_ref, swu_ref, swd_ref,
                    ys_hbm, y_ref, yc_ref, acc_ref, sems, *, tile_offset):
    step = pl.program_id(0)
    i = step + tile_offset
    buf = lax.rem(step, 2)

    def copy_into(b):
        def make(chunk, global_row):
            dst = yc_ref.at[b, pl.ds(pl.multiple_of(chunk * SEG, SEG), SEG)]
            return pltpu.make_async_copy(ys_hbm.at[pl.ds(global_row, SEG)], dst, sems.at[b])
        return make

    @pl.when(step == 0)
    def _():
        _start_chunks(i, dst_sm, nch_sm, copy_into(buf))

    @pl.when(step + 1 < pl.num_programs(0))
    def _():
        _start_chunks(i + 1, dst_sm, nch_sm, copy_into(1 - buf))

    h = h_ref[...]
    gg = _dot(h, swg_ref[...])
    uu = _dot(h, swu_ref[...])
    acc_ref[...] = _dot((gg * _sigmoid(gg) * uu).astype(bf16), swd_ref[...])
    _wait_chunks(i, nch_sm, copy_into(buf))
    sw = sw_ref[...]
    used = nch_sm[i] * SEG
    def weighted(rb):
        cols = (lax.broadcasted_iota(jnp.int32, (TM_MOE, SEL_BLOCK), 1) + rb * SEL_BLOCK).astype(f32)
        wmat = jnp.zeros((TM_MOE, SEL_BLOCK), f32)
        for j in range(TOP_K):
            wmat = jnp.where(cols == sw[:, j:j + 1], sw[:, 8 + j:9 + j], wmat)
        rows = lax.broadcasted_iota(jnp.int32, (SEL_BLOCK, 1), 0) + rb * SEL_BLOCK
        yc = jnp.where(rows < used, yc_ref[buf, rb * SEL_BLOCK:(rb + 1) * SEL_BLOCK, :], jnp.zeros((), bf16))
        return _dot(wmat.astype(bf16), yc)

    moe = acc_ref[...]
    for rb in range(COMMON_BLOCKS):
        moe = moe + weighted(rb)
    acc_ref[...] = moe
    for rb in range(COMMON_BLOCKS, SLOT_ROWS // SEL_BLOCK):
        @pl.when(rb * SEL_BLOCK < used)
        def _():
            acc_ref[...] += weighted(rb)
    gt_f = mod_ref[...][:, 5 * D_MODEL:6 * D_MODEL]
    y_ref[...] = x1_ref[...] + gt_f * _rms(acc_ref[...], gpost_ref[...])


def _combine(ys, dst, nch, sw, h2, x1, mod3, mod_row, g_post, swg, swu, swd, tile_offset):
    n = h2.shape[0]
    row = lambda wd: pl.BlockSpec((TM_MOE, wd), lambda i, d, c: (i, 0))
    const = lambda a: pl.BlockSpec(a.shape, lambda i, d, c: (0,) * a.ndim)
    grid_spec = pltpu.PrefetchScalarGridSpec(
        num_scalar_prefetch=2, grid=(n // TM_MOE,),
        in_specs=[row(LANE), row(D_MODEL), row(D_MODEL),
                  pl.BlockSpec((None, 1, 6 * D_MODEL), lambda i, d, c: (mod_row(i, TM_MOE), 0, 0)),
                  const(g_post), const(swg), const(swu), const(swd),
                  pl.BlockSpec(memory_space=pl.ANY)],
        out_specs=row(D_MODEL),
        scratch_shapes=[pltpu.VMEM((2, SLOT_ROWS, D_MODEL), bf16), pltpu.VMEM((TM_MOE, D_MODEL), f32),
                        pltpu.SemaphoreType.DMA((2,))])
    return pl.pallas_call(
        functools.partial(_combine_kernel, tile_offset=tile_offset), grid_spec=grid_spec,
        out_shape=jax.ShapeDtypeStruct((n, D_MODEL), f32),
        compiler_params=_cparams("arbitrary"), name="moe_combine",
    )(dst, nch, sw, h2, x1, mod3, g_post, swg, swu, swd, ys)


def _sparse_moe(paths, mod3, p):
    cnt = jnp.concatenate([q['cnt'][:, :, 0] for q in paths], axis=0).astype(jnp.int32)
    n_tiles = cnt.shape[0]
    before = jnp.cumsum(cnt, axis=0) - cnt
    rows_e = jnp.sum(cnt, axis=0)
    region = (rows_e + FFN_ROWS - 1) // FFN_ROWS * FFN_ROWS
    region_end = jnp.cumsum(region)
    region_start = region_end - region
    worst = n_tiles * (TOP_K * TM_MOE + N_EXPERTS * (SEG - 1)) + N_EXPERTS * (FFN_ROWS - SEG)
    total_rows = -(-worst // FFN_ROWS) * FFN_ROWS + FFN_ROWS
    spare_row = total_rows - FFN_ROWS
    first_row = jnp.arange(total_rows // FFN_ROWS, dtype=jnp.int32) * FFN_ROWS
    ended = jnp.sum((region_end[None, :] <= first_row[:, None]).astype(jnp.int32), axis=1)
    tile_expert = jnp.minimum(ended, N_EXPERTS - 1).astype(jnp.int32)
    n_valid = (region_end[-1:] // FFN_ROWS).astype(jnp.int32)

    seg_end = jnp.cumsum(cnt, axis=1)
    seg_start = seg_end - cnt
    nch = (seg_end[:, -1] // SEG).astype(jnp.int32)
    chunk_row = jnp.arange(MAX_CHUNKS, dtype=jnp.int32) * SEG
    chunk_e = jnp.sum((seg_end[:, None, :] <= chunk_row[None, :, None]).astype(jnp.int32), axis=2)
    own = chunk_e[:, :, None] == jnp.arange(N_EXPERTS, dtype=jnp.int32)[None, None, :]
    shift = (region_start[None, :] + before - seg_start)[:, None, :]
    dst = jnp.sum(jnp.where(own, shift, 0), axis=2) + chunk_row[None, :]
    chunk_id = jnp.arange(MAX_CHUNKS, dtype=jnp.int32)[None, :]
    tile_id = jnp.arange(n_tiles, dtype=jnp.int32)[:, None]
    spare = spare_row + ((tile_id % 2) * CHUNK_GROUP + chunk_id % CHUNK_GROUP) * SEG
    dst = jnp.where(chunk_id < nch[:, None], dst, spare).astype(jnp.int32).reshape(-1)

    h2_all = jnp.concatenate([q['h2'] for q in paths], axis=0)
    slots_all = jnp.concatenate([q['slots'] for q in paths], axis=1)
    fill = jnp.concatenate([n_valid, region_start + rows_e, region - rows_e]).astype(jnp.int32)
    xs = _dispatch(h2_all, slots_all, dst, nch, fill, total_rows)
    ys = _expert_ffn(xs, tile_expert, n_valid, p['expert_wg'], p['expert_wu'], p['expert_wd'])
    outs, tile_offset = [], 0
    for q in paths:
        outs.append(_combine(ys, dst, nch, q['sw'], q['h2'], q['x1'], mod3, q['mod_row'],
                             p['g_post_ffn'], p['shared_wg'], p['shared_wu'], p['shared_wd'], tile_offset))
        tile_offset += q['h2'].shape[0] // TM_MOE
    return outs


def _pad_heads(w, heads):
    rows = w.shape[0]
    w = w.reshape(rows, heads, HEAD_DIM)
    return jnp.pad(w, ((0, 0), (0, 0), (0, HEAD_PAD - HEAD_DIM))).reshape(rows, heads * HEAD_PAD)


def _block_diag2(m):
    z = jnp.zeros_like(m[0])
    return jnp.concatenate([jnp.concatenate([m[0], z], axis=1), jnp.concatenate([z, m[1]], axis=1)], axis=0)


def _prepare_params(w):
    aw, kw = ATTN_HEADS * HEAD_DIM, KV_HEADS * HEAD_DIM
    w_in = w['w_in']
    lane_row = lambda v: v.reshape(1, -1).astype(f32)
    head_id = np.arange(RWKV_WIDTH) // RWKV_HEAD
    seg = jnp.asarray(head_id[:, None] == head_id[None, :], dtype=bf16)
    gain_pad = lambda gvec: jnp.pad(gvec, (0, HEAD_PAD - HEAD_DIM)).reshape(1, HEAD_PAD)
    return dict(
        g_pre_mix=lane_row(w['g_pre_mix']), g_post_mix=lane_row(w['g_post_mix']),
        g_pre_ffn=lane_row(w['g_pre_ffn']), g_post_ffn=lane_row(w['g_post_ffn']),
        wq=_pad_heads(w_in[:, :aw], ATTN_HEADS).astype(bf16),
        wk=_pad_heads(w_in[:, aw:aw + kw], KV_HEADS).astype(bf16),
        wv=w_in[:, aw + kw:aw + 2 * kw].astype(bf16),
        wr=w_in[:, aw + 2 * kw:].astype(bf16),
        qg=gain_pad(w['q_gain'] * ATTN_SCALE), kg=gain_pad(w['k_gain']),
        mu_prev=lane_row(w['mu_prev']), mu_next=lane_row(w['mu_next']),
        w0=lane_row(w['decay_w0']), w2bd=_block_diag2(w['decay_w2']),
        a0=lane_row(w['iclr_a0']), a2bd=_block_diag2(w['iclr_a2']),
        g2=w['gate_g2'], k_k=lane_row(w['k_k']), k_a=lane_row(w['k_a']), r_k=lane_row(w['r_k']),
        lnx_g=lane_row(w['lnx_g']), lnx_b=lane_row(w['lnx_b']), seg=seg,
        w_out=w['w_out'].astype(bf16),
        router_wt=w['router_w'].T, router_b=w['router_b'].reshape(N_EXPERTS, 1),
        expert_wg=w['expert_wg'], expert_wu=w['expert_wu'], expert_wd=w['expert_wd'],
        shared_wg=w['shared_wg'].astype(bf16), shared_wu=w['shared_wu'].astype(bf16),
        shared_wd=w['shared_wd'].astype(bf16),
    )


def _rope_tables(seq_len):
    rows = seq_len // GRID_W
    row_idx = jnp.repeat(jnp.arange(rows, dtype=jnp.int32), GRID_W)
    col_idx = jnp.tile(jnp.arange(GRID_W, dtype=jnp.int32), rows)
    inv = ROPE_THETA ** (-jnp.arange(0, ROPE_HALF, 2, dtype=f32) / ROPE_HALF)
    ang = jnp.stack([row_idx.astype(f32)[:, None] * inv, col_idx.astype(f32)[:, None] * inv], axis=1)
    cos, sin = jnp.cos(ang), jnp.sin(ang)
    cos_l = jnp.stack([cos, cos], axis=2).reshape(seq_len, HEAD_DIM)
    sin_l = jnp.stack([-sin, sin], axis=2).reshape(seq_len, HEAD_DIM)
    pad = ((0, 0), (0, HEAD_PAD - HEAD_DIM))
    return jnp.pad(cos_l, pad), jnp.pad(sin_l, pad)


def _states_to_pairs(s):
    b = s.shape[0]
    s = s.reshape(b, 2, N_PAIRS, 2, RWKV_HEAD, RWKV_HEAD)
    eye = jnp.eye(2, dtype=s.dtype)
    return jnp.einsum('bdphvk,hg->bdphvgk', s, eye).reshape(b, 2, N_PAIRS, PAIR, PAIR)


def _pairs_to_states(s):
    b = s.shape[0]
    s = s.reshape(b, 2, N_PAIRS, 2, RWKV_HEAD, 2, RWKV_HEAD)
    eye = jnp.eye(2, dtype=s.dtype)
    return jnp.einsum('bdphvgk,hg->bdphvk', s, eye).reshape(b, 2, RWKV_HEADS, RWKV_HEAD, RWKV_HEAD)


def _trunk(x, mod3, mod_row, p, rope_tabs, past_k, past_v, s0_pairs, tm_in, tq, tm_prep):
    b, t, _ = x.shape
    n = b * t
    x2 = x.reshape(n, D_MODEL)
    q_pad, k_own, v_own, k_pad, zr = _in_proj(
        x2, mod3, mod_row, p['g_pre_mix'], p['wq'], p['wk'], p['wv'], p['wr'], p['qg'], p['kg'],
        rope_tabs, t, tm_in)
    k_full = k_pad.reshape(b, t, KV_HEADS * HEAD_PAD)
    v_full = v_own.reshape(b, t, KV_HEADS * HEAD_DIM)
    if past_k is not None:
        k_past = _pad_heads(past_k.reshape(-1, KV_HEADS * HEAD_DIM), KV_HEADS).astype(bf16)
        k_full = jnp.concatenate([k_past.reshape(b, -1, KV_HEADS * HEAD_PAD), k_full], axis=1)
        v_full = jnp.concatenate([past_v.reshape(b, -1, KV_HEADS * HEAD_DIM), v_full], axis=1)
    vt_full = jnp.swapaxes(v_full, 1, 2).astype(bf16)
    attn = _attention(q_pad, k_full, vt_full, b, t, tq)
    r, k, v, kk, lw, a, g, bonus = _rwkv_prep(zr, p, t, tm_prep)
    o_fw, o_bw, s_fin = _rwkv_scan(r, k, v, kk, lw, a, p['k_a'], s0_pairs, b, t)
    x1, h2, slots, sw, cnt = _mix_out(attn, o_fw, o_bw, bonus, g, x2, mod3, mod_row, p, TM_MOE)
    return dict(h2=h2, slots=slots, sw=sw, cnt=cnt, x1=x1, mod_row=mod_row), k_own, v_own, s_fin


def kernel(x_prompt, x_sample, c, cache_k, cache_v, state_rwkv, c_ctx, w_ada, b_ada, g_pre_mix, g_post_mix,
           g_pre_ffn, g_post_ffn, w_in, w_out, q_gain, k_gain, mu_prev, mu_next, decay_w0, decay_w2, iclr_a0,
           iclr_a2, gate_g2, k_k, k_a, r_k, lnx_g, lnx_b, router_w, router_b, expert_wg, expert_wu, expert_wd,
           shared_wg, shared_wu, shared_wd):
    layer = 0
    names = ('g_pre_mix g_post_mix g_pre_ffn g_post_ffn w_in w_out q_gain k_gain mu_prev mu_next decay_w0 '
             'decay_w2 iclr_a0 iclr_a2 gate_g2 k_k k_a r_k lnx_g lnx_b router_w router_b expert_wg expert_wu '
             'expert_wd shared_wg shared_wu shared_wd').split()
    vals = (g_pre_mix, g_post_mix, g_pre_ffn, g_post_ffn, w_in, w_out, q_gain, k_gain, mu_prev, mu_next,
            decay_w0, decay_w2, iclr_a0, iclr_a2, gate_g2, k_k, k_a, r_k, lnx_g, lnx_b, router_w, router_b,
            expert_wg, expert_wu, expert_wd, shared_wg, shared_wu, shared_wd)
    p = _prepare_params({nm: v[layer] for nm, v in zip(names, vals)})

    nb, ts, _ = x_sample.shape
    npb, tp, _ = x_prompt.shape
    mod_rows = 16
    cc = jnp.zeros((mod_rows, D_MODEL), f32).at[0].set(c_ctx).at[1:1 + nb].set(c)
    mod3 = _modulation(cc, w_ada[layer], b_ada[layer]).reshape(mod_rows, 1, 6 * D_MODEL)

    zeros_state = jnp.zeros((npb, 2, N_PAIRS, PAIR, PAIR), f32)
    moe_p, kc, vc, st = _trunk(x_prompt, mod3, lambda i, tm: 0, p, None, None, None, zeros_state,
                               tm_in=256, tq=tp, tm_prep=256)
    s0_lat = _states_to_pairs(state_rwkv[:, layer])
    moe_s, _, _, _ = _trunk(x_sample, mod3, lambda i, tm: 1 + (i * tm) // ts, p, _rope_tables(ts),
                            cache_k[:, layer], cache_v[:, layer], s0_lat,
                            tm_in=512, tq=128, tm_prep=256)
    y_p, y_s = _sparse_moe([moe_p, moe_s], mod3, p)
    y_p = y_p.reshape(x_prompt.shape)
    y_s = y_s.reshape(x_sample.shape)

    new_cache_k = kc.reshape(npb, 1, tp, KV_HEADS, HEAD_DIM)
    new_cache_v = vc.reshape(npb, 1, tp, KV_HEADS, HEAD_DIM)
    new_state = _pairs_to_states(st)[:, None]
    return (y_p, y_s, new_cache_k, new_cache_v, new_state)
```

```python
import functools

import numpy as np
import jax
import jax.numpy as jnp
from jax import lax
from jax.experimental import pallas as pl
from jax.experimental.pallas import tpu as pltpu

f32 = jnp.float32
bf16 = jnp.bfloat16

D_MODEL = 1024
GRID_W = 64
HEAD_DIM = 64
ATTN_HEADS = 8
KV_HEADS = 2
GQA_GROUP = ATTN_HEADS // KV_HEADS
ATTN_SCALE = HEAD_DIM ** -0.5
ROPE_THETA = 10000.0
ROPE_HALF = HEAD_DIM // 2
ROPE_FREQS = ROPE_HALF // 2
RWKV_HEAD = 64
RWKV_HEADS = 8
RWKV_WIDTH = RWKV_HEADS * RWKV_HEAD
DECAY_RANK = 64
ICLR_RANK = 64
GATE_RANK = 128
RWKV_IN = 3 * RWKV_WIDTH + 2 * DECAY_RANK + 2 * ICLR_RANK + GATE_RANK
DECAY_SCALE = 0.606531
GN_EPS = 64e-5
N_EXPERTS = 64
TOP_K = 6
N_GROUPS = 8
GROUP_SIZE = N_EXPERTS // N_GROUPS
TOPK_GROUPS = 4
EXPERT_FF = 256
ROUTED_SCALE = 2.5
EPS = 1e-6

LANE = 128
HEAD_PAD = LANE
CHUNK = 64
QUAD_HEADS = 4
QUAD = QUAD_HEADS * RWKV_HEAD
N_QUADS = RWKV_HEADS // QUAD_HEADS
SCAN_SEQS = 4
VMEM_LIMIT = 56 * 1024 * 1024
SCAN_PASSES = 1
TM_MOE = 256
SEG_SHIFT = 4
SEG = 1 << SEG_SHIFT
SLOT_ROWS = 2560
MAX_CHUNKS = SLOT_ROWS // SEG
CHUNK_GROUP_SHIFT = 3
CHUNK_GROUP = 1 << CHUNK_GROUP_SHIFT
SEL_BLOCK = 512
COMMON_BLOCKS = 4
FFN_ROWS = 512


def _cparams(*sem):
    return pltpu.CompilerParams(dimension_semantics=sem, vmem_limit_bytes=VMEM_LIMIT)


def _dot(a, b, nt=False):
    dims = (((1,), (1,)), ((), ())) if nt else (((1,), (0,)), ((), ()))
    return lax.dot_general(a, b, dims, preferred_element_type=f32)


def _split2(x):
    hi = x.astype(bf16)
    lo = (x - hi.astype(f32)).astype(bf16)
    return hi, lo


def _split3(x):
    h1 = x.astype(bf16)
    r1 = x - h1.astype(f32)
    h2 = r1.astype(bf16)
    h3 = (r1 - h2.astype(f32)).astype(bf16)
    return h1, h2, h3


def _mm(a, b, nt=False, passes=3):
    if passes == 1:
        return _dot(a.astype(bf16), b.astype(bf16), nt)
    ah, al = _split2(a)
    bh, bl = _split2(b)
    return _dot(ah, bh, nt) + (_dot(al, bh, nt) + _dot(ah, bl, nt))


def _mm_exact_rhs(a, b_bf16, nt=False):
    ah, al = _split2(a)
    return _dot(ah, b_bf16, nt) + _dot(al, b_bf16, nt)


def _rms(x, g):
    return x * lax.rsqrt(jnp.mean(x * x, axis=-1, keepdims=True) + EPS) * g


def _sigmoid(x):
    return jax.nn.sigmoid(x)


def _full_spec(a, grid_rank=1):
    zeros = (0,) * a.ndim
    if grid_rank == 1:
        return pl.BlockSpec(a.shape, lambda i: zeros)
    if grid_rank == 2:
        return pl.BlockSpec(a.shape, lambda i, j: zeros)
    return pl.BlockSpec(a.shape, lambda i, j, k: zeros)


def _mod_kernel(c_ref, w_ref, b_ref, o_ref):
    c = c_ref[...]
    s = c * _sigmoid(c)
    o_ref[...] = _mm(s, w_ref[...]) + b_ref[...]


def _modulation(cc, w_ada, b_ada):
    rows, n = cc.shape[0], w_ada.shape[1]
    tn = 512
    return pl.pallas_call(
        _mod_kernel,
        grid=(n // tn,),
        in_specs=[pl.BlockSpec((rows, D_MODEL), lambda j: (0, 0)),
                  pl.BlockSpec((D_MODEL, tn), lambda j: (0, j)),
                  pl.BlockSpec((1, tn), lambda j: (0, j))],
        out_specs=pl.BlockSpec((rows, tn), lambda j: (0, j)),
        out_shape=jax.ShapeDtypeStruct((rows, n), f32),
        compiler_params=_cparams("parallel"),
        name="modulation",
    )(cc, w_ada, b_ada.reshape(1, n))


def _in_kernel(*refs, rope):
    if rope:
        (x_ref, mod_ref, g_ref, wq_ref, wk_ref, wv_ref, wr_ref, qg_ref, kg_ref, cos_ref, sin_ref,
         q_ref, ko_ref, vo_ref, kp_ref, zr_ref) = refs
    else:
        (x_ref, mod_ref, g_ref, wq_ref, wk_ref, wv_ref, wr_ref, qg_ref, kg_ref,
         q_ref, ko_ref, vo_ref, kp_ref, zr_ref) = refs
    x = x_ref[...]
    m = mod_ref[...]
    h = _rms(x, g_ref[...])
    h = h * (1.0 + m[:, D_MODEL:2 * D_MODEL]) + m[:, 0:D_MODEL]
    hb = h.astype(bf16)
    zr_ref[...] = _dot(hb, wr_ref[...])
    zq = _dot(hb, wq_ref[...])
    zk = _dot(hb, wk_ref[...])
    zv = _dot(hb, wv_ref[...])

    lane = lax.broadcasted_iota(jnp.int32, (1, LANE), 1)
    first_half = (lane & (ROPE_HALF - 1)) < ROPE_FREQS

    def head_norm(z, gain):
        ms = jnp.sum(z * z, axis=-1, keepdims=True) * (1.0 / HEAD_DIM)
        return z * lax.rsqrt(ms + EPS) * gain

    def rotate(y):
        if not rope:
            return y
        partner = jnp.where(first_half, pltpu.roll(y, LANE - ROPE_FREQS, 1), pltpu.roll(y, ROPE_FREQS, 1))
        return y * cos_ref[...] + partner * sin_ref[...]

    for hh in range(ATTN_HEADS):
        sl = slice(hh * HEAD_PAD, (hh + 1) * HEAD_PAD)
        q_ref[:, sl] = rotate(head_norm(zq[:, sl], qg_ref[...])).astype(bf16)
    kn = [head_norm(zk[:, hh * HEAD_PAD:(hh + 1) * HEAD_PAD], kg_ref[...]) for hh in range(KV_HEADS)]
    ko_ref[...] = kn[0] + pltpu.roll(kn[1], HEAD_DIM, 1)
    vo_ref[...] = zv
    for hh in range(KV_HEADS):
        sl = slice(hh * HEAD_PAD, (hh + 1) * HEAD_PAD)
        kp_ref[:, sl] = rotate(kn[hh]).astype(bf16)


def _in_proj(x2, mod3, mod_row, g_pre, wq, wk, wv, wr, qg, kg, rope_tabs, seq_len, tm):
    n = x2.shape[0]
    rope = rope_tabs is not None
    in_specs = [pl.BlockSpec((tm, D_MODEL), lambda i: (i, 0)),
                pl.BlockSpec((None, 1, 6 * D_MODEL), lambda i: (mod_row(i, tm), 0, 0)),
                _full_spec(g_pre), _full_spec(wq), _full_spec(wk), _full_spec(wv), _full_spec(wr),
                _full_spec(qg), _full_spec(kg)]
    args = [x2, mod3, g_pre, wq, wk, wv, wr, qg, kg]
    if rope:
        blocks_per_seq = seq_len // tm
        in_specs += [pl.BlockSpec((tm, LANE), lambda i: (i % blocks_per_seq, 0))] * 2
        args += list(rope_tabs)
    out_shape = [jax.ShapeDtypeStruct((n, ATTN_HEADS * HEAD_PAD), bf16),
                 jax.ShapeDtypeStruct((n, KV_HEADS * HEAD_DIM), f32),
                 jax.ShapeDtypeStruct((n, KV_HEADS * HEAD_DIM), f32),
                 jax.ShapeDtypeStruct((n, KV_HEADS * HEAD_PAD), bf16),
                 jax.ShapeDtypeStruct((n, RWKV_IN), f32)]
    out_specs = [pl.BlockSpec((tm, s.shape[1]), lambda i: (i, 0)) for s in out_shape]
    return pl.pallas_call(
        functools.partial(_in_kernel, rope=rope),
        grid=(n // tm,), in_specs=in_specs, out_specs=out_specs, out_shape=out_shape,
        compiler_params=_cparams("parallel"), name="in_proj",
    )(*args)


def _attn_kernel(q_ref, k_ref, vt_ref, o_ref):
    tq = q_ref.shape[0]
    heads_t = []
    for kv in range(KV_HEADS):
        kh = k_ref[:, kv * HEAD_PAD:(kv + 1) * HEAD_PAD]
        vt = vt_ref[kv * HEAD_DIM:(kv + 1) * HEAD_DIM, :]
        q4 = jnp.concatenate(
            [q_ref[:, (GQA_GROUP * kv + g) * HEAD_PAD:(GQA_GROUP * kv + g + 1) * HEAD_PAD]
             for g in range(GQA_GROUP)], axis=0)
        st = _dot(kh, q4, nt=True)
        mx = jnp.max(st, axis=0, keepdims=True)
        p = jnp.exp(st - mx)
        den = jnp.sum(p, axis=0, keepdims=True)
        ot = _dot(vt, p.astype(bf16)) / den
        heads_t += [ot[:, g * tq:(g + 1) * tq] for g in range(GQA_GROUP)]
    o_ref[...] = jnp.concatenate(heads_t, axis=0).T


def _attention(q_pad, k_full, vt_full, batch, seq_len, tq):
    n = q_pad.shape[0]
    kv_len = k_full.shape[1]
    nq = seq_len // tq
    return pl.pallas_call(
        _attn_kernel,
        grid=(batch, nq),
        in_specs=[pl.BlockSpec((tq, ATTN_HEADS * HEAD_PAD), lambda b, j: (b * nq + j, 0)),
                  pl.BlockSpec((None, kv_len, KV_HEADS * HEAD_PAD), lambda b, j: (b, 0, 0)),
                  pl.BlockSpec((None, KV_HEADS * HEAD_DIM, kv_len), lambda b, j: (b, 0, 0))],
        out_specs=pl.BlockSpec((tq, ATTN_HEADS * HEAD_DIM), lambda b, j: (b * nq + j, 0)),
        out_shape=jax.ShapeDtypeStruct((n, ATTN_HEADS * HEAD_DIM), f32),
        compiler_params=_cparams("parallel", "parallel"), name="attention",
    )(q_pad, k_full, vt_full)


def _prep_kernel(z_ref, zp_ref, zn_ref, mup_ref, mun_ref, w0_ref, w2_ref, a0_ref, a2_ref, g2_ref,
                 kkw_ref, ka_ref, rk_ref, seg_ref,
                 r_ref, k_ref, v_ref, kk_ref, lw_ref, a_ref, g_ref, bonus_ref, *, seq_len, tm):
    i = pl.program_id(0)
    z = z_ref[...]
    row = lax.broadcasted_iota(jnp.int32, (tm, 1), 0)
    seq_start = ((i * tm) % seq_len) == 0
    seq_end = (((i + 1) * tm) % seq_len) == 0
    prow = jnp.where(seq_start, 0.0, zp_ref[7:8, :])
    nrow = jnp.where(seq_end, 0.0, zn_ref[0:1, :])
    prev = jnp.where(row == 0, prow, pltpu.roll(z, 1, 0))
    nxt = jnp.where(row == tm - 1, nrow, pltpu.roll(z, tm - 1, 0))
    zs = z + mup_ref[...] * (prev - z) + mun_ref[...] * (nxt - z)
    w = RWKV_WIDTH
    r = zs[:, 0:w]
    k = zs[:, w:2 * w]
    v = zs[:, 2 * w:3 * w]
    dw = zs[:, 3 * w:3 * w + 2 * DECAY_RANK]
    da = zs[:, 3 * w + 2 * DECAY_RANK:3 * w + 2 * DECAY_RANK + 2 * ICLR_RANK]
    dg = zs[:, 3 * w + 2 * DECAY_RANK + 2 * ICLR_RANK:]
    lw = -DECAY_SCALE * _sigmoid(w0_ref[...] + _mm(jnp.tanh(dw), w2_ref[...]))
    a = _sigmoid(a0_ref[...] + _mm(da, a2_ref[...]))
    g = _mm(_sigmoid(dg), g2_ref[...])
    kk = k * kkw_ref[...]
    seg = seg_ref[...]
    kk = kk * lax.rsqrt(_mm_exact_rhs(kk * kk, seg) + 1e-12)
    ka = ka_ref[...]
    ke_sum = k * (2.0 + (a[:, 0:w] + a[:, w:2 * w] - 2.0) * ka)
    bonus = _mm_exact_rhs(r * ke_sum * rk_ref[...], seg) * v
    r_ref[...] = r
    k_ref[...] = k
    v_ref[...] = v
    kk_ref[...] = kk
    lw_ref[...] = lw
    a_ref[...] = a
    g_ref[...] = g
    bonus_ref[...] = bonus


def _rwkv_prep(zr, p, seq_len, tm):
    n = zr.shape[0]
    w = RWKV_WIDTH
    nb8 = n // 8
    consts = [p['mu_prev'], p['mu_next'], p['w0'], p['w2bd'], p['a0'], p['a2bd'], p['g2'],
              p['k_k'], p['k_a'], p['r_k'], p['seg']]
    in_specs = [pl.BlockSpec((tm, RWKV_IN), lambda i: (i, 0)),
                pl.BlockSpec((8, RWKV_IN), lambda i: (jnp.maximum(i * (tm // 8) - 1, 0), 0)),
                pl.BlockSpec((8, RWKV_IN), lambda i: (jnp.minimum((i + 1) * (tm // 8), nb8 - 1), 0))]
    in_specs += [_full_spec(c) for c in consts]
    widths = [w, w, w, w, 2 * w, 2 * w, w, w]
    out_shape = [jax.ShapeDtypeStruct((n, wd), f32) for wd in widths]
    out_specs = [pl.BlockSpec((tm, wd), lambda i: (i, 0)) for wd in widths]
    return pl.pallas_call(
        functools.partial(_prep_kernel, seq_len=seq_len, tm=tm),
        grid=(n // tm,), in_specs=in_specs, out_specs=out_specs, out_shape=out_shape,
        compiler_params=_cparams("parallel"), name="rwkv_prep",
    )(zr, zr, zr, *consts)


def _scan_kernel(rf_ref, kf_ref, vf_ref, kkf_ref, lwf_ref, af_ref,
                 rb_ref, kb_ref, vb_ref, kkb_ref, lwb_ref, ab_ref, ka_ref, s0_ref,
                 of_ref, ob_ref, sfin_ref, st_ref):
    c = pl.program_id(1)

    @pl.when(c == 0)
    def _():
        st_ref[...] = s0_ref[...]

    n_seq = rf_ref.shape[0]
    row = lax.broadcasted_iota(jnp.int32, (CHUNK, CHUNK), 0)
    col = lax.broadcasted_iota(jnp.int32, (CHUNK, CHUNK), 1)
    qr = lax.broadcasted_iota(jnp.int32, (QUAD, QUAD), 0)
    qc = lax.broadcasted_iota(jnp.int32, (QUAD, QUAD), 1)
    head_mask = (qr & -RWKV_HEAD) == (qc & -RWKV_HEAD)
    head_mask_b = jnp.where(head_mask, 1.0, 0.0).astype(bf16)
    tr = lax.broadcasted_iota(jnp.int32, (CHUNK, QUAD), 0)
    tc = lax.broadcasted_iota(jnp.int32, (CHUNK, QUAD), 1) & (CHUNK - 1)
    eye = jnp.where(tr == tc, 1.0, 0.0).astype(f32)
    ka = ka_ref[...]

    def bd(x):
        xb = x.astype(bf16)
        return jnp.concatenate([xb] * QUAD_HEADS, axis=0) * head_mask_b

    def mm(a, b, nt=False):
        return _dot(a.astype(bf16), b.astype(bf16), nt)

    chains = []
    for d, (r_ref, k_ref, v_ref, kk_ref, lw_ref, a_ref, o_ref) in enumerate(
            ((rf_ref, kf_ref, vf_ref, kkf_ref, lwf_ref, af_ref, of_ref),
             (rb_ref, kb_ref, vb_ref, kkb_ref, lwb_ref, ab_ref, ob_ref))):
        sgn = 1 - 2 * d
        tri = jnp.where((row - col) * sgn >= 0, 1.0, 0.0).astype(bf16)
        dist = (tr - tc) * sgn
        for s in range(n_seq):
            lw = lw_ref[s]
            l1, l2, l3 = _split3(lw)
            cum = _dot(tri, l1) + (_dot(tri, l2) + _dot(tri, l3))
            e_incl = jnp.exp(cum)
            e_excl = jnp.exp(cum - lw)
            e_neg = jnp.exp(-cum)
            wc = jnp.exp(jnp.sum(lw, axis=0, keepdims=True))
            kk = kk_ref[s]
            a = a_ref[s]
            bh = kk * a * e_neg
            kh = k_ref[s] * (1.0 + (a - 1.0) * ka) * e_neg
            ops = dict(at=-kk * e_excl, rt=r_ref[s] * e_incl, bh=bh, kh=kh, bw=bh * wc, kw=kh * wc,
                       v=v_ref[s], wc=wc)
            for g in range(N_QUADS):
                ch = {key: val[:, g * QUAD:(g + 1) * QUAD] for key, val in ops.items()}
                ch.update(s=s, d=d, g=g, o_ref=o_ref, strict=dist > 0, incl=dist >= 0, st=st_ref[s, d, g])
                chains.append(ch)

    lane_lo = lax.broadcasted_iota(jnp.int32, (QUAD, LANE), 1) < RWKV_HEAD
    for ch in chains:
        ar = jnp.concatenate([ch['at'], ch['rt']], axis=0)
        ch['ar'] = ar.astype(bf16)
        bk_t = jnp.concatenate([ch['bh'], ch['kh']], axis=0).T
        swapped = pltpu.roll(bk_t, RWKV_HEAD, 1)
        b_t = jnp.where(lane_lo, bk_t, swapped).astype(bf16)
        k_t = jnp.where(lane_lo, swapped, bk_t).astype(bf16)
        w_bk = jnp.concatenate([jnp.concatenate([b_t, b_t], axis=1) * head_mask_b,
                                jnp.concatenate([k_t, k_t], axis=1) * head_mask_b], axis=1)
        amat = _dot(ch['ar'], w_bk)
        ch['n'] = jnp.where(ch['strict'], amat[0:CHUNK, 0:QUAD], 0.0)
        ch['a_kr'] = jnp.concatenate([jnp.where(ch['strict'], amat[0:CHUNK, QUAD:2 * QUAD], 0.0),
                                      jnp.where(ch['incl'], amat[CHUNK:2 * CHUNK, QUAD:2 * QUAD], 0.0)], axis=0)
        ch['a_rb'] = jnp.where(ch['incl'], amat[CHUNK:2 * CHUNK, 0:QUAD], 0.0)
        ch['tinv'] = eye + ch['n']
    for ch in chains:
        ch['n'] = mm(ch['n'], bd(ch['n']))
    for _ in range(4):
        for ch in chains:
            both = mm(jnp.concatenate([ch['n'], ch['tinv']], axis=0), bd(ch['n']))
            ch['n'] = both[0:CHUNK]
            ch['tinv'] = ch['tinv'] + both[CHUNK:2 * CHUNK]
    for ch in chains:
        ch['tinv'] = ch['tinv'] + mm(ch['tinv'], bd(ch['n']))
        sa = _dot(ch['ar'], ch['st'].astype(bf16))
        av = mm(ch['a_kr'], bd(ch['v']))
        ch['rhs'] = sa[0:CHUNK] + av[0:CHUNK]
        ch['o'] = sa[CHUNK:2 * CHUNK] + av[CHUNK:2 * CHUNK]
    for ch in chains:
        ch['u'] = mm(ch['tinv'], bd(ch['rhs']))
    for ch in chains:
        ch['o'] = ch['o'] + mm(ch['a_rb'], bd(ch['u']))
        z_t = jnp.concatenate([ch['bw'], ch['kw'], jnp.broadcast_to(ch['wc'], (2 * CHUNK, QUAD))], axis=0).T
        upd = mm(z_t[:, 0:2 * CHUNK], jnp.concatenate([ch['u'], ch['v']], axis=0))
        decay = jnp.concatenate([z_t[:, 2 * CHUNK:4 * CHUNK]] * 2, axis=1)
        ch['st_new'] = ch['st'] * decay + jnp.where(head_mask, upd, 0.0)
    for ch in chains:
        ch['o_ref'][ch['s'], :, ch['g'] * QUAD:(ch['g'] + 1) * QUAD] = ch['o']
        st_ref[ch['s'], ch['d'], ch['g']] = ch['st_new']

    @pl.when(c == pl.num_programs(1) - 1)
    def _():
        sfin_ref[...] = st_ref[...]


def _rwkv_scan(r, k, v, kk, lw, a, k_a, s0, batch, seq_len):
    n = r.shape[0]
    w = RWKV_WIDTH
    nc = seq_len // CHUNK
    per_seq = lambda x: x.reshape(batch, seq_len, x.shape[-1])
    r, k, v, kk, lw, a = (per_seq(x) for x in (r, k, v, kk, lw, a))
    fwd = lambda b, c: (b, c, 0)
    bwd = lambda b, c: (b, nc - 1 - c, 0)
    bwd_dir = lambda b, c: (b, nc - 1 - c, 1)
    blk = lambda index_map: pl.BlockSpec((SCAN_SEQS, CHUNK, w), index_map)
    state_spec = pl.BlockSpec((SCAN_SEQS, 2, N_QUADS, QUAD, QUAD), lambda b, c: (b, 0, 0, 0, 0))
    o_shape = jax.ShapeDtypeStruct((batch, seq_len, w), f32)
    o_fw, o_bw, s_fin = pl.pallas_call(
        _scan_kernel,
        grid=(batch // SCAN_SEQS, nc),
        in_specs=[blk(fwd)] * 6 + [blk(bwd)] * 4 + [blk(bwd_dir)] * 2
                 + [pl.BlockSpec((1, w), lambda b, c: (0, 0)), state_spec],
        out_specs=[blk(fwd), blk(bwd), state_spec],
        out_shape=[o_shape, o_shape, jax.ShapeDtypeStruct((batch, 2, N_QUADS, QUAD, QUAD), f32)],
        scratch_shapes=[pltpu.VMEM((SCAN_SEQS, 2, N_QUADS, QUAD, QUAD), f32)],
        compiler_params=_cparams("parallel", "arbitrary"), name="rwkv_scan",
    )(r, k, v, kk, lw, a, r, k, v, kk, lw, a, k_a, s0)
    return o_fw.reshape(n, w), o_bw.reshape(n, w), s_fin


def _out_kernel(attn_ref, of_ref, ob_ref, bonus_ref, g_ref, x_ref, mod_ref, lng_ref, lnb_ref, seg_ref,
                wo_ref, gpm_ref, gpf_ref, rwt_ref, rb_ref,
                x1_ref, h2_ref, slot_ref, sw_ref, cnt_ref):
    tm = x_ref.shape[0]
    seg = seg_ref[...]
    inv = 1.0 / RWKV_HEAD
    o = of_ref[...] + ob_ref[...]
    mu = _mm_exact_rhs(o, seg) * inv
    dl = o - mu
    var = _mm_exact_rhs(dl * dl, seg) * inv
    on = dl * lax.rsqrt(var + GN_EPS) * lng_ref[...] + lnb_ref[...]
    rw = (on + bonus_ref[...]) * g_ref[...]
    half = ATTN_HEADS * HEAD_DIM
    y = _dot(attn_ref[...].astype(bf16), wo_ref[0:half, :]) + _dot(rw.astype(bf16), wo_ref[half:, :])
    m = mod_ref[...]
    gt_a = m[:, 2 * D_MODEL:3 * D_MODEL]
    sh_f = m[:, 3 * D_MODEL:4 * D_MODEL]
    sc_f = m[:, 4 * D_MODEL:5 * D_MODEL]
    x1 = x_ref[...] + gt_a * _rms(y, gpm_ref[...])
    x1_ref[...] = x1
    h2 = _rms(x1, gpf_ref[...]) * (1.0 + sc_f) + sh_f
    h2_ref[...] = h2.astype(bf16)

    logits = _mm(rwt_ref[...], h2, nt=True)
    scores = _sigmoid(logits)
    biased = scores + rb_ref[...]
    neg = -jnp.inf
    shape3 = (N_GROUPS, GROUP_SIZE, tm)
    b3 = biased.reshape(shape3)
    s3 = scores.reshape(shape3)
    in_grp = lax.broadcasted_iota(jnp.int32, shape3, 1)
    grp = lax.broadcasted_iota(jnp.int32, shape3, 0)
    m1 = jnp.max(b3, axis=1, keepdims=True)
    i1 = jnp.min(jnp.where(b3 == m1, in_grp, GROUP_SIZE), axis=1, keepdims=True)
    m2 = jnp.max(jnp.where(in_grp == i1, neg, b3), axis=1, keepdims=True)
    gscore = m1 + m2
    gidx = lax.broadcasted_iota(jnp.int32, (N_GROUPS, 1, tm), 0)
    gsel = jnp.zeros((N_GROUPS, 1, tm), f32)
    cur = gscore
    for _ in range(TOPK_GROUPS):
        mx = jnp.max(cur, axis=0, keepdims=True)
        ii = jnp.min(jnp.where(cur == mx, gidx, N_GROUPS), axis=0, keepdims=True)
        hit = gidx == ii
        gsel = jnp.where(hit, 1.0, gsel)
        cur = jnp.where(hit, neg, cur)
    cand = jnp.where(jnp.broadcast_to(gsel, shape3) > 0.0, b3, neg)
    eidx = grp * GROUP_SIZE + in_grp
    wsel = jnp.zeros(shape3, f32)
    mask = jnp.zeros(shape3, f32)
    hits = []
    for _ in range(TOP_K):
        mx = jnp.max(jnp.max(cand, axis=1, keepdims=True), axis=0, keepdims=True)
        ii = jnp.min(jnp.min(jnp.where(cand == mx, eidx, N_EXPERTS), axis=1, keepdims=True),
                     axis=0, keepdims=True)
        hit = eidx == ii
        hits.append(hit)
        wsel = jnp.where(hit, s3, wsel)
        mask = jnp.where(hit, 1.0, mask)
        cand = jnp.where(hit, neg, cand)
    den = jnp.sum(jnp.sum(wsel, axis=1, keepdims=True), axis=0, keepdims=True)
    gates3 = wsel / den * ROUTED_SCALE

    mask2 = mask.reshape(N_EXPERTS, tm)
    cnt = jnp.sum(mask2, axis=1, keepdims=True)
    cnt_pad = jnp.floor((cnt + (SEG - 1)) * (1.0 / SEG)) * SEG
    er = lax.broadcasted_iota(jnp.int32, (N_EXPERTS, N_EXPERTS), 0)
    ec = lax.broadcasted_iota(jnp.int32, (N_EXPERTS, N_EXPERTS), 1)
    before = jnp.where(ec < er, 1.0, 0.0).astype(bf16)
    cnt_pad_l = jnp.broadcast_to(cnt_pad, (N_EXPERTS, LANE))
    seg_start = _dot(before, cnt_pad_l.astype(bf16))
    tr = lax.broadcasted_iota(jnp.int32, (tm, tm), 0)
    tc = lax.broadcasted_iota(jnp.int32, (tm, tm), 1)
    earlier = jnp.where(tr < tc, 1.0, 0.0).astype(bf16)
    rank = _dot(mask2.astype(bf16), earlier)
    slot3 = (rank + seg_start[:, 0:1]).reshape(shape3)

    def pick(hit, val3):
        return jnp.sum(jnp.sum(jnp.where(hit, val3, 0.0), axis=1, keepdims=True), axis=0)

    slots = [pick(h, slot3) for h in hits]
    wts = [pick(h, gates3) for h in hits]
    pad_rows = lambda k, val: [jnp.full((k, tm), val, f32)]
    slot_ref[...] = jnp.concatenate(slots + pad_rows(8 - TOP_K, -1.0), axis=0).astype(jnp.int32)
    sw_t = jnp.concatenate(slots + pad_rows(8 - TOP_K, -1.0) + wts + pad_rows(LANE - 8 - TOP_K, 0.0), axis=0)
    sw_ref[...] = sw_t.T
    cnt_ref[...] = cnt_pad_l


def _mix_out(attn, o_fw, o_bw, bonus, g, x2, mod3, mod_row, p, tm):
    n = x2.shape[0]
    w = RWKV_WIDTH
    consts = [p['lnx_g'], p['lnx_b'], p['seg'], p['w_out'], p['g_post_mix'], p['g_pre_ffn'],
              p['router_wt'], p['router_b']]
    row = lambda wd: pl.BlockSpec((tm, wd), lambda i: (i, 0))
    in_specs = [row(ATTN_HEADS * HEAD_DIM), row(w), row(w),
                row(w), row(w), row(D_MODEL),
                pl.BlockSpec((None, 1, 6 * D_MODEL), lambda i: (mod_row(i, tm), 0, 0))]
    in_specs += [_full_spec(c) for c in consts]
    out_shape = [jax.ShapeDtypeStruct((n, D_MODEL), f32),
                 jax.ShapeDtypeStruct((n, D_MODEL), bf16),
                 jax.ShapeDtypeStruct((8, n), jnp.int32),
                 jax.ShapeDtypeStruct((n, LANE), f32),
                 jax.ShapeDtypeStruct((n // tm, N_EXPERTS, LANE), f32)]
    out_specs = [row(D_MODEL), row(D_MODEL), pl.BlockSpec((8, tm), lambda i: (0, i)), row(LANE),
                 pl.BlockSpec((None, N_EXPERTS, LANE), lambda i: (i, 0, 0))]
    return pl.pallas_call(
        _out_kernel, grid=(n // tm,), in_specs=in_specs, out_specs=out_specs, out_shape=out_shape,
        compiler_params=_cparams("parallel"), name="mix_out",
    )(attn, o_fw, o_bw, bonus, g, x2, mod3, *consts)


def _chunk_groups(tile, nch_sm):
    return lax.shift_right_logical(nch_sm[tile] + (CHUNK_GROUP - 1), CHUNK_GROUP_SHIFT)


def _start_chunks(tile, dst_sm, nch_sm, make_copy):
    def body(g, carry):
        for u in range(CHUNK_GROUP):
            c = g * CHUNK_GROUP + u
            make_copy(c, pl.multiple_of(dst_sm[tile * MAX_CHUNKS + c], SEG)).start()
        return carry

    lax.fori_loop(0, _chunk_groups(tile, nch_sm), body, 0)


def _wait_chunks(tile, nch_sm, make_copy):
    def body(g, carry):
        for _ in range(CHUNK_GROUP):
            make_copy(0, 0).wait()
        return carry

    lax.fori_loop(0, _chunk_groups(tile, nch_sm), body, 0)


def _dispatch_kernel(dst_sm, nch_sm, fill_sm, h_ref, slot_ref, xs_hbm, xc_ref, zero_ref, sems, zsem):
    i = pl.program_id(0)
    last = pl.num_programs(0) - 1
    buf = lax.rem(i, 2)

    def copy_from(b):
        def make(chunk, global_row):
            src = xc_ref.at[b, pl.ds(pl.multiple_of(chunk * SEG, SEG), SEG)]
            return pltpu.make_async_copy(src, xs_hbm.at[pl.ds(global_row, SEG)], sems.at[b])
        return make

    @pl.when(i == 0)
    def _():
        zero_ref[...] = jnp.zeros_like(zero_ref)
        n_tiles = xs_hbm.shape[0] // FFN_ROWS

        def tail_copy(row):
            return pltpu.make_async_copy(zero_ref.at[pl.ds(0, SEG)], xs_hbm.at[pl.ds(row, SEG)], zsem)

        def tile_copy(t):
            return pltpu.make_async_copy(zero_ref, xs_hbm.at[pl.ds(pl.multiple_of(t * FFN_ROWS, FFN_ROWS), FFN_ROWS)],
                                         zsem)

        def per_expert(e, carry):
            start = fill_sm[1 + e]
            chunks = lax.shift_right_logical(fill_sm[1 + N_EXPERTS + e], SEG_SHIFT)
            lax.fori_loop(0, chunks, lambda c, z: (tail_copy(pl.multiple_of(start + c * SEG, SEG)).start(), z)[1], 0)
            lax.fori_loop(0, chunks, lambda c, z: (tail_copy(0).wait(), z)[1], 0)
            return carry

        lax.fori_loop(0, N_EXPERTS, per_expert, 0)
        lax.fori_loop(fill_sm[0], n_tiles, lambda t, z: (tile_copy(t).start(), z)[1], 0)
        lax.fori_loop(fill_sm[0], n_tiles, lambda t, z: (tile_copy(0).wait(), z)[1], 0)

    @pl.when(i >= 2)
    def _():
        _wait_chunks(i - 2, nch_sm, copy_from(buf))

    slot = slot_ref[...]
    h = h_ref[...]
    used = nch_sm[i] * SEG
    def build(rb):
        rows = lax.broadcasted_iota(jnp.int32, (SEL_BLOCK, TM_MOE), 0) + rb * SEL_BLOCK
        sel = jnp.zeros((SEL_BLOCK, TM_MOE), f32)
        for j in range(TOP_K):
            sel = jnp.where(rows == slot[j:j + 1, :], 1.0, sel)
        xc_ref[buf, rb * SEL_BLOCK:(rb + 1) * SEL_BLOCK, :] = _dot(sel.astype(bf16), h).astype(bf16)

    for rb in range(COMMON_BLOCKS):
        build(rb)
    for rb in range(COMMON_BLOCKS, SLOT_ROWS // SEL_BLOCK):
        pl.when(rb * SEL_BLOCK < used)(functools.partial(build, rb))

    _start_chunks(i, dst_sm, nch_sm, copy_from(buf))

    @pl.when(i == last)
    def _():
        @pl.when(i >= 1)
        def _():
            _wait_chunks(i - 1, nch_sm, copy_from(1 - buf))
        _wait_chunks(i, nch_sm, copy_from(buf))


def _dispatch(h2, slots, dst, nch, fill, total_rows):
    n = h2.shape[0]
    grid_spec = pltpu.PrefetchScalarGridSpec(
        num_scalar_prefetch=3, grid=(n // TM_MOE,),
        in_specs=[pl.BlockSpec((TM_MOE, D_MODEL), lambda i, d, c, f: (i, 0)),
                  pl.BlockSpec((8, TM_MOE), lambda i, d, c, f: (0, i))],
        out_specs=pl.BlockSpec(memory_space=pl.ANY),
        scratch_shapes=[pltpu.VMEM((2, SLOT_ROWS, D_MODEL), bf16), pltpu.VMEM((FFN_ROWS, D_MODEL), bf16),
                        pltpu.SemaphoreType.DMA((2,)), pltpu.SemaphoreType.DMA(())])
    return pl.pallas_call(
        _dispatch_kernel, grid_spec=grid_spec,
        out_shape=jax.ShapeDtypeStruct((total_rows, D_MODEL), bf16),
        compiler_params=_cparams("arbitrary"), name="moe_dispatch",
    )(dst, nch, fill, h2, slots)


def _ffn_kernel(te_sm, nv_sm, xs_ref, wg_ref, wu_ref, wd_ref, ys_ref):
    r = pl.program_id(0)

    @pl.when(r < nv_sm[0])
    def _():
        x = xs_ref[...]
        gg = _dot(x, wg_ref[...].astype(bf16))
        uu = _dot(x, wu_ref[...].astype(bf16))
        hm = gg * _sigmoid(gg) * uu
        ys_ref[...] = _dot(hm.astype(bf16), wd_ref[...].astype(bf16)).astype(bf16)

    @pl.when(r >= nv_sm[0])
    def _():
        ys_ref[...] = jnp.zeros_like(ys_ref)


def _expert_ffn(xs, tile_expert, n_valid, wg, wu, wd):
    total_rows = xs.shape[0]
    last = lambda r, nv: jnp.minimum(r, nv[0] - 1)
    grid_spec = pltpu.PrefetchScalarGridSpec(
        num_scalar_prefetch=2, grid=(total_rows // FFN_ROWS,),
        in_specs=[pl.BlockSpec((FFN_ROWS, D_MODEL), lambda r, te, nv: (last(r, nv), 0)),
                  pl.BlockSpec((None, D_MODEL, EXPERT_FF), lambda r, te, nv: (te[last(r, nv)], 0, 0)),
                  pl.BlockSpec((None, D_MODEL, EXPERT_FF), lambda r, te, nv: (te[last(r, nv)], 0, 0)),
                  pl.BlockSpec((None, EXPERT_FF, D_MODEL), lambda r, te, nv: (te[last(r, nv)], 0, 0))],
        out_specs=pl.BlockSpec((FFN_ROWS, D_MODEL), lambda r, te, nv: (r, 0)))
    return pl.pallas_call(
        _ffn_kernel, grid_spec=grid_spec,
        out_shape=jax.ShapeDtypeStruct((total_rows, D_MODEL), bf16),
        compiler_params=_cparams("arbitrary"), name="moe_ffn",
    )(tile_expert, n_valid, xs, wg, wu, wd)


def _combine_kernel(dst_sm, nch_sm, sw_ref, h_ref, x1_ref, mod_ref, gpost_ref, swg_ref, swu_ref, swd_ref,
                    ys_hbm, y_ref, yc_ref, acc_ref, sems, *, tile_offset):
    step = pl.program_id(0)
    i = step + tile_offset
    buf = lax.rem(step, 2)

    def copy_into(b):
        def make(chunk, global_row):
            dst = yc_ref.at[b, pl.ds(pl.multiple_of(chunk * SEG, SEG), SEG)]
            return pltpu.make_async_copy(ys_hbm.at[pl.ds(global_row, SEG)], dst, sems.at[b])
        return make

    @pl.when(step == 0)
    def _():
        _start_chunks(i, dst_sm, nch_sm, copy_into(buf))

    @pl.when(step + 1 < pl.num_programs(0))
    def _():
        _start_chunks(i + 1, dst_sm, nch_sm, copy_into(1 - buf))

    h = h_ref[...]
    gg = _dot(h, swg_ref[...])
    uu = _dot(h, swu_ref[...])
    acc_ref[...] = _dot((gg * _sigmoid(gg) * uu).astype(bf16), swd_ref[...])
    _wait_chunks(i, nch_sm, copy_into(buf))
    sw = sw_ref[...]
    used = nch_sm[i] * SEG
    def weighted(rb):
        cols = (lax.broadcasted_iota(jnp.int32, (TM_MOE, SEL_BLOCK), 1) + rb * SEL_BLOCK).astype(f32)
        wmat = jnp.zeros((TM_MOE, SEL_BLOCK), f32)
        for j in range(TOP_K):
            wmat = jnp.where(cols == sw[:, j:j + 1], sw[:, 8 + j:9 + j], wmat)
        rows = lax.broadcasted_iota(jnp.int32, (SEL_BLOCK, 1), 0) + rb * SEL_BLOCK
        yc = jnp.where(rows < used, yc_ref[buf, rb * SEL_BLOCK:(rb + 1) * SEL_BLOCK, :], jnp.zeros((), bf16))
        return _dot(wmat.astype(bf16), yc)

    moe = acc_ref[...]
    for rb in range(COMMON_BLOCKS):
        moe = moe + weighted(rb)
    acc_ref[...] = moe
    for rb in range(COMMON_BLOCKS, SLOT_ROWS // SEL_BLOCK):
        @pl.when(rb * SEL_BLOCK < used)
        def _():
            acc_ref[...] += weighted(rb)
    gt_f = mod_ref[...][:, 5 * D_MODEL:6 * D_MODEL]
    y_ref[...] = x1_ref[...] + gt_f * _rms(acc_ref[...], gpost_ref[...])


def _combine(ys, dst, nch, sw, h2, x1, mod3, mod_row, g_post, swg, swu, swd, tile_offset):
    n = h2.shape[0]
    row = lambda wd: pl.BlockSpec((TM_MOE, wd), lambda i, d, c: (i, 0))
    const = lambda a: pl.BlockSpec(a.shape, lambda i, d, c: (0,) * a.ndim)
    grid_spec = pltpu.PrefetchScalarGridSpec(
        num_scalar_prefetch=2, grid=(n // TM_MOE,),
        in_specs=[row(LANE), row(D_MODEL), row(D_MODEL),
                  pl.BlockSpec((None, 1, 6 * D_MODEL), lambda i, d, c: (mod_row(i, TM_MOE), 0, 0)),
                  const(g_post), const(swg), const(swu), const(swd),
                  pl.BlockSpec(memory_space=pl.ANY)],
        out_specs=row(D_MODEL),
        scratch_shapes=[pltpu.VMEM((2, SLOT_ROWS, D_MODEL), bf16), pltpu.VMEM((TM_MOE, D_MODEL), f32),
                        pltpu.SemaphoreType.DMA((2,))])
    return pl.pallas_call(
        functools.partial(_combine_kernel, tile_offset=tile_offset), grid_spec=grid_spec,
        out_shape=jax.ShapeDtypeStruct((n, D_MODEL), f32),
        compiler_params=_cparams("arbitrary"), name="moe_combine",
    )(dst, nch, sw, h2, x1, mod3, g_post, swg, swu, swd, ys)


def _sparse_moe(paths, mod3, p):
    cnt = jnp.concatenate([q['cnt'][:, :, 0] for q in paths], axis=0).astype(jnp.int32)
    n_tiles = cnt.shape[0]
    before = jnp.cumsum(cnt, axis=0) - cnt
    rows_e = jnp.sum(cnt, axis=0)
    region = (rows_e + FFN_ROWS - 1) // FFN_ROWS * FFN_ROWS
    region_end = jnp.cumsum(region)
    region_start = region_end - region
    worst = n_tiles * (TOP_K * TM_MOE + N_EXPERTS * (SEG - 1)) + N_EXPERTS * (FFN_ROWS - SEG)
    total_rows = -(-worst // FFN_ROWS) * FFN_ROWS + FFN_ROWS
    spare_row = total_rows - FFN_ROWS
    first_row = jnp.arange(total_rows // FFN_ROWS, dtype=jnp.int32) * FFN_ROWS
    ended = jnp.sum((region_end[None, :] <= first_row[:, None]).astype(jnp.int32), axis=1)
    tile_expert = jnp.minimum(ended, N_EXPERTS - 1).astype(jnp.int32)
    n_valid = (region_end[-1:] // FFN_ROWS).astype(jnp.int32)

    seg_end = jnp.cumsum(cnt, axis=1)
    seg_start = seg_end - cnt
    nch = (seg_end[:, -1] // SEG).astype(jnp.int32)
    chunk_row = jnp.arange(MAX_CHUNKS, dtype=jnp.int32) * SEG
    chunk_e = jnp.sum((seg_end[:, None, :] <= chunk_row[None, :, None]).astype(jnp.int32), axis=2)
    own = chunk_e[:, :, None] == jnp.arange(N_EXPERTS, dtype=jnp.int32)[None, None, :]
    shift = (region_start[None, :] + before - seg_start)[:, None, :]
    dst = jnp.sum(jnp.where(own, shift, 0), axis=2) + chunk_row[None, :]
    chunk_id = jnp.arange(MAX_CHUNKS, dtype=jnp.int32)[None, :]
    tile_id = jnp.arange(n_tiles, dtype=jnp.int32)[:, None]
    spare = spare_row + ((tile_id % 2) * CHUNK_GROUP + chunk_id % CHUNK_GROUP) * SEG
    dst = jnp.where(chunk_id < nch[:, None], dst, spare).astype(jnp.int32).reshape(-1)

    h2_all = jnp.concatenate([q['h2'] for q in paths], axis=0)
    slots_all = jnp.concatenate([q['slots'] for q in paths], axis=1)
    fill = jnp.concatenate([n_valid, region_start + rows_e, region - rows_e]).astype(jnp.int32)
    xs = _dispatch(h2_all, slots_all, dst, nch, fill, total_rows)
    ys = _expert_ffn(xs, tile_expert, n_valid, p['expert_wg'], p['expert_wu'], p['expert_wd'])
    outs, tile_offset = [], 0
    for q in paths:
        outs.append(_combine(ys, dst, nch, q['sw'], q['h2'], q['x1'], mod3, q['mod_row'],
                             p['g_post_ffn'], p['shared_wg'], p['shared_wu'], p['shared_wd'], tile_offset))
        tile_offset += q['h2'].shape[0] // TM_MOE
    return outs


def _pad_heads(w, heads):
    rows = w.shape[0]
    w = w.reshape(rows, heads, HEAD_DIM)
    return jnp.pad(w, ((0, 0), (0, 0), (0, HEAD_PAD - HEAD_DIM))).reshape(rows, heads * HEAD_PAD)


def _block_diag2(m):
    z = jnp.zeros_like(m[0])
    return jnp.concatenate([jnp.concatenate([m[0], z], axis=1), jnp.concatenate([z, m[1]], axis=1)], axis=0)


def _prepare_params(w):
    aw, kw = ATTN_HEADS * HEAD_DIM, KV_HEADS * HEAD_DIM
    w_in = w['w_in']
    lane_row = lambda v: v.reshape(1, -1).astype(f32)
    head_id = np.arange(RWKV_WIDTH) // RWKV_HEAD
    seg = jnp.asarray(head_id[:, None] == head_id[None, :], dtype=bf16)
    gain_pad = lambda gvec: jnp.pad(gvec, (0, HEAD_PAD - HEAD_DIM)).reshape(1, HEAD_PAD)
    return dict(
        g_pre_mix=lane_row(w['g_pre_mix']), g_post_mix=lane_row(w['g_post_mix']),
        g_pre_ffn=lane_row(w['g_pre_ffn']), g_post_ffn=lane_row(w['g_post_ffn']),
        wq=_pad_heads(w_in[:, :aw], ATTN_HEADS).astype(bf16),
        wk=_pad_heads(w_in[:, aw:aw + kw], KV_HEADS).astype(bf16),
        wv=w_in[:, aw + kw:aw + 2 * kw].astype(bf16),
        wr=w_in[:, aw + 2 * kw:].astype(bf16),
        qg=gain_pad(w['q_gain'] * ATTN_SCALE), kg=gain_pad(w['k_gain']),
        mu_prev=lane_row(w['mu_prev']), mu_next=lane_row(w['mu_next']),
        w0=lane_row(w['decay_w0']), w2bd=_block_diag2(w['decay_w2']),
        a0=lane_row(w['iclr_a0']), a2bd=_block_diag2(w['iclr_a2']),
        g2=w['gate_g2'], k_k=lane_row(w['k_k']), k_a=lane_row(w['k_a']), r_k=lane_row(w['r_k']),
        lnx_g=lane_row(w['lnx_g']), lnx_b=lane_row(w['lnx_b']), seg=seg,
        w_out=w['w_out'].astype(bf16),
        router_wt=w['router_w'].T, router_b=w['router_b'].reshape(N_EXPERTS, 1),
        expert_wg=w['expert_wg'], expert_wu=w['expert_wu'], expert_wd=w['expert_wd'],
        shared_wg=w['shared_wg'].astype(bf16), shared_wu=w['shared_wu'].astype(bf16),
        shared_wd=w['shared_wd'].astype(bf16),
    )


def _rope_tables(seq_len):
    rows = seq_len // GRID_W
    row_idx = jnp.repeat(jnp.arange(rows, dtype=jnp.int32), GRID_W)
    col_idx = jnp.tile(jnp.arange(GRID_W, dtype=jnp.int32), rows)
    inv = ROPE_THETA ** (-jnp.arange(0, ROPE_HALF, 2, dtype=f32) / ROPE_HALF)
    ang = jnp.stack([row_idx.astype(f32)[:, None] * inv, col_idx.astype(f32)[:, None] * inv], axis=1)
    cos, sin = jnp.cos(ang), jnp.sin(ang)
    cos_l = jnp.stack([cos, cos], axis=2).reshape(seq_len, HEAD_DIM)
    sin_l = jnp.stack([-sin, sin], axis=2).reshape(seq_len, HEAD_DIM)
    pad = ((0, 0), (0, HEAD_PAD - HEAD_DIM))
    return jnp.pad(cos_l, pad), jnp.pad(sin_l, pad)


def _states_to_pairs(s):
    b = s.shape[0]
    s = s.reshape(b, 2, N_QUADS, QUAD_HEADS, RWKV_HEAD, RWKV_HEAD)
    eye = jnp.eye(QUAD_HEADS, dtype=s.dtype)
    return jnp.einsum('bdphvk,hg->bdphkgv', s, eye).reshape(b, 2, N_QUADS, QUAD, QUAD)


def _pairs_to_states(s):
    b = s.shape[0]
    s = s.reshape(b, 2, N_QUADS, QUAD_HEADS, RWKV_HEAD, QUAD_HEADS, RWKV_HEAD)
    eye = jnp.eye(QUAD_HEADS, dtype=s.dtype)
    return jnp.einsum('bdphkgv,hg->bdphvk', s, eye).reshape(b, 2, RWKV_HEADS, RWKV_HEAD, RWKV_HEAD)


def _trunk(x, mod3, mod_row, p, rope_tabs, past_k, past_v, s0_pairs, tm_in, tq, tm_prep):
    b, t, _ = x.shape
    n = b * t
    x2 = x.reshape(n, D_MODEL)
    q_pad, k_own, v_own, k_pad, zr = _in_proj(
        x2, mod3, mod_row, p['g_pre_mix'], p['wq'], p['wk'], p['wv'], p['wr'], p['qg'], p['kg'],
        rope_tabs, t, tm_in)
    k_full = k_pad.reshape(b, t, KV_HEADS * HEAD_PAD)
    v_full = v_own.reshape(b, t, KV_HEADS * HEAD_DIM)
    if past_k is not None:
        k_past = _pad_heads(past_k.reshape(-1, KV_HEADS * HEAD_DIM), KV_HEADS).astype(bf16)
        k_full = jnp.concatenate([k_past.reshape(b, -1, KV_HEADS * HEAD_PAD), k_full], axis=1)
        v_full = jnp.concatenate([past_v.reshape(b, -1, KV_HEADS * HEAD_DIM), v_full], axis=1)
    vt_full = jnp.swapaxes(v_full, 1, 2).astype(bf16)
    attn = _attention(q_pad, k_full, vt_full, b, t, tq)
    r, k, v, kk, lw, a, g, bonus = _rwkv_prep(zr, p, t, tm_prep)
    o_fw, o_bw, s_fin = _rwkv_scan(r, k, v, kk, lw, a, p['k_a'], s0_pairs, b, t)
    x1, h2, slots, sw, cnt = _mix_out(attn, o_fw, o_bw, bonus, g, x2, mod3, mod_row, p, TM_MOE)
    return dict(h2=h2, slots=slots, sw=sw, cnt=cnt, x1=x1, mod_row=mod_row), k_own, v_own, s_fin


def kernel(x_prompt, x_sample, c, cache_k, cache_v, state_rwkv, c_ctx, w_ada, b_ada, g_pre_mix, g_post_mix,
           g_pre_ffn, g_post_ffn, w_in, w_out, q_gain, k_gain, mu_prev, mu_next, decay_w0, decay_w2, iclr_a0,
           iclr_a2, gate_g2, k_k, k_a, r_k, lnx_g, lnx_b, router_w, router_b, expert_wg, expert_wu, expert_wd,
           shared_wg, shared_wu, shared_wd):
    layer = 0
    names = ('g_pre_mix g_post_mix g_pre_ffn g_post_ffn w_in w_out q_gain k_gain mu_prev mu_next decay_w0 '
             'decay_w2 iclr_a0 iclr_a2 gate_g2 k_k k_a r_k lnx_g lnx_b router_w router_b expert_wg expert_wu '
             'expert_wd shared_wg shared_wu shared_wd').split()
    vals = (g_pre_mix, g_post_mix, g_pre_ffn, g_post_ffn, w_in, w_out, q_gain, k_gain, mu_prev, mu_next,
            decay_w0, decay_w2, iclr_a0, iclr_a2, gate_g2, k_k, k_a, r_k, lnx_g, lnx_b, router_w, router_b,
            expert_wg, expert_wu, expert_wd, shared_wg, shared_wu, shared_wd)
    p = _prepare_params({nm: v[layer] for nm, v in zip(names, vals)})

    nb, ts, _ = x_sample.shape
    npb, tp, _ = x_prompt.shape
    mod_rows = 16
    cc = jnp.zeros((mod_rows, D_MODEL), f32).at[0].set(c_ctx).at[1:1 + nb].set(c)
    mod3 = _modulation(cc, w_ada[layer], b_ada[layer]).reshape(mod_rows, 1, 6 * D_MODEL)

    zeros_state = jnp.zeros((npb, 2, N_QUADS, QUAD, QUAD), f32)
    moe_p, kc, vc, st = _trunk(x_prompt, mod3, lambda i, tm: 0, p, None, None, None, zeros_state,
                               tm_in=256, tq=tp, tm_prep=256)
    s0_lat = _states_to_pairs(state_rwkv[:, layer])
    moe_s, _, _, _ = _trunk(x_sample, mod3, lambda i, tm: 1 + (i * tm) // ts, p, _rope_tables(ts),
                            cache_k[:, layer], cache_v[:, layer], s0_lat,
                            tm_in=512, tq=256, tm_prep=256)
    y_p, y_s = _sparse_moe([moe_p, moe_s], mod3, p)
    y_p = y_p.reshape(x_prompt.shape)
    y_s = y_s.reshape(x_sample.shape)

    new_cache_k = kc.reshape(npb, 1, tp, KV_HEADS, HEAD_DIM)
    new_cache_v = vc.reshape(npb, 1, tp, KV_HEADS, HEAD_DIM)
    new_state = _pairs_to_states(st)[:, None]
    return (y_p, y_s, new_cache_k, new_cache_v, new_state)
```

```python
import functools

import numpy as np
import jax
import jax.numpy as jnp
from jax import lax
from jax.experimental import pallas as pl
from jax.experimental.pallas import tpu as pltpu

f32 = jnp.float32
bf16 = jnp.bfloat16

D_MODEL = 1024
GRID_W = 64
HEAD_DIM = 64
ATTN_HEADS = 8
KV_HEADS = 2
GQA_GROUP = ATTN_HEADS // KV_HEADS
ATTN_SCALE = HEAD_DIM ** -0.5
ROPE_THETA = 10000.0
ROPE_HALF = HEAD_DIM // 2
ROPE_FREQS = ROPE_HALF // 2
RWKV_HEAD = 64
RWKV_HEADS = 8
RWKV_WIDTH = RWKV_HEADS * RWKV_HEAD
DECAY_RANK = 64
ICLR_RANK = 64
GATE_RANK = 128
RWKV_IN = 3 * RWKV_WIDTH + 2 * DECAY_RANK + 2 * ICLR_RANK + GATE_RANK
DECAY_SCALE = 0.606531
GN_EPS = 64e-5
N_EXPERTS = 64
TOP_K = 6
N_GROUPS = 8
GROUP_SIZE = N_EXPERTS // N_GROUPS
TOPK_GROUPS = 4
EXPERT_FF = 256
ROUTED_SCALE = 2.5
EPS = 1e-6

LANE = 128
HEAD_PAD = LANE
CHUNK = 64
QUAD_HEADS = 4
QUAD = QUAD_HEADS * RWKV_HEAD
N_QUADS = RWKV_HEADS // QUAD_HEADS
SCAN_SEQS = 4
VMEM_LIMIT = 56 * 1024 * 1024
SCAN_PASSES = 1
TM_MOE = 256
MIX_TILES = 1
SEG_SHIFT = 4
SEG = 1 << SEG_SHIFT
SLOT_ROWS = 2560
MAX_CHUNKS = SLOT_ROWS // SEG
CHUNK_GROUP_SHIFT = 3
CHUNK_GROUP = 1 << CHUNK_GROUP_SHIFT
SEL_BLOCK = 512
COMMON_BLOCKS = 4
FFN_ROWS = 512


def _cparams(*sem):
    return pltpu.CompilerParams(dimension_semantics=sem, vmem_limit_bytes=VMEM_LIMIT)


def _dot(a, b, nt=False):
    dims = (((1,), (1,)), ((), ())) if nt else (((1,), (0,)), ((), ()))
    return lax.dot_general(a, b, dims, preferred_element_type=f32)


def _split2(x):
    hi = x.astype(bf16)
    lo = (x - hi.astype(f32)).astype(bf16)
    return hi, lo


def _split3(x):
    h1 = x.astype(bf16)
    r1 = x - h1.astype(f32)
    h2 = r1.astype(bf16)
    h3 = (r1 - h2.astype(f32)).astype(bf16)
    return h1, h2, h3


def _mm(a, b, nt=False, passes=3):
    if passes == 1:
        return _dot(a.astype(bf16), b.astype(bf16), nt)
    ah, al = _split2(a)
    bh, bl = _split2(b)
    return _dot(ah, bh, nt) + (_dot(al, bh, nt) + _dot(ah, bl, nt))


def _mm_exact_rhs(a, b_bf16, nt=False):
    ah, al = _split2(a)
    return _dot(ah, b_bf16, nt) + _dot(al, b_bf16, nt)


def _rms(x, g):
    return x * lax.rsqrt(jnp.mean(x * x, axis=-1, keepdims=True) + EPS) * g


def _sigmoid(x):
    return jax.nn.sigmoid(x)


def _full_spec(a, grid_rank=1):
    zeros = (0,) * a.ndim
    if grid_rank == 1:
        return pl.BlockSpec(a.shape, lambda i: zeros)
    if grid_rank == 2:
        return pl.BlockSpec(a.shape, lambda i, j: zeros)
    return pl.BlockSpec(a.shape, lambda i, j, k: zeros)


def _mod_kernel(c_ref, w_ref, b_ref, o_ref):
    c = c_ref[...]
    s = c * _sigmoid(c)
    o_ref[...] = _mm(s, w_ref[...]) + b_ref[...]


def _modulation(cc, w_ada, b_ada):
    rows, n = cc.shape[0], w_ada.shape[1]
    tn = 512
    return pl.pallas_call(
        _mod_kernel,
        grid=(n // tn,),
        in_specs=[pl.BlockSpec((rows, D_MODEL), lambda j: (0, 0)),
                  pl.BlockSpec((D_MODEL, tn), lambda j: (0, j)),
                  pl.BlockSpec((1, tn), lambda j: (0, j))],
        out_specs=pl.BlockSpec((rows, tn), lambda j: (0, j)),
        out_shape=jax.ShapeDtypeStruct((rows, n), f32),
        compiler_params=_cparams("parallel"),
        name="modulation",
    )(cc, w_ada, b_ada.reshape(1, n))


def _in_kernel(*refs, rope):
    if rope:
        (x_ref, mod_ref, g_ref, wq_ref, wk_ref, wv_ref, wr_ref, qg_ref, kg_ref, cos_ref, sin_ref,
         q_ref, ko_ref, vo_ref, kp_ref, zr_ref) = refs
    else:
        (x_ref, mod_ref, g_ref, wq_ref, wk_ref, wv_ref, wr_ref, qg_ref, kg_ref,
         q_ref, ko_ref, vo_ref, kp_ref, zr_ref) = refs
    x = x_ref[...]
    m = mod_ref[...]
    h = _rms(x, g_ref[...])
    h = h * (1.0 + m[:, D_MODEL:2 * D_MODEL]) + m[:, 0:D_MODEL]
    hb = h.astype(bf16)
    zr_ref[...] = _dot(hb, wr_ref[...])
    zq = _dot(hb, wq_ref[...])
    zk = _dot(hb, wk_ref[...])
    zv = _dot(hb, wv_ref[...])

    lane = lax.broadcasted_iota(jnp.int32, (1, LANE), 1)
    first_half = (lane & (ROPE_HALF - 1)) < ROPE_FREQS

    def head_norm(z, gain):
        ms = jnp.sum(z * z, axis=-1, keepdims=True) * (1.0 / HEAD_DIM)
        return z * lax.rsqrt(ms + EPS) * gain

    def rotate(y):
        if not rope:
            return y
        partner = jnp.where(first_half, pltpu.roll(y, LANE - ROPE_FREQS, 1), pltpu.roll(y, ROPE_FREQS, 1))
        return y * cos_ref[...] + partner * sin_ref[...]

    for hh in range(ATTN_HEADS):
        sl = slice(hh * HEAD_PAD, (hh + 1) * HEAD_PAD)
        q_ref[:, sl] = rotate(head_norm(zq[:, sl], qg_ref[...])).astype(bf16)
    kn = [head_norm(zk[:, hh * HEAD_PAD:(hh + 1) * HEAD_PAD], kg_ref[...]) for hh in range(KV_HEADS)]
    ko_ref[...] = kn[0] + pltpu.roll(kn[1], HEAD_DIM, 1)
    vo_ref[...] = zv
    for hh in range(KV_HEADS):
        sl = slice(hh * HEAD_PAD, (hh + 1) * HEAD_PAD)
        kp_ref[:, sl] = rotate(kn[hh]).astype(bf16)


def _in_proj(x2, mod3, mod_row, g_pre, wq, wk, wv, wr, qg, kg, rope_tabs, seq_len, tm):
    n = x2.shape[0]
    rope = rope_tabs is not None
    in_specs = [pl.BlockSpec((tm, D_MODEL), lambda i: (i, 0)),
                pl.BlockSpec((None, 1, 6 * D_MODEL), lambda i: (mod_row(i, tm), 0, 0)),
                _full_spec(g_pre), _full_spec(wq), _full_spec(wk), _full_spec(wv), _full_spec(wr),
                _full_spec(qg), _full_spec(kg)]
    args = [x2, mod3, g_pre, wq, wk, wv, wr, qg, kg]
    if rope:
        blocks_per_seq = seq_len // tm
        in_specs += [pl.BlockSpec((tm, LANE), lambda i: (i % blocks_per_seq, 0))] * 2
        args += list(rope_tabs)
    out_shape = [jax.ShapeDtypeStruct((n, ATTN_HEADS * HEAD_PAD), bf16),
                 jax.ShapeDtypeStruct((n, KV_HEADS * HEAD_DIM), f32),
                 jax.ShapeDtypeStruct((n, KV_HEADS * HEAD_DIM), f32),
                 jax.ShapeDtypeStruct((n, KV_HEADS * HEAD_PAD), bf16),
                 jax.ShapeDtypeStruct((n, RWKV_IN), f32)]
    out_specs = [pl.BlockSpec((tm, s.shape[1]), lambda i: (i, 0)) for s in out_shape]
    return pl.pallas_call(
        functools.partial(_in_kernel, rope=rope),
        grid=(n // tm,), in_specs=in_specs, out_specs=out_specs, out_shape=out_shape,
        compiler_params=_cparams("parallel"), name="in_proj",
    )(*args)


def _attn_kernel(q_ref, k_ref, vt_ref, o_ref):
    tq = q_ref.shape[0]
    heads_t = []
    for kv in range(KV_HEADS):
        kh = k_ref[:, kv * HEAD_PAD:(kv + 1) * HEAD_PAD]
        vt = vt_ref[kv * HEAD_DIM:(kv + 1) * HEAD_DIM, :]
        q4 = jnp.concatenate(
            [q_ref[:, (GQA_GROUP * kv + g) * HEAD_PAD:(GQA_GROUP * kv + g + 1) * HEAD_PAD]
             for g in range(GQA_GROUP)], axis=0)
        st = _dot(kh, q4, nt=True)
        mx = jnp.max(st, axis=0, keepdims=True)
        p = jnp.exp(st - mx)
        den = jnp.sum(p, axis=0, keepdims=True)
        ot = _dot(vt, p.astype(bf16)) / den
        heads_t += [ot[:, g * tq:(g + 1) * tq] for g in range(GQA_GROUP)]
    o_ref[...] = jnp.concatenate(heads_t, axis=0).T


def _attention(q_pad, k_full, vt_full, batch, seq_len, tq):
    n = q_pad.shape[0]
    kv_len = k_full.shape[1]
    nq = seq_len // tq
    return pl.pallas_call(
        _attn_kernel,
        grid=(batch, nq),
        in_specs=[pl.BlockSpec((tq, ATTN_HEADS * HEAD_PAD), lambda b, j: (b * nq + j, 0)),
                  pl.BlockSpec((None, kv_len, KV_HEADS * HEAD_PAD), lambda b, j: (b, 0, 0)),
                  pl.BlockSpec((None, KV_HEADS * HEAD_DIM, kv_len), lambda b, j: (b, 0, 0))],
        out_specs=pl.BlockSpec((tq, ATTN_HEADS * HEAD_DIM), lambda b, j: (b * nq + j, 0)),
        out_shape=jax.ShapeDtypeStruct((n, ATTN_HEADS * HEAD_DIM), f32),
        compiler_params=_cparams("parallel", "parallel"), name="attention",
    )(q_pad, k_full, vt_full)


def _prep_kernel(z_ref, zp_ref, zn_ref, mup_ref, mun_ref, w0_ref, w2_ref, a0_ref, a2_ref, g2_ref,
                 kkw_ref, ka_ref, rk_ref, seg_ref,
                 r_ref, k_ref, v_ref, kk_ref, lw_ref, a_ref, g_ref, bonus_ref, *, seq_len, tm):
    i = pl.program_id(0)
    z = z_ref[...]
    row = lax.broadcasted_iota(jnp.int32, (tm, 1), 0)
    seq_start = ((i * tm) % seq_len) == 0
    seq_end = (((i + 1) * tm) % seq_len) == 0
    prow = jnp.where(seq_start, 0.0, zp_ref[7:8, :])
    nrow = jnp.where(seq_end, 0.0, zn_ref[0:1, :])
    prev = jnp.where(row == 0, prow, pltpu.roll(z, 1, 0))
    nxt = jnp.where(row == tm - 1, nrow, pltpu.roll(z, tm - 1, 0))
    zs = z + mup_ref[...] * (prev - z) + mun_ref[...] * (nxt - z)
    w = RWKV_WIDTH
    r = zs[:, 0:w]
    k = zs[:, w:2 * w]
    v = zs[:, 2 * w:3 * w]
    dw = zs[:, 3 * w:3 * w + 2 * DECAY_RANK]
    da = zs[:, 3 * w + 2 * DECAY_RANK:3 * w + 2 * DECAY_RANK + 2 * ICLR_RANK]
    dg = zs[:, 3 * w + 2 * DECAY_RANK + 2 * ICLR_RANK:]
    lw = -DECAY_SCALE * _sigmoid(w0_ref[...] + _mm(jnp.tanh(dw), w2_ref[...], passes=1))
    a = _sigmoid(a0_ref[...] + _mm(da, a2_ref[...], passes=1))
    g = _mm(_sigmoid(dg), g2_ref[...], passes=1)
    kk = k * kkw_ref[...]
    seg = seg_ref[...]
    kk = kk * lax.rsqrt(_mm_exact_rhs(kk * kk, seg) + 1e-12)
    ka = ka_ref[...]
    ke_sum = k * (2.0 + (a[:, 0:w] + a[:, w:2 * w] - 2.0) * ka)
    bonus = _mm_exact_rhs(r * ke_sum * rk_ref[...], seg) * v
    r_ref[...] = r
    k_ref[...] = k
    v_ref[...] = v
    kk_ref[...] = kk
    lw_ref[...] = lw
    a_ref[...] = a
    g_ref[...] = g
    bonus_ref[...] = bonus


def _rwkv_prep(zr, p, seq_len, tm):
    n = zr.shape[0]
    w = RWKV_WIDTH
    nb8 = n // 8
    consts = [p['mu_prev'], p['mu_next'], p['w0'], p['w2bd'], p['a0'], p['a2bd'], p['g2'],
              p['k_k'], p['k_a'], p['r_k'], p['seg']]
    in_specs = [pl.BlockSpec((tm, RWKV_IN), lambda i: (i, 0)),
                pl.BlockSpec((8, RWKV_IN), lambda i: (jnp.maximum(i * (tm // 8) - 1, 0), 0)),
                pl.BlockSpec((8, RWKV_IN), lambda i: (jnp.minimum((i + 1) * (tm // 8), nb8 - 1), 0))]
    in_specs += [_full_spec(c) for c in consts]
    widths = [w, w, w, w, 2 * w, 2 * w, w, w]
    out_shape = [jax.ShapeDtypeStruct((n, wd), f32) for wd in widths]
    out_specs = [pl.BlockSpec((tm, wd), lambda i: (i, 0)) for wd in widths]
    return pl.pallas_call(
        functools.partial(_prep_kernel, seq_len=seq_len, tm=tm),
        grid=(n // tm,), in_specs=in_specs, out_specs=out_specs, out_shape=out_shape,
        compiler_params=_cparams("parallel"), name="rwkv_prep",
    )(zr, zr, zr, *consts)


def _scan_kernel(rf_ref, kf_ref, vf_ref, kkf_ref, lwf_ref, af_ref,
                 rb_ref, kb_ref, vb_ref, kkb_ref, lwb_ref, ab_ref, ka_ref, s0_ref,
                 of_ref, ob_ref, sfin_ref, st_ref):
    c = pl.program_id(1)

    @pl.when(c == 0)
    def _():
        st_ref[...] = s0_ref[...]

    n_seq = rf_ref.shape[0]
    row = lax.broadcasted_iota(jnp.int32, (CHUNK, CHUNK), 0)
    col = lax.broadcasted_iota(jnp.int32, (CHUNK, CHUNK), 1)
    qr = lax.broadcasted_iota(jnp.int32, (QUAD, QUAD), 0)
    qc = lax.broadcasted_iota(jnp.int32, (QUAD, QUAD), 1)
    head_mask = (qr & -RWKV_HEAD) == (qc & -RWKV_HEAD)
    head_mask_b = jnp.where(head_mask, 1.0, 0.0).astype(bf16)
    tr = lax.broadcasted_iota(jnp.int32, (CHUNK, QUAD), 0)
    tc = lax.broadcasted_iota(jnp.int32, (CHUNK, QUAD), 1) & (CHUNK - 1)
    eye = jnp.where(tr == tc, 1.0, 0.0).astype(f32)
    ka = ka_ref[...]

    def bd(x):
        xb = x.astype(bf16)
        return jnp.concatenate([xb] * QUAD_HEADS, axis=0) * head_mask_b

    def mm(a, b, nt=False):
        return _dot(a.astype(bf16), b.astype(bf16), nt)

    chains = []
    for d, (r_ref, k_ref, v_ref, kk_ref, lw_ref, a_ref, o_ref) in enumerate(
            ((rf_ref, kf_ref, vf_ref, kkf_ref, lwf_ref, af_ref, of_ref),
             (rb_ref, kb_ref, vb_ref, kkb_ref, lwb_ref, ab_ref, ob_ref))):
        sgn = 1 - 2 * d
        tri = jnp.where((row - col) * sgn >= 0, 1.0, 0.0).astype(bf16)
        dist = (tr - tc) * sgn
        for s in range(n_seq):
            lw = lw_ref[s]
            l1, l2, l3 = _split3(lw)
            cum = _dot(tri, l1) + (_dot(tri, l2) + _dot(tri, l3))
            e_incl = jnp.exp(cum)
            e_excl = jnp.exp(cum - lw)
            e_neg = jnp.exp(-cum)
            wc = jnp.exp(jnp.sum(lw, axis=0, keepdims=True))
            kk = kk_ref[s]
            a = a_ref[s]
            bh = kk * a * e_neg
            kh = k_ref[s] * (1.0 + (a - 1.0) * ka) * e_neg
            ops = dict(at=-kk * e_excl, rt=r_ref[s] * e_incl, bh=bh, kh=kh, bw=bh * wc, kw=kh * wc,
                       v=v_ref[s], wc=wc)
            for g in range(N_QUADS):
                ch = {key: val[:, g * QUAD:(g + 1) * QUAD] for key, val in ops.items()}
                ch.update(s=s, d=d, g=g, o_ref=o_ref, strict=dist > 0, incl=dist >= 0, st=st_ref[s, d, g])
                chains.append(ch)

    lane_lo = lax.broadcasted_iota(jnp.int32, (QUAD, LANE), 1) < RWKV_HEAD
    for ch in chains:
        ar = jnp.concatenate([ch['at'], ch['rt']], axis=0)
        ch['ar'] = ar.astype(bf16)
        bk_t = jnp.concatenate([ch['bh'], ch['kh']], axis=0).T
        swapped = pltpu.roll(bk_t, RWKV_HEAD, 1)
        b_t = jnp.where(lane_lo, bk_t, swapped).astype(bf16)
        k_t = jnp.where(lane_lo, swapped, bk_t).astype(bf16)
        w_bk = jnp.concatenate([jnp.concatenate([b_t, b_t], axis=1) * head_mask_b,
                                jnp.concatenate([k_t, k_t], axis=1) * head_mask_b], axis=1)
        amat = _dot(ch['ar'], w_bk)
        ch['n'] = jnp.where(ch['strict'], amat[0:CHUNK, 0:QUAD], 0.0)
        ch['a_kr'] = jnp.concatenate([jnp.where(ch['strict'], amat[0:CHUNK, QUAD:2 * QUAD], 0.0),
                                      jnp.where(ch['incl'], amat[CHUNK:2 * CHUNK, QUAD:2 * QUAD], 0.0)], axis=0)
        ch['a_rb'] = jnp.where(ch['incl'], amat[CHUNK:2 * CHUNK, 0:QUAD], 0.0)
        ch['tinv'] = eye + ch['n']
    for ch in chains:
        ch['n'] = mm(ch['n'], bd(ch['n']))
    for _ in range(4):
        for ch in chains:
            both = mm(jnp.concatenate([ch['n'], ch['tinv']], axis=0), bd(ch['n']))
            ch['n'] = both[0:CHUNK]
            ch['tinv'] = ch['tinv'] + both[CHUNK:2 * CHUNK]
    for ch in chains:
        ch['tinv'] = ch['tinv'] + mm(ch['tinv'], bd(ch['n']))
        sa = _dot(ch['ar'], ch['st'].astype(bf16))
        av = mm(ch['a_kr'], bd(ch['v']))
        ch['rhs'] = sa[0:CHUNK] + av[0:CHUNK]
        ch['o'] = sa[CHUNK:2 * CHUNK] + av[CHUNK:2 * CHUNK]
    for ch in chains:
        ch['u'] = mm(ch['tinv'], bd(ch['rhs']))
    for ch in chains:
        ch['o'] = ch['o'] + mm(ch['a_rb'], bd(ch['u']))
        z_t = jnp.concatenate([ch['bw'], ch['kw'], jnp.broadcast_to(ch['wc'], (2 * CHUNK, QUAD))], axis=0).T
        upd = mm(z_t[:, 0:2 * CHUNK], jnp.concatenate([ch['u'], ch['v']], axis=0))
        decay = jnp.concatenate([z_t[:, 2 * CHUNK:4 * CHUNK]] * 2, axis=1)
        ch['st_new'] = ch['st'] * decay + jnp.where(head_mask, upd, 0.0)
    for ch in chains:
        ch['o_ref'][ch['s'], :, ch['g'] * QUAD:(ch['g'] + 1) * QUAD] = ch['o']
        st_ref[ch['s'], ch['d'], ch['g']] = ch['st_new']

    @pl.when(c == pl.num_programs(1) - 1)
    def _():
        sfin_ref[...] = st_ref[...]


def _rwkv_scan(r, k, v, kk, lw, a, k_a, s0, batch, seq_len):
    n = r.shape[0]
    w = RWKV_WIDTH
    nc = seq_len // CHUNK
    per_seq = lambda x: x.reshape(batch, seq_len, x.shape[-1])
    r, k, v, kk, lw, a = (per_seq(x) for x in (r, k, v, kk, lw, a))
    fwd = lambda b, c: (b, c, 0)
    bwd = lambda b, c: (b, nc - 1 - c, 0)
    bwd_dir = lambda b, c: (b, nc - 1 - c, 1)
    blk = lambda index_map: pl.BlockSpec((SCAN_SEQS, CHUNK, w), index_map)
    state_spec = pl.BlockSpec((SCAN_SEQS, 2, N_QUADS, QUAD, QUAD), lambda b, c: (b, 0, 0, 0, 0))
    o_shape = jax.ShapeDtypeStruct((batch, seq_len, w), f32)
    o_fw, o_bw, s_fin = pl.pallas_call(
        _scan_kernel,
        grid=(batch // SCAN_SEQS, nc),
        in_specs=[blk(fwd)] * 6 + [blk(bwd)] * 4 + [blk(bwd_dir)] * 2
                 + [pl.BlockSpec((1, w), lambda b, c: (0, 0)), state_spec],
        out_specs=[blk(fwd), blk(bwd), state_spec],
        out_shape=[o_shape, o_shape, jax.ShapeDtypeStruct((batch, 2, N_QUADS, QUAD, QUAD), f32)],
        scratch_shapes=[pltpu.VMEM((SCAN_SEQS, 2, N_QUADS, QUAD, QUAD), f32)],
        compiler_params=_cparams("parallel", "arbitrary"), name="rwkv_scan",
    )(r, k, v, kk, lw, a, r, k, v, kk, lw, a, k_a, s0)
    return o_fw.reshape(n, w), o_bw.reshape(n, w), s_fin


def _out_kernel(attn_ref, of_ref, ob_ref, bonus_ref, g_ref, x_ref, mod_ref, lng_ref, lnb_ref, seg_ref,
                wo_ref, gpm_ref, gpf_ref, rwt_ref, rb_ref,
                x1_ref, h2_ref, slot_ref, sw_ref, cnt_ref):
    outs = []
    for t in range(x_ref.shape[0] // TM_MOE):
        rows = pl.ds(t * TM_MOE, TM_MOE)
        outs.append(_out_tile(attn_ref.at[rows], of_ref.at[rows], ob_ref.at[rows], bonus_ref.at[rows],
                              g_ref.at[rows], x_ref.at[rows], mod_ref, lng_ref, lnb_ref, seg_ref, wo_ref,
                              gpm_ref, gpf_ref, rwt_ref, rb_ref))
    for t, (x1, h2, slots, sw, cnt) in enumerate(outs):
        rows = pl.ds(t * TM_MOE, TM_MOE)
        x1_ref[rows, :] = x1
        h2_ref[rows, :] = h2
        slot_ref[:, rows] = slots
        sw_ref[rows, :] = sw
        cnt_ref[t] = cnt


def _out_tile(attn_ref, of_ref, ob_ref, bonus_ref, g_ref, x_ref, mod_ref, lng_ref, lnb_ref, seg_ref,
              wo_ref, gpm_ref, gpf_ref, rwt_ref, rb_ref):
    tm = x_ref.shape[0]
    seg = seg_ref[...]
    inv = 1.0 / RWKV_HEAD
    o = of_ref[...] + ob_ref[...]
    mu = _mm_exact_rhs(o, seg) * inv
    dl = o - mu
    var = _mm_exact_rhs(dl * dl, seg) * inv
    on = dl * lax.rsqrt(var + GN_EPS) * lng_ref[...] + lnb_ref[...]
    rw = (on + bonus_ref[...]) * g_ref[...]
    half = ATTN_HEADS * HEAD_DIM
    y = _dot(attn_ref[...].astype(bf16), wo_ref[0:half, :]) + _dot(rw.astype(bf16), wo_ref[half:, :])
    m = mod_ref[...]
    gt_a = m[:, 2 * D_MODEL:3 * D_MODEL]
    sh_f = m[:, 3 * D_MODEL:4 * D_MODEL]
    sc_f = m[:, 4 * D_MODEL:5 * D_MODEL]
    x1 = x_ref[...] + gt_a * _rms(y, gpm_ref[...])
    h2 = _rms(x1, gpf_ref[...]) * (1.0 + sc_f) + sh_f

    logits = _mm(rwt_ref[...], h2, nt=True)
    scores = _sigmoid(logits)
    biased = scores + rb_ref[...]
    neg = -jnp.inf
    shape3 = (N_GROUPS, GROUP_SIZE, tm)
    b3 = biased.reshape(shape3)
    s3 = scores.reshape(shape3)
    in_grp = lax.broadcasted_iota(jnp.int32, shape3, 1)
    grp = lax.broadcasted_iota(jnp.int32, shape3, 0)
    m1 = jnp.max(b3, axis=1, keepdims=True)
    i1 = jnp.min(jnp.where(b3 == m1, in_grp, GROUP_SIZE), axis=1, keepdims=True)
    m2 = jnp.max(jnp.where(in_grp == i1, neg, b3), axis=1, keepdims=True)
    gscore = m1 + m2
    gidx = lax.broadcasted_iota(jnp.int32, (N_GROUPS, 1, tm), 0)
    gsel = jnp.zeros((N_GROUPS, 1, tm), f32)
    cur = gscore
    for _ in range(TOPK_GROUPS):
        mx = jnp.max(cur, axis=0, keepdims=True)
        ii = jnp.min(jnp.where(cur == mx, gidx, N_GROUPS), axis=0, keepdims=True)
        hit = gidx == ii
        gsel = jnp.where(hit, 1.0, gsel)
        cur = jnp.where(hit, neg, cur)
    cand = jnp.where(jnp.broadcast_to(gsel, shape3) > 0.0, b3, neg)
    eidx = grp * GROUP_SIZE + in_grp
    wsel = jnp.zeros(shape3, f32)
    mask = jnp.zeros(shape3, f32)
    hits = []
    for _ in range(TOP_K):
        mx = jnp.max(jnp.max(cand, axis=1, keepdims=True), axis=0, keepdims=True)
        ii = jnp.min(jnp.min(jnp.where(cand == mx, eidx, N_EXPERTS), axis=1, keepdims=True),
                     axis=0, keepdims=True)
        hit = eidx == ii
        hits.append(hit)
        wsel = jnp.where(hit, s3, wsel)
        mask = jnp.where(hit, 1.0, mask)
        cand = jnp.where(hit, neg, cand)
    den = jnp.sum(jnp.sum(wsel, axis=1, keepdims=True), axis=0, keepdims=True)
    gates3 = wsel / den * ROUTED_SCALE

    mask2 = mask.reshape(N_EXPERTS, tm)
    cnt = jnp.sum(mask2, axis=1, keepdims=True)
    cnt_pad = jnp.floor((cnt + (SEG - 1)) * (1.0 / SEG)) * SEG
    er = lax.broadcasted_iota(jnp.int32, (N_EXPERTS, N_EXPERTS), 0)
    ec = lax.broadcasted_iota(jnp.int32, (N_EXPERTS, N_EXPERTS), 1)
    before = jnp.where(ec < er, 1.0, 0.0).astype(bf16)
    cnt_pad_l = jnp.broadcast_to(cnt_pad, (N_EXPERTS, LANE))
    seg_start = _dot(before, cnt_pad_l.astype(bf16))
    tr = lax.broadcasted_iota(jnp.int32, (tm, tm), 0)
    tc = lax.broadcasted_iota(jnp.int32, (tm, tm), 1)
    earlier = jnp.where(tr < tc, 1.0, 0.0).astype(bf16)
    rank = _dot(mask2.astype(bf16), earlier)
    slot3 = (rank + seg_start[:, 0:1]).reshape(shape3)

    def pick(hit, val3):
        return jnp.sum(jnp.sum(jnp.where(hit, val3, 0.0), axis=1, keepdims=True), axis=0)

    slots = [pick(h, slot3) for h in hits]
    wts = [pick(h, gates3) for h in hits]
    pad_rows = lambda k, val: [jnp.full((k, tm), val, f32)]
    slot_rows = jnp.concatenate(slots + pad_rows(8 - TOP_K, -1.0), axis=0).astype(jnp.int32)
    sw_t = jnp.concatenate(slots + pad_rows(8 - TOP_K, -1.0) + wts + pad_rows(LANE - 8 - TOP_K, 0.0), axis=0)
    return x1, h2.astype(bf16), slot_rows, sw_t.T, cnt_pad_l


def _mix_out(attn, o_fw, o_bw, bonus, g, x2, mod3, mod_row, p, tm):
    n = x2.shape[0]
    w = RWKV_WIDTH
    consts = [p['lnx_g'], p['lnx_b'], p['seg'], p['w_out'], p['g_post_mix'], p['g_pre_ffn'],
              p['router_wt'], p['router_b']]
    row = lambda wd: pl.BlockSpec((tm, wd), lambda i: (i, 0))
    in_specs = [row(ATTN_HEADS * HEAD_DIM), row(w), row(w),
                row(w), row(w), row(D_MODEL),
                pl.BlockSpec((None, 1, 6 * D_MODEL), lambda i: (mod_row(i, tm), 0, 0))]
    in_specs += [_full_spec(c) for c in consts]
    out_shape = [jax.ShapeDtypeStruct((n, D_MODEL), f32),
                 jax.ShapeDtypeStruct((n, D_MODEL), bf16),
                 jax.ShapeDtypeStruct((8, n), jnp.int32),
                 jax.ShapeDtypeStruct((n, LANE), f32),
                 jax.ShapeDtypeStruct((n // TM_MOE, N_EXPERTS, LANE), f32)]
    out_specs = [row(D_MODEL), row(D_MODEL), pl.BlockSpec((8, tm), lambda i: (0, i)), row(LANE),
                 pl.BlockSpec((tm // TM_MOE, N_EXPERTS, LANE), lambda i: (i, 0, 0))]
    return pl.pallas_call(
        _out_kernel, grid=(n // tm,), in_specs=in_specs, out_specs=out_specs, out_shape=out_shape,
        compiler_params=_cparams("parallel"), name="mix_out",
    )(attn, o_fw, o_bw, bonus, g, x2, mod3, *consts)


def _chunk_groups(tile, nch_sm):
    return lax.shift_right_logical(nch_sm[tile] + (CHUNK_GROUP - 1), CHUNK_GROUP_SHIFT)


def _start_chunks(tile, dst_sm, nch_sm, make_copy):
    def body(g, carry):
        for u in range(CHUNK_GROUP):
            c = g * CHUNK_GROUP + u
            make_copy(c, pl.multiple_of(dst_sm[tile * MAX_CHUNKS + c], SEG)).start()
        return carry

    lax.fori_loop(0, _chunk_groups(tile, nch_sm), body, 0)


def _wait_chunks(tile, nch_sm, make_copy):
    def body(g, carry):
        for _ in range(CHUNK_GROUP):
            make_copy(0, 0).wait()
        return carry

    lax.fori_loop(0, _chunk_groups(tile, nch_sm), body, 0)


def _dispatch_kernel(dst_sm, nch_sm, fill_sm, *refs, tile_starts):
    n_paths = len(tile_starts)
    h_refs, slot_refs = refs[:n_paths], refs[n_paths:2 * n_paths]
    xs_hbm, xc_ref, zero_ref, sems, zsem = refs[2 * n_paths:]
    i = pl.program_id(0)
    last = pl.num_programs(0) - 1
    buf = lax.rem(i, 2)

    def copy_from(b):
        def make(chunk, global_row):
            src = xc_ref.at[b, pl.ds(pl.multiple_of(chunk * SEG, SEG), SEG)]
            return pltpu.make_async_copy(src, xs_hbm.at[pl.ds(global_row, SEG)], sems.at[b])
        return make

    @pl.when(i == 0)
    def _():
        zero_ref[...] = jnp.zeros_like(zero_ref)
        n_tiles = xs_hbm.shape[0] // FFN_ROWS

        def tail_copy(row):
            return pltpu.make_async_copy(zero_ref.at[pl.ds(0, SEG)], xs_hbm.at[pl.ds(row, SEG)], zsem)

        def tile_copy(t):
            return pltpu.make_async_copy(zero_ref, xs_hbm.at[pl.ds(pl.multiple_of(t * FFN_ROWS, FFN_ROWS), FFN_ROWS)],
                                         zsem)

        def per_expert(e, carry):
            start = fill_sm[1 + e]
            chunks = lax.shift_right_logical(fill_sm[1 + N_EXPERTS + e], SEG_SHIFT)
            lax.fori_loop(0, chunks, lambda c, z: (tail_copy(pl.multiple_of(start + c * SEG, SEG)).start(), z)[1], 0)
            lax.fori_loop(0, chunks, lambda c, z: (tail_copy(0).wait(), z)[1], 0)
            return carry

        lax.fori_loop(0, N_EXPERTS, per_expert, 0)
        lax.fori_loop(fill_sm[0], n_tiles, lambda t, z: (tile_copy(t).start(), z)[1], 0)
        lax.fori_loop(fill_sm[0], n_tiles, lambda t, z: (tile_copy(0).wait(), z)[1], 0)

    @pl.when(i >= 2)
    def _():
        _wait_chunks(i - 2, nch_sm, copy_from(buf))

    slot, h = slot_refs[0][...], h_refs[0][...]
    for q in range(1, n_paths):
        mine = i >= tile_starts[q]
        slot = jnp.where(mine, slot_refs[q][...], slot)
        h = jnp.where(mine, h_refs[q][...], h)
    used = nch_sm[i] * SEG

    def build(rb):
        rows = lax.broadcasted_iota(jnp.int32, (SEL_BLOCK, TM_MOE), 0) + rb * SEL_BLOCK
        sel = jnp.zeros((SEL_BLOCK, TM_MOE), f32)
        for j in range(TOP_K):
            sel = jnp.where(rows == slot[j:j + 1, :], 1.0, sel)
        xc_ref[buf, rb * SEL_BLOCK:(rb + 1) * SEL_BLOCK, :] = _dot(sel.astype(bf16), h).astype(bf16)

    for rb in range(COMMON_BLOCKS):
        build(rb)
    for rb in range(COMMON_BLOCKS, SLOT_ROWS // SEL_BLOCK):
        pl.when(rb * SEL_BLOCK < used)(functools.partial(build, rb))

    _start_chunks(i, dst_sm, nch_sm, copy_from(buf))

    @pl.when(i == last)
    def _():
        @pl.when(i >= 1)
        def _():
            _wait_chunks(i - 1, nch_sm, copy_from(1 - buf))
        _wait_chunks(i, nch_sm, copy_from(buf))


def _dispatch(h2s, slots, dst, nch, fill, total_rows):
    tiles = [h.shape[0] // TM_MOE for h in h2s]
    starts = [sum(tiles[:q]) for q in range(len(tiles))]

    def in_path(q):
        return lambda i: jnp.clip(i - starts[q], 0, tiles[q] - 1)

    h_specs = [pl.BlockSpec((TM_MOE, D_MODEL), lambda i, d, c, f, q=q: (in_path(q)(i), 0)) for q in range(len(tiles))]
    s_specs = [pl.BlockSpec((8, TM_MOE), lambda i, d, c, f, q=q: (0, in_path(q)(i))) for q in range(len(tiles))]
    grid_spec = pltpu.PrefetchScalarGridSpec(
        num_scalar_prefetch=3, grid=(sum(tiles),),
        in_specs=h_specs + s_specs,
        out_specs=pl.BlockSpec(memory_space=pl.ANY),
        scratch_shapes=[pltpu.VMEM((2, SLOT_ROWS, D_MODEL), bf16), pltpu.VMEM((FFN_ROWS, D_MODEL), bf16),
                        pltpu.SemaphoreType.DMA((2,)), pltpu.SemaphoreType.DMA(())])
    return pl.pallas_call(
        functools.partial(_dispatch_kernel, tile_starts=tuple(starts)), grid_spec=grid_spec,
        out_shape=jax.ShapeDtypeStruct((total_rows, D_MODEL), bf16),
        compiler_params=_cparams("arbitrary"), name="moe_dispatch",
    )(dst, nch, fill, *h2s, *slots)


def _ffn_kernel(te_sm, nv_sm, xs_ref, wg_ref, wu_ref, wd_ref, ys_ref, wg_b, wu_b, wd_b):
    r = pl.program_id(0)
    valid = r < nv_sm[0]

    @pl.when(jnp.logical_and(valid, jnp.logical_or(r == 0, te_sm[r] != te_sm[jnp.maximum(r - 1, 0)])))
    def _():
        wg_b[...] = wg_ref[...].astype(bf16)
        wu_b[...] = wu_ref[...].astype(bf16)
        wd_b[...] = wd_ref[...].astype(bf16)

    @pl.when(valid)
    def _():
        x = xs_ref[...]
        gg = _dot(x, wg_b[...])
        uu = _dot(x, wu_b[...])
        hm = gg * _sigmoid(gg) * uu
        ys_ref[...] = _dot(hm.astype(bf16), wd_b[...]).astype(bf16)

    @pl.when(r >= nv_sm[0])
    def _():
        ys_ref[...] = jnp.zeros_like(ys_ref)


def _expert_ffn(xs, tile_expert, n_valid, wg, wu, wd):
    total_rows = xs.shape[0]
    last = lambda r, nv: jnp.minimum(r, nv[0] - 1)
    grid_spec = pltpu.PrefetchScalarGridSpec(
        num_scalar_prefetch=2, grid=(total_rows // FFN_ROWS,),
        in_specs=[pl.BlockSpec((FFN_ROWS, D_MODEL), lambda r, te, nv: (last(r, nv), 0)),
                  pl.BlockSpec((None, D_MODEL, EXPERT_FF), lambda r, te, nv: (te[last(r, nv)], 0, 0)),
                  pl.BlockSpec((None, D_MODEL, EXPERT_FF), lambda r, te, nv: (te[last(r, nv)], 0, 0)),
                  pl.BlockSpec((None, EXPERT_FF, D_MODEL), lambda r, te, nv: (te[last(r, nv)], 0, 0))],
        out_specs=pl.BlockSpec((FFN_ROWS, D_MODEL), lambda r, te, nv: (r, 0)),
        scratch_shapes=[pltpu.VMEM((D_MODEL, EXPERT_FF), bf16), pltpu.VMEM((D_MODEL, EXPERT_FF), bf16),
                        pltpu.VMEM((EXPERT_FF, D_MODEL), bf16)])
    return pl.pallas_call(
        _ffn_kernel, grid_spec=grid_spec,
        out_shape=jax.ShapeDtypeStruct((total_rows, D_MODEL), bf16),
        compiler_params=_cparams("arbitrary"), name="moe_ffn",
    )(tile_expert, n_valid, xs, wg, wu, wd)


def _combine_kernel(dst_sm, nch_sm, sw_ref, h_ref, x1_ref, mod_ref, gpost_ref, swg_ref, swu_ref, swd_ref,
                    ys_hbm, y_ref, yc_ref, acc_ref, sems, *, tile_offset):
    step = pl.program_id(0)
    i = step + tile_offset
    buf = lax.rem(step, 2)

    def copy_into(b):
        def make(chunk, global_row):
            dst = yc_ref.at[b, pl.ds(pl.multiple_of(chunk * SEG, SEG), SEG)]
            return pltpu.make_async_copy(ys_hbm.at[pl.ds(global_row, SEG)], dst, sems.at[b])
        return make

    @pl.when(step == 0)
    def _():
        yc_ref[...] = jnp.zeros_like(yc_ref)
        _start_chunks(i, dst_sm, nch_sm, copy_into(buf))

    @pl.when(step + 1 < pl.num_programs(0))
    def _():
        _start_chunks(i + 1, dst_sm, nch_sm, copy_into(1 - buf))

    h = h_ref[...]
    gg = _dot(h, swg_ref[...])
    uu = _dot(h, swu_ref[...])
    acc_ref[...] = _dot((gg * _sigmoid(gg) * uu).astype(bf16), swd_ref[...])
    _wait_chunks(i, nch_sm, copy_into(buf))
    sw = sw_ref[...]
    used = nch_sm[i] * SEG
    def weighted(rb):
        cols = (lax.broadcasted_iota(jnp.int32, (TM_MOE, SEL_BLOCK), 1) + rb * SEL_BLOCK).astype(f32)
        wmat = jnp.zeros((TM_MOE, SEL_BLOCK), f32)
        for j in range(TOP_K):
            wmat = jnp.where(cols == sw[:, j:j + 1], sw[:, 8 + j:9 + j], wmat)
        rows = lax.broadcasted_iota(jnp.int32, (SEL_BLOCK, 1), 0) + rb * SEL_BLOCK
        yc = jnp.where(rows < used, yc_ref[buf, rb * SEL_BLOCK:(rb + 1) * SEL_BLOCK, :], jnp.zeros((), bf16))
        return _dot(wmat.astype(bf16), yc)

    moe = acc_ref[...]
    for rb in range(COMMON_BLOCKS):
        moe = moe + weighted(rb)
    acc_ref[...] = moe
    for rb in range(COMMON_BLOCKS, SLOT_ROWS // SEL_BLOCK):
        @pl.when(rb * SEL_BLOCK < used)
        def _():
            acc_ref[...] += weighted(rb)
    gt_f = mod_ref[...][:, 5 * D_MODEL:6 * D_MODEL]
    y_ref[...] = x1_ref[...] + gt_f * _rms(acc_ref[...], gpost_ref[...])


def _combine(ys, dst, nch, sw, h2, x1, mod3, mod_row, g_post, swg, swu, swd, tile_offset):
    n = h2.shape[0]
    row = lambda wd: pl.BlockSpec((TM_MOE, wd), lambda i, d, c: (i, 0))
    const = lambda a: pl.BlockSpec(a.shape, lambda i, d, c: (0,) * a.ndim)
    grid_spec = pltpu.PrefetchScalarGridSpec(
        num_scalar_prefetch=2, grid=(n // TM_MOE,),
        in_specs=[row(LANE), row(D_MODEL), row(D_MODEL),
                  pl.BlockSpec((None, 1, 6 * D_MODEL), lambda i, d, c: (mod_row(i, TM_MOE), 0, 0)),
                  const(g_post), const(swg), const(swu), const(swd),
                  pl.BlockSpec(memory_space=pl.ANY)],
        out_specs=row(D_MODEL),
        scratch_shapes=[pltpu.VMEM((2, SLOT_ROWS, D_MODEL), bf16), pltpu.VMEM((TM_MOE, D_MODEL), f32),
                        pltpu.SemaphoreType.DMA((2,))])
    return pl.pallas_call(
        functools.partial(_combine_kernel, tile_offset=tile_offset), grid_spec=grid_spec,
        out_shape=jax.ShapeDtypeStruct((n, D_MODEL), f32),
        compiler_params=_cparams("arbitrary"), name="moe_combine",
    )(dst, nch, sw, h2, x1, mod3, g_post, swg, swu, swd, ys)


def _sparse_moe(paths, mod3, p):
    cnt = jnp.concatenate([q['cnt'][:, :, 0] for q in paths], axis=0).astype(jnp.int32)
    n_tiles = cnt.shape[0]
    before = jnp.cumsum(cnt, axis=0) - cnt
    rows_e = jnp.sum(cnt, axis=0)
    region = (rows_e + FFN_ROWS - 1) // FFN_ROWS * FFN_ROWS
    region_end = jnp.cumsum(region)
    region_start = region_end - region
    worst = n_tiles * (TOP_K * TM_MOE + N_EXPERTS * (SEG - 1)) + N_EXPERTS * (FFN_ROWS - SEG)
    total_rows = -(-worst // FFN_ROWS) * FFN_ROWS + FFN_ROWS
    spare_row = total_rows - FFN_ROWS
    first_row = jnp.arange(total_rows // FFN_ROWS, dtype=jnp.int32) * FFN_ROWS
    ended = jnp.sum((region_end[None, :] <= first_row[:, None]).astype(jnp.int32), axis=1)
    tile_expert = jnp.minimum(ended, N_EXPERTS - 1).astype(jnp.int32)
    n_valid = (region_end[-1:] // FFN_ROWS).astype(jnp.int32)

    seg_end = jnp.cumsum(cnt, axis=1)
    seg_start = seg_end - cnt
    nch = (seg_end[:, -1] // SEG).astype(jnp.int32)
    chunk_row = jnp.arange(MAX_CHUNKS, dtype=jnp.int32) * SEG
    chunk_e = jnp.sum((seg_end[:, None, :] <= chunk_row[None, :, None]).astype(jnp.int32), axis=2)
    own = chunk_e[:, :, None] == jnp.arange(N_EXPERTS, dtype=jnp.int32)[None, None, :]
    shift = (region_start[None, :] + before - seg_start)[:, None, :]
    dst = jnp.sum(jnp.where(own, shift, 0), axis=2) + chunk_row[None, :]
    chunk_id = jnp.arange(MAX_CHUNKS, dtype=jnp.int32)[None, :]
    tile_id = jnp.arange(n_tiles, dtype=jnp.int32)[:, None]
    spare = spare_row + ((tile_id % 2) * CHUNK_GROUP + chunk_id % CHUNK_GROUP) * SEG
    dst = jnp.where(chunk_id < nch[:, None], dst, spare).astype(jnp.int32).reshape(-1)

    fill = jnp.concatenate([n_valid, region_start + rows_e, region - rows_e]).astype(jnp.int32)
    xs = _dispatch([q['h2'] for q in paths], [q['slots'] for q in paths], dst, nch, fill, total_rows)
    ys = _expert_ffn(xs, tile_expert, n_valid, p['expert_wg'], p['expert_wu'], p['expert_wd'])
    outs, tile_offset = [], 0
    for q in paths:
        outs.append(_combine(ys, dst, nch, q['sw'], q['h2'], q['x1'], mod3, q['mod_row'],
                             p['g_post_ffn'], p['shared_wg'], p['shared_wu'], p['shared_wd'], tile_offset))
        tile_offset += q['h2'].shape[0] // TM_MOE
    return outs


def _pad_heads(w, heads):
    rows = w.shape[0]
    w = w.reshape(rows, heads, HEAD_DIM)
    return jnp.pad(w, ((0, 0), (0, 0), (0, HEAD_PAD - HEAD_DIM))).reshape(rows, heads * HEAD_PAD)


def _block_diag2(m):
    z = jnp.zeros_like(m[0])
    return jnp.concatenate([jnp.concatenate([m[0], z], axis=1), jnp.concatenate([z, m[1]], axis=1)], axis=0)


def _prepare_params(w):
    aw, kw = ATTN_HEADS * HEAD_DIM, KV_HEADS * HEAD_DIM
    w_in = w['w_in']
    lane_row = lambda v: v.reshape(1, -1).astype(f32)
    head_id = np.arange(RWKV_WIDTH) // RWKV_HEAD
    seg = jnp.asarray(head_id[:, None] == head_id[None, :], dtype=bf16)
    gain_pad = lambda gvec: jnp.pad(gvec, (0, HEAD_PAD - HEAD_DIM)).reshape(1, HEAD_PAD)
    return dict(
        g_pre_mix=lane_row(w['g_pre_mix']), g_post_mix=lane_row(w['g_post_mix']),
        g_pre_ffn=lane_row(w['g_pre_ffn']), g_post_ffn=lane_row(w['g_post_ffn']),
        wq=_pad_heads(w_in[:, :aw], ATTN_HEADS).astype(bf16),
        wk=_pad_heads(w_in[:, aw:aw + kw], KV_HEADS).astype(bf16),
        wv=w_in[:, aw + kw:aw + 2 * kw].astype(bf16),
        wr=w_in[:, aw + 2 * kw:].astype(bf16),
        qg=gain_pad(w['q_gain'] * ATTN_SCALE), kg=gain_pad(w['k_gain']),
        mu_prev=lane_row(w['mu_prev']), mu_next=lane_row(w['mu_next']),
        w0=lane_row(w['decay_w0']), w2bd=_block_diag2(w['decay_w2']),
        a0=lane_row(w['iclr_a0']), a2bd=_block_diag2(w['iclr_a2']),
        g2=w['gate_g2'], k_k=lane_row(w['k_k']), k_a=lane_row(w['k_a']), r_k=lane_row(w['r_k']),
        lnx_g=lane_row(w['lnx_g']), lnx_b=lane_row(w['lnx_b']), seg=seg,
        w_out=w['w_out'].astype(bf16),
        router_wt=w['router_w'].T, router_b=w['router_b'].reshape(N_EXPERTS, 1),
        expert_wg=w['expert_wg'], expert_wu=w['expert_wu'], expert_wd=w['expert_wd'],
        shared_wg=w['shared_wg'].astype(bf16), shared_wu=w['shared_wu'].astype(bf16),
        shared_wd=w['shared_wd'].astype(bf16),
    )


def _rope_tables(seq_len):
    rows = seq_len // GRID_W
    row_idx = jnp.repeat(jnp.arange(rows, dtype=jnp.int32), GRID_W)
    col_idx = jnp.tile(jnp.arange(GRID_W, dtype=jnp.int32), rows)
    inv = ROPE_THETA ** (-jnp.arange(0, ROPE_HALF, 2, dtype=f32) / ROPE_HALF)
    ang = jnp.stack([row_idx.astype(f32)[:, None] * inv, col_idx.astype(f32)[:, None] * inv], axis=1)
    cos, sin = jnp.cos(ang), jnp.sin(ang)
    cos_l = jnp.stack([cos, cos], axis=2).reshape(seq_len, HEAD_DIM)
    sin_l = jnp.stack([-sin, sin], axis=2).reshape(seq_len, HEAD_DIM)
    pad = ((0, 0), (0, HEAD_PAD - HEAD_DIM))
    return jnp.pad(cos_l, pad), jnp.pad(sin_l, pad)


def _states_to_pairs(s):
    b = s.shape[0]
    s = s.reshape(b, 2, N_QUADS, QUAD_HEADS, RWKV_HEAD, RWKV_HEAD)
    eye = jnp.eye(QUAD_HEADS, dtype=s.dtype)
    return jnp.einsum('bdphvk,hg->bdphkgv', s, eye).reshape(b, 2, N_QUADS, QUAD, QUAD)


def _pairs_to_states(s):
    b = s.shape[0]
    s = s.reshape(b, 2, N_QUADS, QUAD_HEADS, RWKV_HEAD, QUAD_HEADS, RWKV_HEAD)
    diag = jnp.stack([s[:, :, :, h, :, h, :] for h in range(QUAD_HEADS)], axis=3)
    return jnp.swapaxes(diag, -1, -2).reshape(b, 2, RWKV_HEADS, RWKV_HEAD, RWKV_HEAD)


def _trunk(x, mod3, mod_row, p, rope_tabs, past_k, past_v, s0_pairs, tm_in, tq, tm_prep):
    b, t, _ = x.shape
    n = b * t
    x2 = x.reshape(n, D_MODEL)
    q_pad, k_own, v_own, k_pad, zr = _in_proj(
        x2, mod3, mod_row, p['g_pre_mix'], p['wq'], p['wk'], p['wv'], p['wr'], p['qg'], p['kg'],
        rope_tabs, t, tm_in)
    k_full = k_pad.reshape(b, t, KV_HEADS * HEAD_PAD)
    v_full = v_own.reshape(b, t, KV_HEADS * HEAD_DIM)
    if past_k is not None:
        k_past = _pad_heads(past_k.reshape(-1, KV_HEADS * HEAD_DIM), KV_HEADS).astype(bf16)
        k_full = jnp.concatenate([k_past.reshape(b, -1, KV_HEADS * HEAD_PAD), k_full], axis=1)
        v_full = jnp.concatenate([past_v.reshape(b, -1, KV_HEADS * HEAD_DIM), v_full], axis=1)
    vt_full = jnp.swapaxes(v_full, 1, 2).astype(bf16)
    attn = _attention(q_pad, k_full, vt_full, b, t, tq)
    r, k, v, kk, lw, a, g, bonus = _rwkv_prep(zr, p, t, tm_prep)
    o_fw, o_bw, s_fin = _rwkv_scan(r, k, v, kk, lw, a, p['k_a'], s0_pairs, b, t)
    x1, h2, slots, sw, cnt = _mix_out(attn, o_fw, o_bw, bonus, g, x2, mod3, mod_row, p, MIX_TILES * TM_MOE)
    return dict(h2=h2, slots=slots, sw=sw, cnt=cnt, x1=x1, mod_row=mod_row), k_own, v_own, s_fin


def kernel(x_prompt, x_sample, c, cache_k, cache_v, state_rwkv, c_ctx, w_ada, b_ada, g_pre_mix, g_post_mix,
           g_pre_ffn, g_post_ffn, w_in, w_out, q_gain, k_gain, mu_prev, mu_next, decay_w0, decay_w2, iclr_a0,
           iclr_a2, gate_g2, k_k, k_a, r_k, lnx_g, lnx_b, router_w, router_b, expert_wg, expert_wu, expert_wd,
           shared_wg, shared_wu, shared_wd):
    layer = 0
    names = ('g_pre_mix g_post_mix g_pre_ffn g_post_ffn w_in w_out q_gain k_gain mu_prev mu_next decay_w0 '
             'decay_w2 iclr_a0 iclr_a2 gate_g2 k_k k_a r_k lnx_g lnx_b router_w router_b expert_wg expert_wu '
             'expert_wd shared_wg shared_wu shared_wd').split()
    vals = (g_pre_mix, g_post_mix, g_pre_ffn, g_post_ffn, w_in, w_out, q_gain, k_gain, mu_prev, mu_next,
            decay_w0, decay_w2, iclr_a0, iclr_a2, gate_g2, k_k, k_a, r_k, lnx_g, lnx_b, router_w, router_b,
            expert_wg, expert_wu, expert_wd, shared_wg, shared_wu, shared_wd)
    p = _prepare_params({nm: v[layer] for nm, v in zip(names, vals)})

    nb, ts, _ = x_sample.shape
    npb, tp, _ = x_prompt.shape
    mod_rows = 16
    cc = jnp.zeros((mod_rows, D_MODEL), f32).at[0].set(c_ctx).at[1:1 + nb].set(c)
    mod3 = _modulation(cc, w_ada[layer], b_ada[layer]).reshape(mod_rows, 1, 6 * D_MODEL)

    zeros_state = jnp.zeros((npb, 2, N_QUADS, QUAD, QUAD), f32)
    moe_p, kc, vc, st = _trunk(x_prompt, mod3, lambda i, tm: 0, p, None, None, None, zeros_state,
                               tm_in=256, tq=tp, tm_prep=256)
    s0_lat = _states_to_pairs(state_rwkv[:, layer])
    moe_s, _, _, _ = _trunk(x_sample, mod3, lambda i, tm: 1 + (i * tm) // ts, p, _rope_tables(ts),
                            cache_k[:, layer], cache_v[:, layer], s0_lat,
                            tm_in=512, tq=256, tm_prep=256)
    y_p, y_s = _sparse_moe([moe_p, moe_s], mod3, p)
    y_p = y_p.reshape(x_prompt.shape)
    y_s = y_s.reshape(x_sample.shape)

    new_cache_k = kc.reshape(npb, 1, tp, KV_HEADS, HEAD_DIM)
    new_cache_v = vc.reshape(npb, 1, tp, KV_HEADS, HEAD_DIM)
    new_state = _pairs_to_states(st)[:, None]
    return (y_p, y_s, new_cache_k, new_cache_v, new_state)
```

```python
import functools

import numpy as np
import jax
import jax.numpy as jnp
from jax import lax
from jax.experimental import pallas as pl
from jax.experimental.pallas import tpu as pltpu

f32 = jnp.float32
bf16 = jnp.bfloat16

D_MODEL = 1024
GRID_W = 64
HEAD_DIM = 64
ATTN_HEADS = 8
KV_HEADS = 2
GQA_GROUP = ATTN_HEADS // KV_HEADS
ATTN_SCALE = HEAD_DIM ** -0.5
ROPE_THETA = 10000.0
ROPE_HALF = HEAD_DIM // 2
ROPE_FREQS = ROPE_HALF // 2
RWKV_HEAD = 64
RWKV_HEADS = 8
RWKV_WIDTH = RWKV_HEADS * RWKV_HEAD
DECAY_RANK = 64
ICLR_RANK = 64
GATE_RANK = 128
RWKV_IN = 3 * RWKV_WIDTH + 2 * DECAY_RANK + 2 * ICLR_RANK + GATE_RANK
DECAY_SCALE = 0.606531
GN_EPS = 64e-5
N_EXPERTS = 64
TOP_K = 6
N_GROUPS = 8
GROUP_SIZE = N_EXPERTS // N_GROUPS
TOPK_GROUPS = 4
EXPERT_FF = 256
ROUTED_SCALE = 2.5
EPS = 1e-6

LANE = 128
HEAD_PAD = LANE
CHUNK = 64
QUAD_HEADS = 4
QUAD = QUAD_HEADS * RWKV_HEAD
N_QUADS = RWKV_HEADS // QUAD_HEADS
SCAN_SEQS = 4
VMEM_LIMIT = 56 * 1024 * 1024
TM_MOE = 256
MIX_TILES = 1
SEG_SHIFT = 4
SEG = 1 << SEG_SHIFT
SLOT_ROWS = 2560
MAX_CHUNKS = SLOT_ROWS // SEG
CHUNK_GROUP_SHIFT = 3
CHUNK_GROUP = 1 << CHUNK_GROUP_SHIFT
SEL_BLOCK = 512
COMMON_BLOCKS = 4
FFN_ROWS = 1024


def _cparams(*sem):
    return pltpu.CompilerParams(dimension_semantics=sem, vmem_limit_bytes=VMEM_LIMIT)


def _dot(a, b, nt=False):
    dims = (((1,), (1,)), ((), ())) if nt else (((1,), (0,)), ((), ()))
    return lax.dot_general(a, b, dims, preferred_element_type=f32)


def _split2(x):
    hi = x.astype(bf16)
    lo = (x - hi.astype(f32)).astype(bf16)
    return hi, lo


def _split3(x):
    h1 = x.astype(bf16)
    r1 = x - h1.astype(f32)
    h2 = r1.astype(bf16)
    h3 = (r1 - h2.astype(f32)).astype(bf16)
    return h1, h2, h3


def _mm(a, b, nt=False, passes=3):
    if passes == 1:
        return _dot(a.astype(bf16), b.astype(bf16), nt)
    ah, al = _split2(a)
    bh, bl = _split2(b)
    return _dot(ah, bh, nt) + (_dot(al, bh, nt) + _dot(ah, bl, nt))


def _seg_sum(x, seg_bf16):
    return _dot(x.astype(bf16), seg_bf16)


def _rms(x, g):
    return x * lax.rsqrt(jnp.mean(x * x, axis=-1, keepdims=True) + EPS) * g


def _sigmoid(x):
    return jax.nn.sigmoid(x)


def _full_spec(a, grid_rank=1):
    zeros = (0,) * a.ndim
    if grid_rank == 1:
        return pl.BlockSpec(a.shape, lambda i: zeros)
    if grid_rank == 2:
        return pl.BlockSpec(a.shape, lambda i, j: zeros)
    return pl.BlockSpec(a.shape, lambda i, j, k: zeros)


def _mod_kernel(c_ref, w_ref, b_ref, o_ref):
    c = c_ref[...]
    s = c * _sigmoid(c)
    o_ref[...] = _mm(s, w_ref[...]) + b_ref[...]


def _modulation(cc, w_ada, b_ada):
    rows, n = cc.shape[0], w_ada.shape[1]
    tn = 512
    return pl.pallas_call(
        _mod_kernel,
        grid=(n // tn,),
        in_specs=[pl.BlockSpec((rows, D_MODEL), lambda j: (0, 0)),
                  pl.BlockSpec((D_MODEL, tn), lambda j: (0, j)),
                  pl.BlockSpec((1, tn), lambda j: (0, j))],
        out_specs=pl.BlockSpec((rows, tn), lambda j: (0, j)),
        out_shape=jax.ShapeDtypeStruct((rows, n), f32),
        compiler_params=_cparams("parallel"),
        name="modulation",
    )(cc, w_ada, b_ada.reshape(1, n))


def _in_kernel(*refs, rope):
    if rope:
        (x_ref, mod_ref, g_ref, wq_ref, wk_ref, wv_ref, wr_ref, qg_ref, kg_ref, cos_ref, sin_ref,
         q_ref, ko_ref, vo_ref, kp_ref, zr_ref) = refs
    else:
        (x_ref, mod_ref, g_ref, wq_ref, wk_ref, wv_ref, wr_ref, qg_ref, kg_ref,
         q_ref, ko_ref, vo_ref, kp_ref, zr_ref) = refs
    x = x_ref[...]
    m = mod_ref[...]
    h = _rms(x, g_ref[...])
    h = h * (1.0 + m[:, D_MODEL:2 * D_MODEL]) + m[:, 0:D_MODEL]
    hb = h.astype(bf16)
    zr_ref[...] = _dot(hb, wr_ref[...])
    zq = _dot(hb, wq_ref[...])
    zk = _dot(hb, wk_ref[...])
    zv = _dot(hb, wv_ref[...])

    lane = lax.broadcasted_iota(jnp.int32, (1, LANE), 1)
    first_half = (lane & (ROPE_HALF - 1)) < ROPE_FREQS

    def head_norm(z, gain):
        ms = jnp.sum(z * z, axis=-1, keepdims=True) * (1.0 / HEAD_DIM)
        return z * lax.rsqrt(ms + EPS) * gain

    def rotate(y):
        if not rope:
            return y
        partner = jnp.where(first_half, pltpu.roll(y, LANE - ROPE_FREQS, 1), pltpu.roll(y, ROPE_FREQS, 1))
        return y * cos_ref[...] + partner * sin_ref[...]

    for hh in range(ATTN_HEADS):
        sl = slice(hh * HEAD_PAD, (hh + 1) * HEAD_PAD)
        q_ref[:, sl] = rotate(head_norm(zq[:, sl], qg_ref[...])).astype(bf16)
    kn = [head_norm(zk[:, hh * HEAD_PAD:(hh + 1) * HEAD_PAD], kg_ref[...]) for hh in range(KV_HEADS)]
    ko_ref[...] = kn[0] + pltpu.roll(kn[1], HEAD_DIM, 1)
    vo_ref[...] = zv
    for hh in range(KV_HEADS):
        sl = slice(hh * HEAD_PAD, (hh + 1) * HEAD_PAD)
        kp_ref[:, sl] = rotate(kn[hh]).astype(bf16)


def _in_proj(x2, mod3, mod_row, g_pre, wq, wk, wv, wr, qg, kg, rope_tabs, seq_len, tm):
    n = x2.shape[0]
    rope = rope_tabs is not None
    in_specs = [pl.BlockSpec((tm, D_MODEL), lambda i: (i, 0)),
                pl.BlockSpec((None, 1, 6 * D_MODEL), lambda i: (mod_row(i, tm), 0, 0)),
                _full_spec(g_pre), _full_spec(wq), _full_spec(wk), _full_spec(wv), _full_spec(wr),
                _full_spec(qg), _full_spec(kg)]
    args = [x2, mod3, g_pre, wq, wk, wv, wr, qg, kg]
    if rope:
        blocks_per_seq = seq_len // tm
        in_specs += [pl.BlockSpec((tm, LANE), lambda i: (i % blocks_per_seq, 0))] * 2
        args += list(rope_tabs)
    out_shape = [jax.ShapeDtypeStruct((n, ATTN_HEADS * HEAD_PAD), bf16),
                 jax.ShapeDtypeStruct((n, KV_HEADS * HEAD_DIM), f32),
                 jax.ShapeDtypeStruct((n, KV_HEADS * HEAD_DIM), f32),
                 jax.ShapeDtypeStruct((n, KV_HEADS * HEAD_PAD), bf16),
                 jax.ShapeDtypeStruct((n, RWKV_IN), f32)]
    out_specs = [pl.BlockSpec((tm, s.shape[1]), lambda i: (i, 0)) for s in out_shape]
    return pl.pallas_call(
        functools.partial(_in_kernel, rope=rope),
        grid=(n // tm,), in_specs=in_specs, out_specs=out_specs, out_shape=out_shape,
        compiler_params=_cparams("parallel"), name="in_proj",
    )(*args)


def _attn_kernel(q_ref, k_ref, vt_ref, o_ref):
    tq = q_ref.shape[0]
    heads_t = []
    for kv in range(KV_HEADS):
        kh = k_ref[:, kv * HEAD_PAD:(kv + 1) * HEAD_PAD]
        vt = vt_ref[kv * HEAD_DIM:(kv + 1) * HEAD_DIM, :]
        q4 = jnp.concatenate(
            [q_ref[:, (GQA_GROUP * kv + g) * HEAD_PAD:(GQA_GROUP * kv + g + 1) * HEAD_PAD]
             for g in range(GQA_GROUP)], axis=0)
        st = _dot(kh, q4, nt=True)
        mx = jnp.max(st, axis=0, keepdims=True)
        p = jnp.exp(st - mx)
        den = jnp.sum(p, axis=0, keepdims=True)
        ot = _dot(vt, p.astype(bf16)) / den
        heads_t += [ot[:, g * tq:(g + 1) * tq] for g in range(GQA_GROUP)]
    o_ref[...] = jnp.concatenate(heads_t, axis=0).T


def _attention(q_pad, k_full, vt_full, batch, seq_len, tq):
    n = q_pad.shape[0]
    kv_len = k_full.shape[1]
    nq = seq_len // tq
    return pl.pallas_call(
        _attn_kernel,
        grid=(batch, nq),
        in_specs=[pl.BlockSpec((tq, ATTN_HEADS * HEAD_PAD), lambda b, j: (b * nq + j, 0)),
                  pl.BlockSpec((None, kv_len, KV_HEADS * HEAD_PAD), lambda b, j: (b, 0, 0)),
                  pl.BlockSpec((None, KV_HEADS * HEAD_DIM, kv_len), lambda b, j: (b, 0, 0))],
        out_specs=pl.BlockSpec((tq, ATTN_HEADS * HEAD_DIM), lambda b, j: (b * nq + j, 0)),
        out_shape=jax.ShapeDtypeStruct((n, ATTN_HEADS * HEAD_DIM), f32),
        compiler_params=_cparams("parallel", "parallel"), name="attention",
    )(q_pad, k_full, vt_full)


def _prep_kernel(z_ref, zp_ref, zn_ref, mup_ref, mun_ref, w0_ref, w2_ref, a0_ref, a2_ref, g2_ref,
                 kkw_ref, ka_ref, rk_ref, seg_ref,
                 r_ref, k_ref, v_ref, kk_ref, lw_ref, a_ref, g_ref, bonus_ref, *, seq_len, tm):
    i = pl.program_id(0)
    z = z_ref[...]
    row = lax.broadcasted_iota(jnp.int32, (tm, 1), 0)
    seq_start = ((i * tm) % seq_len) == 0
    seq_end = (((i + 1) * tm) % seq_len) == 0
    prow = jnp.where(seq_start, 0.0, zp_ref[7:8, :])
    nrow = jnp.where(seq_end, 0.0, zn_ref[0:1, :])
    prev = jnp.where(row == 0, prow, pltpu.roll(z, 1, 0))
    nxt = jnp.where(row == tm - 1, nrow, pltpu.roll(z, tm - 1, 0))
    zs = z + mup_ref[...] * (prev - z) + mun_ref[...] * (nxt - z)
    w = RWKV_WIDTH
    r = zs[:, 0:w]
    k = zs[:, w:2 * w]
    v = zs[:, 2 * w:3 * w]
    dw = zs[:, 3 * w:3 * w + 2 * DECAY_RANK]
    da = zs[:, 3 * w + 2 * DECAY_RANK:3 * w + 2 * DECAY_RANK + 2 * ICLR_RANK]
    dg = zs[:, 3 * w + 2 * DECAY_RANK + 2 * ICLR_RANK:]
    lw = -DECAY_SCALE * _sigmoid(w0_ref[...] + _mm(jnp.tanh(dw), w2_ref[...], passes=1))
    a = _sigmoid(a0_ref[...] + _mm(da, a2_ref[...], passes=1))
    g = _mm(_sigmoid(dg), g2_ref[...], passes=1)
    kk = k * kkw_ref[...]
    seg = seg_ref[...]
    kk = kk * lax.rsqrt(_seg_sum(kk * kk, seg) + 1e-12)
    ka = ka_ref[...]
    ke_sum = k * (2.0 + (a[:, 0:w] + a[:, w:2 * w] - 2.0) * ka)
    bonus = _seg_sum(r * ke_sum * rk_ref[...], seg) * v
    r_ref[...] = r
    k_ref[...] = k
    v_ref[...] = v
    kk_ref[...] = kk
    lw_ref[...] = lw
    a_ref[...] = a
    g_ref[...] = g
    bonus_ref[...] = bonus


def _rwkv_prep(zr, p, seq_len, tm):
    n = zr.shape[0]
    w = RWKV_WIDTH
    nb8 = n // 8
    consts = [p['mu_prev'], p['mu_next'], p['w0'], p['w2bd'], p['a0'], p['a2bd'], p['g2'],
              p['k_k'], p['k_a'], p['r_k'], p['seg']]
    in_specs = [pl.BlockSpec((tm, RWKV_IN), lambda i: (i, 0)),
                pl.BlockSpec((8, RWKV_IN), lambda i: (jnp.maximum(i * (tm // 8) - 1, 0), 0)),
                pl.BlockSpec((8, RWKV_IN), lambda i: (jnp.minimum((i + 1) * (tm // 8), nb8 - 1), 0))]
    in_specs += [_full_spec(c) for c in consts]
    widths = [w, w, w, w, 2 * w, 2 * w, w, w]
    out_shape = [jax.ShapeDtypeStruct((n, wd), f32) for wd in widths]
    out_specs = [pl.BlockSpec((tm, wd), lambda i: (i, 0)) for wd in widths]
    return pl.pallas_call(
        functools.partial(_prep_kernel, seq_len=seq_len, tm=tm),
        grid=(n // tm,), in_specs=in_specs, out_specs=out_specs, out_shape=out_shape,
        compiler_params=_cparams("parallel"), name="rwkv_prep",
    )(zr, zr, zr, *consts)


def _scan_kernel(rf_ref, kf_ref, vf_ref, kkf_ref, lwf_ref, af_ref,
                 rb_ref, kb_ref, vb_ref, kkb_ref, lwb_ref, ab_ref, ka_ref, s0_ref,
                 of_ref, ob_ref, sfin_ref, st_ref):
    c = pl.program_id(1)

    @pl.when(c == 0)
    def _():
        st_ref[...] = s0_ref[...]

    n_seq = rf_ref.shape[0]
    row = lax.broadcasted_iota(jnp.int32, (CHUNK, CHUNK), 0)
    col = lax.broadcasted_iota(jnp.int32, (CHUNK, CHUNK), 1)
    qr = lax.broadcasted_iota(jnp.int32, (QUAD, QUAD), 0)
    qc = lax.broadcasted_iota(jnp.int32, (QUAD, QUAD), 1)
    head_mask = (qr & -RWKV_HEAD) == (qc & -RWKV_HEAD)
    head_mask_b = jnp.where(head_mask, 1.0, 0.0).astype(bf16)
    tr = lax.broadcasted_iota(jnp.int32, (CHUNK, QUAD), 0)
    tc = lax.broadcasted_iota(jnp.int32, (CHUNK, QUAD), 1) & (CHUNK - 1)
    eye = jnp.where(tr == tc, 1.0, 0.0).astype(f32)
    ka = ka_ref[...]

    def bd(x):
        xb = x.astype(bf16)
        return jnp.concatenate([xb] * QUAD_HEADS, axis=0) * head_mask_b

    def mm(a, b, nt=False):
        return _dot(a.astype(bf16), b.astype(bf16), nt)

    chains = []
    for d, (r_ref, k_ref, v_ref, kk_ref, lw_ref, a_ref, o_ref) in enumerate(
            ((rf_ref, kf_ref, vf_ref, kkf_ref, lwf_ref, af_ref, of_ref),
             (rb_ref, kb_ref, vb_ref, kkb_ref, lwb_ref, ab_ref, ob_ref))):
        sgn = 1 - 2 * d
        tri = jnp.where((row - col) * sgn >= 0, 1.0, 0.0).astype(bf16)
        dist = (tr - tc) * sgn
        for s in range(n_seq):
            lw = lw_ref[s]
            l1, l2, l3 = _split3(lw)
            cum = _dot(tri, l1) + (_dot(tri, l2) + _dot(tri, l3))
            e_incl = jnp.exp(cum)
            e_excl = jnp.exp(cum - lw)
            e_neg = jnp.exp(-cum)
            wc = jnp.exp(jnp.sum(lw, axis=0, keepdims=True))
            kk = kk_ref[s]
            a = a_ref[s]
            bh = kk * a * e_neg
            kh = k_ref[s] * (1.0 + (a - 1.0) * ka) * e_neg
            ops = dict(at=-kk * e_excl, rt=r_ref[s] * e_incl, bh=bh, kh=kh, bw=bh * wc, kw=kh * wc,
                       v=v_ref[s], wc=wc)
            for g in range(N_QUADS):
                ch = {key: val[:, g * QUAD:(g + 1) * QUAD] for key, val in ops.items()}
                ch.update(s=s, d=d, g=g, o_ref=o_ref, strict=dist > 0, incl=dist >= 0, st=st_ref[s, d, g])
                chains.append(ch)

    lane_lo = lax.broadcasted_iota(jnp.int32, (QUAD, LANE), 1) < RWKV_HEAD
    for ch in chains:
        ar = jnp.concatenate([ch['at'], ch['rt']], axis=0)
        ch['ar'] = ar.astype(bf16)
        bk_t = jnp.concatenate([ch['bh'], ch['kh']], axis=0).T
        swapped = pltpu.roll(bk_t, RWKV_HEAD, 1)
        b_t = jnp.where(lane_lo, bk_t, swapped).astype(bf16)
        k_t = jnp.where(lane_lo, swapped, bk_t).astype(bf16)
        w_bk = jnp.concatenate([jnp.concatenate([b_t, b_t], axis=1) * head_mask_b,
                                jnp.concatenate([k_t, k_t], axis=1) * head_mask_b], axis=1)
        amat = _dot(ch['ar'], w_bk)
        ch['n'] = jnp.where(ch['strict'], amat[0:CHUNK, 0:QUAD], 0.0)
        ch['a_kr'] = jnp.concatenate([jnp.where(ch['strict'], amat[0:CHUNK, QUAD:2 * QUAD], 0.0),
                                      jnp.where(ch['incl'], amat[CHUNK:2 * CHUNK, QUAD:2 * QUAD], 0.0)], axis=0)
        ch['a_rb'] = jnp.where(ch['incl'], amat[CHUNK:2 * CHUNK, 0:QUAD], 0.0)
        ch['tinv'] = eye + ch['n']
    for ch in chains:
        ch['n'] = mm(ch['n'], bd(ch['n']))
    for _ in range(4):
        for ch in chains:
            both = mm(jnp.concatenate([ch['n'], ch['tinv']], axis=0), bd(ch['n']))
            ch['n'] = both[0:CHUNK]
            ch['tinv'] = ch['tinv'] + both[CHUNK:2 * CHUNK]
    for ch in chains:
        ch['tinv'] = ch['tinv'] + mm(ch['tinv'], bd(ch['n']))
        sa = _dot(ch['ar'], ch['st'].astype(bf16))
        av = mm(ch['a_kr'], bd(ch['v']))
        ch['rhs'] = sa[0:CHUNK] + av[0:CHUNK]
        ch['o'] = sa[CHUNK:2 * CHUNK] + av[CHUNK:2 * CHUNK]
    for ch in chains:
        ch['u'] = mm(ch['tinv'], bd(ch['rhs']))
    for ch in chains:
        ch['o'] = ch['o'] + mm(ch['a_rb'], bd(ch['u']))
        z_t = jnp.concatenate([ch['bw'], ch['kw'], jnp.broadcast_to(ch['wc'], (2 * CHUNK, QUAD))], axis=0).T
        upd = mm(z_t[:, 0:2 * CHUNK], jnp.concatenate([ch['u'], ch['v']], axis=0))
        decay = jnp.concatenate([z_t[:, 2 * CHUNK:4 * CHUNK]] * 2, axis=1)
        ch['st_new'] = ch['st'] * decay + jnp.where(head_mask, upd, 0.0)
    for ch in chains:
        ch['o_ref'][ch['s'], :, ch['g'] * QUAD:(ch['g'] + 1) * QUAD] = ch['o']
        st_ref[ch['s'], ch['d'], ch['g']] = ch['st_new']

    @pl.when(c == pl.num_programs(1) - 1)
    def _():
        sfin_ref[...] = st_ref[...]


def _rwkv_scan(r, k, v, kk, lw, a, k_a, s0, batch, seq_len):
    n = r.shape[0]
    w = RWKV_WIDTH
    nc = seq_len // CHUNK
    per_seq = lambda x: x.reshape(batch, seq_len, x.shape[-1])
    r, k, v, kk, lw, a = (per_seq(x) for x in (r, k, v, kk, lw, a))
    fwd = lambda b, c: (b, c, 0)
    bwd = lambda b, c: (b, nc - 1 - c, 0)
    bwd_dir = lambda b, c: (b, nc - 1 - c, 1)
    blk = lambda index_map: pl.BlockSpec((SCAN_SEQS, CHUNK, w), index_map)
    state_spec = pl.BlockSpec((SCAN_SEQS, 2, N_QUADS, QUAD, QUAD), lambda b, c: (b, 0, 0, 0, 0))
    o_shape = jax.ShapeDtypeStruct((batch, seq_len, w), f32)
    o_fw, o_bw, s_fin = pl.pallas_call(
        _scan_kernel,
        grid=(batch // SCAN_SEQS, nc),
        in_specs=[blk(fwd)] * 6 + [blk(bwd)] * 4 + [blk(bwd_dir)] * 2
                 + [pl.BlockSpec((1, w), lambda b, c: (0, 0)), state_spec],
        out_specs=[blk(fwd), blk(bwd), state_spec],
        out_shape=[o_shape, o_shape, jax.ShapeDtypeStruct((batch, 2, N_QUADS, QUAD, QUAD), f32)],
        scratch_shapes=[pltpu.VMEM((SCAN_SEQS, 2, N_QUADS, QUAD, QUAD), f32)],
        compiler_params=_cparams("parallel", "arbitrary"), name="rwkv_scan",
    )(r, k, v, kk, lw, a, r, k, v, kk, lw, a, k_a, s0)
    return o_fw.reshape(n, w), o_bw.reshape(n, w), s_fin


def _out_kernel(attn_ref, of_ref, ob_ref, bonus_ref, g_ref, x_ref, mod_ref, lng_ref, lnb_ref, seg_ref,
                wo_ref, gpm_ref, gpf_ref, rwt_ref, rb_ref,
                x1_ref, h2_ref, slot_ref, sw_ref, cnt_ref):
    outs = []
    for t in range(x_ref.shape[0] // TM_MOE):
        rows = pl.ds(t * TM_MOE, TM_MOE)
        outs.append(_out_tile(attn_ref.at[rows], of_ref.at[rows], ob_ref.at[rows], bonus_ref.at[rows],
                              g_ref.at[rows], x_ref.at[rows], mod_ref, lng_ref, lnb_ref, seg_ref, wo_ref,
                              gpm_ref, gpf_ref, rwt_ref, rb_ref))
    for t, (x1, h2, slots, sw, cnt) in enumerate(outs):
        rows = pl.ds(t * TM_MOE, TM_MOE)
        x1_ref[rows, :] = x1
        h2_ref[rows, :] = h2
        slot_ref[:, rows] = slots
        sw_ref[rows, :] = sw
        cnt_ref[t] = cnt


def _out_tile(attn_ref, of_ref, ob_ref, bonus_ref, g_ref, x_ref, mod_ref, lng_ref, lnb_ref, seg_ref,
              wo_ref, gpm_ref, gpf_ref, rwt_ref, rb_ref):
    tm = x_ref.shape[0]
    seg = seg_ref[...]
    inv = 1.0 / RWKV_HEAD
    o = of_ref[...] + ob_ref[...]
    mu = _seg_sum(o, seg) * inv
    dl = o - mu
    var = _seg_sum(dl * dl, seg) * inv
    on = dl * lax.rsqrt(var + GN_EPS) * lng_ref[...] + lnb_ref[...]
    rw = (on + bonus_ref[...]) * g_ref[...]
    half = ATTN_HEADS * HEAD_DIM
    y = _dot(attn_ref[...].astype(bf16), wo_ref[0:half, :]) + _dot(rw.astype(bf16), wo_ref[half:, :])
    m = mod_ref[...]
    gt_a = m[:, 2 * D_MODEL:3 * D_MODEL]
    sh_f = m[:, 3 * D_MODEL:4 * D_MODEL]
    sc_f = m[:, 4 * D_MODEL:5 * D_MODEL]
    x1 = x_ref[...] + gt_a * _rms(y, gpm_ref[...])
    h2 = _rms(x1, gpf_ref[...]) * (1.0 + sc_f) + sh_f

    logits = _mm(rwt_ref[...], h2, nt=True)
    scores = _sigmoid(logits)
    biased = scores + rb_ref[...]
    neg = -jnp.inf
    shape3 = (N_GROUPS, GROUP_SIZE, tm)
    b3 = biased.reshape(shape3)
    s3 = scores.reshape(shape3)
    in_grp = lax.broadcasted_iota(jnp.int32, shape3, 1)
    grp = lax.broadcasted_iota(jnp.int32, shape3, 0)
    m1 = jnp.max(b3, axis=1, keepdims=True)
    i1 = jnp.min(jnp.where(b3 == m1, in_grp, GROUP_SIZE), axis=1, keepdims=True)
    m2 = jnp.max(jnp.where(in_grp == i1, neg, b3), axis=1, keepdims=True)
    gscore = m1 + m2
    gidx = lax.broadcasted_iota(jnp.int32, (N_GROUPS, 1, tm), 0)
    gsel = jnp.zeros((N_GROUPS, 1, tm), f32)
    cur = gscore
    for _ in range(TOPK_GROUPS):
        mx = jnp.max(cur, axis=0, keepdims=True)
        ii = jnp.min(jnp.where(cur == mx, gidx, N_GROUPS), axis=0, keepdims=True)
        hit = gidx == ii
        gsel = jnp.where(hit, 1.0, gsel)
        cur = jnp.where(hit, neg, cur)
    cand = jnp.where(jnp.broadcast_to(gsel, shape3) > 0.0, b3, neg)
    eidx = grp * GROUP_SIZE + in_grp
    wsel = jnp.zeros(shape3, f32)
    mask = jnp.zeros(shape3, f32)
    hits = []
    for _ in range(TOP_K):
        mx = jnp.max(jnp.max(cand, axis=1, keepdims=True), axis=0, keepdims=True)
        ii = jnp.min(jnp.min(jnp.where(cand == mx, eidx, N_EXPERTS), axis=1, keepdims=True),
                     axis=0, keepdims=True)
        hit = eidx == ii
        hits.append(hit)
        wsel = jnp.where(hit, s3, wsel)
        mask = jnp.where(hit, 1.0, mask)
        cand = jnp.where(hit, neg, cand)
    den = jnp.sum(jnp.sum(wsel, axis=1, keepdims=True), axis=0, keepdims=True)
    gates3 = wsel / den * ROUTED_SCALE

    mask2 = mask.reshape(N_EXPERTS, tm)
    cnt = jnp.sum(mask2, axis=1, keepdims=True)
    cnt_pad = jnp.floor((cnt + (SEG - 1)) * (1.0 / SEG)) * SEG
    er = lax.broadcasted_iota(jnp.int32, (N_EXPERTS, N_EXPERTS), 0)
    ec = lax.broadcasted_iota(jnp.int32, (N_EXPERTS, N_EXPERTS), 1)
    before = jnp.where(ec < er, 1.0, 0.0).astype(bf16)
    cnt_pad_l = jnp.broadcast_to(cnt_pad, (N_EXPERTS, LANE))
    seg_start = _dot(before, cnt_pad_l.astype(bf16))
    tr = lax.broadcasted_iota(jnp.int32, (tm, tm), 0)
    tc = lax.broadcasted_iota(jnp.int32, (tm, tm), 1)
    earlier = jnp.where(tr < tc, 1.0, 0.0).astype(bf16)
    rank = _dot(mask2.astype(bf16), earlier)
    slot3 = (rank + seg_start[:, 0:1]).reshape(shape3)

    def pick(hit, val3):
        return jnp.sum(jnp.sum(jnp.where(hit, val3, 0.0), axis=1, keepdims=True), axis=0)

    slots = [pick(h, slot3) for h in hits]
    wts = [pick(h, gates3) for h in hits]
    pad_rows = lambda k, val: [jnp.full((k, tm), val, f32)]
    slot_rows = jnp.concatenate(slots + pad_rows(8 - TOP_K, -1.0), axis=0).astype(jnp.int32)
    sw_t = jnp.concatenate(slots + pad_rows(8 - TOP_K, -1.0) + wts + pad_rows(LANE - 8 - TOP_K, 0.0), axis=0)
    return x1, h2.astype(bf16), slot_rows, sw_t.T, cnt_pad_l


def _mix_out(attn, o_fw, o_bw, bonus, g, x2, mod3, mod_row, p, tm):
    n = x2.shape[0]
    w = RWKV_WIDTH
    consts = [p['lnx_g'], p['lnx_b'], p['seg'], p['w_out'], p['g_post_mix'], p['g_pre_ffn'],
              p['router_wt'], p['router_b']]
    row = lambda wd: pl.BlockSpec((tm, wd), lambda i: (i, 0))
    in_specs = [row(ATTN_HEADS * HEAD_DIM), row(w), row(w),
                row(w), row(w), row(D_MODEL),
                pl.BlockSpec((None, 1, 6 * D_MODEL), lambda i: (mod_row(i, tm), 0, 0))]
    in_specs += [_full_spec(c) for c in consts]
    out_shape = [jax.ShapeDtypeStruct((n, D_MODEL), f32),
                 jax.ShapeDtypeStruct((n, D_MODEL), bf16),
                 jax.ShapeDtypeStruct((8, n), jnp.int32),
                 jax.ShapeDtypeStruct((n, LANE), f32),
                 jax.ShapeDtypeStruct((n // TM_MOE, N_EXPERTS, LANE), f32)]
    out_specs = [row(D_MODEL), row(D_MODEL), pl.BlockSpec((8, tm), lambda i: (0, i)), row(LANE),
                 pl.BlockSpec((tm // TM_MOE, N_EXPERTS, LANE), lambda i: (i, 0, 0))]
    return pl.pallas_call(
        _out_kernel, grid=(n // tm,), in_specs=in_specs, out_specs=out_specs, out_shape=out_shape,
        compiler_params=_cparams("parallel"), name="mix_out",
    )(attn, o_fw, o_bw, bonus, g, x2, mod3, *consts)


def _chunk_groups(tile, nch_sm):
    return lax.shift_right_logical(nch_sm[tile] + (CHUNK_GROUP - 1), CHUNK_GROUP_SHIFT)


def _start_chunks(tile, dst_sm, nch_sm, make_copy):
    def body(g, carry):
        for u in range(CHUNK_GROUP):
            c = g * CHUNK_GROUP + u
            make_copy(c, pl.multiple_of(dst_sm[tile * MAX_CHUNKS + c], SEG)).start()
        return carry

    lax.fori_loop(0, _chunk_groups(tile, nch_sm), body, 0)


def _wait_chunks(tile, nch_sm, make_copy):
    def body(g, carry):
        for _ in range(CHUNK_GROUP):
            make_copy(0, 0).wait()
        return carry

    lax.fori_loop(0, _chunk_groups(tile, nch_sm), body, 0)


def _dispatch_kernel(dst_sm, nch_sm, fill_sm, *refs, tile_starts):
    n_paths = len(tile_starts)
    h_refs, slot_refs = refs[:n_paths], refs[n_paths:2 * n_paths]
    xs_hbm, xc_ref, zero_ref, sems, zsem = refs[2 * n_paths:]
    i = pl.program_id(0)
    last = pl.num_programs(0) - 1
    buf = lax.rem(i, 2)

    def copy_from(b):
        def make(chunk, global_row):
            src = xc_ref.at[b, pl.ds(pl.multiple_of(chunk * SEG, SEG), SEG)]
            return pltpu.make_async_copy(src, xs_hbm.at[pl.ds(global_row, SEG)], sems.at[b])
        return make

    @pl.when(i == 0)
    def _():
        zero_ref[...] = jnp.zeros_like(zero_ref)
        n_tiles = xs_hbm.shape[0] // FFN_ROWS

        def tail_copy(row):
            return pltpu.make_async_copy(zero_ref.at[pl.ds(0, SEG)], xs_hbm.at[pl.ds(row, SEG)], zsem)

        def tile_copy(t):
            return pltpu.make_async_copy(zero_ref, xs_hbm.at[pl.ds(pl.multiple_of(t * FFN_ROWS, FFN_ROWS), FFN_ROWS)],
                                         zsem)

        def per_expert(e, carry):
            start = fill_sm[1 + e]
            chunks = lax.shift_right_logical(fill_sm[1 + N_EXPERTS + e], SEG_SHIFT)
            lax.fori_loop(0, chunks, lambda c, z: (tail_copy(pl.multiple_of(start + c * SEG, SEG)).start(), z)[1], 0)
            lax.fori_loop(0, chunks, lambda c, z: (tail_copy(0).wait(), z)[1], 0)
            return carry

        lax.fori_loop(0, N_EXPERTS, per_expert, 0)
        lax.fori_loop(fill_sm[0], n_tiles, lambda t, z: (tile_copy(t).start(), z)[1], 0)
        lax.fori_loop(fill_sm[0], n_tiles, lambda t, z: (tile_copy(0).wait(), z)[1], 0)

    @pl.when(i >= 2)
    def _():
        _wait_chunks(i - 2, nch_sm, copy_from(buf))

    slot, h = slot_refs[0][...], h_refs[0][...]
    for q in range(1, n_paths):
        mine = i >= tile_starts[q]
        slot = jnp.where(mine, slot_refs[q][...], slot)
        h = jnp.where(mine, h_refs[q][...], h)
    used = nch_sm[i] * SEG

    def build(rb):
        rows = lax.broadcasted_iota(jnp.int32, (SEL_BLOCK, TM_MOE), 0) + rb * SEL_BLOCK
        sel = jnp.zeros((SEL_BLOCK, TM_MOE), f32)
        for j in range(TOP_K):
            sel = jnp.where(rows == slot[j:j + 1, :], 1.0, sel)
        xc_ref[buf, rb * SEL_BLOCK:(rb + 1) * SEL_BLOCK, :] = _dot(sel.astype(bf16), h).astype(bf16)

    for rb in range(COMMON_BLOCKS):
        build(rb)
    for rb in range(COMMON_BLOCKS, SLOT_ROWS // SEL_BLOCK):
        pl.when(rb * SEL_BLOCK < used)(functools.partial(build, rb))

    _start_chunks(i, dst_sm, nch_sm, copy_from(buf))

    @pl.when(i == last)
    def _():
        @pl.when(i >= 1)
        def _():
            _wait_chunks(i - 1, nch_sm, copy_from(1 - buf))
        _wait_chunks(i, nch_sm, copy_from(buf))


def _dispatch(h2s, slots, dst, nch, fill, total_rows):
    tiles = [h.shape[0] // TM_MOE for h in h2s]
    starts = [sum(tiles[:q]) for q in range(len(tiles))]

    def in_path(q):
        return lambda i: jnp.clip(i - starts[q], 0, tiles[q] - 1)

    h_specs = [pl.BlockSpec((TM_MOE, D_MODEL), lambda i, d, c, f, q=q: (in_path(q)(i), 0)) for q in range(len(tiles))]
    s_specs = [pl.BlockSpec((8, TM_MOE), lambda i, d, c, f, q=q: (0, in_path(q)(i))) for q in range(len(tiles))]
    grid_spec = pltpu.PrefetchScalarGridSpec(
        num_scalar_prefetch=3, grid=(sum(tiles),),
        in_specs=h_specs + s_specs,
        out_specs=pl.BlockSpec(memory_space=pl.ANY),
        scratch_shapes=[pltpu.VMEM((2, SLOT_ROWS, D_MODEL), bf16), pltpu.VMEM((FFN_ROWS, D_MODEL), bf16),
                        pltpu.SemaphoreType.DMA((2,)), pltpu.SemaphoreType.DMA(())])
    return pl.pallas_call(
        functools.partial(_dispatch_kernel, tile_starts=tuple(starts)), grid_spec=grid_spec,
        out_shape=jax.ShapeDtypeStruct((total_rows, D_MODEL), bf16),
        compiler_params=_cparams("arbitrary"), name="moe_dispatch",
    )(dst, nch, fill, *h2s, *slots)


def _ffn_kernel(te_sm, nv_sm, xs_ref, wg_ref, wu_ref, wd_ref, ys_ref, wg_b, wu_b, wd_b):
    r = pl.program_id(0)
    valid = r < nv_sm[0]

    @pl.when(jnp.logical_and(valid, jnp.logical_or(r == 0, te_sm[r] != te_sm[jnp.maximum(r - 1, 0)])))
    def _():
        wg_b[...] = wg_ref[...].astype(bf16)
        wu_b[...] = wu_ref[...].astype(bf16)
        wd_b[...] = wd_ref[...].astype(bf16)

    @pl.when(valid)
    def _():
        x = xs_ref[...]
        gg = _dot(x, wg_b[...])
        uu = _dot(x, wu_b[...])
        hm = gg * _sigmoid(gg) * uu
        ys_ref[...] = _dot(hm.astype(bf16), wd_b[...]).astype(bf16)

    @pl.when(r >= nv_sm[0])
    def _():
        ys_ref[...] = jnp.zeros_like(ys_ref)


def _expert_ffn(xs, tile_expert, n_valid, wg, wu, wd):
    total_rows = xs.shape[0]
    last = lambda r, nv: jnp.minimum(r, nv[0] - 1)
    grid_spec = pltpu.PrefetchScalarGridSpec(
        num_scalar_prefetch=2, grid=(total_rows // FFN_ROWS,),
        in_specs=[pl.BlockSpec((FFN_ROWS, D_MODEL), lambda r, te, nv: (last(r, nv), 0)),
                  pl.BlockSpec((None, D_MODEL, EXPERT_FF), lambda r, te, nv: (te[last(r, nv)], 0, 0)),
                  pl.BlockSpec((None, D_MODEL, EXPERT_FF), lambda r, te, nv: (te[last(r, nv)], 0, 0)),
                  pl.BlockSpec((None, EXPERT_FF, D_MODEL), lambda r, te, nv: (te[last(r, nv)], 0, 0))],
        out_specs=pl.BlockSpec((FFN_ROWS, D_MODEL), lambda r, te, nv: (r, 0)),
        scratch_shapes=[pltpu.VMEM((D_MODEL, EXPERT_FF), bf16), pltpu.VMEM((D_MODEL, EXPERT_FF), bf16),
                        pltpu.VMEM((EXPERT_FF, D_MODEL), bf16)])
    return pl.pallas_call(
        _ffn_kernel, grid_spec=grid_spec,
        out_shape=jax.ShapeDtypeStruct((total_rows, D_MODEL), bf16),
        compiler_params=_cparams("arbitrary"), name="moe_ffn",
    )(tile_expert, n_valid, xs, wg, wu, wd)


def _combine_kernel(dst_sm, nch_sm, sw_ref, h_ref, x1_ref, mod_ref, gpost_ref, swg_ref, swu_ref, swd_ref,
                    ys_hbm, y_ref, yc_ref, acc_ref, sems, *, tile_offset):
    step = pl.program_id(0)
    i = step + tile_offset
    buf = lax.rem(step, 2)

    def copy_into(b):
        def make(chunk, global_row):
            dst = yc_ref.at[b, pl.ds(pl.multiple_of(chunk * SEG, SEG), SEG)]
            return pltpu.make_async_copy(ys_hbm.at[pl.ds(global_row, SEG)], dst, sems.at[b])
        return make

    @pl.when(step == 0)
    def _():
        yc_ref[...] = jnp.zeros_like(yc_ref)
        _start_chunks(i, dst_sm, nch_sm, copy_into(buf))

    @pl.when(step + 1 < pl.num_programs(0))
    def _():
        _start_chunks(i + 1, dst_sm, nch_sm, copy_into(1 - buf))

    h = h_ref[...]
    gg = _dot(h, swg_ref[...])
    uu = _dot(h, swu_ref[...])
    acc_ref[...] = _dot((gg * _sigmoid(gg) * uu).astype(bf16), swd_ref[...])
    _wait_chunks(i, nch_sm, copy_into(buf))
    sw = sw_ref[...]
    used = nch_sm[i] * SEG
    def weighted(rb):
        cols = (lax.broadcasted_iota(jnp.int32, (TM_MOE, SEL_BLOCK), 1) + rb * SEL_BLOCK).astype(f32)
        wmat = jnp.zeros((TM_MOE, SEL_BLOCK), f32)
        for j in range(TOP_K):
            wmat = jnp.where(cols == sw[:, j:j + 1], sw[:, 8 + j:9 + j], wmat)
        rows = lax.broadcasted_iota(jnp.int32, (SEL_BLOCK, 1), 0) + rb * SEL_BLOCK
        yc = jnp.where(rows < used, yc_ref[buf, rb * SEL_BLOCK:(rb + 1) * SEL_BLOCK, :], jnp.zeros((), bf16))
        return _dot(wmat.astype(bf16), yc)

    moe = acc_ref[...]
    for rb in range(COMMON_BLOCKS):
        moe = moe + weighted(rb)
    acc_ref[...] = moe
    for rb in range(COMMON_BLOCKS, SLOT_ROWS // SEL_BLOCK):
        @pl.when(rb * SEL_BLOCK < used)
        def _():
            acc_ref[...] += weighted(rb)
    gt_f = mod_ref[...][:, 5 * D_MODEL:6 * D_MODEL]
    y_ref[...] = x1_ref[...] + gt_f * _rms(acc_ref[...], gpost_ref[...])


def _combine(ys, dst, nch, sw, h2, x1, mod3, mod_row, g_post, swg, swu, swd, tile_offset):
    n = h2.shape[0]
    row = lambda wd: pl.BlockSpec((TM_MOE, wd), lambda i, d, c: (i, 0))
    const = lambda a: pl.BlockSpec(a.shape, lambda i, d, c: (0,) * a.ndim)
    grid_spec = pltpu.PrefetchScalarGridSpec(
        num_scalar_prefetch=2, grid=(n // TM_MOE,),
        in_specs=[row(LANE), row(D_MODEL), row(D_MODEL),
                  pl.BlockSpec((None, 1, 6 * D_MODEL), lambda i, d, c: (mod_row(i, TM_MOE), 0, 0)),
                  const(g_post), const(swg), const(swu), const(swd),
                  pl.BlockSpec(memory_space=pl.ANY)],
        out_specs=row(D_MODEL),
        scratch_shapes=[pltpu.VMEM((2, SLOT_ROWS, D_MODEL), bf16), pltpu.VMEM((TM_MOE, D_MODEL), f32),
                        pltpu.SemaphoreType.DMA((2,))])
    return pl.pallas_call(
        functools.partial(_combine_kernel, tile_offset=tile_offset), grid_spec=grid_spec,
        out_shape=jax.ShapeDtypeStruct((n, D_MODEL), f32),
        compiler_params=_cparams("arbitrary"), name="moe_combine",
    )(dst, nch, sw, h2, x1, mod3, g_post, swg, swu, swd, ys)


def _sparse_moe(paths, mod3, p):
    cnt = jnp.concatenate([q['cnt'][:, :, 0] for q in paths], axis=0).astype(jnp.int32)
    n_tiles = cnt.shape[0]
    before = jnp.cumsum(cnt, axis=0) - cnt
    rows_e = jnp.sum(cnt, axis=0)
    region = (rows_e + FFN_ROWS - 1) // FFN_ROWS * FFN_ROWS
    region_end = jnp.cumsum(region)
    region_start = region_end - region
    worst = n_tiles * (TOP_K * TM_MOE + N_EXPERTS * (SEG - 1)) + N_EXPERTS * (FFN_ROWS - SEG)
    total_rows = -(-worst // FFN_ROWS) * FFN_ROWS + FFN_ROWS
    spare_row = total_rows - FFN_ROWS
    first_row = jnp.arange(total_rows // FFN_ROWS, dtype=jnp.int32) * FFN_ROWS
    ended = jnp.sum((region_end[None, :] <= first_row[:, None]).astype(jnp.int32), axis=1)
    tile_expert = jnp.minimum(ended, N_EXPERTS - 1).astype(jnp.int32)
    n_valid = (region_end[-1:] // FFN_ROWS).astype(jnp.int32)

    seg_end = jnp.cumsum(cnt, axis=1)
    seg_start = seg_end - cnt
    nch = (seg_end[:, -1] // SEG).astype(jnp.int32)
    chunk_row = jnp.arange(MAX_CHUNKS, dtype=jnp.int32) * SEG
    chunk_e = jnp.sum((seg_end[:, None, :] <= chunk_row[None, :, None]).astype(jnp.int32), axis=2)
    own = chunk_e[:, :, None] == jnp.arange(N_EXPERTS, dtype=jnp.int32)[None, None, :]
    shift = (region_start[None, :] + before - seg_start)[:, None, :]
    dst = jnp.sum(jnp.where(own, shift, 0), axis=2) + chunk_row[None, :]
    chunk_id = jnp.arange(MAX_CHUNKS, dtype=jnp.int32)[None, :]
    tile_id = jnp.arange(n_tiles, dtype=jnp.int32)[:, None]
    spare = spare_row + ((tile_id % 2) * CHUNK_GROUP + chunk_id % CHUNK_GROUP) * SEG
    dst = jnp.where(chunk_id < nch[:, None], dst, spare).astype(jnp.int32).reshape(-1)

    fill = jnp.concatenate([n_valid, region_start + rows_e, region - rows_e]).astype(jnp.int32)
    xs = _dispatch([q['h2'] for q in paths], [q['slots'] for q in paths], dst, nch, fill, total_rows)
    ys = _expert_ffn(xs, tile_expert, n_valid, p['expert_wg'], p['expert_wu'], p['expert_wd'])
    outs, tile_offset = [], 0
    for q in paths:
        outs.append(_combine(ys, dst, nch, q['sw'], q['h2'], q['x1'], mod3, q['mod_row'],
                             p['g_post_ffn'], p['shared_wg'], p['shared_wu'], p['shared_wd'], tile_offset))
        tile_offset += q['h2'].shape[0] // TM_MOE
    return outs


def _pad_heads(w, heads):
    rows = w.shape[0]
    w = w.reshape(rows, heads, HEAD_DIM)
    return jnp.pad(w, ((0, 0), (0, 0), (0, HEAD_PAD - HEAD_DIM))).reshape(rows, heads * HEAD_PAD)


def _block_diag2(m):
    z = jnp.zeros_like(m[0])
    return jnp.concatenate([jnp.concatenate([m[0], z], axis=1), jnp.concatenate([z, m[1]], axis=1)], axis=0)


def _prepare_params(w):
    aw, kw = ATTN_HEADS * HEAD_DIM, KV_HEADS * HEAD_DIM
    w_in = w['w_in']
    lane_row = lambda v: v.reshape(1, -1).astype(f32)
    head_id = np.arange(RWKV_WIDTH) // RWKV_HEAD
    seg = jnp.asarray(head_id[:, None] == head_id[None, :], dtype=bf16)
    gain_pad = lambda gvec: jnp.pad(gvec, (0, HEAD_PAD - HEAD_DIM)).reshape(1, HEAD_PAD)
    return dict(
        g_pre_mix=lane_row(w['g_pre_mix']), g_post_mix=lane_row(w['g_post_mix']),
        g_pre_ffn=lane_row(w['g_pre_ffn']), g_post_ffn=lane_row(w['g_post_ffn']),
        wq=_pad_heads(w_in[:, :aw], ATTN_HEADS).astype(bf16),
        wk=_pad_heads(w_in[:, aw:aw + kw], KV_HEADS).astype(bf16),
        wv=w_in[:, aw + kw:aw + 2 * kw].astype(bf16),
        wr=w_in[:, aw + 2 * kw:].astype(bf16),
        qg=gain_pad(w['q_gain'] * ATTN_SCALE), kg=gain_pad(w['k_gain']),
        mu_prev=lane_row(w['mu_prev']), mu_next=lane_row(w['mu_next']),
        w0=lane_row(w['decay_w0']), w2bd=_block_diag2(w['decay_w2']),
        a0=lane_row(w['iclr_a0']), a2bd=_block_diag2(w['iclr_a2']),
        g2=w['gate_g2'], k_k=lane_row(w['k_k']), k_a=lane_row(w['k_a']), r_k=lane_row(w['r_k']),
        lnx_g=lane_row(w['lnx_g']), lnx_b=lane_row(w['lnx_b']), seg=seg,
        w_out=w['w_out'].astype(bf16),
        router_wt=w['router_w'].T, router_b=w['router_b'].reshape(N_EXPERTS, 1),
        expert_wg=w['expert_wg'], expert_wu=w['expert_wu'], expert_wd=w['expert_wd'],
        shared_wg=w['shared_wg'].astype(bf16), shared_wu=w['shared_wu'].astype(bf16),
        shared_wd=w['shared_wd'].astype(bf16),
    )


def _rope_tables(seq_len):
    rows = seq_len // GRID_W
    row_idx = jnp.repeat(jnp.arange(rows, dtype=jnp.int32), GRID_W)
    col_idx = jnp.tile(jnp.arange(GRID_W, dtype=jnp.int32), rows)
    inv = ROPE_THETA ** (-jnp.arange(0, ROPE_HALF, 2, dtype=f32) / ROPE_HALF)
    ang = jnp.stack([row_idx.astype(f32)[:, None] * inv, col_idx.astype(f32)[:, None] * inv], axis=1)
    cos, sin = jnp.cos(ang), jnp.sin(ang)
    cos_l = jnp.stack([cos, cos], axis=2).reshape(seq_len, HEAD_DIM)
    sin_l = jnp.stack([-sin, sin], axis=2).reshape(seq_len, HEAD_DIM)
    pad = ((0, 0), (0, HEAD_PAD - HEAD_DIM))
    return jnp.pad(cos_l, pad), jnp.pad(sin_l, pad)


def _states_to_pairs(s):
    b = s.shape[0]
    s = s.reshape(b, 2, N_QUADS, QUAD_HEADS, RWKV_HEAD, RWKV_HEAD)
    eye = jnp.eye(QUAD_HEADS, dtype=s.dtype)
    return jnp.einsum('bdphvk,hg->bdphkgv', s, eye).reshape(b, 2, N_QUADS, QUAD, QUAD)


def _pairs_to_states(s):
    b = s.shape[0]
    s = s.reshape(b, 2, N_QUADS, QUAD_HEADS, RWKV_HEAD, QUAD_HEADS, RWKV_HEAD)
    diag = jnp.stack([s[:, :, :, h, :, h, :] for h in range(QUAD_HEADS)], axis=3)
    return jnp.swapaxes(diag, -1, -2).reshape(b, 2, RWKV_HEADS, RWKV_HEAD, RWKV_HEAD)


def _trunk(x, mod3, mod_row, p, rope_tabs, past_k, past_v, s0_pairs, tm_in, tq, tm_prep):
    b, t, _ = x.shape
    n = b * t
    x2 = x.reshape(n, D_MODEL)
    q_pad, k_own, v_own, k_pad, zr = _in_proj(
        x2, mod3, mod_row, p['g_pre_mix'], p['wq'], p['wk'], p['wv'], p['wr'], p['qg'], p['kg'],
        rope_tabs, t, tm_in)
    k_full = k_pad.reshape(b, t, KV_HEADS * HEAD_PAD)
    v_full = v_own.reshape(b, t, KV_HEADS * HEAD_DIM)
    if past_k is not None:
        k_past = _pad_heads(past_k.reshape(-1, KV_HEADS * HEAD_DIM), KV_HEADS).astype(bf16)
        k_full = jnp.concatenate([k_past.reshape(b, -1, KV_HEADS * HEAD_PAD), k_full], axis=1)
        v_full = jnp.concatenate([past_v.reshape(b, -1, KV_HEADS * HEAD_DIM), v_full], axis=1)
    vt_full = jnp.swapaxes(v_full, 1, 2).astype(bf16)
    attn = _attention(q_pad, k_full, vt_full, b, t, tq)
    r, k, v, kk, lw, a, g, bonus = _rwkv_prep(zr, p, t, tm_prep)
    o_fw, o_bw, s_fin = _rwkv_scan(r, k, v, kk, lw, a, p['k_a'], s0_pairs, b, t)
    x1, h2, slots, sw, cnt = _mix_out(attn, o_fw, o_bw, bonus, g, x2, mod3, mod_row, p, MIX_TILES * TM_MOE)
    return dict(h2=h2, slots=slots, sw=sw, cnt=cnt, x1=x1, mod_row=mod_row), k_own, v_own, s_fin


def kernel(x_prompt, x_sample, c, cache_k, cache_v, state_rwkv, c_ctx, w_ada, b_ada, g_pre_mix, g_post_mix,
           g_pre_ffn, g_post_ffn, w_in, w_out, q_gain, k_gain, mu_prev, mu_next, decay_w0, decay_w2, iclr_a0,
           iclr_a2, gate_g2, k_k, k_a, r_k, lnx_g, lnx_b, router_w, router_b, expert_wg, expert_wu, expert_wd,
           shared_wg, shared_wu, shared_wd):
    layer = 0
    names = ('g_pre_mix g_post_mix g_pre_ffn g_post_ffn w_in w_out q_gain k_gain mu_prev mu_next decay_w0 '
             'decay_w2 iclr_a0 iclr_a2 gate_g2 k_k k_a r_k lnx_g lnx_b router_w router_b expert_wg expert_wu '
             'expert_wd shared_wg shared_wu shared_wd').split()
    vals = (g_pre_mix, g_post_mix, g_pre_ffn, g_post_ffn, w_in, w_out, q_gain, k_gain, mu_prev, mu_next,
            decay_w0, decay_w2, iclr_a0, iclr_a2, gate_g2, k_k, k_a, r_k, lnx_g, lnx_b, router_w, router_b,
            expert_wg, expert_wu, expert_wd, shared_wg, shared_wu, shared_wd)
    p = _prepare_params({nm: v[layer] for nm, v in zip(names, vals)})

    nb, ts, _ = x_sample.shape
    npb, tp, _ = x_prompt.shape
    mod_rows = 16
    cc = jnp.zeros((mod_rows, D_MODEL), f32).at[0].set(c_ctx).at[1:1 + nb].set(c)
    mod3 = _modulation(cc, w_ada[layer], b_ada[layer]).reshape(mod_rows, 1, 6 * D_MODEL)

    zeros_state = jnp.zeros((npb, 2, N_QUADS, QUAD, QUAD), f32)
    moe_p, kc, vc, st = _trunk(x_prompt, mod3, lambda i, tm: 0, p, None, None, None, zeros_state,
                               tm_in=256, tq=tp, tm_prep=256)
    s0_lat = _states_to_pairs(state_rwkv[:, layer])
    moe_s, _, _, _ = _trunk(x_sample, mod3, lambda i, tm: 1 + (i * tm) // ts, p, _rope_tables(ts),
                            cache_k[:, layer], cache_v[:, layer], s0_lat,
                            tm_in=512, tq=256, tm_prep=256)
    y_p, y_s = _sparse_moe([moe_p, moe_s], mod3, p)
    y_p = y_p.reshape(x_prompt.shape)
    y_s = y_s.reshape(x_sample.shape)

    new_cache_k = kc.reshape(npb, 1, tp, KV_HEADS, HEAD_DIM)
    new_cache_v = vc.reshape(npb, 1, tp, KV_HEADS, HEAD_DIM)
    new_state = _pairs_to_states(st)[:, None]
    return (y_p, y_s, new_cache_k, new_cache_v, new_state)
```

```python
import functools

import numpy as np
import jax
import jax.numpy as jnp
from jax import lax
from jax.experimental import pallas as pl
from jax.experimental.pallas import tpu as pltpu

f32 = jnp.float32
bf16 = jnp.bfloat16

D_MODEL = 1024
GRID_W = 64
HEAD_DIM = 64
ATTN_HEADS = 8
KV_HEADS = 2
GQA_GROUP = ATTN_HEADS // KV_HEADS
ATTN_SCALE = HEAD_DIM ** -0.5
ROPE_THETA = 10000.0
ROPE_HALF = HEAD_DIM // 2
ROPE_FREQS = ROPE_HALF // 2
RWKV_HEAD = 64
RWKV_HEADS = 8
RWKV_WIDTH = RWKV_HEADS * RWKV_HEAD
DECAY_RANK = 64
ICLR_RANK = 64
GATE_RANK = 128
RWKV_IN = 3 * RWKV_WIDTH + 2 * DECAY_RANK + 2 * ICLR_RANK + GATE_RANK
DECAY_SCALE = 0.606531
GN_EPS = 64e-5
N_EXPERTS = 64
TOP_K = 6
N_GROUPS = 8
GROUP_SIZE = N_EXPERTS // N_GROUPS
TOPK_GROUPS = 4
EXPERT_FF = 256
ROUTED_SCALE = 2.5
EPS = 1e-6

LANE = 128
HEAD_PAD = LANE
CHUNK = 64
QUAD_HEADS = 4
QUAD = QUAD_HEADS * RWKV_HEAD
N_QUADS = RWKV_HEADS // QUAD_HEADS
SCAN_SEQS = 4
VMEM_LIMIT = 56 * 1024 * 1024
TM_MOE = 256
MIX_TILES = 1
SEG_SHIFT = 4
SEG = 1 << SEG_SHIFT
SLOT_ROWS = 2560
MAX_CHUNKS = SLOT_ROWS // SEG
CHUNK_GROUP_SHIFT = 3
CHUNK_GROUP = 1 << CHUNK_GROUP_SHIFT
SEL_BLOCK = 512
COMMON_BLOCKS = 4
FFN_ROWS = 1024


def _cparams(*sem):
    return pltpu.CompilerParams(dimension_semantics=sem, vmem_limit_bytes=VMEM_LIMIT)


def _dot(a, b, nt=False):
    dims = (((1,), (1,)), ((), ())) if nt else (((1,), (0,)), ((), ()))
    return lax.dot_general(a, b, dims, preferred_element_type=f32)


def _split2(x):
    hi = x.astype(bf16)
    lo = (x - hi.astype(f32)).astype(bf16)
    return hi, lo


def _split3(x):
    h1 = x.astype(bf16)
    r1 = x - h1.astype(f32)
    h2 = r1.astype(bf16)
    h3 = (r1 - h2.astype(f32)).astype(bf16)
    return h1, h2, h3


def _mm(a, b, nt=False, passes=3):
    if passes == 1:
        return _dot(a.astype(bf16), b.astype(bf16), nt)
    ah, al = _split2(a)
    bh, bl = _split2(b)
    return _dot(ah, bh, nt) + (_dot(al, bh, nt) + _dot(ah, bl, nt))


def _seg_sum(x, seg_bf16):
    return _dot(x.astype(bf16), seg_bf16)


def _rms(x, g):
    return x * lax.rsqrt(jnp.mean(x * x, axis=-1, keepdims=True) + EPS) * g


def _sigmoid(x):
    return jax.nn.sigmoid(x)


def _full_spec(a, grid_rank=1):
    zeros = (0,) * a.ndim
    if grid_rank == 1:
        return pl.BlockSpec(a.shape, lambda i: zeros)
    if grid_rank == 2:
        return pl.BlockSpec(a.shape, lambda i, j: zeros)
    return pl.BlockSpec(a.shape, lambda i, j, k: zeros)


def _mod_kernel(c_ref, w_ref, b_ref, o_ref):
    c = c_ref[...]
    s = c * _sigmoid(c)
    o_ref[...] = _mm(s, w_ref[...]) + b_ref[...]


def _modulation(cc, w_ada, b_ada):
    rows, n = cc.shape[0], w_ada.shape[1]
    tn = 512
    return pl.pallas_call(
        _mod_kernel,
        grid=(n // tn,),
        in_specs=[pl.BlockSpec((rows, D_MODEL), lambda j: (0, 0)),
                  pl.BlockSpec((D_MODEL, tn), lambda j: (0, j)),
                  pl.BlockSpec((1, tn), lambda j: (0, j))],
        out_specs=pl.BlockSpec((rows, tn), lambda j: (0, j)),
        out_shape=jax.ShapeDtypeStruct((rows, n), f32),
        compiler_params=_cparams("parallel"),
        name="modulation",
    )(cc, w_ada, b_ada.reshape(1, n))


def _in_kernel(*refs, rope):
    if rope:
        (x_ref, mod_ref, g_ref, wq_ref, wk_ref, wv_ref, wr_ref, qg_ref, kg_ref, cos_ref, sin_ref,
         q_ref, ko_ref, vo_ref, kp_ref, zr_ref) = refs
    else:
        (x_ref, mod_ref, g_ref, wq_ref, wk_ref, wv_ref, wr_ref, qg_ref, kg_ref,
         q_ref, ko_ref, vo_ref, kp_ref, zr_ref) = refs
    x = x_ref[...]
    m = mod_ref[...]
    h = _rms(x, g_ref[...])
    h = h * (1.0 + m[:, D_MODEL:2 * D_MODEL]) + m[:, 0:D_MODEL]
    hb = h.astype(bf16)
    zr_ref[...] = _dot(hb, wr_ref[...])
    zq = _dot(hb, wq_ref[...])
    zk = _dot(hb, wk_ref[...])
    zv = _dot(hb, wv_ref[...])

    lane = lax.broadcasted_iota(jnp.int32, (1, LANE), 1)
    first_half = (lane & (ROPE_HALF - 1)) < ROPE_FREQS

    def head_norm(z, gain):
        ms = jnp.sum(z * z, axis=-1, keepdims=True) * (1.0 / HEAD_DIM)
        return z * lax.rsqrt(ms + EPS) * gain

    def rotate(y):
        if not rope:
            return y
        partner = jnp.where(first_half, pltpu.roll(y, LANE - ROPE_FREQS, 1), pltpu.roll(y, ROPE_FREQS, 1))
        return y * cos_ref[...] + partner * sin_ref[...]

    for hh in range(ATTN_HEADS):
        sl = slice(hh * HEAD_PAD, (hh + 1) * HEAD_PAD)
        q_ref[:, sl] = rotate(head_norm(zq[:, sl], qg_ref[...])).astype(bf16)
    kn = [head_norm(zk[:, hh * HEAD_PAD:(hh + 1) * HEAD_PAD], kg_ref[...]) for hh in range(KV_HEADS)]
    ko_ref[...] = kn[0] + pltpu.roll(kn[1], HEAD_DIM, 1)
    vo_ref[...] = zv
    for hh in range(KV_HEADS):
        sl = slice(hh * HEAD_PAD, (hh + 1) * HEAD_PAD)
        kp_ref[:, sl] = rotate(kn[hh]).astype(bf16)


def _in_proj(x2, mod3, mod_row, g_pre, wq, wk, wv, wr, qg, kg, rope_tabs, seq_len, tm):
    n = x2.shape[0]
    rope = rope_tabs is not None
    in_specs = [pl.BlockSpec((tm, D_MODEL), lambda i: (i, 0)),
                pl.BlockSpec((None, 1, 6 * D_MODEL), lambda i: (mod_row(i, tm), 0, 0)),
                _full_spec(g_pre), _full_spec(wq), _full_spec(wk), _full_spec(wv), _full_spec(wr),
                _full_spec(qg), _full_spec(kg)]
    args = [x2, mod3, g_pre, wq, wk, wv, wr, qg, kg]
    if rope:
        blocks_per_seq = seq_len // tm
        in_specs += [pl.BlockSpec((tm, LANE), lambda i: (i % blocks_per_seq, 0))] * 2
        args += list(rope_tabs)
    out_shape = [jax.ShapeDtypeStruct((n, ATTN_HEADS * HEAD_PAD), bf16),
                 jax.ShapeDtypeStruct((n, KV_HEADS * HEAD_DIM), f32),
                 jax.ShapeDtypeStruct((n, KV_HEADS * HEAD_DIM), f32),
                 jax.ShapeDtypeStruct((n, KV_HEADS * HEAD_PAD), bf16),
                 jax.ShapeDtypeStruct((n, RWKV_IN), f32)]
    out_specs = [pl.BlockSpec((tm, s.shape[1]), lambda i: (i, 0)) for s in out_shape]
    return pl.pallas_call(
        functools.partial(_in_kernel, rope=rope),
        grid=(n // tm,), in_specs=in_specs, out_specs=out_specs, out_shape=out_shape,
        compiler_params=_cparams("parallel"), name="in_proj",
    )(*args)


def _attn_kernel(q_ref, k_ref, vt_ref, o_ref):
    tq = q_ref.shape[0]
    heads_t = []
    for kv in range(KV_HEADS):
        kh = k_ref[:, kv * HEAD_PAD:(kv + 1) * HEAD_PAD]
        vt = vt_ref[kv * HEAD_DIM:(kv + 1) * HEAD_DIM, :]
        q4 = jnp.concatenate(
            [q_ref[:, (GQA_GROUP * kv + g) * HEAD_PAD:(GQA_GROUP * kv + g + 1) * HEAD_PAD]
             for g in range(GQA_GROUP)], axis=0)
        st = _dot(kh, q4, nt=True)
        mx = jnp.max(st, axis=0, keepdims=True)
        p = jnp.exp(st - mx)
        den = jnp.sum(p, axis=0, keepdims=True)
        ot = _dot(vt, p.astype(bf16)) / den
        heads_t += [ot[:, g * tq:(g + 1) * tq] for g in range(GQA_GROUP)]
    o_ref[...] = jnp.concatenate(heads_t, axis=0).T.astype(bf16)


def _attention(q_pad, k_full, vt_full, batch, seq_len, tq):
    n = q_pad.shape[0]
    kv_len = k_full.shape[1]
    nq = seq_len // tq
    return pl.pallas_call(
        _attn_kernel,
        grid=(batch, nq),
        in_specs=[pl.BlockSpec((tq, ATTN_HEADS * HEAD_PAD), lambda b, j: (b * nq + j, 0)),
                  pl.BlockSpec((None, kv_len, KV_HEADS * HEAD_PAD), lambda b, j: (b, 0, 0)),
                  pl.BlockSpec((None, KV_HEADS * HEAD_DIM, kv_len), lambda b, j: (b, 0, 0))],
        out_specs=pl.BlockSpec((tq, ATTN_HEADS * HEAD_DIM), lambda b, j: (b * nq + j, 0)),
        out_shape=jax.ShapeDtypeStruct((n, ATTN_HEADS * HEAD_DIM), bf16),
        compiler_params=_cparams("parallel", "parallel"), name="attention",
    )(q_pad, k_full, vt_full)


def _prep_kernel(z_ref, zp_ref, zn_ref, mup_ref, mun_ref, w0_ref, w2_ref, a0_ref, a2_ref, g2_ref,
                 kkw_ref, ka_ref, rk_ref, seg_ref,
                 r_ref, k_ref, v_ref, kk_ref, lw_ref, a_ref, g_ref, bonus_ref, *, seq_len, tm):
    i = pl.program_id(0)
    z = z_ref[...]
    row = lax.broadcasted_iota(jnp.int32, (tm, 1), 0)
    seq_start = ((i * tm) % seq_len) == 0
    seq_end = (((i + 1) * tm) % seq_len) == 0
    prow = jnp.where(seq_start, 0.0, zp_ref[7:8, :])
    nrow = jnp.where(seq_end, 0.0, zn_ref[0:1, :])
    prev = jnp.where(row == 0, prow, pltpu.roll(z, 1, 0))
    nxt = jnp.where(row == tm - 1, nrow, pltpu.roll(z, tm - 1, 0))
    zs = z + mup_ref[...] * (prev - z) + mun_ref[...] * (nxt - z)
    w = RWKV_WIDTH
    r = zs[:, 0:w]
    k = zs[:, w:2 * w]
    v = zs[:, 2 * w:3 * w]
    dw = zs[:, 3 * w:3 * w + 2 * DECAY_RANK]
    da = zs[:, 3 * w + 2 * DECAY_RANK:3 * w + 2 * DECAY_RANK + 2 * ICLR_RANK]
    dg = zs[:, 3 * w + 2 * DECAY_RANK + 2 * ICLR_RANK:]
    lw = -DECAY_SCALE * _sigmoid(w0_ref[...] + _mm(jnp.tanh(dw), w2_ref[...], passes=1))
    a = _sigmoid(a0_ref[...] + _mm(da, a2_ref[...], passes=1))
    g = _mm(_sigmoid(dg), g2_ref[...], passes=1)
    kk = k * kkw_ref[...]
    seg = seg_ref[...]
    kk = kk * lax.rsqrt(_seg_sum(kk * kk, seg) + 1e-12)
    ka = ka_ref[...]
    ke_sum = k * (2.0 + (a[:, 0:w] + a[:, w:2 * w] - 2.0) * ka)
    bonus = _seg_sum(r * ke_sum * rk_ref[...], seg) * v
    r_ref[...] = r
    k_ref[...] = k
    v_ref[...] = v.astype(bf16)
    kk_ref[...] = kk
    lw_ref[...] = lw
    a_ref[...] = a
    g_ref[...] = g.astype(bf16)
    bonus_ref[...] = bonus.astype(bf16)


def _rwkv_prep(zr, p, seq_len, tm):
    n = zr.shape[0]
    w = RWKV_WIDTH
    nb8 = n // 8
    consts = [p['mu_prev'], p['mu_next'], p['w0'], p['w2bd'], p['a0'], p['a2bd'], p['g2'],
              p['k_k'], p['k_a'], p['r_k'], p['seg']]
    in_specs = [pl.BlockSpec((tm, RWKV_IN), lambda i: (i, 0)),
                pl.BlockSpec((8, RWKV_IN), lambda i: (jnp.maximum(i * (tm // 8) - 1, 0), 0)),
                pl.BlockSpec((8, RWKV_IN), lambda i: (jnp.minimum((i + 1) * (tm // 8), nb8 - 1), 0))]
    in_specs += [_full_spec(c) for c in consts]
    widths = [w, w, w, w, 2 * w, 2 * w, w, w]
    dtypes = [f32, f32, bf16, f32, f32, f32, bf16, bf16]
    out_shape = [jax.ShapeDtypeStruct((n, wd), dt) for wd, dt in zip(widths, dtypes)]
    out_specs = [pl.BlockSpec((tm, wd), lambda i: (i, 0)) for wd in widths]
    return pl.pallas_call(
        functools.partial(_prep_kernel, seq_len=seq_len, tm=tm),
        grid=(n // tm,), in_specs=in_specs, out_specs=out_specs, out_shape=out_shape,
        compiler_params=_cparams("parallel"), name="rwkv_prep",
    )(zr, zr, zr, *consts)


def _scan_kernel(rf_ref, kf_ref, vf_ref, kkf_ref, lwf_ref, af_ref,
                 rb_ref, kb_ref, vb_ref, kkb_ref, lwb_ref, ab_ref, ka_ref, s0_ref,
                 of_ref, ob_ref, sfin_ref, st_ref):
    c = pl.program_id(1)

    @pl.when(c == 0)
    def _():
        st_ref[...] = s0_ref[...]

    n_seq = rf_ref.shape[0]
    row = lax.broadcasted_iota(jnp.int32, (CHUNK, CHUNK), 0)
    col = lax.broadcasted_iota(jnp.int32, (CHUNK, CHUNK), 1)
    qr = lax.broadcasted_iota(jnp.int32, (QUAD, QUAD), 0)
    qc = lax.broadcasted_iota(jnp.int32, (QUAD, QUAD), 1)
    head_mask = (qr & -RWKV_HEAD) == (qc & -RWKV_HEAD)
    head_mask_b = jnp.where(head_mask, 1.0, 0.0).astype(bf16)
    tr = lax.broadcasted_iota(jnp.int32, (CHUNK, QUAD), 0)
    tc = lax.broadcasted_iota(jnp.int32, (CHUNK, QUAD), 1) & (CHUNK - 1)
    eye = jnp.where(tr == tc, 1.0, 0.0).astype(f32)
    ka = ka_ref[...]

    def bd(x):
        xb = x.astype(bf16)
        return jnp.concatenate([xb] * QUAD_HEADS, axis=0) * head_mask_b

    def mm(a, b, nt=False):
        return _dot(a.astype(bf16), b.astype(bf16), nt)

    chains = []
    for d, (r_ref, k_ref, v_ref, kk_ref, lw_ref, a_ref, o_ref) in enumerate(
            ((rf_ref, kf_ref, vf_ref, kkf_ref, lwf_ref, af_ref, of_ref),
             (rb_ref, kb_ref, vb_ref, kkb_ref, lwb_ref, ab_ref, ob_ref))):
        sgn = 1 - 2 * d
        tri = jnp.where((row - col) * sgn >= 0, 1.0, 0.0).astype(bf16)
        dist = (tr - tc) * sgn
        for s in range(n_seq):
            lw = lw_ref[s]
            l1, l2, l3 = _split3(lw)
            cum = _dot(tri, l1) + (_dot(tri, l2) + _dot(tri, l3))
            e_incl = jnp.exp(cum)
            e_excl = jnp.exp(cum - lw)
            e_neg = jnp.exp(-cum)
            wc = jnp.exp(jnp.sum(lw, axis=0, keepdims=True))
            kk = kk_ref[s]
            a = a_ref[s]
            bh = kk * a * e_neg
            kh = k_ref[s] * (1.0 + (a - 1.0) * ka) * e_neg
            ops = dict(at=-kk * e_excl, rt=r_ref[s] * e_incl, bh=bh, kh=kh, bw=bh * wc, kw=kh * wc,
                       v=v_ref[s], wc=wc)
            for g in range(N_QUADS):
                ch = {key: val[:, g * QUAD:(g + 1) * QUAD] for key, val in ops.items()}
                ch.update(s=s, d=d, g=g, o_ref=o_ref, strict=dist > 0, incl=dist >= 0, st=st_ref[s, d, g])
                chains.append(ch)

    lane_lo = lax.broadcasted_iota(jnp.int32, (QUAD, LANE), 1) < RWKV_HEAD
    for ch in chains:
        ar = jnp.concatenate([ch['at'], ch['rt']], axis=0)
        ch['ar'] = ar.astype(bf16)
        bk_t = jnp.concatenate([ch['bh'], ch['kh']], axis=0).T
        swapped = pltpu.roll(bk_t, RWKV_HEAD, 1)
        b_t = jnp.where(lane_lo, bk_t, swapped).astype(bf16)
        k_t = jnp.where(lane_lo, swapped, bk_t).astype(bf16)
        w_bk = jnp.concatenate([jnp.concatenate([b_t, b_t], axis=1) * head_mask_b,
                                jnp.concatenate([k_t, k_t], axis=1) * head_mask_b], axis=1)
        amat = _dot(ch['ar'], w_bk)
        ch['n'] = jnp.where(ch['strict'], amat[0:CHUNK, 0:QUAD], 0.0)
        ch['a_kr'] = jnp.concatenate([jnp.where(ch['strict'], amat[0:CHUNK, QUAD:2 * QUAD], 0.0),
                                      jnp.where(ch['incl'], amat[CHUNK:2 * CHUNK, QUAD:2 * QUAD], 0.0)], axis=0)
        ch['a_rb'] = jnp.where(ch['incl'], amat[CHUNK:2 * CHUNK, 0:QUAD], 0.0)
        ch['tinv'] = eye + ch['n']
    for ch in chains:
        ch['n'] = mm(ch['n'], bd(ch['n']))
    for _ in range(4):
        for ch in chains:
            both = mm(jnp.concatenate([ch['n'], ch['tinv']], axis=0), bd(ch['n']))
            ch['n'] = both[0:CHUNK]
            ch['tinv'] = ch['tinv'] + both[CHUNK:2 * CHUNK]
    for ch in chains:
        ch['tinv'] = ch['tinv'] + mm(ch['tinv'], bd(ch['n']))
        sa = _dot(ch['ar'], ch['st'].astype(bf16))
        av = mm(ch['a_kr'], bd(ch['v']))
        ch['rhs'] = sa[0:CHUNK] + av[0:CHUNK]
        ch['o'] = sa[CHUNK:2 * CHUNK] + av[CHUNK:2 * CHUNK]
    for ch in chains:
        ch['u'] = mm(ch['tinv'], bd(ch['rhs']))
    for ch in chains:
        ch['o'] = ch['o'] + mm(ch['a_rb'], bd(ch['u']))
        z_t = jnp.concatenate([ch['bw'], ch['kw'], jnp.broadcast_to(ch['wc'], (2 * CHUNK, QUAD))], axis=0).T
        upd = mm(z_t[:, 0:2 * CHUNK], jnp.concatenate([ch['u'].astype(bf16), ch['v']], axis=0))
        decay = jnp.concatenate([z_t[:, 2 * CHUNK:4 * CHUNK]] * 2, axis=1)
        ch['st_new'] = ch['st'] * decay + jnp.where(head_mask, upd, 0.0)
    for ch in chains:
        ch['o_ref'][ch['s'], :, ch['g'] * QUAD:(ch['g'] + 1) * QUAD] = ch['o']
        st_ref[ch['s'], ch['d'], ch['g']] = ch['st_new']

    @pl.when(c == pl.num_programs(1) - 1)
    def _():
        sfin_ref[...] = st_ref[...]


def _rwkv_scan(r, k, v, kk, lw, a, k_a, s0, batch, seq_len):
    n = r.shape[0]
    w = RWKV_WIDTH
    nc = seq_len // CHUNK
    per_seq = lambda x: x.reshape(batch, seq_len, x.shape[-1])
    r, k, v, kk, lw, a = (per_seq(x) for x in (r, k, v, kk, lw, a))
    fwd = lambda b, c: (b, c, 0)
    bwd = lambda b, c: (b, nc - 1 - c, 0)
    bwd_dir = lambda b, c: (b, nc - 1 - c, 1)
    blk = lambda index_map: pl.BlockSpec((SCAN_SEQS, CHUNK, w), index_map)
    state_spec = pl.BlockSpec((SCAN_SEQS, 2, N_QUADS, QUAD, QUAD), lambda b, c: (b, 0, 0, 0, 0))
    o_shape = jax.ShapeDtypeStruct((batch, seq_len, w), f32)
    o_fw, o_bw, s_fin = pl.pallas_call(
        _scan_kernel,
        grid=(batch // SCAN_SEQS, nc),
        in_specs=[blk(fwd)] * 6 + [blk(bwd)] * 4 + [blk(bwd_dir)] * 2
                 + [pl.BlockSpec((1, w), lambda b, c: (0, 0)), state_spec],
        out_specs=[blk(fwd), blk(bwd), state_spec],
        out_shape=[o_shape, o_shape, jax.ShapeDtypeStruct((batch, 2, N_QUADS, QUAD, QUAD), f32)],
        scratch_shapes=[pltpu.VMEM((SCAN_SEQS, 2, N_QUADS, QUAD, QUAD), f32)],
        compiler_params=_cparams("parallel", "arbitrary"), name="rwkv_scan",
    )(r, k, v, kk, lw, a, r, k, v, kk, lw, a, k_a, s0)
    return o_fw.reshape(n, w), o_bw.reshape(n, w), s_fin


def _out_kernel(attn_ref, of_ref, ob_ref, bonus_ref, g_ref, x_ref, mod_ref, lng_ref, lnb_ref, seg_ref,
                wo_ref, gpm_ref, gpf_ref, rwt_ref, rb_ref,
                x1_ref, h2_ref, slot_ref, sw_ref, cnt_ref):
    outs = []
    for t in range(x_ref.shape[0] // TM_MOE):
        rows = pl.ds(t * TM_MOE, TM_MOE)
        outs.append(_out_tile(attn_ref.at[rows], of_ref.at[rows], ob_ref.at[rows], bonus_ref.at[rows],
                              g_ref.at[rows], x_ref.at[rows], mod_ref, lng_ref, lnb_ref, seg_ref, wo_ref,
                              gpm_ref, gpf_ref, rwt_ref, rb_ref))
    for t, (x1, h2, slots, sw, cnt) in enumerate(outs):
        rows = pl.ds(t * TM_MOE, TM_MOE)
        x1_ref[rows, :] = x1
        h2_ref[rows, :] = h2
        slot_ref[:, rows] = slots
        sw_ref[rows, :] = sw
        cnt_ref[t] = cnt


def _out_tile(attn_ref, of_ref, ob_ref, bonus_ref, g_ref, x_ref, mod_ref, lng_ref, lnb_ref, seg_ref,
              wo_ref, gpm_ref, gpf_ref, rwt_ref, rb_ref):
    tm = x_ref.shape[0]
    seg = seg_ref[...]
    inv = 1.0 / RWKV_HEAD
    o = of_ref[...] + ob_ref[...]
    mu = _seg_sum(o, seg) * inv
    dl = o - mu
    var = _seg_sum(dl * dl, seg) * inv
    on = dl * lax.rsqrt(var + GN_EPS) * lng_ref[...] + lnb_ref[...]
    rw = (on + bonus_ref[...]) * g_ref[...]
    half = ATTN_HEADS * HEAD_DIM
    y = _dot(attn_ref[...].astype(bf16), wo_ref[0:half, :]) + _dot(rw.astype(bf16), wo_ref[half:, :])
    m = mod_ref[...]
    gt_a = m[:, 2 * D_MODEL:3 * D_MODEL]
    sh_f = m[:, 3 * D_MODEL:4 * D_MODEL]
    sc_f = m[:, 4 * D_MODEL:5 * D_MODEL]
    x1 = x_ref[...] + gt_a * _rms(y, gpm_ref[...])
    h2 = _rms(x1, gpf_ref[...]) * (1.0 + sc_f) + sh_f

    logits = _mm(rwt_ref[...], h2, nt=True)
    scores = _sigmoid(logits)
    biased = scores + rb_ref[...]
    neg = -jnp.inf
    shape3 = (N_GROUPS, GROUP_SIZE, tm)
    b3 = biased.reshape(shape3)
    s3 = scores.reshape(shape3)
    in_grp = lax.broadcasted_iota(jnp.int32, shape3, 1)
    grp = lax.broadcasted_iota(jnp.int32, shape3, 0)
    m1 = jnp.max(b3, axis=1, keepdims=True)
    i1 = jnp.min(jnp.where(b3 == m1, in_grp, GROUP_SIZE), axis=1, keepdims=True)
    m2 = jnp.max(jnp.where(in_grp == i1, neg, b3), axis=1, keepdims=True)
    gscore = m1 + m2
    gidx = lax.broadcasted_iota(jnp.int32, (N_GROUPS, 1, tm), 0)
    gsel = jnp.zeros((N_GROUPS, 1, tm), f32)
    cur = gscore
    for _ in range(TOPK_GROUPS):
        mx = jnp.max(cur, axis=0, keepdims=True)
        ii = jnp.min(jnp.where(cur == mx, gidx, N_GROUPS), axis=0, keepdims=True)
        hit = gidx == ii
        gsel = jnp.where(hit, 1.0, gsel)
        cur = jnp.where(hit, neg, cur)
    cand = jnp.where(jnp.broadcast_to(gsel, shape3) > 0.0, b3, neg)
    eidx = grp * GROUP_SIZE + in_grp
    wsel = jnp.zeros(shape3, f32)
    mask = jnp.zeros(shape3, f32)
    hits = []
    for _ in range(TOP_K):
        mx = jnp.max(jnp.max(cand, axis=1, keepdims=True), axis=0, keepdims=True)
        ii = jnp.min(jnp.min(jnp.where(cand == mx, eidx, N_EXPERTS), axis=1, keepdims=True),
                     axis=0, keepdims=True)
        hit = eidx == ii
        hits.append(hit)
        wsel = jnp.where(hit, s3, wsel)
        mask = jnp.where(hit, 1.0, mask)
        cand = jnp.where(hit, neg, cand)
    den = jnp.sum(jnp.sum(wsel, axis=1, keepdims=True), axis=0, keepdims=True)
    gates3 = wsel / den * ROUTED_SCALE

    mask2 = mask.reshape(N_EXPERTS, tm)
    cnt = jnp.sum(mask2, axis=1, keepdims=True)
    cnt_pad = jnp.floor((cnt + (SEG - 1)) * (1.0 / SEG)) * SEG
    er = lax.broadcasted_iota(jnp.int32, (N_EXPERTS, N_EXPERTS), 0)
    ec = lax.broadcasted_iota(jnp.int32, (N_EXPERTS, N_EXPERTS), 1)
    before = jnp.where(ec < er, 1.0, 0.0).astype(bf16)
    cnt_pad_l = jnp.broadcast_to(cnt_pad, (N_EXPERTS, LANE))
    seg_start = _dot(before, cnt_pad_l.astype(bf16))
    tr = lax.broadcasted_iota(jnp.int32, (tm, tm), 0)
    tc = lax.broadcasted_iota(jnp.int32, (tm, tm), 1)
    earlier = jnp.where(tr < tc, 1.0, 0.0).astype(bf16)
    rank = _dot(mask2.astype(bf16), earlier)
    slot3 = (rank + seg_start[:, 0:1]).reshape(shape3)

    def pick(hit, val3):
        return jnp.sum(jnp.sum(jnp.where(hit, val3, 0.0), axis=1, keepdims=True), axis=0)

    slots = [pick(h, slot3) for h in hits]
    wts = [pick(h, gates3) for h in hits]
    pad_rows = lambda k, val: [jnp.full((k, tm), val, f32)]
    slot_rows = jnp.concatenate(slots + pad_rows(8 - TOP_K, -1.0), axis=0).astype(jnp.int32)
    sw_t = jnp.concatenate(slots + pad_rows(8 - TOP_K, -1.0) + wts + pad_rows(LANE - 8 - TOP_K, 0.0), axis=0)
    return x1, h2.astype(bf16), slot_rows, sw_t.T, cnt_pad_l


def _mix_out(attn, o_fw, o_bw, bonus, g, x2, mod3, mod_row, p, tm):
    n = x2.shape[0]
    w = RWKV_WIDTH
    consts = [p['lnx_g'], p['lnx_b'], p['seg'], p['w_out'], p['g_post_mix'], p['g_pre_ffn'],
              p['router_wt'], p['router_b']]
    row = lambda wd: pl.BlockSpec((tm, wd), lambda i: (i, 0))
    in_specs = [row(ATTN_HEADS * HEAD_DIM), row(w), row(w),
                row(w), row(w), row(D_MODEL),
                pl.BlockSpec((None, 1, 6 * D_MODEL), lambda i: (mod_row(i, tm), 0, 0))]
    in_specs += [_full_spec(c) for c in consts]
    out_shape = [jax.ShapeDtypeStruct((n, D_MODEL), f32),
                 jax.ShapeDtypeStruct((n, D_MODEL), bf16),
                 jax.ShapeDtypeStruct((8, n), jnp.int32),
                 jax.ShapeDtypeStruct((n, LANE), f32),
                 jax.ShapeDtypeStruct((n // TM_MOE, N_EXPERTS, LANE), f32)]
    out_specs = [row(D_MODEL), row(D_MODEL), pl.BlockSpec((8, tm), lambda i: (0, i)), row(LANE),
                 pl.BlockSpec((tm // TM_MOE, N_EXPERTS, LANE), lambda i: (i, 0, 0))]
    return pl.pallas_call(
        _out_kernel, grid=(n // tm,), in_specs=in_specs, out_specs=out_specs, out_shape=out_shape,
        compiler_params=_cparams("parallel"), name="mix_out",
    )(attn, o_fw, o_bw, bonus, g, x2, mod3, *consts)


def _chunk_groups(tile, nch_sm):
    return lax.shift_right_logical(nch_sm[tile] + (CHUNK_GROUP - 1), CHUNK_GROUP_SHIFT)


def _start_chunks(tile, dst_sm, nch_sm, make_copy):
    def body(g, carry):
        for u in range(CHUNK_GROUP):
            c = g * CHUNK_GROUP + u
            make_copy(c, pl.multiple_of(dst_sm[tile * MAX_CHUNKS + c], SEG)).start()
        return carry

    lax.fori_loop(0, _chunk_groups(tile, nch_sm), body, 0)


def _wait_chunks(tile, nch_sm, make_copy):
    def body(g, carry):
        for _ in range(CHUNK_GROUP):
            make_copy(0, 0).wait()
        return carry

    lax.fori_loop(0, _chunk_groups(tile, nch_sm), body, 0)


def _dispatch_kernel(dst_sm, nch_sm, fill_sm, *refs, tile_starts):
    n_paths = len(tile_starts)
    h_refs, slot_refs = refs[:n_paths], refs[n_paths:2 * n_paths]
    xs_hbm, xc_ref, zero_ref, sems, zsem = refs[2 * n_paths:]
    i = pl.program_id(0)
    last = pl.num_programs(0) - 1
    buf = lax.rem(i, 2)

    def copy_from(b):
        def make(chunk, global_row):
            src = xc_ref.at[b, pl.ds(pl.multiple_of(chunk * SEG, SEG), SEG)]
            return pltpu.make_async_copy(src, xs_hbm.at[pl.ds(global_row, SEG)], sems.at[b])
        return make

    @pl.when(i == 0)
    def _():
        zero_ref[...] = jnp.zeros_like(zero_ref)
        n_tiles = xs_hbm.shape[0] // FFN_ROWS

        def tail_copy(row):
            return pltpu.make_async_copy(zero_ref.at[pl.ds(0, SEG)], xs_hbm.at[pl.ds(row, SEG)], zsem)

        def tile_copy(t):
            return pltpu.make_async_copy(zero_ref, xs_hbm.at[pl.ds(pl.multiple_of(t * FFN_ROWS, FFN_ROWS), FFN_ROWS)],
                                         zsem)

        def per_expert(e, carry):
            start = fill_sm[1 + e]
            chunks = lax.shift_right_logical(fill_sm[1 + N_EXPERTS + e], SEG_SHIFT)
            lax.fori_loop(0, chunks, lambda c, z: (tail_copy(pl.multiple_of(start + c * SEG, SEG)).start(), z)[1], 0)
            lax.fori_loop(0, chunks, lambda c, z: (tail_copy(0).wait(), z)[1], 0)
            return carry

        lax.fori_loop(0, N_EXPERTS, per_expert, 0)
        tile_copy(n_tiles - 1).start()
        tile_copy(0).wait()
        lax.fori_loop(fill_sm[0], n_tiles - 1, lambda t, z: (tile_copy(t).start(), z)[1], 0)

    @pl.when(i >= 2)
    def _():
        _wait_chunks(i - 2, nch_sm, copy_from(buf))

    slot, h = slot_refs[0][...], h_refs[0][...]
    for q in range(1, n_paths):
        mine = i >= tile_starts[q]
        slot = jnp.where(mine, slot_refs[q][...], slot)
        h = jnp.where(mine, h_refs[q][...], h)
    used = nch_sm[i] * SEG

    def build(rb):
        rows = lax.broadcasted_iota(jnp.int32, (SEL_BLOCK, TM_MOE), 0) + rb * SEL_BLOCK
        sel = jnp.zeros((SEL_BLOCK, TM_MOE), f32)
        for j in range(TOP_K):
            sel = jnp.where(rows == slot[j:j + 1, :], 1.0, sel)
        xc_ref[buf, rb * SEL_BLOCK:(rb + 1) * SEL_BLOCK, :] = _dot(sel.astype(bf16), h).astype(bf16)

    for rb in range(COMMON_BLOCKS):
        build(rb)
    for rb in range(COMMON_BLOCKS, SLOT_ROWS // SEL_BLOCK):
        pl.when(rb * SEL_BLOCK < used)(functools.partial(build, rb))

    _start_chunks(i, dst_sm, nch_sm, copy_from(buf))

    @pl.when(i == last)
    def _():
        @pl.when(i >= 1)
        def _():
            _wait_chunks(i - 1, nch_sm, copy_from(1 - buf))
        _wait_chunks(i, nch_sm, copy_from(buf))
        n_tiles = xs_hbm.shape[0] // FFN_ROWS
        whole_tile = pltpu.make_async_copy(zero_ref, xs_hbm.at[pl.ds(0, FFN_ROWS)], zsem)
        lax.fori_loop(fill_sm[0], n_tiles - 1, lambda t, z: (whole_tile.wait(), z)[1], 0)


def _dispatch(h2s, slots, dst, nch, fill, total_rows):
    tiles = [h.shape[0] // TM_MOE for h in h2s]
    starts = [sum(tiles[:q]) for q in range(len(tiles))]

    def in_path(q):
        return lambda i: jnp.clip(i - starts[q], 0, tiles[q] - 1)

    h_specs = [pl.BlockSpec((TM_MOE, D_MODEL), lambda i, d, c, f, q=q: (in_path(q)(i), 0)) for q in range(len(tiles))]
    s_specs = [pl.BlockSpec((8, TM_MOE), lambda i, d, c, f, q=q: (0, in_path(q)(i))) for q in range(len(tiles))]
    grid_spec = pltpu.PrefetchScalarGridSpec(
        num_scalar_prefetch=3, grid=(sum(tiles),),
        in_specs=h_specs + s_specs,
        out_specs=pl.BlockSpec(memory_space=pl.ANY),
        scratch_shapes=[pltpu.VMEM((2, SLOT_ROWS, D_MODEL), bf16), pltpu.VMEM((FFN_ROWS, D_MODEL), bf16),
                        pltpu.SemaphoreType.DMA((2,)), pltpu.SemaphoreType.DMA(())])
    return pl.pallas_call(
        functools.partial(_dispatch_kernel, tile_starts=tuple(starts)), grid_spec=grid_spec,
        out_shape=jax.ShapeDtypeStruct((total_rows, D_MODEL), bf16),
        compiler_params=_cparams("arbitrary"), name="moe_dispatch",
    )(dst, nch, fill, *h2s, *slots)


def _ffn_kernel(te_sm, nv_sm, xs_ref, wg_ref, wu_ref, wd_ref, ys_ref, wg_b, wu_b, wd_b):
    r = pl.program_id(0)
    valid = r < nv_sm[0]

    @pl.when(jnp.logical_and(valid, jnp.logical_or(r == 0, te_sm[r] != te_sm[jnp.maximum(r - 1, 0)])))
    def _():
        wg_b[...] = wg_ref[...].astype(bf16)
        wu_b[...] = wu_ref[...].astype(bf16)
        wd_b[...] = wd_ref[...].astype(bf16)

    @pl.when(valid)
    def _():
        x = xs_ref[...]
        gg = _dot(x, wg_b[...])
        uu = _dot(x, wu_b[...])
        hm = gg * _sigmoid(gg) * uu
        ys_ref[...] = _dot(hm.astype(bf16), wd_b[...]).astype(bf16)

    @pl.when(r >= nv_sm[0])
    def _():
        ys_ref[...] = jnp.zeros_like(ys_ref)


def _expert_ffn(xs, tile_expert, n_valid, wg, wu, wd):
    total_rows = xs.shape[0]
    last = lambda r, nv: jnp.minimum(r, nv[0] - 1)
    grid_spec = pltpu.PrefetchScalarGridSpec(
        num_scalar_prefetch=2, grid=(total_rows // FFN_ROWS,),
        in_specs=[pl.BlockSpec((FFN_ROWS, D_MODEL), lambda r, te, nv: (last(r, nv), 0)),
                  pl.BlockSpec((None, D_MODEL, EXPERT_FF), lambda r, te, nv: (te[last(r, nv)], 0, 0)),
                  pl.BlockSpec((None, D_MODEL, EXPERT_FF), lambda r, te, nv: (te[last(r, nv)], 0, 0)),
                  pl.BlockSpec((None, EXPERT_FF, D_MODEL), lambda r, te, nv: (te[last(r, nv)], 0, 0))],
        out_specs=pl.BlockSpec((FFN_ROWS, D_MODEL), lambda r, te, nv: (r, 0)),
        scratch_shapes=[pltpu.VMEM((D_MODEL, EXPERT_FF), bf16), pltpu.VMEM((D_MODEL, EXPERT_FF), bf16),
                        pltpu.VMEM((EXPERT_FF, D_MODEL), bf16)])
    return pl.pallas_call(
        _ffn_kernel, grid_spec=grid_spec,
        out_shape=jax.ShapeDtypeStruct((total_rows, D_MODEL), bf16),
        compiler_params=_cparams("arbitrary"), name="moe_ffn",
    )(tile_expert, n_valid, xs, wg, wu, wd)


def _combine_kernel(dst_sm, nch_sm, sw_ref, h_ref, x1_ref, mod_ref, gpost_ref, swg_ref, swu_ref, swd_ref,
                    ys_hbm, y_ref, yc_ref, acc_ref, sems, *, tile_offset):
    step = pl.program_id(0)
    i = step + tile_offset
    buf = lax.rem(step, 2)

    def copy_into(b):
        def make(chunk, global_row):
            dst = yc_ref.at[b, pl.ds(pl.multiple_of(chunk * SEG, SEG), SEG)]
            return pltpu.make_async_copy(ys_hbm.at[pl.ds(global_row, SEG)], dst, sems.at[b])
        return make

    @pl.when(step == 0)
    def _():
        yc_ref[...] = jnp.zeros_like(yc_ref)
        _start_chunks(i, dst_sm, nch_sm, copy_into(buf))

    @pl.when(step + 1 < pl.num_programs(0))
    def _():
        _start_chunks(i + 1, dst_sm, nch_sm, copy_into(1 - buf))

    h = h_ref[...]
    gg = _dot(h, swg_ref[...])
    uu = _dot(h, swu_ref[...])
    acc_ref[...] = _dot((gg * _sigmoid(gg) * uu).astype(bf16), swd_ref[...])
    _wait_chunks(i, nch_sm, copy_into(buf))
    sw = sw_ref[...]
    used = nch_sm[i] * SEG
    def weighted(rb):
        cols = (lax.broadcasted_iota(jnp.int32, (TM_MOE, SEL_BLOCK), 1) + rb * SEL_BLOCK).astype(f32)
        wmat = jnp.zeros((TM_MOE, SEL_BLOCK), f32)
        for j in range(TOP_K):
            wmat = jnp.where(cols == sw[:, j:j + 1], sw[:, 8 + j:9 + j], wmat)
        rows = lax.broadcasted_iota(jnp.int32, (SEL_BLOCK, 1), 0) + rb * SEL_BLOCK
        yc = jnp.where(rows < used, yc_ref[buf, rb * SEL_BLOCK:(rb + 1) * SEL_BLOCK, :], jnp.zeros((), bf16))
        return _dot(wmat.astype(bf16), yc)

    moe = acc_ref[...]
    for rb in range(COMMON_BLOCKS):
        moe = moe + weighted(rb)
    acc_ref[...] = moe
    for rb in range(COMMON_BLOCKS, SLOT_ROWS // SEL_BLOCK):
        @pl.when(rb * SEL_BLOCK < used)
        def _():
            acc_ref[...] += weighted(rb)
    gt_f = mod_ref[...][:, 5 * D_MODEL:6 * D_MODEL]
    y_ref[...] = x1_ref[...] + gt_f * _rms(acc_ref[...], gpost_ref[...])


def _combine(ys, dst, nch, sw, h2, x1, mod3, mod_row, g_post, swg, swu, swd, tile_offset):
    n = h2.shape[0]
    row = lambda wd: pl.BlockSpec((TM_MOE, wd), lambda i, d, c: (i, 0))
    const = lambda a: pl.BlockSpec(a.shape, lambda i, d, c: (0,) * a.ndim)
    grid_spec = pltpu.PrefetchScalarGridSpec(
        num_scalar_prefetch=2, grid=(n // TM_MOE,),
        in_specs=[row(LANE), row(D_MODEL), row(D_MODEL),
                  pl.BlockSpec((None, 1, 6 * D_MODEL), lambda i, d, c: (mod_row(i, TM_MOE), 0, 0)),
                  const(g_post), const(swg), const(swu), const(swd),
                  pl.BlockSpec(memory_space=pl.ANY)],
        out_specs=row(D_MODEL),
        scratch_shapes=[pltpu.VMEM((2, SLOT_ROWS, D_MODEL), bf16), pltpu.VMEM((TM_MOE, D_MODEL), f32),
                        pltpu.SemaphoreType.DMA((2,))])
    return pl.pallas_call(
        functools.partial(_combine_kernel, tile_offset=tile_offset), grid_spec=grid_spec,
        out_shape=jax.ShapeDtypeStruct((n, D_MODEL), f32),
        compiler_params=_cparams("arbitrary"), name="moe_combine",
    )(dst, nch, sw, h2, x1, mod3, g_post, swg, swu, swd, ys)


def _sparse_moe(paths, mod3, p):
    cnt = jnp.concatenate([q['cnt'][:, :, 0] for q in paths], axis=0).astype(jnp.int32)
    n_tiles = cnt.shape[0]
    before = jnp.cumsum(cnt, axis=0) - cnt
    rows_e = jnp.sum(cnt, axis=0)
    region = (rows_e + FFN_ROWS - 1) // FFN_ROWS * FFN_ROWS
    region_end = jnp.cumsum(region)
    region_start = region_end - region
    worst = n_tiles * (TOP_K * TM_MOE + N_EXPERTS * (SEG - 1)) + N_EXPERTS * (FFN_ROWS - SEG)
    total_rows = -(-worst // FFN_ROWS) * FFN_ROWS + FFN_ROWS
    spare_row = total_rows - FFN_ROWS
    first_row = jnp.arange(total_rows // FFN_ROWS, dtype=jnp.int32) * FFN_ROWS
    ended = jnp.sum((region_end[None, :] <= first_row[:, None]).astype(jnp.int32), axis=1)
    tile_expert = jnp.minimum(ended, N_EXPERTS - 1).astype(jnp.int32)
    n_valid = (region_end[-1:] // FFN_ROWS).astype(jnp.int32)

    seg_end = jnp.cumsum(cnt, axis=1)
    seg_start = seg_end - cnt
    nch = (seg_end[:, -1] // SEG).astype(jnp.int32)
    chunk_row = jnp.arange(MAX_CHUNKS, dtype=jnp.int32) * SEG
    chunk_e = jnp.sum((seg_end[:, None, :] <= chunk_row[None, :, None]).astype(jnp.int32), axis=2)
    own = chunk_e[:, :, None] == jnp.arange(N_EXPERTS, dtype=jnp.int32)[None, None, :]
    shift = (region_start[None, :] + before - seg_start)[:, None, :]
    dst = jnp.sum(jnp.where(own, shift, 0), axis=2) + chunk_row[None, :]
    chunk_id = jnp.arange(MAX_CHUNKS, dtype=jnp.int32)[None, :]
    tile_id = jnp.arange(n_tiles, dtype=jnp.int32)[:, None]
    spare = spare_row + ((tile_id % 2) * CHUNK_GROUP + chunk_id % CHUNK_GROUP) * SEG
    dst = jnp.where(chunk_id < nch[:, None], dst, spare).astype(jnp.int32).reshape(-1)

    fill = jnp.concatenate([n_valid, region_start + rows_e, region - rows_e]).astype(jnp.int32)
    xs = _dispatch([q['h2'] for q in paths], [q['slots'] for q in paths], dst, nch, fill, total_rows)
    ys = _expert_ffn(xs, tile_expert, n_valid, p['expert_wg'], p['expert_wu'], p['expert_wd'])
    outs, tile_offset = [], 0
    for q in paths:
        outs.append(_combine(ys, dst, nch, q['sw'], q['h2'], q['x1'], mod3, q['mod_row'],
                             p['g_post_ffn'], p['shared_wg'], p['shared_wu'], p['shared_wd'], tile_offset))
        tile_offset += q['h2'].shape[0] // TM_MOE
    return outs


def _pad_heads(w, heads):
    rows = w.shape[0]
    w = w.reshape(rows, heads, HEAD_DIM)
    return jnp.pad(w, ((0, 0), (0, 0), (0, HEAD_PAD - HEAD_DIM))).reshape(rows, heads * HEAD_PAD)


def _block_diag2(m):
    z = jnp.zeros_like(m[0])
    return jnp.concatenate([jnp.concatenate([m[0], z], axis=1), jnp.concatenate([z, m[1]], axis=1)], axis=0)


def _prepare_params(w):
    aw, kw = ATTN_HEADS * HEAD_DIM, KV_HEADS * HEAD_DIM
    w_in = w['w_in']
    lane_row = lambda v: v.reshape(1, -1).astype(f32)
    head_id = np.arange(RWKV_WIDTH) // RWKV_HEAD
    seg = jnp.asarray(head_id[:, None] == head_id[None, :], dtype=bf16)
    gain_pad = lambda gvec: jnp.pad(gvec, (0, HEAD_PAD - HEAD_DIM)).reshape(1, HEAD_PAD)
    return dict(
        g_pre_mix=lane_row(w['g_pre_mix']), g_post_mix=lane_row(w['g_post_mix']),
        g_pre_ffn=lane_row(w['g_pre_ffn']), g_post_ffn=lane_row(w['g_post_ffn']),
        wq=_pad_heads(w_in[:, :aw], ATTN_HEADS).astype(bf16),
        wk=_pad_heads(w_in[:, aw:aw + kw], KV_HEADS).astype(bf16),
        wv=w_in[:, aw + kw:aw + 2 * kw].astype(bf16),
        wr=w_in[:, aw + 2 * kw:].astype(bf16),
        qg=gain_pad(w['q_gain'] * ATTN_SCALE), kg=gain_pad(w['k_gain']),
        mu_prev=lane_row(w['mu_prev']), mu_next=lane_row(w['mu_next']),
        w0=lane_row(w['decay_w0']), w2bd=_block_diag2(w['decay_w2']),
        a0=lane_row(w['iclr_a0']), a2bd=_block_diag2(w['iclr_a2']),
        g2=w['gate_g2'], k_k=lane_row(w['k_k']), k_a=lane_row(w['k_a']), r_k=lane_row(w['r_k']),
        lnx_g=lane_row(w['lnx_g']), lnx_b=lane_row(w['lnx_b']), seg=seg,
        w_out=w['w_out'].astype(bf16),
        router_wt=w['router_w'].T, router_b=w['router_b'].reshape(N_EXPERTS, 1),
        expert_wg=w['expert_wg'], expert_wu=w['expert_wu'], expert_wd=w['expert_wd'],
        shared_wg=w['shared_wg'].astype(bf16), shared_wu=w['shared_wu'].astype(bf16),
        shared_wd=w['shared_wd'].astype(bf16),
    )


def _rope_tables(seq_len):
    rows = seq_len // GRID_W
    row_idx = jnp.repeat(jnp.arange(rows, dtype=jnp.int32), GRID_W)
    col_idx = jnp.tile(jnp.arange(GRID_W, dtype=jnp.int32), rows)
    inv = ROPE_THETA ** (-jnp.arange(0, ROPE_HALF, 2, dtype=f32) / ROPE_HALF)
    ang = jnp.stack([row_idx.astype(f32)[:, None] * inv, col_idx.astype(f32)[:, None] * inv], axis=1)
    cos, sin = jnp.cos(ang), jnp.sin(ang)
    cos_l = jnp.stack([cos, cos], axis=2).reshape(seq_len, HEAD_DIM)
    sin_l = jnp.stack([-sin, sin], axis=2).reshape(seq_len, HEAD_DIM)
    pad = ((0, 0), (0, HEAD_PAD - HEAD_DIM))
    return jnp.pad(cos_l, pad), jnp.pad(sin_l, pad)


def _states_to_pairs(s):
    b = s.shape[0]
    s = s.reshape(b, 2, N_QUADS, QUAD_HEADS, RWKV_HEAD, RWKV_HEAD)
    eye = jnp.eye(QUAD_HEADS, dtype=s.dtype)
    return jnp.einsum('bdphvk,hg->bdphkgv', s, eye).reshape(b, 2, N_QUADS, QUAD, QUAD)


def _pairs_to_states(s):
    b = s.shape[0]
    s = s.reshape(b, 2, N_QUADS, QUAD_HEADS, RWKV_HEAD, QUAD_HEADS, RWKV_HEAD)
    diag = jnp.stack([s[:, :, :, h, :, h, :] for h in range(QUAD_HEADS)], axis=3)
    return jnp.swapaxes(diag, -1, -2).reshape(b, 2, RWKV_HEADS, RWKV_HEAD, RWKV_HEAD)


def _trunk(x, mod3, mod_row, p, rope_tabs, past_k, past_v, s0_pairs, tm_in, tq, tm_prep):
    b, t, _ = x.shape
    n = b * t
    x2 = x.reshape(n, D_MODEL)
    q_pad, k_own, v_own, k_pad, zr = _in_proj(
        x2, mod3, mod_row, p['g_pre_mix'], p['wq'], p['wk'], p['wv'], p['wr'], p['qg'], p['kg'],
        rope_tabs, t, tm_in)
    k_full = k_pad.reshape(b, t, KV_HEADS * HEAD_PAD)
    v_full = v_own.reshape(b, t, KV_HEADS * HEAD_DIM)
    if past_k is not None:
        k_past = _pad_heads(past_k.reshape(-1, KV_HEADS * HEAD_DIM), KV_HEADS).astype(bf16)
        k_full = jnp.concatenate([k_past.reshape(b, -1, KV_HEADS * HEAD_PAD), k_full], axis=1)
        v_full = jnp.concatenate([past_v.reshape(b, -1, KV_HEADS * HEAD_DIM), v_full], axis=1)
    vt_full = jnp.swapaxes(v_full, 1, 2).astype(bf16)
    attn = _attention(q_pad, k_full, vt_full, b, t, tq)
    r, k, v, kk, lw, a, g, bonus = _rwkv_prep(zr, p, t, tm_prep)
    o_fw, o_bw, s_fin = _rwkv_scan(r, k, v, kk, lw, a, p['k_a'], s0_pairs, b, t)
    x1, h2, slots, sw, cnt = _mix_out(attn, o_fw, o_bw, bonus, g, x2, mod3, mod_row, p, MIX_TILES * TM_MOE)
    return dict(h2=h2, slots=slots, sw=sw, cnt=cnt, x1=x1, mod_row=mod_row), k_own, v_own, s_fin


def kernel(x_prompt, x_sample, c, cache_k, cache_v, state_rwkv, c_ctx, w_ada, b_ada, g_pre_mix, g_post_mix,
           g_pre_ffn, g_post_ffn, w_in, w_out, q_gain, k_gain, mu_prev, mu_next, decay_w0, decay_w2, iclr_a0,
           iclr_a2, gate_g2, k_k, k_a, r_k, lnx_g, lnx_b, router_w, router_b, expert_wg, expert_wu, expert_wd,
           shared_wg, shared_wu, shared_wd):
    layer = 0
    names = ('g_pre_mix g_post_mix g_pre_ffn g_post_ffn w_in w_out q_gain k_gain mu_prev mu_next decay_w0 '
             'decay_w2 iclr_a0 iclr_a2 gate_g2 k_k k_a r_k lnx_g lnx_b router_w router_b expert_wg expert_wu '
             'expert_wd shared_wg shared_wu shared_wd').split()
    vals = (g_pre_mix, g_post_mix, g_pre_ffn, g_post_ffn, w_in, w_out, q_gain, k_gain, mu_prev, mu_next,
            decay_w0, decay_w2, iclr_a0, iclr_a2, gate_g2, k_k, k_a, r_k, lnx_g, lnx_b, router_w, router_b,
            expert_wg, expert_wu, expert_wd, shared_wg, shared_wu, shared_wd)
    p = _prepare_params({nm: v[layer] for nm, v in zip(names, vals)})

    nb, ts, _ = x_sample.shape
    npb, tp, _ = x_prompt.shape
    mod_rows = 16
    cc = jnp.zeros((mod_rows, D_MODEL), f32).at[0].set(c_ctx).at[1:1 + nb].set(c)
    mod3 = _modulation(cc, w_ada[layer], b_ada[layer]).reshape(mod_rows, 1, 6 * D_MODEL)

    zeros_state = jnp.zeros((npb, 2, N_QUADS, QUAD, QUAD), f32)
    moe_p, kc, vc, st = _trunk(x_prompt, mod3, lambda i, tm: 0, p, None, None, None, zeros_state,
                               tm_in=256, tq=tp, tm_prep=256)
    s0_lat = _states_to_pairs(state_rwkv[:, layer])
    moe_s, _, _, _ = _trunk(x_sample, mod3, lambda i, tm: 1 + (i * tm) // ts, p, _rope_tables(ts),
                            cache_k[:, layer], cache_v[:, layer], s0_lat,
                            tm_in=512, tq=256, tm_prep=256)
    y_p, y_s = _sparse_moe([moe_p, moe_s], mod3, p)
    y_p = y_p.reshape(x_prompt.shape)
    y_s = y_s.reshape(x_sample.shape)

    new_cache_k = kc.reshape(npb, 1, tp, KV_HEADS, HEAD_DIM)
    new_cache_v = vc.reshape(npb, 1, tp, KV_HEADS, HEAD_DIM)
    new_state = _pairs_to_states(st)[:, None]
    return (y_p, y_s, new_cache_k, new_cache_v, new_state)
```

```python
import functools

import numpy as np
import jax
import jax.numpy as jnp
from jax import lax
from jax.experimental import pallas as pl
from jax.experimental.pallas import tpu as pltpu

f32 = jnp.float32
bf16 = jnp.bfloat16

D_MODEL = 1024
GRID_W = 64
HEAD_DIM = 64
ATTN_HEADS = 8
KV_HEADS = 2
GQA_GROUP = ATTN_HEADS // KV_HEADS
ATTN_SCALE = HEAD_DIM ** -0.5
ROPE_THETA = 10000.0
ROPE_HALF = HEAD_DIM // 2
ROPE_FREQS = ROPE_HALF // 2
RWKV_HEAD = 64
RWKV_HEADS = 8
RWKV_WIDTH = RWKV_HEADS * RWKV_HEAD
DECAY_RANK = 64
ICLR_RANK = 64
GATE_RANK = 128
RWKV_IN = 3 * RWKV_WIDTH + 2 * DECAY_RANK + 2 * ICLR_RANK + GATE_RANK
DECAY_SCALE = 0.606531
GN_EPS = 64e-5
N_EXPERTS = 64
TOP_K = 6
N_GROUPS = 8
GROUP_SIZE = N_EXPERTS // N_GROUPS
TOPK_GROUPS = 4
EXPERT_FF = 256
ROUTED_SCALE = 2.5
EPS = 1e-6

LANE = 128
HEAD_PAD = LANE
CHUNK = 64
QUAD_HEADS = 4
QUAD = QUAD_HEADS * RWKV_HEAD
N_QUADS = RWKV_HEADS // QUAD_HEADS
SCAN_SEQS = 4
VMEM_LIMIT = 56 * 1024 * 1024
TM_MOE = 256
MIX_TILES = 4
IN_SUB = 256
SEG_SHIFT = 4
SEG = 1 << SEG_SHIFT
SLOT_ROWS = 2560
MAX_CHUNKS = SLOT_ROWS // SEG
CHUNK_GROUP_SHIFT = 3
CHUNK_GROUP = 1 << CHUNK_GROUP_SHIFT
SEL_BLOCK = 512
COMMON_BLOCKS = 4
FFN_ROWS = 1024


def _cparams(*sem):
    return pltpu.CompilerParams(dimension_semantics=sem, vmem_limit_bytes=VMEM_LIMIT)


def _dot(a, b, nt=False):
    dims = (((1,), (1,)), ((), ())) if nt else (((1,), (0,)), ((), ()))
    return lax.dot_general(a, b, dims, preferred_element_type=f32)


def _split2(x):
    hi = x.astype(bf16)
    lo = (x - hi.astype(f32)).astype(bf16)
    return hi, lo


def _split3(x):
    h1 = x.astype(bf16)
    r1 = x - h1.astype(f32)
    h2 = r1.astype(bf16)
    h3 = (r1 - h2.astype(f32)).astype(bf16)
    return h1, h2, h3


def _mm(a, b, nt=False, passes=3):
    if passes == 1:
        return _dot(a.astype(bf16), b.astype(bf16), nt)
    ah, al = _split2(a)
    bh, bl = _split2(b)
    return _dot(ah, bh, nt) + (_dot(al, bh, nt) + _dot(ah, bl, nt))


def _seg_sum(x, seg_bf16):
    return _dot(x.astype(bf16), seg_bf16)


def _rms(x, g):
    return x * lax.rsqrt(jnp.mean(x * x, axis=-1, keepdims=True) + EPS) * g


def _sigmoid(x):
    return jax.nn.sigmoid(x)


def _full_spec(a, grid_rank=1):
    zeros = (0,) * a.ndim
    if grid_rank == 1:
        return pl.BlockSpec(a.shape, lambda i: zeros)
    if grid_rank == 2:
        return pl.BlockSpec(a.shape, lambda i, j: zeros)
    return pl.BlockSpec(a.shape, lambda i, j, k: zeros)


def _mod_kernel(c_ref, w_ref, b_ref, o_ref):
    c = c_ref[...]
    s = c * _sigmoid(c)
    o_ref[...] = _mm(s, w_ref[...]) + b_ref[...]


def _modulation(cc, w_ada, b_ada):
    rows, n = cc.shape[0], w_ada.shape[1]
    tn = 512
    return pl.pallas_call(
        _mod_kernel,
        grid=(n // tn,),
        in_specs=[pl.BlockSpec((rows, D_MODEL), lambda j: (0, 0)),
                  pl.BlockSpec((D_MODEL, tn), lambda j: (0, j)),
                  pl.BlockSpec((1, tn), lambda j: (0, j))],
        out_specs=pl.BlockSpec((rows, tn), lambda j: (0, j)),
        out_shape=jax.ShapeDtypeStruct((rows, n), f32),
        compiler_params=_cparams("parallel"),
        name="modulation",
    )(cc, w_ada, b_ada.reshape(1, n))


def _in_kernel(*refs, rope):
    if rope:
        (x_ref, mod_ref, g_ref, wq_ref, wk_ref, wv_ref, wr_ref, qg_ref, kg_ref, cos_ref, sin_ref,
         q_ref, ko_ref, vo_ref, kp_ref, zr_ref) = refs
    else:
        (x_ref, mod_ref, g_ref, wq_ref, wk_ref, wv_ref, wr_ref, qg_ref, kg_ref,
         q_ref, ko_ref, vo_ref, kp_ref, zr_ref) = refs
    m = mod_ref[...]
    lane = lax.broadcasted_iota(jnp.int32, (1, LANE), 1)
    first_half = (lane & (ROPE_HALF - 1)) < ROPE_FREQS
    subs = [pl.ds(t * IN_SUB, IN_SUB) for t in range(x_ref.shape[0] // IN_SUB)]

    def front(rows):
        h = _rms(x_ref[rows, :], g_ref[...])
        return (h * (1.0 + m[:, D_MODEL:2 * D_MODEL]) + m[:, 0:D_MODEL]).astype(bf16)

    def head_norm(z, gain):
        ms = jnp.sum(z * z, axis=-1, keepdims=True) * (1.0 / HEAD_DIM)
        return z * lax.rsqrt(ms + EPS) * gain

    def rotate(y, rows):
        if not rope:
            return y
        partner = jnp.where(first_half, pltpu.roll(y, LANE - ROPE_FREQS, 1), pltpu.roll(y, ROPE_FREQS, 1))
        return y * cos_ref[rows, :] + partner * sin_ref[rows, :]

    def epilogue(rows, zq, zk):
        q = jnp.concatenate(
            [rotate(head_norm(zq[:, hh * HEAD_PAD:(hh + 1) * HEAD_PAD], qg_ref[...]), rows).astype(bf16)
             for hh in range(ATTN_HEADS)], axis=1)
        kn = [head_norm(zk[:, hh * HEAD_PAD:(hh + 1) * HEAD_PAD], kg_ref[...]) for hh in range(KV_HEADS)]
        k_own = kn[0] + pltpu.roll(kn[1], HEAD_DIM, 1)
        k_rot = jnp.concatenate([rotate(kn[hh], rows).astype(bf16) for hh in range(KV_HEADS)], axis=1)
        return q, k_own, k_rot

    hbs = [front(rows) for rows in subs]
    zs = [(_dot(hb, wr_ref[...]), _dot(hb, wq_ref[...]), _dot(hb, wk_ref[...]), _dot(hb, wv_ref[...])) for hb in hbs]
    outs = [epilogue(rows, zq, zk) for rows, (_, zq, zk, _) in zip(subs, zs)]
    for rows, (zr, _, _, zv), (q, k_own, k_rot) in zip(subs, zs, outs):
        zr_ref[rows, :] = zr
        q_ref[rows, :] = q
        ko_ref[rows, :] = k_own
        vo_ref[rows, :] = zv
        kp_ref[rows, :] = k_rot


def _in_proj(x2, mod3, mod_row, g_pre, wq, wk, wv, wr, qg, kg, rope_tabs, seq_len, tm):
    n = x2.shape[0]
    rope = rope_tabs is not None
    in_specs = [pl.BlockSpec((tm, D_MODEL), lambda i: (i, 0)),
                pl.BlockSpec((None, 1, 6 * D_MODEL), lambda i: (mod_row(i, tm), 0, 0)),
                _full_spec(g_pre), _full_spec(wq), _full_spec(wk), _full_spec(wv), _full_spec(wr),
                _full_spec(qg), _full_spec(kg)]
    args = [x2, mod3, g_pre, wq, wk, wv, wr, qg, kg]
    if rope:
        blocks_per_seq = seq_len // tm
        in_specs += [pl.BlockSpec((tm, LANE), lambda i: (i % blocks_per_seq, 0))] * 2
        args += list(rope_tabs)
    out_shape = [jax.ShapeDtypeStruct((n, ATTN_HEADS * HEAD_PAD), bf16),
                 jax.ShapeDtypeStruct((n, KV_HEADS * HEAD_DIM), f32),
                 jax.ShapeDtypeStruct((n, KV_HEADS * HEAD_DIM), f32),
                 jax.ShapeDtypeStruct((n, KV_HEADS * HEAD_PAD), bf16),
                 jax.ShapeDtypeStruct((n, RWKV_IN), f32)]
    out_specs = [pl.BlockSpec((tm, s.shape[1]), lambda i: (i, 0)) for s in out_shape]
    return pl.pallas_call(
        functools.partial(_in_kernel, rope=rope),
        grid=(n // tm,), in_specs=in_specs, out_specs=out_specs, out_shape=out_shape,
        compiler_params=_cparams("parallel"), name="in_proj",
    )(*args)


def _attn_kernel(q_ref, k_ref, vt_ref, o_ref):
    tq = q_ref.shape[0]
    def scores(kv):
        kh = k_ref[:, kv * HEAD_PAD:(kv + 1) * HEAD_PAD]
        q4 = jnp.concatenate(
            [q_ref[:, (GQA_GROUP * kv + g) * HEAD_PAD:(GQA_GROUP * kv + g + 1) * HEAD_PAD]
             for g in range(GQA_GROUP)], axis=0)
        return _dot(kh, q4, nt=True)

    def softmax(st):
        p = jnp.exp(st - jnp.max(st, axis=0, keepdims=True))
        return p.astype(bf16), jnp.sum(p, axis=0, keepdims=True)

    def values(kv, p, den):
        vt = vt_ref[kv * HEAD_DIM:(kv + 1) * HEAD_DIM, :]
        ot = _dot(vt, p) / den
        return [ot[:, g * tq:(g + 1) * tq] for g in range(GQA_GROUP)]

    st0 = scores(0)
    p0, den0 = softmax(st0)
    st1 = scores(1)
    heads_t = values(0, p0, den0)
    p1, den1 = softmax(st1)
    heads_t += values(1, p1, den1)
    o_ref[...] = jnp.concatenate(heads_t, axis=0).T.astype(bf16)


def _attention(q_pad, k_full, vt_full, batch, seq_len, tq):
    n = q_pad.shape[0]
    kv_len = k_full.shape[1]
    nq = seq_len // tq
    return pl.pallas_call(
        _attn_kernel,
        grid=(batch, nq),
        in_specs=[pl.BlockSpec((tq, ATTN_HEADS * HEAD_PAD), lambda b, j: (b * nq + j, 0)),
                  pl.BlockSpec((None, kv_len, KV_HEADS * HEAD_PAD), lambda b, j: (b, 0, 0)),
                  pl.BlockSpec((None, KV_HEADS * HEAD_DIM, kv_len), lambda b, j: (b, 0, 0))],
        out_specs=pl.BlockSpec((tq, ATTN_HEADS * HEAD_DIM), lambda b, j: (b * nq + j, 0)),
        out_shape=jax.ShapeDtypeStruct((n, ATTN_HEADS * HEAD_DIM), bf16),
        compiler_params=_cparams("parallel", "parallel"), name="attention",
    )(q_pad, k_full, vt_full)


def _prep_kernel(z_ref, zp_ref, zn_ref, mup_ref, mun_ref, w0_ref, w2_ref, a0_ref, a2_ref, g2_ref,
                 kkw_ref, ka_ref, rk_ref, seg_ref,
                 r_ref, k_ref, v_ref, kk_ref, lw_ref, a_ref, g_ref, bonus_ref, *, seq_len, tm):
    i = pl.program_id(0)
    z = z_ref[...]
    row = lax.broadcasted_iota(jnp.int32, (tm, 1), 0)
    seq_start = ((i * tm) % seq_len) == 0
    seq_end = (((i + 1) * tm) % seq_len) == 0
    prow = jnp.where(seq_start, 0.0, zp_ref[7:8, :])
    nrow = jnp.where(seq_end, 0.0, zn_ref[0:1, :])
    prev = jnp.where(row == 0, prow, pltpu.roll(z, 1, 0))
    nxt = jnp.where(row == tm - 1, nrow, pltpu.roll(z, tm - 1, 0))
    zs = z + mup_ref[...] * (prev - z) + mun_ref[...] * (nxt - z)
    w = RWKV_WIDTH
    r = zs[:, 0:w]
    k = zs[:, w:2 * w]
    v = zs[:, 2 * w:3 * w]
    dw = zs[:, 3 * w:3 * w + 2 * DECAY_RANK]
    da = zs[:, 3 * w + 2 * DECAY_RANK:3 * w + 2 * DECAY_RANK + 2 * ICLR_RANK]
    dg = zs[:, 3 * w + 2 * DECAY_RANK + 2 * ICLR_RANK:]
    lw = -DECAY_SCALE * _sigmoid(w0_ref[...] + _mm(jnp.tanh(dw), w2_ref[...], passes=1))
    a = _sigmoid(a0_ref[...] + _mm(da, a2_ref[...], passes=1))
    g = _mm(_sigmoid(dg), g2_ref[...], passes=1)
    kk = k * kkw_ref[...]
    seg = seg_ref[...]
    kk = kk * lax.rsqrt(_seg_sum(kk * kk, seg) + 1e-12)
    ka = ka_ref[...]
    ke_sum = k * (2.0 + (a[:, 0:w] + a[:, w:2 * w] - 2.0) * ka)
    bonus = _seg_sum(r * ke_sum * rk_ref[...], seg) * v
    r_ref[...] = r
    k_ref[...] = k
    v_ref[...] = v.astype(bf16)
    kk_ref[...] = kk
    lw_ref[...] = lw
    a_ref[...] = a
    g_ref[...] = g.astype(bf16)
    bonus_ref[...] = bonus.astype(bf16)


def _rwkv_prep(zr, p, seq_len, tm):
    n = zr.shape[0]
    w = RWKV_WIDTH
    nb8 = n // 8
    consts = [p['mu_prev'], p['mu_next'], p['w0'], p['w2bd'], p['a0'], p['a2bd'], p['g2'],
              p['k_k'], p['k_a'], p['r_k'], p['seg']]
    in_specs = [pl.BlockSpec((tm, RWKV_IN), lambda i: (i, 0)),
                pl.BlockSpec((8, RWKV_IN), lambda i: (jnp.maximum(i * (tm // 8) - 1, 0), 0)),
                pl.BlockSpec((8, RWKV_IN), lambda i: (jnp.minimum((i + 1) * (tm // 8), nb8 - 1), 0))]
    in_specs += [_full_spec(c) for c in consts]
    widths = [w, w, w, w, 2 * w, 2 * w, w, w]
    dtypes = [f32, f32, bf16, f32, f32, f32, bf16, bf16]
    out_shape = [jax.ShapeDtypeStruct((n, wd), dt) for wd, dt in zip(widths, dtypes)]
    out_specs = [pl.BlockSpec((tm, wd), lambda i: (i, 0)) for wd in widths]
    return pl.pallas_call(
        functools.partial(_prep_kernel, seq_len=seq_len, tm=tm),
        grid=(n // tm,), in_specs=in_specs, out_specs=out_specs, out_shape=out_shape,
        compiler_params=_cparams("parallel"), name="rwkv_prep",
    )(zr, zr, zr, *consts)


def _scan_kernel(rf_ref, kf_ref, vf_ref, kkf_ref, lwf_ref, af_ref,
                 rb_ref, kb_ref, vb_ref, kkb_ref, lwb_ref, ab_ref, ka_ref, s0_ref,
                 of_ref, ob_ref, sfin_ref, st_ref):
    c = pl.program_id(1)

    @pl.when(c == 0)
    def _():
        st_ref[...] = s0_ref[...]

    n_seq = rf_ref.shape[0]
    row = lax.broadcasted_iota(jnp.int32, (CHUNK, CHUNK), 0)
    col = lax.broadcasted_iota(jnp.int32, (CHUNK, CHUNK), 1)
    qr = lax.broadcasted_iota(jnp.int32, (QUAD, QUAD), 0)
    qc = lax.broadcasted_iota(jnp.int32, (QUAD, QUAD), 1)
    head_mask = (qr & -RWKV_HEAD) == (qc & -RWKV_HEAD)
    head_mask_b = jnp.where(head_mask, 1.0, 0.0).astype(bf16)
    tr = lax.broadcasted_iota(jnp.int32, (CHUNK, QUAD), 0)
    tc = lax.broadcasted_iota(jnp.int32, (CHUNK, QUAD), 1) & (CHUNK - 1)
    eye = jnp.where(tr == tc, 1.0, 0.0).astype(f32)
    ka = ka_ref[...]

    def bd(x):
        xb = x.astype(bf16)
        return jnp.concatenate([xb] * QUAD_HEADS, axis=0) * head_mask_b

    def mm(a, b, nt=False):
        return _dot(a.astype(bf16), b.astype(bf16), nt)

    chains = []
    for d, (r_ref, k_ref, v_ref, kk_ref, lw_ref, a_ref, o_ref) in enumerate(
            ((rf_ref, kf_ref, vf_ref, kkf_ref, lwf_ref, af_ref, of_ref),
             (rb_ref, kb_ref, vb_ref, kkb_ref, lwb_ref, ab_ref, ob_ref))):
        sgn = 1 - 2 * d
        tri = jnp.where((row - col) * sgn >= 0, 1.0, 0.0).astype(bf16)
        dist = (tr - tc) * sgn
        for s in range(n_seq):
            lw = lw_ref[s]
            l1, l2, l3 = _split3(lw)
            cum = _dot(tri, l1) + (_dot(tri, l2) + _dot(tri, l3))
            e_incl = jnp.exp(cum)
            e_excl = jnp.exp(cum - lw)
            e_neg = jnp.exp(-cum)
            wc = jnp.exp(jnp.sum(lw, axis=0, keepdims=True))
            kk = kk_ref[s]
            a = a_ref[s]
            bh = kk * a * e_neg
            kh = k_ref[s] * (1.0 + (a - 1.0) * ka) * e_neg
            ops = dict(at=-kk * e_excl, rt=r_ref[s] * e_incl, bh=bh, kh=kh, bw=bh * wc, kw=kh * wc,
                       v=v_ref[s], wc=wc)
            for g in range(N_QUADS):
                ch = {key: val[:, g * QUAD:(g + 1) * QUAD] for key, val in ops.items()}
                ch.update(s=s, d=d, g=g, o_ref=o_ref, strict=dist > 0, incl=dist >= 0, st=st_ref[s, d, g])
                chains.append(ch)

    lane_lo = lax.broadcasted_iota(jnp.int32, (QUAD, LANE), 1) < RWKV_HEAD
    for ch in chains:
        ar = jnp.concatenate([ch['at'], ch['rt']], axis=0)
        ch['ar'] = ar.astype(bf16)
        bk_t = jnp.concatenate([ch['bh'], ch['kh']], axis=0).T
        swapped = pltpu.roll(bk_t, RWKV_HEAD, 1)
        b_t = jnp.where(lane_lo, bk_t, swapped).astype(bf16)
        k_t = jnp.where(lane_lo, swapped, bk_t).astype(bf16)
        w_bk = jnp.concatenate([jnp.concatenate([b_t, b_t], axis=1) * head_mask_b,
                                jnp.concatenate([k_t, k_t], axis=1) * head_mask_b], axis=1)
        amat = _dot(ch['ar'], w_bk)
        ch['n'] = jnp.where(ch['strict'], amat[0:CHUNK, 0:QUAD], 0.0)
        ch['a_kr'] = jnp.concatenate([jnp.where(ch['strict'], amat[0:CHUNK, QUAD:2 * QUAD], 0.0),
                                      jnp.where(ch['incl'], amat[CHUNK:2 * CHUNK, QUAD:2 * QUAD], 0.0)], axis=0)
        ch['a_rb'] = jnp.where(ch['incl'], amat[CHUNK:2 * CHUNK, 0:QUAD], 0.0)
        ch['tinv'] = eye + ch['n']
    for ch in chains:
        ch['n'] = mm(ch['n'], bd(ch['n']))
    for _ in range(4):
        for ch in chains:
            both = mm(jnp.concatenate([ch['n'], ch['tinv']], axis=0), bd(ch['n']))
            ch['n'] = both[0:CHUNK]
            ch['tinv'] = ch['tinv'] + both[CHUNK:2 * CHUNK]
    for ch in chains:
        ch['tinv'] = ch['tinv'] + mm(ch['tinv'], bd(ch['n']))
        sa = _dot(ch['ar'], ch['st'].astype(bf16))
        av = mm(ch['a_kr'], bd(ch['v']))
        ch['rhs'] = sa[0:CHUNK] + av[0:CHUNK]
        ch['o'] = sa[CHUNK:2 * CHUNK] + av[CHUNK:2 * CHUNK]
    for ch in chains:
        ch['u'] = mm(ch['tinv'], bd(ch['rhs']))
    for ch in chains:
        ch['o'] = ch['o'] + mm(ch['a_rb'], bd(ch['u']))
        z_t = jnp.concatenate([ch['bw'], ch['kw'], jnp.broadcast_to(ch['wc'], (2 * CHUNK, QUAD))], axis=0).T
        upd = mm(z_t[:, 0:2 * CHUNK], jnp.concatenate([ch['u'].astype(bf16), ch['v']], axis=0))
        decay = jnp.concatenate([z_t[:, 2 * CHUNK:4 * CHUNK]] * 2, axis=1)
        ch['st_new'] = ch['st'] * decay + jnp.where(head_mask, upd, 0.0)
    for ch in chains:
        ch['o_ref'][ch['s'], :, ch['g'] * QUAD:(ch['g'] + 1) * QUAD] = ch['o']
        st_ref[ch['s'], ch['d'], ch['g']] = ch['st_new']

    @pl.when(c == pl.num_programs(1) - 1)
    def _():
        for s in range(n_seq):
            for d in range(2):
                for g in range(N_QUADS):
                    s_vk = st_ref[s, d, g].T
                    for h in range(QUAD_HEADS):
                        band = s_vk[h * RWKV_HEAD:(h + 1) * RWKV_HEAD, :]
                        if h:
                            band = pltpu.roll(band, QUAD - h * RWKV_HEAD, 1)
                        sfin_ref[s, d, g * QUAD_HEADS + h] = band[:, 0:RWKV_HEAD]


def _rwkv_scan(r, k, v, kk, lw, a, k_a, s0, batch, seq_len):
    n = r.shape[0]
    w = RWKV_WIDTH
    nc = seq_len // CHUNK
    per_seq = lambda x: x.reshape(batch, seq_len, x.shape[-1])
    r, k, v, kk, lw, a = (per_seq(x) for x in (r, k, v, kk, lw, a))
    fwd = lambda b, c: (b, c, 0)
    bwd = lambda b, c: (b, nc - 1 - c, 0)
    bwd_dir = lambda b, c: (b, nc - 1 - c, 1)
    blk = lambda index_map: pl.BlockSpec((SCAN_SEQS, CHUNK, w), index_map)
    state_spec = pl.BlockSpec((SCAN_SEQS, 2, N_QUADS, QUAD, QUAD), lambda b, c: (b, 0, 0, 0, 0))
    o_shape = jax.ShapeDtypeStruct((batch, seq_len, w), f32)
    o_fw, o_bw, s_fin = pl.pallas_call(
        _scan_kernel,
        grid=(batch // SCAN_SEQS, nc),
        in_specs=[blk(fwd)] * 6 + [blk(bwd)] * 4 + [blk(bwd_dir)] * 2
                 + [pl.BlockSpec((1, w), lambda b, c: (0, 0)), state_spec],
        out_specs=[blk(fwd), blk(bwd),
                   pl.BlockSpec((SCAN_SEQS, 2, RWKV_HEADS, RWKV_HEAD, RWKV_HEAD), lambda b, c: (b, 0, 0, 0, 0))],
        out_shape=[o_shape, o_shape, jax.ShapeDtypeStruct((batch, 2, RWKV_HEADS, RWKV_HEAD, RWKV_HEAD), f32)],
        scratch_shapes=[pltpu.VMEM((SCAN_SEQS, 2, N_QUADS, QUAD, QUAD), f32)],
        compiler_params=_cparams("parallel", "arbitrary"), name="rwkv_scan",
    )(r, k, v, kk, lw, a, r, k, v, kk, lw, a, k_a, s0)
    return o_fw.reshape(n, w), o_bw.reshape(n, w), s_fin


def _out_kernel(attn_ref, of_ref, ob_ref, bonus_ref, g_ref, x_ref, mod_ref, lng_ref, lnb_ref, seg_ref,
                wo_ref, gpm_ref, gpf_ref, rwt_ref, rb_ref,
                x1_ref, h2_ref, slot_ref, sw_ref, cnt_ref):
    tiles = [pl.ds(t * TM_MOE, TM_MOE) for t in range(x_ref.shape[0] // TM_MOE)]
    fronts = [_mix_front(attn_ref.at[rows], of_ref.at[rows], ob_ref.at[rows], bonus_ref.at[rows],
                         g_ref.at[rows], x_ref.at[rows], mod_ref, lng_ref, lnb_ref, seg_ref, wo_ref,
                         gpm_ref, gpf_ref) for rows in tiles]
    picks = [_router_topk(h2, rwt_ref, rb_ref) for _, h2 in fronts]
    routes = [_router_slots(*pick) for pick in picks]
    for t, (rows, (x1, h2), (slots, sw, cnt)) in enumerate(zip(tiles, fronts, routes)):
        x1_ref[rows, :] = x1
        h2_ref[rows, :] = h2.astype(bf16)
        slot_ref[:, rows] = slots
        sw_ref[rows, :] = sw
        cnt_ref[t] = cnt


def _mix_front(attn_ref, of_ref, ob_ref, bonus_ref, g_ref, x_ref, mod_ref, lng_ref, lnb_ref, seg_ref,
               wo_ref, gpm_ref, gpf_ref):
    seg = seg_ref[...]
    inv = 1.0 / RWKV_HEAD
    o = of_ref[...] + ob_ref[...]
    mu = _seg_sum(o, seg) * inv
    dl = o - mu
    var = _seg_sum(dl * dl, seg) * inv
    on = dl * lax.rsqrt(var + GN_EPS) * lng_ref[...] + lnb_ref[...]
    rw = (on + bonus_ref[...]) * g_ref[...]
    half = ATTN_HEADS * HEAD_DIM
    y = _dot(attn_ref[...].astype(bf16), wo_ref[0:half, :]) + _dot(rw.astype(bf16), wo_ref[half:, :])
    m = mod_ref[...]
    gt_a = m[:, 2 * D_MODEL:3 * D_MODEL]
    sh_f = m[:, 3 * D_MODEL:4 * D_MODEL]
    sc_f = m[:, 4 * D_MODEL:5 * D_MODEL]
    x1 = x_ref[...] + gt_a * _rms(y, gpm_ref[...])
    h2 = _rms(x1, gpf_ref[...]) * (1.0 + sc_f) + sh_f
    return x1, h2


def _router_topk(h2, rwt_ref, rb_ref):
    tm = h2.shape[0]
    logits = _mm(rwt_ref[...], h2, nt=True)
    scores = _sigmoid(logits)
    biased = scores + rb_ref[...]
    neg = -jnp.inf
    shape3 = (N_GROUPS, GROUP_SIZE, tm)
    b3 = biased.reshape(shape3)
    s3 = scores.reshape(shape3)
    in_grp = lax.broadcasted_iota(jnp.int32, shape3, 1)
    grp = lax.broadcasted_iota(jnp.int32, shape3, 0)
    m1 = jnp.max(b3, axis=1, keepdims=True)
    i1 = jnp.min(jnp.where(b3 == m1, in_grp, GROUP_SIZE), axis=1, keepdims=True)
    m2 = jnp.max(jnp.where(in_grp == i1, neg, b3), axis=1, keepdims=True)
    gscore = m1 + m2
    gidx = lax.broadcasted_iota(jnp.int32, (N_GROUPS, 1, tm), 0)
    gsel = jnp.zeros((N_GROUPS, 1, tm), f32)
    cur = gscore
    for _ in range(TOPK_GROUPS):
        mx = jnp.max(cur, axis=0, keepdims=True)
        ii = jnp.min(jnp.where(cur == mx, gidx, N_GROUPS), axis=0, keepdims=True)
        hit = gidx == ii
        gsel = jnp.where(hit, 1.0, gsel)
        cur = jnp.where(hit, neg, cur)
    cand = jnp.where(jnp.broadcast_to(gsel, shape3) > 0.0, b3, neg)
    eidx = grp * GROUP_SIZE + in_grp
    wsel = jnp.zeros(shape3, f32)
    mask = jnp.zeros(shape3, f32)
    hits = []
    for _ in range(TOP_K):
        mx = jnp.max(jnp.max(cand, axis=1, keepdims=True), axis=0, keepdims=True)
        ii = jnp.min(jnp.min(jnp.where(cand == mx, eidx, N_EXPERTS), axis=1, keepdims=True),
                     axis=0, keepdims=True)
        hit = eidx == ii
        hits.append(hit)
        wsel = jnp.where(hit, s3, wsel)
        mask = jnp.where(hit, 1.0, mask)
        cand = jnp.where(hit, neg, cand)
    den = jnp.sum(jnp.sum(wsel, axis=1, keepdims=True), axis=0, keepdims=True)
    gates3 = wsel / den * ROUTED_SCALE
    return hits, gates3, mask


def _router_slots(hits, gates3, mask):
    tm = mask.shape[-1]
    shape3 = mask.shape
    mask2 = mask.reshape(N_EXPERTS, tm)
    cnt = jnp.sum(mask2, axis=1, keepdims=True)
    cnt_pad = jnp.floor((cnt + (SEG - 1)) * (1.0 / SEG)) * SEG
    er = lax.broadcasted_iota(jnp.int32, (N_EXPERTS, N_EXPERTS), 0)
    ec = lax.broadcasted_iota(jnp.int32, (N_EXPERTS, N_EXPERTS), 1)
    before = jnp.where(ec < er, 1.0, 0.0).astype(bf16)
    cnt_pad_l = jnp.broadcast_to(cnt_pad, (N_EXPERTS, LANE))
    seg_start = _dot(before, cnt_pad_l.astype(bf16))
    tr = lax.broadcasted_iota(jnp.int32, (tm, tm), 0)
    tc = lax.broadcasted_iota(jnp.int32, (tm, tm), 1)
    earlier = jnp.where(tr < tc, 1.0, 0.0).astype(bf16)
    rank = _dot(mask2.astype(bf16), earlier)
    slot3 = (rank + seg_start[:, 0:1]).reshape(shape3)

    def pick(hit, val3):
        return jnp.sum(jnp.sum(jnp.where(hit, val3, 0.0), axis=1, keepdims=True), axis=0)

    slots = [pick(h, slot3) for h in hits]
    wts = [pick(h, gates3) for h in hits]
    pad_rows = lambda k, val: [jnp.full((k, tm), val, f32)]
    slot_rows = jnp.concatenate(slots + pad_rows(8 - TOP_K, -1.0), axis=0).astype(jnp.int32)
    sw_t = jnp.concatenate(slots + pad_rows(8 - TOP_K, -1.0) + wts + pad_rows(LANE - 8 - TOP_K, 0.0), axis=0)
    return slot_rows, sw_t.T, cnt_pad_l


def _mix_out(attn, o_fw, o_bw, bonus, g, x2, mod3, mod_row, p, tm):
    n = x2.shape[0]
    w = RWKV_WIDTH
    consts = [p['lnx_g'], p['lnx_b'], p['seg'], p['w_out'], p['g_post_mix'], p['g_pre_ffn'],
              p['router_wt'], p['router_b']]
    row = lambda wd: pl.BlockSpec((tm, wd), lambda i: (i, 0))
    in_specs = [row(ATTN_HEADS * HEAD_DIM), row(w), row(w),
                row(w), row(w), row(D_MODEL),
                pl.BlockSpec((None, 1, 6 * D_MODEL), lambda i: (mod_row(i, tm), 0, 0))]
    in_specs += [_full_spec(c) for c in consts]
    out_shape = [jax.ShapeDtypeStruct((n, D_MODEL), f32),
                 jax.ShapeDtypeStruct((n, D_MODEL), bf16),
                 jax.ShapeDtypeStruct((8, n), jnp.int32),
                 jax.ShapeDtypeStruct((n, LANE), f32),
                 jax.ShapeDtypeStruct((n // TM_MOE, N_EXPERTS, LANE), f32)]
    out_specs = [row(D_MODEL), row(D_MODEL), pl.BlockSpec((8, tm), lambda i: (0, i)), row(LANE),
                 pl.BlockSpec((tm // TM_MOE, N_EXPERTS, LANE), lambda i: (i, 0, 0))]
    return pl.pallas_call(
        _out_kernel, grid=(n // tm,), in_specs=in_specs, out_specs=out_specs, out_shape=out_shape,
        compiler_params=_cparams("parallel"), name="mix_out",
    )(attn, o_fw, o_bw, bonus, g, x2, mod3, *consts)


def _chunk_groups(tile, nch_sm):
    return lax.shift_right_logical(nch_sm[tile] + (CHUNK_GROUP - 1), CHUNK_GROUP_SHIFT)


def _start_chunks(tile, dst_sm, nch_sm, make_copy):
    def body(g, carry):
        for u in range(CHUNK_GROUP):
            c = g * CHUNK_GROUP + u
            make_copy(c, pl.multiple_of(dst_sm[tile * MAX_CHUNKS + c], SEG)).start()
        return carry

    lax.fori_loop(0, _chunk_groups(tile, nch_sm), body, 0)


def _wait_chunks(tile, nch_sm, make_copy):
    def body(g, carry):
        for _ in range(CHUNK_GROUP):
            make_copy(0, 0).wait()
        return carry

    lax.fori_loop(0, _chunk_groups(tile, nch_sm), body, 0)


def _dispatch_kernel(dst_sm, nch_sm, fill_sm, *refs, tile_starts):
    n_paths = len(tile_starts)
    h_refs, slot_refs = refs[:n_paths], refs[n_paths:2 * n_paths]
    xs_hbm, xc_ref, zero_ref, sems, zsem = refs[2 * n_paths:]
    i = pl.program_id(0)
    last = pl.num_programs(0) - 1
    buf = lax.rem(i, 2)

    def copy_from(b):
        def make(chunk, global_row):
            src = xc_ref.at[b, pl.ds(pl.multiple_of(chunk * SEG, SEG), SEG)]
            return pltpu.make_async_copy(src, xs_hbm.at[pl.ds(global_row, SEG)], sems.at[b])
        return make

    @pl.when(i == 0)
    def _():
        zero_ref[...] = jnp.zeros_like(zero_ref)
        n_tiles = xs_hbm.shape[0] // FFN_ROWS

        def tail_copy(row):
            return pltpu.make_async_copy(zero_ref.at[pl.ds(0, SEG)], xs_hbm.at[pl.ds(row, SEG)], zsem)

        def tile_copy(t):
            return pltpu.make_async_copy(zero_ref, xs_hbm.at[pl.ds(pl.multiple_of(t * FFN_ROWS, FFN_ROWS), FFN_ROWS)],
                                         zsem)

        def per_expert(e, carry):
            start = fill_sm[1 + e]
            chunks = lax.shift_right_logical(fill_sm[1 + N_EXPERTS + e], SEG_SHIFT)
            lax.fori_loop(0, chunks, lambda c, z: (tail_copy(pl.multiple_of(start + c * SEG, SEG)).start(), z)[1], 0)
            lax.fori_loop(0, chunks, lambda c, z: (tail_copy(0).wait(), z)[1], 0)
            return carry

        lax.fori_loop(0, N_EXPERTS, per_expert, 0)
        tile_copy(n_tiles - 1).start()
        tile_copy(0).wait()
        lax.fori_loop(fill_sm[0], n_tiles - 1, lambda t, z: (tile_copy(t).start(), z)[1], 0)

    @pl.when(i >= 2)
    def _():
        _wait_chunks(i - 2, nch_sm, copy_from(buf))

    slot, h = slot_refs[0][...], h_refs[0][...]
    for q in range(1, n_paths):
        mine = i >= tile_starts[q]
        slot = jnp.where(mine, slot_refs[q][...], slot)
        h = jnp.where(mine, h_refs[q][...], h)
    used = nch_sm[i] * SEG

    def build(rb):
        rows = lax.broadcasted_iota(jnp.int32, (SEL_BLOCK, TM_MOE), 0) + rb * SEL_BLOCK
        sel = jnp.zeros((SEL_BLOCK, TM_MOE), f32)
        for j in range(TOP_K):
            sel = jnp.where(rows == slot[j:j + 1, :], 1.0, sel)
        xc_ref[buf, rb * SEL_BLOCK:(rb + 1) * SEL_BLOCK, :] = _dot(sel.astype(bf16), h).astype(bf16)

    for rb in range(COMMON_BLOCKS):
        build(rb)
    for rb in range(COMMON_BLOCKS, SLOT_ROWS // SEL_BLOCK):
        pl.when(rb * SEL_BLOCK < used)(functools.partial(build, rb))

    _start_chunks(i, dst_sm, nch_sm, copy_from(buf))

    @pl.when(i == last)
    def _():
        @pl.when(i >= 1)
        def _():
            _wait_chunks(i - 1, nch_sm, copy_from(1 - buf))
        _wait_chunks(i, nch_sm, copy_from(buf))
        n_tiles = xs_hbm.shape[0] // FFN_ROWS
        whole_tile = pltpu.make_async_copy(zero_ref, xs_hbm.at[pl.ds(0, FFN_ROWS)], zsem)
        lax.fori_loop(fill_sm[0], n_tiles - 1, lambda t, z: (whole_tile.wait(), z)[1], 0)


def _dispatch(h2s, slots, dst, nch, fill, total_rows):
    tiles = [h.shape[0] // TM_MOE for h in h2s]
    starts = [sum(tiles[:q]) for q in range(len(tiles))]

    def in_path(q):
        return lambda i: jnp.clip(i - starts[q], 0, tiles[q] - 1)

    h_specs = [pl.BlockSpec((TM_MOE, D_MODEL), lambda i, d, c, f, q=q: (in_path(q)(i), 0)) for q in range(len(tiles))]
    s_specs = [pl.BlockSpec((8, TM_MOE), lambda i, d, c, f, q=q: (0, in_path(q)(i))) for q in range(len(tiles))]
    grid_spec = pltpu.PrefetchScalarGridSpec(
        num_scalar_prefetch=3, grid=(sum(tiles),),
        in_specs=h_specs + s_specs,
        out_specs=pl.BlockSpec(memory_space=pl.ANY),
        scratch_shapes=[pltpu.VMEM((2, SLOT_ROWS, D_MODEL), bf16), pltpu.VMEM((FFN_ROWS, D_MODEL), bf16),
                        pltpu.SemaphoreType.DMA((2,)), pltpu.SemaphoreType.DMA(())])
    return pl.pallas_call(
        functools.partial(_dispatch_kernel, tile_starts=tuple(starts)), grid_spec=grid_spec,
        out_shape=jax.ShapeDtypeStruct((total_rows, D_MODEL), bf16),
        compiler_params=_cparams("arbitrary"), name="moe_dispatch",
    )(dst, nch, fill, *h2s, *slots)


def _ffn_kernel(te_sm, nv_sm, xs_ref, wg_ref, wu_ref, wd_ref, ys_ref, wg_b, wu_b, wd_b):
    r = pl.program_id(0)
    valid = r < nv_sm[0]

    @pl.when(jnp.logical_and(valid, jnp.logical_or(r == 0, te_sm[r] != te_sm[jnp.maximum(r - 1, 0)])))
    def _():
        wg_b[...] = wg_ref[...].astype(bf16)
        wu_b[...] = wu_ref[...].astype(bf16)
        wd_b[...] = wd_ref[...].astype(bf16)

    @pl.when(valid)
    def _():
        x = xs_ref[...]
        gg = _dot(x, wg_b[...])
        uu = _dot(x, wu_b[...])
        hm = gg * _sigmoid(gg) * uu
        ys_ref[...] = _dot(hm.astype(bf16), wd_b[...]).astype(bf16)

    @pl.when(r >= nv_sm[0])
    def _():
        ys_ref[...] = jnp.zeros_like(ys_ref)


def _expert_ffn(xs, tile_expert, n_valid, wg, wu, wd):
    total_rows = xs.shape[0]
    last = lambda r, nv: jnp.minimum(r, nv[0] - 1)
    grid_spec = pltpu.PrefetchScalarGridSpec(
        num_scalar_prefetch=2, grid=(total_rows // FFN_ROWS,),
        in_specs=[pl.BlockSpec((FFN_ROWS, D_MODEL), lambda r, te, nv: (last(r, nv), 0)),
                  pl.BlockSpec((None, D_MODEL, EXPERT_FF), lambda r, te, nv: (te[last(r, nv)], 0, 0)),
                  pl.BlockSpec((None, D_MODEL, EXPERT_FF), lambda r, te, nv: (te[last(r, nv)], 0, 0)),
                  pl.BlockSpec((None, EXPERT_FF, D_MODEL), lambda r, te, nv: (te[last(r, nv)], 0, 0))],
        out_specs=pl.BlockSpec((FFN_ROWS, D_MODEL), lambda r, te, nv: (r, 0)),
        scratch_shapes=[pltpu.VMEM((D_MODEL, EXPERT_FF), bf16), pltpu.VMEM((D_MODEL, EXPERT_FF), bf16),
                        pltpu.VMEM((EXPERT_FF, D_MODEL), bf16)])
    return pl.pallas_call(
        _ffn_kernel, grid_spec=grid_spec,
        out_shape=jax.ShapeDtypeStruct((total_rows, D_MODEL), bf16),
        compiler_params=_cparams("arbitrary"), name="moe_ffn",
    )(tile_expert, n_valid, xs, wg, wu, wd)


def _combine_kernel(dst_sm, nch_sm, sw_ref, h_ref, x1_ref, mod_ref, gpost_ref, swg_ref, swu_ref, swd_ref,
                    ys_hbm, y_ref, yc_ref, acc_ref, sems, *, tile_offset):
    step = pl.program_id(0)
    i = step + tile_offset
    buf = lax.rem(step, 2)

    def copy_into(b):
        def make(chunk, global_row):
            dst = yc_ref.at[b, pl.ds(pl.multiple_of(chunk * SEG, SEG), SEG)]
            return pltpu.make_async_copy(ys_hbm.at[pl.ds(global_row, SEG)], dst, sems.at[b])
        return make

    @pl.when(step == 0)
    def _():
        yc_ref[...] = jnp.zeros_like(yc_ref)
        _start_chunks(i, dst_sm, nch_sm, copy_into(buf))

    @pl.when(step + 1 < pl.num_programs(0))
    def _():
        _start_chunks(i + 1, dst_sm, nch_sm, copy_into(1 - buf))

    h = h_ref[...]
    gg = _dot(h, swg_ref[...])
    uu = _dot(h, swu_ref[...])
    acc_ref[...] = _dot((gg * _sigmoid(gg) * uu).astype(bf16), swd_ref[...])
    _wait_chunks(i, nch_sm, copy_into(buf))
    sw = sw_ref[...]
    used = nch_sm[i] * SEG
    def weighted(rb):
        cols = (lax.broadcasted_iota(jnp.int32, (TM_MOE, SEL_BLOCK), 1) + rb * SEL_BLOCK).astype(f32)
        wmat = jnp.zeros((TM_MOE, SEL_BLOCK), f32)
        for j in range(TOP_K):
            wmat = jnp.where(cols == sw[:, j:j + 1], sw[:, 8 + j:9 + j], wmat)
        rows = lax.broadcasted_iota(jnp.int32, (SEL_BLOCK, 1), 0) + rb * SEL_BLOCK
        yc = jnp.where(rows < used, yc_ref[buf, rb * SEL_BLOCK:(rb + 1) * SEL_BLOCK, :], jnp.zeros((), bf16))
        return _dot(wmat.astype(bf16), yc)

    moe = acc_ref[...]
    for rb in range(COMMON_BLOCKS):
        moe = moe + weighted(rb)
    acc_ref[...] = moe
    for rb in range(COMMON_BLOCKS, SLOT_ROWS // SEL_BLOCK):
        @pl.when(rb * SEL_BLOCK < used)
        def _():
            acc_ref[...] += weighted(rb)
    gt_f = mod_ref[...][:, 5 * D_MODEL:6 * D_MODEL]
    y_ref[...] = x1_ref[...] + gt_f * _rms(acc_ref[...], gpost_ref[...])


def _combine(ys, dst, nch, sw, h2, x1, mod3, mod_row, g_post, swg, swu, swd, tile_offset):
    n = h2.shape[0]
    row = lambda wd: pl.BlockSpec((TM_MOE, wd), lambda i, d, c: (i, 0))
    const = lambda a: pl.BlockSpec(a.shape, lambda i, d, c: (0,) * a.ndim)
    grid_spec = pltpu.PrefetchScalarGridSpec(
        num_scalar_prefetch=2, grid=(n // TM_MOE,),
        in_specs=[row(LANE), row(D_MODEL), row(D_MODEL),
                  pl.BlockSpec((None, 1, 6 * D_MODEL), lambda i, d, c: (mod_row(i, TM_MOE), 0, 0)),
                  const(g_post), const(swg), const(swu), const(swd),
                  pl.BlockSpec(memory_space=pl.ANY)],
        out_specs=row(D_MODEL),
        scratch_shapes=[pltpu.VMEM((2, SLOT_ROWS, D_MODEL), bf16), pltpu.VMEM((TM_MOE, D_MODEL), f32),
                        pltpu.SemaphoreType.DMA((2,))])
    return pl.pallas_call(
        functools.partial(_combine_kernel, tile_offset=tile_offset), grid_spec=grid_spec,
        out_shape=jax.ShapeDtypeStruct((n, D_MODEL), f32),
        compiler_params=_cparams("arbitrary"), name="moe_combine",
    )(dst, nch, sw, h2, x1, mod3, g_post, swg, swu, swd, ys)


def _sparse_moe(paths, mod3, p):
    cnt = jnp.concatenate([q['cnt'][:, :, 0] for q in paths], axis=0).astype(jnp.int32)
    n_tiles = cnt.shape[0]
    before = jnp.cumsum(cnt, axis=0) - cnt
    rows_e = jnp.sum(cnt, axis=0)
    region = (rows_e + FFN_ROWS - 1) // FFN_ROWS * FFN_ROWS
    region_end = jnp.cumsum(region)
    region_start = region_end - region
    worst = n_tiles * (TOP_K * TM_MOE + N_EXPERTS * (SEG - 1)) + N_EXPERTS * (FFN_ROWS - SEG)
    total_rows = -(-worst // FFN_ROWS) * FFN_ROWS + FFN_ROWS
    spare_row = total_rows - FFN_ROWS
    first_row = jnp.arange(total_rows // FFN_ROWS, dtype=jnp.int32) * FFN_ROWS
    ended = jnp.sum((region_end[None, :] <= first_row[:, None]).astype(jnp.int32), axis=1)
    tile_expert = jnp.minimum(ended, N_EXPERTS - 1).astype(jnp.int32)
    n_valid = (region_end[-1:] // FFN_ROWS).astype(jnp.int32)

    seg_end = jnp.cumsum(cnt, axis=1)
    seg_start = seg_end - cnt
    nch = (seg_end[:, -1] // SEG).astype(jnp.int32)
    chunk_row = jnp.arange(MAX_CHUNKS, dtype=jnp.int32) * SEG
    chunk_e = jnp.sum((seg_end[:, None, :] <= chunk_row[None, :, None]).astype(jnp.int32), axis=2)
    own = chunk_e[:, :, None] == jnp.arange(N_EXPERTS, dtype=jnp.int32)[None, None, :]
    shift = (region_start[None, :] + before - seg_start)[:, None, :]
    dst = jnp.sum(jnp.where(own, shift, 0), axis=2) + chunk_row[None, :]
    chunk_id = jnp.arange(MAX_CHUNKS, dtype=jnp.int32)[None, :]
    tile_id = jnp.arange(n_tiles, dtype=jnp.int32)[:, None]
    spare = spare_row + ((tile_id % 2) * CHUNK_GROUP + chunk_id % CHUNK_GROUP) * SEG
    dst = jnp.where(chunk_id < nch[:, None], dst, spare).astype(jnp.int32).reshape(-1)

    fill = jnp.concatenate([n_valid, region_start + rows_e, region - rows_e]).astype(jnp.int32)
    xs = _dispatch([q['h2'] for q in paths], [q['slots'] for q in paths], dst, nch, fill, total_rows)
    ys = _expert_ffn(xs, tile_expert, n_valid, p['expert_wg'], p['expert_wu'], p['expert_wd'])
    outs, tile_offset = [], 0
    for q in paths:
        outs.append(_combine(ys, dst, nch, q['sw'], q['h2'], q['x1'], mod3, q['mod_row'],
                             p['g_post_ffn'], p['shared_wg'], p['shared_wu'], p['shared_wd'], tile_offset))
        tile_offset += q['h2'].shape[0] // TM_MOE
    return outs


def _pad_heads(w, heads):
    rows = w.shape[0]
    w = w.reshape(rows, heads, HEAD_DIM)
    return jnp.pad(w, ((0, 0), (0, 0), (0, HEAD_PAD - HEAD_DIM))).reshape(rows, heads * HEAD_PAD)


def _block_diag2(m):
    z = jnp.zeros_like(m[0])
    return jnp.concatenate([jnp.concatenate([m[0], z], axis=1), jnp.concatenate([z, m[1]], axis=1)], axis=0)


def _prepare_params(w):
    aw, kw = ATTN_HEADS * HEAD_DIM, KV_HEADS * HEAD_DIM
    w_in = w['w_in']
    lane_row = lambda v: v.reshape(1, -1).astype(f32)
    head_id = np.arange(RWKV_WIDTH) // RWKV_HEAD
    seg = jnp.asarray(head_id[:, None] == head_id[None, :], dtype=bf16)
    gain_pad = lambda gvec: jnp.pad(gvec, (0, HEAD_PAD - HEAD_DIM)).reshape(1, HEAD_PAD)
    return dict(
        g_pre_mix=lane_row(w['g_pre_mix']), g_post_mix=lane_row(w['g_post_mix']),
        g_pre_ffn=lane_row(w['g_pre_ffn']), g_post_ffn=lane_row(w['g_post_ffn']),
        wq=_pad_heads(w_in[:, :aw], ATTN_HEADS).astype(bf16),
        wk=_pad_heads(w_in[:, aw:aw + kw], KV_HEADS).astype(bf16),
        wv=w_in[:, aw + kw:aw + 2 * kw].astype(bf16),
        wr=w_in[:, aw + 2 * kw:].astype(bf16),
        qg=gain_pad(w['q_gain'] * ATTN_SCALE), kg=gain_pad(w['k_gain']),
        mu_prev=lane_row(w['mu_prev']), mu_next=lane_row(w['mu_next']),
        w0=lane_row(w['decay_w0']), w2bd=_block_diag2(w['decay_w2']),
        a0=lane_row(w['iclr_a0']), a2bd=_block_diag2(w['iclr_a2']),
        g2=w['gate_g2'], k_k=lane_row(w['k_k']), k_a=lane_row(w['k_a']), r_k=lane_row(w['r_k']),
        lnx_g=lane_row(w['lnx_g']), lnx_b=lane_row(w['lnx_b']), seg=seg,
        w_out=w['w_out'].astype(bf16),
        router_wt=w['router_w'].T, router_b=w['router_b'].reshape(N_EXPERTS, 1),
        expert_wg=w['expert_wg'], expert_wu=w['expert_wu'], expert_wd=w['expert_wd'],
        shared_wg=w['shared_wg'].astype(bf16), shared_wu=w['shared_wu'].astype(bf16),
        shared_wd=w['shared_wd'].astype(bf16),
    )


def _rope_tables(seq_len):
    rows = seq_len // GRID_W
    row_idx = jnp.repeat(jnp.arange(rows, dtype=jnp.int32), GRID_W)
    col_idx = jnp.tile(jnp.arange(GRID_W, dtype=jnp.int32), rows)
    inv = ROPE_THETA ** (-jnp.arange(0, ROPE_HALF, 2, dtype=f32) / ROPE_HALF)
    ang = jnp.stack([row_idx.astype(f32)[:, None] * inv, col_idx.astype(f32)[:, None] * inv], axis=1)
    cos, sin = jnp.cos(ang), jnp.sin(ang)
    cos_l = jnp.stack([cos, cos], axis=2).reshape(seq_len, HEAD_DIM)
    sin_l = jnp.stack([-sin, sin], axis=2).reshape(seq_len, HEAD_DIM)
    pad = ((0, 0), (0, HEAD_PAD - HEAD_DIM))
    return jnp.pad(cos_l, pad), jnp.pad(sin_l, pad)


def _states_to_pairs(s):
    b = s.shape[0]
    s = s.reshape(b, 2, N_QUADS, QUAD_HEADS, RWKV_HEAD, RWKV_HEAD)
    eye = jnp.eye(QUAD_HEADS, dtype=s.dtype)
    return jnp.einsum('bdphvk,hg->bdphkgv', s, eye).reshape(b, 2, N_QUADS, QUAD, QUAD)


def _trunk(x, mod3, mod_row, p, rope_tabs, past_k, past_v, s0_pairs, tm_in, tq, tm_prep):
    b, t, _ = x.shape
    n = b * t
    x2 = x.reshape(n, D_MODEL)
    q_pad, k_own, v_own, k_pad, zr = _in_proj(
        x2, mod3, mod_row, p['g_pre_mix'], p['wq'], p['wk'], p['wv'], p['wr'], p['qg'], p['kg'],
        rope_tabs, t, tm_in)
    k_full = k_pad.reshape(b, t, KV_HEADS * HEAD_PAD)
    v_full = v_own.reshape(b, t, KV_HEADS * HEAD_DIM)
    if past_k is not None:
        k_past = _pad_heads(past_k.reshape(-1, KV_HEADS * HEAD_DIM), KV_HEADS).astype(bf16)
        k_full = jnp.concatenate([k_past.reshape(b, -1, KV_HEADS * HEAD_PAD), k_full], axis=1)
        v_full = jnp.concatenate([past_v.reshape(b, -1, KV_HEADS * HEAD_DIM), v_full], axis=1)
    vt_full = jnp.swapaxes(v_full, 1, 2).astype(bf16)
    attn = _attention(q_pad, k_full, vt_full, b, t, tq)
    r, k, v, kk, lw, a, g, bonus = _rwkv_prep(zr, p, t, tm_prep)
    o_fw, o_bw, s_fin = _rwkv_scan(r, k, v, kk, lw, a, p['k_a'], s0_pairs, b, t)
    x1, h2, slots, sw, cnt = _mix_out(attn, o_fw, o_bw, bonus, g, x2, mod3, mod_row, p, MIX_TILES * TM_MOE)
    return dict(h2=h2, slots=slots, sw=sw, cnt=cnt, x1=x1, mod_row=mod_row), k_own, v_own, s_fin


def kernel(x_prompt, x_sample, c, cache_k, cache_v, state_rwkv, c_ctx, w_ada, b_ada, g_pre_mix, g_post_mix,
           g_pre_ffn, g_post_ffn, w_in, w_out, q_gain, k_gain, mu_prev, mu_next, decay_w0, decay_w2, iclr_a0,
           iclr_a2, gate_g2, k_k, k_a, r_k, lnx_g, lnx_b, router_w, router_b, expert_wg, expert_wu, expert_wd,
           shared_wg, shared_wu, shared_wd):
    layer = 0
    names = ('g_pre_mix g_post_mix g_pre_ffn g_post_ffn w_in w_out q_gain k_gain mu_prev mu_next decay_w0 '
             'decay_w2 iclr_a0 iclr_a2 gate_g2 k_k k_a r_k lnx_g lnx_b router_w router_b expert_wg expert_wu '
             'expert_wd shared_wg shared_wu shared_wd').split()
    vals = (g_pre_mix, g_post_mix, g_pre_ffn, g_post_ffn, w_in, w_out, q_gain, k_gain, mu_prev, mu_next,
            decay_w0, decay_w2, iclr_a0, iclr_a2, gate_g2, k_k, k_a, r_k, lnx_g, lnx_b, router_w, router_b,
            expert_wg, expert_wu, expert_wd, shared_wg, shared_wu, shared_wd)
    p = _prepare_params({nm: v[layer] for nm, v in zip(names, vals)})

    nb, ts, _ = x_sample.shape
    npb, tp, _ = x_prompt.shape
    latent_tm_in = 512
    assert ts % (MIX_TILES * TM_MOE) == 0 and ts % latent_tm_in == 0 and ts % GRID_W == 0
    assert nb % SCAN_SEQS == 0 and npb % SCAN_SEQS == 0 and (npb * tp) % (MIX_TILES * TM_MOE) == 0
    mod_rows = 16
    assert nb < mod_rows
    cc = jnp.zeros((mod_rows, D_MODEL), f32).at[0].set(c_ctx).at[1:1 + nb].set(c)
    mod3 = _modulation(cc, w_ada[layer], b_ada[layer]).reshape(mod_rows, 1, 6 * D_MODEL)

    zeros_state = jnp.zeros((npb, 2, N_QUADS, QUAD, QUAD), f32)
    moe_p, kc, vc, st = _trunk(x_prompt, mod3, lambda i, tm: 0, p, None, None, None, zeros_state,
                               tm_in=512, tq=tp, tm_prep=256)
    s0_lat = _states_to_pairs(state_rwkv[:, layer])
    moe_s, _, _, _ = _trunk(x_sample, mod3, lambda i, tm: 1 + (i * tm) // ts, p, _rope_tables(ts),
                            cache_k[:, layer], cache_v[:, layer], s0_lat,
                            tm_in=latent_tm_in, tq=256, tm_prep=256)
    y_p, y_s = _sparse_moe([moe_p, moe_s], mod3, p)
    y_p = y_p.reshape(x_prompt.shape)
    y_s = y_s.reshape(x_sample.shape)

    new_cache_k = kc.reshape(npb, 1, tp, KV_HEADS, HEAD_DIM)
    new_cache_v = vc.reshape(npb, 1, tp, KV_HEADS, HEAD_DIM)
    new_state = st[:, None]
    return (y_p, y_s, new_cache_k, new_cache_v, new_state)
```

```python
import functools
from typing import NamedTuple

import numpy as np
import jax
import jax.numpy as jnp
from jax import lax
from jax.experimental import pallas as pl
from jax.experimental.pallas import tpu as pltpu

f32 = jnp.float32
bf16 = jnp.bfloat16

D_MODEL = 1024
GRID_W = 64
HEAD_DIM = 64
ATTN_HEADS = 8
KV_HEADS = 2
GQA_GROUP = ATTN_HEADS // KV_HEADS
ATTN_SCALE = HEAD_DIM ** -0.5
ROPE_THETA = 10000.0
ROPE_HALF = HEAD_DIM // 2
ROPE_FREQS = ROPE_HALF // 2
RWKV_HEAD = 64
RWKV_HEADS = 8
RWKV_WIDTH = RWKV_HEADS * RWKV_HEAD
DECAY_RANK = 64
ICLR_RANK = 64
GATE_RANK = 128
RWKV_IN = 3 * RWKV_WIDTH + 2 * DECAY_RANK + 2 * ICLR_RANK + GATE_RANK
DECAY_SCALE = 0.606531
GN_EPS = 64e-5
N_EXPERTS = 64
TOP_K = 6
N_GROUPS = 8
GROUP_SIZE = N_EXPERTS // N_GROUPS
TOPK_GROUPS = 4
EXPERT_FF = 256
ROUTED_SCALE = 2.5
EPS = 1e-6

LANE = 128
HEAD_PAD = LANE
CHUNK = 64
QUAD_HEADS = 4
QUAD = QUAD_HEADS * RWKV_HEAD
N_QUADS = RWKV_HEADS // QUAD_HEADS
SCAN_SEQS = 4
VMEM_LIMIT = 56 * 1024 * 1024


class PathTiles(NamedTuple):
    in_rows: int
    attn_queries: int
    prep_rows: int


CONTEXT_TILES = PathTiles(in_rows=512, attn_queries=0, prep_rows=256)
LATENT_TILES = PathTiles(in_rows=512, attn_queries=256, prep_rows=256)
TM_MOE = 256
MIX_TILES = 4
IN_SUB = 256
SEG_SHIFT = 4
SEG = 1 << SEG_SHIFT
SLOT_ROWS = 2560
MAX_CHUNKS = SLOT_ROWS // SEG
CHUNK_GROUP_SHIFT = 3
CHUNK_GROUP = 1 << CHUNK_GROUP_SHIFT
SEL_BLOCK = 512
COMMON_BLOCKS = 4
FFN_ROWS = 1024


def _cparams(*sem):
    return pltpu.CompilerParams(dimension_semantics=sem, vmem_limit_bytes=VMEM_LIMIT)


def _dot(a, b, nt=False):
    dims = (((1,), (1,)), ((), ())) if nt else (((1,), (0,)), ((), ()))
    return lax.dot_general(a, b, dims, preferred_element_type=f32)


def _split2(x):
    hi = x.astype(bf16)
    lo = (x - hi.astype(f32)).astype(bf16)
    return hi, lo


def _split3(x):
    h1 = x.astype(bf16)
    r1 = x - h1.astype(f32)
    h2 = r1.astype(bf16)
    h3 = (r1 - h2.astype(f32)).astype(bf16)
    return h1, h2, h3


def _mm(a, b, nt=False, passes=3):
    if passes == 1:
        return _dot(a.astype(bf16), b.astype(bf16), nt)
    ah, al = _split2(a)
    bh, bl = _split2(b)
    return _dot(ah, bh, nt) + (_dot(al, bh, nt) + _dot(ah, bl, nt))


def _seg_sum(x, seg_bf16):
    return _dot(x.astype(bf16), seg_bf16)


def _rms(x, g):
    return x * lax.rsqrt(jnp.mean(x * x, axis=-1, keepdims=True) + EPS) * g


def _sigmoid(x):
    return jax.nn.sigmoid(x)


def _full_spec(a, grid_rank=1):
    zeros = (0,) * a.ndim
    if grid_rank == 1:
        return pl.BlockSpec(a.shape, lambda i: zeros)
    if grid_rank == 2:
        return pl.BlockSpec(a.shape, lambda i, j: zeros)
    return pl.BlockSpec(a.shape, lambda i, j, k: zeros)


def _mod_kernel(c_ref, w_ref, b_ref, o_ref):
    c = c_ref[...]
    s = c * _sigmoid(c)
    o_ref[...] = _mm(s, w_ref[...]) + b_ref[...]


def _modulation(cc, w_ada, b_ada):
    rows, n = cc.shape[0], w_ada.shape[1]
    tn = 512
    return pl.pallas_call(
        _mod_kernel,
        grid=(n // tn,),
        in_specs=[pl.BlockSpec((rows, D_MODEL), lambda j: (0, 0)),
                  pl.BlockSpec((D_MODEL, tn), lambda j: (0, j)),
                  pl.BlockSpec((1, tn), lambda j: (0, j))],
        out_specs=pl.BlockSpec((rows, tn), lambda j: (0, j)),
        out_shape=jax.ShapeDtypeStruct((rows, n), f32),
        compiler_params=_cparams("parallel"),
        name="modulation",
    )(cc, w_ada, b_ada.reshape(1, n))


def _in_kernel(*refs, rope):
    if rope:
        (x_ref, mod_ref, g_ref, wq_ref, wk_ref, wv_ref, wr_ref, qg_ref, kg_ref, cos_ref, sin_ref,
         q_ref, ko_ref, vo_ref, kp_ref, zr_ref) = refs
    else:
        (x_ref, mod_ref, g_ref, wq_ref, wk_ref, wv_ref, wr_ref, qg_ref, kg_ref,
         q_ref, ko_ref, vo_ref, kp_ref, zr_ref) = refs
    m = mod_ref[...]
    lane = lax.broadcasted_iota(jnp.int32, (1, LANE), 1)
    first_half = (lane & (ROPE_HALF - 1)) < ROPE_FREQS
    subs = [pl.ds(t * IN_SUB, IN_SUB) for t in range(x_ref.shape[0] // IN_SUB)]

    def front(rows):
        h = _rms(x_ref[rows, :], g_ref[...])
        return (h * (1.0 + m[:, D_MODEL:2 * D_MODEL]) + m[:, 0:D_MODEL]).astype(bf16)

    def head_norm(z, gain):
        ms = jnp.sum(z * z, axis=-1, keepdims=True) * (1.0 / HEAD_DIM)
        return z * lax.rsqrt(ms + EPS) * gain

    def rotate(y, rows):
        if not rope:
            return y
        partner = jnp.where(first_half, pltpu.roll(y, LANE - ROPE_FREQS, 1), pltpu.roll(y, ROPE_FREQS, 1))
        return y * cos_ref[rows, :] + partner * sin_ref[rows, :]

    def epilogue(rows, zq, zk):
        q = jnp.concatenate(
            [rotate(head_norm(zq[:, hh * HEAD_PAD:(hh + 1) * HEAD_PAD], qg_ref[...]), rows).astype(bf16)
             for hh in range(ATTN_HEADS)], axis=1)
        kn = [head_norm(zk[:, hh * HEAD_PAD:(hh + 1) * HEAD_PAD], kg_ref[...]) for hh in range(KV_HEADS)]
        k_own = kn[0] + pltpu.roll(kn[1], HEAD_DIM, 1)
        k_rot = jnp.concatenate([rotate(kn[hh], rows).astype(bf16) for hh in range(KV_HEADS)], axis=1)
        return q, k_own, k_rot

    hbs = [front(rows) for rows in subs]
    zs = [(_dot(hb, wr_ref[...]), _dot(hb, wq_ref[...]), _dot(hb, wk_ref[...]), _dot(hb, wv_ref[...])) for hb in hbs]
    outs = [epilogue(rows, zq, zk) for rows, (_, zq, zk, _) in zip(subs, zs)]
    for rows, (zr, _, _, zv), (q, k_own, k_rot) in zip(subs, zs, outs):
        zr_ref[rows, :] = zr
        q_ref[rows, :] = q
        ko_ref[rows, :] = k_own
        vo_ref[rows, :] = zv
        kp_ref[rows, :] = k_rot


def _in_proj(x2, mod3, mod_row, g_pre, wq, wk, wv, wr, qg, kg, rope_tabs, seq_len, tm):
    n = x2.shape[0]
    rope = rope_tabs is not None
    in_specs = [pl.BlockSpec((tm, D_MODEL), lambda i: (i, 0)),
                pl.BlockSpec((None, 1, 6 * D_MODEL), lambda i: (mod_row(i, tm), 0, 0)),
                _full_spec(g_pre), _full_spec(wq), _full_spec(wk), _full_spec(wv), _full_spec(wr),
                _full_spec(qg), _full_spec(kg)]
    args = [x2, mod3, g_pre, wq, wk, wv, wr, qg, kg]
    if rope:
        blocks_per_seq = seq_len // tm
        in_specs += [pl.BlockSpec((tm, LANE), lambda i: (i % blocks_per_seq, 0))] * 2
        args += list(rope_tabs)
    out_shape = [jax.ShapeDtypeStruct((n, ATTN_HEADS * HEAD_PAD), bf16),
                 jax.ShapeDtypeStruct((n, KV_HEADS * HEAD_DIM), f32),
                 jax.ShapeDtypeStruct((n, KV_HEADS * HEAD_DIM), f32),
                 jax.ShapeDtypeStruct((n, KV_HEADS * HEAD_PAD), bf16),
                 jax.ShapeDtypeStruct((n, RWKV_IN), f32)]
    out_specs = [pl.BlockSpec((tm, s.shape[1]), lambda i: (i, 0)) for s in out_shape]
    return pl.pallas_call(
        functools.partial(_in_kernel, rope=rope),
        grid=(n // tm,), in_specs=in_specs, out_specs=out_specs, out_shape=out_shape,
        compiler_params=_cparams("parallel"), name="in_proj",
    )(*args)


def _attn_kernel(q_ref, k_ref, vt_ref, o_ref):
    tq = q_ref.shape[0]
    def scores(kv):
        kh = k_ref[:, kv * HEAD_PAD:(kv + 1) * HEAD_PAD]
        q4 = jnp.concatenate(
            [q_ref[:, (GQA_GROUP * kv + g) * HEAD_PAD:(GQA_GROUP * kv + g + 1) * HEAD_PAD]
             for g in range(GQA_GROUP)], axis=0)
        return _dot(kh, q4, nt=True)

    def softmax(st):
        p = jnp.exp(st - jnp.max(st, axis=0, keepdims=True))
        return p.astype(bf16), jnp.sum(p, axis=0, keepdims=True)

    def values(kv, p, den):
        vt = vt_ref[kv * HEAD_DIM:(kv + 1) * HEAD_DIM, :]
        ot = _dot(vt, p) / den
        return [ot[:, g * tq:(g + 1) * tq] for g in range(GQA_GROUP)]

    st0 = scores(0)
    p0, den0 = softmax(st0)
    st1 = scores(1)
    heads_t = values(0, p0, den0)
    p1, den1 = softmax(st1)
    heads_t += values(1, p1, den1)
    o_ref[...] = jnp.concatenate(heads_t, axis=0).T.astype(bf16)


def _attention(q_pad, k_full, vt_full, batch, seq_len, tq):
    n = q_pad.shape[0]
    kv_len = k_full.shape[1]
    nq = seq_len // tq
    return pl.pallas_call(
        _attn_kernel,
        grid=(batch, nq),
        in_specs=[pl.BlockSpec((tq, ATTN_HEADS * HEAD_PAD), lambda b, j: (b * nq + j, 0)),
                  pl.BlockSpec((None, kv_len, KV_HEADS * HEAD_PAD), lambda b, j: (b, 0, 0)),
                  pl.BlockSpec((None, KV_HEADS * HEAD_DIM, kv_len), lambda b, j: (b, 0, 0))],
        out_specs=pl.BlockSpec((tq, ATTN_HEADS * HEAD_DIM), lambda b, j: (b * nq + j, 0)),
        out_shape=jax.ShapeDtypeStruct((n, ATTN_HEADS * HEAD_DIM), bf16),
        compiler_params=_cparams("parallel", "parallel"), name="attention",
    )(q_pad, k_full, vt_full)


def _prep_kernel(z_ref, zp_ref, zn_ref, mup_ref, mun_ref, w0_ref, w2_ref, a0_ref, a2_ref, g2_ref,
                 kkw_ref, ka_ref, rk_ref, seg_ref,
                 r_ref, k_ref, v_ref, kk_ref, lw_ref, a_ref, g_ref, bonus_ref, *, seq_len, tm):
    i = pl.program_id(0)
    z = z_ref[...]
    row = lax.broadcasted_iota(jnp.int32, (tm, 1), 0)
    seq_start = ((i * tm) % seq_len) == 0
    seq_end = (((i + 1) * tm) % seq_len) == 0
    prow = jnp.where(seq_start, 0.0, zp_ref[7:8, :])
    nrow = jnp.where(seq_end, 0.0, zn_ref[0:1, :])
    prev = jnp.where(row == 0, prow, pltpu.roll(z, 1, 0))
    nxt = jnp.where(row == tm - 1, nrow, pltpu.roll(z, tm - 1, 0))
    zs = z + mup_ref[...] * (prev - z) + mun_ref[...] * (nxt - z)
    w = RWKV_WIDTH
    r = zs[:, 0:w]
    k = zs[:, w:2 * w]
    v = zs[:, 2 * w:3 * w]
    dw = zs[:, 3 * w:3 * w + 2 * DECAY_RANK]
    da = zs[:, 3 * w + 2 * DECAY_RANK:3 * w + 2 * DECAY_RANK + 2 * ICLR_RANK]
    dg = zs[:, 3 * w + 2 * DECAY_RANK + 2 * ICLR_RANK:]
    lw = -DECAY_SCALE * _sigmoid(w0_ref[...] + _mm(jnp.tanh(dw), w2_ref[...], passes=1))
    a = _sigmoid(a0_ref[...] + _mm(da, a2_ref[...], passes=1))
    g = _mm(_sigmoid(dg), g2_ref[...], passes=1)
    kk = k * kkw_ref[...]
    seg = seg_ref[...]
    kk = kk * lax.rsqrt(_seg_sum(kk * kk, seg) + 1e-12)
    ka = ka_ref[...]
    ke_sum = k * (2.0 + (a[:, 0:w] + a[:, w:2 * w] - 2.0) * ka)
    bonus = _seg_sum(r * ke_sum * rk_ref[...], seg) * v
    r_ref[...] = r
    k_ref[...] = k
    v_ref[...] = v.astype(bf16)
    kk_ref[...] = kk
    lw_ref[...] = lw
    a_ref[...] = a
    g_ref[...] = g.astype(bf16)
    bonus_ref[...] = bonus.astype(bf16)


def _rwkv_prep(zr, p, seq_len, tm):
    n = zr.shape[0]
    w = RWKV_WIDTH
    nb8 = n // 8
    consts = [p['mu_prev'], p['mu_next'], p['w0'], p['w2bd'], p['a0'], p['a2bd'], p['g2'],
              p['k_k'], p['k_a'], p['r_k'], p['seg']]
    in_specs = [pl.BlockSpec((tm, RWKV_IN), lambda i: (i, 0)),
                pl.BlockSpec((8, RWKV_IN), lambda i: (jnp.maximum(i * (tm // 8) - 1, 0), 0)),
                pl.BlockSpec((8, RWKV_IN), lambda i: (jnp.minimum((i + 1) * (tm // 8), nb8 - 1), 0))]
    in_specs += [_full_spec(c) for c in consts]
    widths = [w, w, w, w, 2 * w, 2 * w, w, w]
    dtypes = [f32, f32, bf16, f32, f32, f32, bf16, bf16]
    out_shape = [jax.ShapeDtypeStruct((n, wd), dt) for wd, dt in zip(widths, dtypes)]
    out_specs = [pl.BlockSpec((tm, wd), lambda i: (i, 0)) for wd in widths]
    return pl.pallas_call(
        functools.partial(_prep_kernel, seq_len=seq_len, tm=tm),
        grid=(n // tm,), in_specs=in_specs, out_specs=out_specs, out_shape=out_shape,
        compiler_params=_cparams("parallel"), name="rwkv_prep",
    )(zr, zr, zr, *consts)


def _scan_kernel(rf_ref, kf_ref, vf_ref, kkf_ref, lwf_ref, af_ref,
                 rb_ref, kb_ref, vb_ref, kkb_ref, lwb_ref, ab_ref, ka_ref, *rest, has_init):
    s0_ref = rest[0] if has_init else None
    of_ref, ob_ref, sfin_ref, st_ref = rest[-4:]
    c = pl.program_id(1)

    @pl.when(c == 0)
    def _():
        st_ref[...] = s0_ref[...] if has_init else jnp.zeros_like(st_ref)

    n_seq = rf_ref.shape[0]
    row = lax.broadcasted_iota(jnp.int32, (CHUNK, CHUNK), 0)
    col = lax.broadcasted_iota(jnp.int32, (CHUNK, CHUNK), 1)
    qr = lax.broadcasted_iota(jnp.int32, (QUAD, QUAD), 0)
    qc = lax.broadcasted_iota(jnp.int32, (QUAD, QUAD), 1)
    head_mask = (qr & -RWKV_HEAD) == (qc & -RWKV_HEAD)
    head_mask_b = jnp.where(head_mask, 1.0, 0.0).astype(bf16)
    tr = lax.broadcasted_iota(jnp.int32, (CHUNK, QUAD), 0)
    tc = lax.broadcasted_iota(jnp.int32, (CHUNK, QUAD), 1) & (CHUNK - 1)
    eye = jnp.where(tr == tc, 1.0, 0.0).astype(f32)
    ka = ka_ref[...]

    def bd(x):
        xb = x.astype(bf16)
        return jnp.concatenate([xb] * QUAD_HEADS, axis=0) * head_mask_b

    def mm(a, b, nt=False):
        return _dot(a.astype(bf16), b.astype(bf16), nt)

    lane_lo = lax.broadcasted_iota(jnp.int32, (QUAD, LANE), 1) < RWKV_HEAD

    def first_stage(ch):
        ar = jnp.concatenate([ch['at'], ch['rt']], axis=0)
        ch['ar'] = ar.astype(bf16)
        bk_t = jnp.concatenate([ch['bh'], ch['kh']], axis=0).T
        swapped = pltpu.roll(bk_t, RWKV_HEAD, 1)
        b_t = jnp.where(lane_lo, bk_t, swapped).astype(bf16)
        k_t = jnp.where(lane_lo, swapped, bk_t).astype(bf16)
        w_bk = jnp.concatenate([jnp.concatenate([b_t, b_t], axis=1) * head_mask_b,
                                jnp.concatenate([k_t, k_t], axis=1) * head_mask_b], axis=1)
        amat = _dot(ch['ar'], w_bk)
        ch['n'] = jnp.where(ch['strict'], amat[0:CHUNK, 0:QUAD], 0.0)
        ch['a_kr'] = jnp.concatenate([jnp.where(ch['strict'], amat[0:CHUNK, QUAD:2 * QUAD], 0.0),
                                      jnp.where(ch['incl'], amat[CHUNK:2 * CHUNK, QUAD:2 * QUAD], 0.0)], axis=0)
        ch['a_rb'] = jnp.where(ch['incl'], amat[CHUNK:2 * CHUNK, 0:QUAD], 0.0)
        ch['tinv'] = eye + ch['n']

    chains = []
    for d, (r_ref, k_ref, v_ref, kk_ref, lw_ref, a_ref, o_ref) in enumerate(
            ((rf_ref, kf_ref, vf_ref, kkf_ref, lwf_ref, af_ref, of_ref),
             (rb_ref, kb_ref, vb_ref, kkb_ref, lwb_ref, ab_ref, ob_ref))):
        sgn = 1 - 2 * d
        tri = jnp.where((row - col) * sgn >= 0, 1.0, 0.0).astype(bf16)
        dist = (tr - tc) * sgn
        for s in range(n_seq):
            lw = lw_ref[s]
            l1, l2, l3 = _split3(lw)
            cum = _dot(tri, l1) + (_dot(tri, l2) + _dot(tri, l3))
            e_incl = jnp.exp(cum)
            e_excl = jnp.exp(cum - lw)
            e_neg = jnp.exp(-cum)
            wc = jnp.exp(jnp.sum(lw, axis=0, keepdims=True))
            kk = kk_ref[s]
            a = a_ref[s]
            bh = kk * a * e_neg
            kh = k_ref[s] * (1.0 + (a - 1.0) * ka) * e_neg
            ops = dict(at=-kk * e_excl, rt=r_ref[s] * e_incl, bh=bh, kh=kh, bw=bh * wc, kw=kh * wc,
                       v=v_ref[s], wc=wc)
            for g in range(N_QUADS):
                ch = {key: val[:, g * QUAD:(g + 1) * QUAD] for key, val in ops.items()}
                ch.update(s=s, d=d, g=g, o_ref=o_ref, strict=dist > 0, incl=dist >= 0, st=st_ref[s, d, g])
                chains.append(ch)

    for ch in chains:
        first_stage(ch)
    for ch in chains:
        ch['n'] = mm(ch['n'], bd(ch['n']))
    for _ in range(4):
        for ch in chains:
            both = mm(jnp.concatenate([ch['n'], ch['tinv']], axis=0), bd(ch['n']))
            ch['n'] = both[0:CHUNK]
            ch['tinv'] = ch['tinv'] + both[CHUNK:2 * CHUNK]
    for ch in chains:
        ch['tinv'] = ch['tinv'] + mm(ch['tinv'], bd(ch['n']))
        sa = _dot(ch['ar'], ch['st'].astype(bf16))
        av = mm(ch['a_kr'], bd(ch['v']))
        ch['rhs'] = sa[0:CHUNK] + av[0:CHUNK]
        ch['o'] = sa[CHUNK:2 * CHUNK] + av[CHUNK:2 * CHUNK]
    for ch in chains:
        ch['u'] = mm(ch['tinv'], bd(ch['rhs']))
    for ch in chains:
        ch['o'] = ch['o'] + mm(ch['a_rb'], bd(ch['u']))
        z_t = jnp.concatenate([ch['bw'], ch['kw'], jnp.broadcast_to(ch['wc'], (2 * CHUNK, QUAD))], axis=0).T
        upd = mm(z_t[:, 0:2 * CHUNK], jnp.concatenate([ch['u'].astype(bf16), ch['v']], axis=0))
        decay = jnp.concatenate([z_t[:, 2 * CHUNK:4 * CHUNK]] * 2, axis=1)
        ch['st_new'] = ch['st'] * decay + jnp.where(head_mask, upd, 0.0)
    for ch in chains:
        ch['o_ref'][ch['s'], :, ch['g'] * QUAD:(ch['g'] + 1) * QUAD] = ch['o']
        st_ref[ch['s'], ch['d'], ch['g']] = ch['st_new']

    @pl.when(c == pl.num_programs(1) - 1)
    def _():
        for s in range(n_seq):
            for d in range(2):
                for g in range(N_QUADS):
                    s_vk = st_ref[s, d, g].T
                    for h in range(QUAD_HEADS):
                        band = s_vk[h * RWKV_HEAD:(h + 1) * RWKV_HEAD, :]
                        if h:
                            band = pltpu.roll(band, QUAD - h * RWKV_HEAD, 1)
                        sfin_ref[s, d, g * QUAD_HEADS + h] = band[:, 0:RWKV_HEAD]


def _rwkv_scan(r, k, v, kk, lw, a, k_a, s0, batch, seq_len):
    n = r.shape[0]
    w = RWKV_WIDTH
    nc = seq_len // CHUNK
    per_seq = lambda x: x.reshape(batch, seq_len, x.shape[-1])
    r, k, v, kk, lw, a = (per_seq(x) for x in (r, k, v, kk, lw, a))
    fwd = lambda b, c: (b, c, 0)
    bwd = lambda b, c: (b, nc - 1 - c, 0)
    bwd_dir = lambda b, c: (b, nc - 1 - c, 1)
    blk = lambda index_map: pl.BlockSpec((SCAN_SEQS, CHUNK, w), index_map)
    state_spec = pl.BlockSpec((SCAN_SEQS, 2, N_QUADS, QUAD, QUAD), lambda b, c: (b, 0, 0, 0, 0))
    o_shape = jax.ShapeDtypeStruct((batch, seq_len, w), f32)
    init = [] if s0 is None else [s0]
    o_fw, o_bw, s_fin = pl.pallas_call(
        functools.partial(_scan_kernel, has_init=s0 is not None),
        grid=(batch // SCAN_SEQS, nc),
        in_specs=[blk(fwd)] * 6 + [blk(bwd)] * 4 + [blk(bwd_dir)] * 2
                 + [pl.BlockSpec((1, w), lambda b, c: (0, 0))] + [state_spec] * len(init),
        out_specs=[blk(fwd), blk(bwd),
                   pl.BlockSpec((SCAN_SEQS, 2, RWKV_HEADS, RWKV_HEAD, RWKV_HEAD), lambda b, c: (b, 0, 0, 0, 0))],
        out_shape=[o_shape, o_shape, jax.ShapeDtypeStruct((batch, 2, RWKV_HEADS, RWKV_HEAD, RWKV_HEAD), f32)],
        scratch_shapes=[pltpu.VMEM((SCAN_SEQS, 2, N_QUADS, QUAD, QUAD), f32)],
        compiler_params=_cparams("parallel", "arbitrary"), name="rwkv_scan",
    )(r, k, v, kk, lw, a, r, k, v, kk, lw, a, k_a, *init)
    return o_fw.reshape(n, w), o_bw.reshape(n, w), s_fin


def _out_kernel(attn_ref, of_ref, ob_ref, bonus_ref, g_ref, x_ref, mod_ref, lng_ref, lnb_ref, seg_ref,
                wo_ref, gpm_ref, gpf_ref, rwt_ref, rb_ref,
                x1_ref, h2_ref, slot_ref, sw_ref, cnt_ref):
    tiles = [pl.ds(t * TM_MOE, TM_MOE) for t in range(x_ref.shape[0] // TM_MOE)]
    fronts = [_mix_front(attn_ref.at[rows], of_ref.at[rows], ob_ref.at[rows], bonus_ref.at[rows],
                         g_ref.at[rows], x_ref.at[rows], mod_ref, lng_ref, lnb_ref, seg_ref, wo_ref,
                         gpm_ref, gpf_ref) for rows in tiles]
    picks = [_router_topk(h2, rwt_ref, rb_ref) for _, h2 in fronts]
    routes = [_router_slots(*pick) for pick in picks]
    for t, (rows, (x1, h2), (slots, sw, cnt)) in enumerate(zip(tiles, fronts, routes)):
        x1_ref[rows, :] = x1
        h2_ref[rows, :] = h2.astype(bf16)
        slot_ref[:, rows] = slots
        sw_ref[rows, :] = sw
        cnt_ref[t] = cnt


def _mix_front(attn_ref, of_ref, ob_ref, bonus_ref, g_ref, x_ref, mod_ref, lng_ref, lnb_ref, seg_ref,
               wo_ref, gpm_ref, gpf_ref):
    seg = seg_ref[...]
    inv = 1.0 / RWKV_HEAD
    o = of_ref[...] + ob_ref[...]
    mu = _seg_sum(o, seg) * inv
    dl = o - mu
    var = _seg_sum(dl * dl, seg) * inv
    on = dl * lax.rsqrt(var + GN_EPS) * lng_ref[...] + lnb_ref[...]
    rw = (on + bonus_ref[...]) * g_ref[...]
    half = ATTN_HEADS * HEAD_DIM
    y = _dot(attn_ref[...].astype(bf16), wo_ref[0:half, :]) + _dot(rw.astype(bf16), wo_ref[half:, :])
    m = mod_ref[...]
    gt_a = m[:, 2 * D_MODEL:3 * D_MODEL]
    sh_f = m[:, 3 * D_MODEL:4 * D_MODEL]
    sc_f = m[:, 4 * D_MODEL:5 * D_MODEL]
    x1 = x_ref[...] + gt_a * _rms(y, gpm_ref[...])
    h2 = _rms(x1, gpf_ref[...]) * (1.0 + sc_f) + sh_f
    return x1, h2


def _router_topk(h2, rwt_ref, rb_ref):
    tm = h2.shape[0]
    logits = _mm(rwt_ref[...], h2, nt=True)
    scores = _sigmoid(logits)
    biased = scores + rb_ref[...]
    neg = -jnp.inf
    shape3 = (N_GROUPS, GROUP_SIZE, tm)
    b3 = biased.reshape(shape3)
    s3 = scores.reshape(shape3)
    in_grp = lax.broadcasted_iota(jnp.int32, shape3, 1)
    grp = lax.broadcasted_iota(jnp.int32, shape3, 0)
    m1 = jnp.max(b3, axis=1, keepdims=True)
    i1 = jnp.min(jnp.where(b3 == m1, in_grp, GROUP_SIZE), axis=1, keepdims=True)
    m2 = jnp.max(jnp.where(in_grp == i1, neg, b3), axis=1, keepdims=True)
    gscore = m1 + m2
    gidx = lax.broadcasted_iota(jnp.int32, (N_GROUPS, 1, tm), 0)
    gsel = jnp.zeros((N_GROUPS, 1, tm), f32)
    cur = gscore
    for _ in range(TOPK_GROUPS):
        mx = jnp.max(cur, axis=0, keepdims=True)
        ii = jnp.min(jnp.where(cur == mx, gidx, N_GROUPS), axis=0, keepdims=True)
        hit = gidx == ii
        gsel = jnp.where(hit, 1.0, gsel)
        cur = jnp.where(hit, neg, cur)
    cand = jnp.where(jnp.broadcast_to(gsel, shape3) > 0.0, b3, neg)
    eidx = grp * GROUP_SIZE + in_grp
    wsel = jnp.zeros(shape3, f32)
    mask = jnp.zeros(shape3, f32)
    hits = []
    for _ in range(TOP_K):
        mx = jnp.max(jnp.max(cand, axis=1, keepdims=True), axis=0, keepdims=True)
        ii = jnp.min(jnp.min(jnp.where(cand == mx, eidx, N_EXPERTS), axis=1, keepdims=True),
                     axis=0, keepdims=True)
        hit = eidx == ii
        hits.append(hit)
        wsel = jnp.where(hit, s3, wsel)
        mask = jnp.where(hit, 1.0, mask)
        cand = jnp.where(hit, neg, cand)
    den = jnp.sum(jnp.sum(wsel, axis=1, keepdims=True), axis=0, keepdims=True)
    gates3 = wsel / den * ROUTED_SCALE
    return hits, gates3, mask


def _router_slots(hits, gates3, mask):
    tm = mask.shape[-1]
    shape3 = mask.shape
    mask2 = mask.reshape(N_EXPERTS, tm)
    cnt = jnp.sum(mask2, axis=1, keepdims=True)
    cnt_pad = jnp.floor((cnt + (SEG - 1)) * (1.0 / SEG)) * SEG
    er = lax.broadcasted_iota(jnp.int32, (N_EXPERTS, N_EXPERTS), 0)
    ec = lax.broadcasted_iota(jnp.int32, (N_EXPERTS, N_EXPERTS), 1)
    before = jnp.where(ec < er, 1.0, 0.0).astype(bf16)
    cnt_pad_l = jnp.broadcast_to(cnt_pad, (N_EXPERTS, LANE))
    seg_start = _dot(before, cnt_pad_l.astype(bf16))
    tr = lax.broadcasted_iota(jnp.int32, (tm, tm), 0)
    tc = lax.broadcasted_iota(jnp.int32, (tm, tm), 1)
    earlier = jnp.where(tr < tc, 1.0, 0.0).astype(bf16)
    rank = _dot(mask2.astype(bf16), earlier)
    slot3 = (rank + seg_start[:, 0:1]).reshape(shape3)

    def pick(hit, val3):
        return jnp.sum(jnp.sum(jnp.where(hit, val3, 0.0), axis=1, keepdims=True), axis=0)

    slots = [pick(h, slot3) for h in hits]
    wts = [pick(h, gates3) for h in hits]
    pad_rows = lambda k, val: [jnp.full((k, tm), val, f32)]
    slot_rows = jnp.concatenate(slots + pad_rows(8 - TOP_K, -1.0), axis=0).astype(jnp.int32)
    sw_t = jnp.concatenate(slots + pad_rows(8 - TOP_K, -1.0) + wts + pad_rows(LANE - 8 - TOP_K, 0.0), axis=0)
    return slot_rows, sw_t.T, cnt_pad_l


def _mix_out(attn, o_fw, o_bw, bonus, g, x2, mod3, mod_row, p, tm):
    n = x2.shape[0]
    w = RWKV_WIDTH
    consts = [p['lnx_g'], p['lnx_b'], p['seg'], p['w_out'], p['g_post_mix'], p['g_pre_ffn'],
              p['router_wt'], p['router_b']]
    row = lambda wd: pl.BlockSpec((tm, wd), lambda i: (i, 0))
    in_specs = [row(ATTN_HEADS * HEAD_DIM), row(w), row(w),
                row(w), row(w), row(D_MODEL),
                pl.BlockSpec((None, 1, 6 * D_MODEL), lambda i: (mod_row(i, tm), 0, 0))]
    in_specs += [_full_spec(c) for c in consts]
    out_shape = [jax.ShapeDtypeStruct((n, D_MODEL), f32),
                 jax.ShapeDtypeStruct((n, D_MODEL), bf16),
                 jax.ShapeDtypeStruct((8, n), jnp.int32),
                 jax.ShapeDtypeStruct((n, LANE), f32),
                 jax.ShapeDtypeStruct((n // TM_MOE, N_EXPERTS, LANE), f32)]
    out_specs = [row(D_MODEL), row(D_MODEL), pl.BlockSpec((8, tm), lambda i: (0, i)), row(LANE),
                 pl.BlockSpec((tm // TM_MOE, N_EXPERTS, LANE), lambda i: (i, 0, 0))]
    return pl.pallas_call(
        _out_kernel, grid=(n // tm,), in_specs=in_specs, out_specs=out_specs, out_shape=out_shape,
        compiler_params=_cparams("parallel"), name="mix_out",
    )(attn, o_fw, o_bw, bonus, g, x2, mod3, *consts)


def _repeat(lo, hi, body):
    def step(idx, carry):
        body(idx)
        return carry

    lax.fori_loop(lo, hi, step, 0)


def _chunk_groups(tile, nch_sm):
    return lax.shift_right_logical(nch_sm[tile] + (CHUNK_GROUP - 1), CHUNK_GROUP_SHIFT)


def _start_chunks(tile, dst_sm, nch_sm, make_copy):
    def group(g):
        for u in range(CHUNK_GROUP):
            c = g * CHUNK_GROUP + u
            make_copy(c, pl.multiple_of(dst_sm[tile * MAX_CHUNKS + c], SEG)).start()

    _repeat(0, _chunk_groups(tile, nch_sm), group)


def _wait_chunks(tile, nch_sm, make_copy):
    def group(g):
        for _ in range(CHUNK_GROUP):
            make_copy(0, 0).wait()

    _repeat(0, _chunk_groups(tile, nch_sm), group)


def _dispatch_kernel(dst_sm, nch_sm, fill_sm, *refs, tile_starts):
    n_paths = len(tile_starts)
    h_refs, slot_refs = refs[:n_paths], refs[n_paths:2 * n_paths]
    xs_hbm, xc_ref, zero_ref, sems, zsem = refs[2 * n_paths:]
    i = pl.program_id(0)
    last = pl.num_programs(0) - 1
    buf = lax.rem(i, 2)

    def copy_from(b):
        def make(chunk, global_row):
            src = xc_ref.at[b, pl.ds(pl.multiple_of(chunk * SEG, SEG), SEG)]
            return pltpu.make_async_copy(src, xs_hbm.at[pl.ds(global_row, SEG)], sems.at[b])
        return make

    @pl.when(i == 0)
    def _():
        zero_ref[...] = jnp.zeros_like(zero_ref)
        n_tiles = xs_hbm.shape[0] // FFN_ROWS

        def tail_copy(row):
            return pltpu.make_async_copy(zero_ref.at[pl.ds(0, SEG)], xs_hbm.at[pl.ds(row, SEG)], zsem)

        def tile_copy(t):
            return pltpu.make_async_copy(zero_ref, xs_hbm.at[pl.ds(pl.multiple_of(t * FFN_ROWS, FFN_ROWS), FFN_ROWS)],
                                         zsem)

        def per_expert(e):
            start = fill_sm[1 + e]
            chunks = lax.shift_right_logical(fill_sm[1 + N_EXPERTS + e], SEG_SHIFT)
            _repeat(0, chunks, lambda c: tail_copy(pl.multiple_of(start + c * SEG, SEG)).start())
            _repeat(0, chunks, lambda c: tail_copy(0).wait())

        _repeat(0, N_EXPERTS, per_expert)
        tile_copy(n_tiles - 1).start()
        tile_copy(0).wait()
        _repeat(fill_sm[0], n_tiles - 1, lambda t: tile_copy(t).start())

    @pl.when(i >= 2)
    def _():
        _wait_chunks(i - 2, nch_sm, copy_from(buf))

    slot, h = slot_refs[0][...], h_refs[0][...]
    for q in range(1, n_paths):
        mine = i >= tile_starts[q]
        slot = jnp.where(mine, slot_refs[q][...], slot)
        h = jnp.where(mine, h_refs[q][...], h)
    used = nch_sm[i] * SEG

    def build(rb):
        rows = lax.broadcasted_iota(jnp.int32, (SEL_BLOCK, TM_MOE), 0) + rb * SEL_BLOCK
        sel = jnp.zeros((SEL_BLOCK, TM_MOE), f32)
        for j in range(TOP_K):
            sel = jnp.where(rows == slot[j:j + 1, :], 1.0, sel)
        xc_ref[buf, rb * SEL_BLOCK:(rb + 1) * SEL_BLOCK, :] = _dot(sel.astype(bf16), h).astype(bf16)

    for rb in range(COMMON_BLOCKS):
        build(rb)
    for rb in range(COMMON_BLOCKS, SLOT_ROWS // SEL_BLOCK):
        pl.when(rb * SEL_BLOCK < used)(functools.partial(build, rb))

    _start_chunks(i, dst_sm, nch_sm, copy_from(buf))

    @pl.when(i == last)
    def _():
        @pl.when(i >= 1)
        def _():
            _wait_chunks(i - 1, nch_sm, copy_from(1 - buf))
        _wait_chunks(i, nch_sm, copy_from(buf))
        n_tiles = xs_hbm.shape[0] // FFN_ROWS
        whole_tile = pltpu.make_async_copy(zero_ref, xs_hbm.at[pl.ds(0, FFN_ROWS)], zsem)
        _repeat(fill_sm[0], n_tiles - 1, lambda t: whole_tile.wait())


def _dispatch(h2s, slots, dst, nch, fill, total_rows):
    tiles = [h.shape[0] // TM_MOE for h in h2s]
    starts = [sum(tiles[:q]) for q in range(len(tiles))]

    def in_path(q):
        return lambda i: jnp.clip(i - starts[q], 0, tiles[q] - 1)

    h_specs = [pl.BlockSpec((TM_MOE, D_MODEL), lambda i, d, c, f, q=q: (in_path(q)(i), 0)) for q in range(len(tiles))]
    s_specs = [pl.BlockSpec((8, TM_MOE), lambda i, d, c, f, q=q: (0, in_path(q)(i))) for q in range(len(tiles))]
    grid_spec = pltpu.PrefetchScalarGridSpec(
        num_scalar_prefetch=3, grid=(sum(tiles),),
        in_specs=h_specs + s_specs,
        out_specs=pl.BlockSpec(memory_space=pl.ANY),
        scratch_shapes=[pltpu.VMEM((2, SLOT_ROWS, D_MODEL), bf16), pltpu.VMEM((FFN_ROWS, D_MODEL), bf16),
                        pltpu.SemaphoreType.DMA((2,)), pltpu.SemaphoreType.DMA(())])
    return pl.pallas_call(
        functools.partial(_dispatch_kernel, tile_starts=tuple(starts)), grid_spec=grid_spec,
        out_shape=jax.ShapeDtypeStruct((total_rows, D_MODEL), bf16),
        compiler_params=_cparams("arbitrary"), name="moe_dispatch",
    )(dst, nch, fill, *h2s, *slots)


def _ffn_kernel(te_sm, nv_sm, xs_ref, wg_ref, wu_ref, wd_ref, ys_ref, wg_b, wu_b, wd_b):
    r = pl.program_id(0)
    valid = r < nv_sm[0]

    @pl.when(jnp.logical_and(valid, jnp.logical_or(r == 0, te_sm[r] != te_sm[jnp.maximum(r - 1, 0)])))
    def _():
        wg_b[...] = wg_ref[...].astype(bf16)
        wu_b[...] = wu_ref[...].astype(bf16)
        wd_b[...] = wd_ref[...].astype(bf16)

    @pl.when(valid)
    def _():
        x = xs_ref[...]
        gg = _dot(x, wg_b[...])
        uu = _dot(x, wu_b[...])
        hm = gg * _sigmoid(gg) * uu
        ys_ref[...] = _dot(hm.astype(bf16), wd_b[...]).astype(bf16)

    @pl.when(r >= nv_sm[0])
    def _():
        ys_ref[...] = jnp.zeros_like(ys_ref)


def _expert_ffn(xs, tile_expert, n_valid, wg, wu, wd):
    total_rows = xs.shape[0]
    last = lambda r, nv: jnp.minimum(r, nv[0] - 1)
    grid_spec = pltpu.PrefetchScalarGridSpec(
        num_scalar_prefetch=2, grid=(total_rows // FFN_ROWS,),
        in_specs=[pl.BlockSpec((FFN_ROWS, D_MODEL), lambda r, te, nv: (last(r, nv), 0)),
                  pl.BlockSpec((None, D_MODEL, EXPERT_FF), lambda r, te, nv: (te[last(r, nv)], 0, 0)),
                  pl.BlockSpec((None, D_MODEL, EXPERT_FF), lambda r, te, nv: (te[last(r, nv)], 0, 0)),
                  pl.BlockSpec((None, EXPERT_FF, D_MODEL), lambda r, te, nv: (te[last(r, nv)], 0, 0))],
        out_specs=pl.BlockSpec((FFN_ROWS, D_MODEL), lambda r, te, nv: (r, 0)),
        scratch_shapes=[pltpu.VMEM((D_MODEL, EXPERT_FF), bf16), pltpu.VMEM((D_MODEL, EXPERT_FF), bf16),
                        pltpu.VMEM((EXPERT_FF, D_MODEL), bf16)])
    return pl.pallas_call(
        _ffn_kernel, grid_spec=grid_spec,
        out_shape=jax.ShapeDtypeStruct((total_rows, D_MODEL), bf16),
        compiler_params=_cparams("arbitrary"), name="moe_ffn",
    )(tile_expert, n_valid, xs, wg, wu, wd)


def _combine_kernel(dst_sm, nch_sm, sw_ref, h_ref, x1_ref, mod_ref, gpost_ref, swg_ref, swu_ref, swd_ref,
                    ys_hbm, y_ref, yc_ref, acc_ref, sems, *, tile_offset):
    step = pl.program_id(0)
    i = step + tile_offset
    buf = lax.rem(step, 2)

    def copy_into(b):
        def make(chunk, global_row):
            dst = yc_ref.at[b, pl.ds(pl.multiple_of(chunk * SEG, SEG), SEG)]
            return pltpu.make_async_copy(ys_hbm.at[pl.ds(global_row, SEG)], dst, sems.at[b])
        return make

    @pl.when(step == 0)
    def _():
        yc_ref[...] = jnp.zeros_like(yc_ref)
        _start_chunks(i, dst_sm, nch_sm, copy_into(buf))

    @pl.when(step + 1 < pl.num_programs(0))
    def _():
        _start_chunks(i + 1, dst_sm, nch_sm, copy_into(1 - buf))

    h = h_ref[...]
    gg = _dot(h, swg_ref[...])
    uu = _dot(h, swu_ref[...])
    acc_ref[...] = _dot((gg * _sigmoid(gg) * uu).astype(bf16), swd_ref[...])
    _wait_chunks(i, nch_sm, copy_into(buf))
    sw = sw_ref[...]
    used = nch_sm[i] * SEG
    def weighted(rb):
        cols = (lax.broadcasted_iota(jnp.int32, (TM_MOE, SEL_BLOCK), 1) + rb * SEL_BLOCK).astype(f32)
        wmat = jnp.zeros((TM_MOE, SEL_BLOCK), f32)
        for j in range(TOP_K):
            wmat = jnp.where(cols == sw[:, j:j + 1], sw[:, 8 + j:9 + j], wmat)
        rows = lax.broadcasted_iota(jnp.int32, (SEL_BLOCK, 1), 0) + rb * SEL_BLOCK
        yc = jnp.where(rows < used, yc_ref[buf, rb * SEL_BLOCK:(rb + 1) * SEL_BLOCK, :], jnp.zeros((), bf16))
        return _dot(wmat.astype(bf16), yc)

    moe = acc_ref[...]
    for rb in range(COMMON_BLOCKS):
        moe = moe + weighted(rb)
    acc_ref[...] = moe
    for rb in range(COMMON_BLOCKS, SLOT_ROWS // SEL_BLOCK):
        @pl.when(rb * SEL_BLOCK < used)
        def _():
            acc_ref[...] += weighted(rb)
    gt_f = mod_ref[...][:, 5 * D_MODEL:6 * D_MODEL]
    y_ref[...] = x1_ref[...] + gt_f * _rms(acc_ref[...], gpost_ref[...])


def _combine(ys, dst, nch, sw, h2, x1, mod3, mod_row, g_post, swg, swu, swd, tile_offset):
    n = h2.shape[0]
    row = lambda wd: pl.BlockSpec((TM_MOE, wd), lambda i, d, c: (i, 0))
    const = lambda a: pl.BlockSpec(a.shape, lambda i, d, c: (0,) * a.ndim)
    grid_spec = pltpu.PrefetchScalarGridSpec(
        num_scalar_prefetch=2, grid=(n // TM_MOE,),
        in_specs=[row(LANE), row(D_MODEL), row(D_MODEL),
                  pl.BlockSpec((None, 1, 6 * D_MODEL), lambda i, d, c: (mod_row(i, TM_MOE), 0, 0)),
                  const(g_post), const(swg), const(swu), const(swd),
                  pl.BlockSpec(memory_space=pl.ANY)],
        out_specs=row(D_MODEL),
        scratch_shapes=[pltpu.VMEM((2, SLOT_ROWS, D_MODEL), bf16), pltpu.VMEM((TM_MOE, D_MODEL), f32),
                        pltpu.SemaphoreType.DMA((2,))])
    return pl.pallas_call(
        functools.partial(_combine_kernel, tile_offset=tile_offset), grid_spec=grid_spec,
        out_shape=jax.ShapeDtypeStruct((n, D_MODEL), f32),
        compiler_params=_cparams("arbitrary"), name="moe_combine",
    )(dst, nch, sw, h2, x1, mod3, g_post, swg, swu, swd, ys)


def _sparse_moe(paths, mod3, p):
    cnt = jnp.concatenate([q['cnt'][:, :, 0] for q in paths], axis=0).astype(jnp.int32)
    n_tiles = cnt.shape[0]
    before = jnp.cumsum(cnt, axis=0) - cnt
    rows_e = jnp.sum(cnt, axis=0)
    region = (rows_e + FFN_ROWS - 1) // FFN_ROWS * FFN_ROWS
    region_end = jnp.cumsum(region)
    region_start = region_end - region
    worst = n_tiles * (TOP_K * TM_MOE + N_EXPERTS * (SEG - 1)) + N_EXPERTS * (FFN_ROWS - SEG)
    total_rows = -(-worst // FFN_ROWS) * FFN_ROWS + FFN_ROWS
    spare_row = total_rows - FFN_ROWS
    first_row = jnp.arange(total_rows // FFN_ROWS, dtype=jnp.int32) * FFN_ROWS
    ended = jnp.sum((region_end[None, :] <= first_row[:, None]).astype(jnp.int32), axis=1)
    tile_expert = jnp.minimum(ended, N_EXPERTS - 1).astype(jnp.int32)
    n_valid = (region_end[-1:] // FFN_ROWS).astype(jnp.int32)

    seg_end = jnp.cumsum(cnt, axis=1)
    seg_start = seg_end - cnt
    nch = (seg_end[:, -1] // SEG).astype(jnp.int32)
    chunk_row = jnp.arange(MAX_CHUNKS, dtype=jnp.int32) * SEG
    chunk_e = jnp.sum((seg_end[:, None, :] <= chunk_row[None, :, None]).astype(jnp.int32), axis=2)
    own = chunk_e[:, :, None] == jnp.arange(N_EXPERTS, dtype=jnp.int32)[None, None, :]
    shift = (region_start[None, :] + before - seg_start)[:, None, :]
    dst = jnp.sum(jnp.where(own, shift, 0), axis=2) + chunk_row[None, :]
    chunk_id = jnp.arange(MAX_CHUNKS, dtype=jnp.int32)[None, :]
    tile_id = jnp.arange(n_tiles, dtype=jnp.int32)[:, None]
    spare = spare_row + ((tile_id % 2) * CHUNK_GROUP + chunk_id % CHUNK_GROUP) * SEG
    dst = jnp.where(chunk_id < nch[:, None], dst, spare).astype(jnp.int32).reshape(-1)

    fill = jnp.concatenate([n_valid, region_start + rows_e, region - rows_e]).astype(jnp.int32)
    xs = _dispatch([q['h2'] for q in paths], [q['slots'] for q in paths], dst, nch, fill, total_rows)
    ys = _expert_ffn(xs, tile_expert, n_valid, p['expert_wg'], p['expert_wu'], p['expert_wd'])
    outs, tile_offset = [], 0
    for q in paths:
        outs.append(_combine(ys, dst, nch, q['sw'], q['h2'], q['x1'], mod3, q['mod_row'],
                             p['g_post_ffn'], p['shared_wg'], p['shared_wu'], p['shared_wd'], tile_offset))
        tile_offset += q['h2'].shape[0] // TM_MOE
    return outs


def _pad_heads(w, heads):
    rows = w.shape[0]
    w = w.reshape(rows, heads, HEAD_DIM)
    return jnp.pad(w, ((0, 0), (0, 0), (0, HEAD_PAD - HEAD_DIM))).reshape(rows, heads * HEAD_PAD)


def _block_diag2(m):
    z = jnp.zeros_like(m[0])
    return jnp.concatenate([jnp.concatenate([m[0], z], axis=1), jnp.concatenate([z, m[1]], axis=1)], axis=0)


def _prepare_params(w):
    aw, kw = ATTN_HEADS * HEAD_DIM, KV_HEADS * HEAD_DIM
    w_in = w['w_in']
    lane_row = lambda v: v.reshape(1, -1).astype(f32)
    head_id = np.arange(RWKV_WIDTH) // RWKV_HEAD
    seg = jnp.asarray(head_id[:, None] == head_id[None, :], dtype=bf16)
    gain_pad = lambda gvec: jnp.pad(gvec, (0, HEAD_PAD - HEAD_DIM)).reshape(1, HEAD_PAD)
    return dict(
        g_pre_mix=lane_row(w['g_pre_mix']), g_post_mix=lane_row(w['g_post_mix']),
        g_pre_ffn=lane_row(w['g_pre_ffn']), g_post_ffn=lane_row(w['g_post_ffn']),
        wq=_pad_heads(w_in[:, :aw], ATTN_HEADS).astype(bf16),
        wk=_pad_heads(w_in[:, aw:aw + kw], KV_HEADS).astype(bf16),
        wv=w_in[:, aw + kw:aw + 2 * kw].astype(bf16),
        wr=w_in[:, aw + 2 * kw:].astype(bf16),
        qg=gain_pad(w['q_gain'] * ATTN_SCALE), kg=gain_pad(w['k_gain']),
        mu_prev=lane_row(w['mu_prev']), mu_next=lane_row(w['mu_next']),
        w0=lane_row(w['decay_w0']), w2bd=_block_diag2(w['decay_w2']),
        a0=lane_row(w['iclr_a0']), a2bd=_block_diag2(w['iclr_a2']),
        g2=w['gate_g2'], k_k=lane_row(w['k_k']), k_a=lane_row(w['k_a']), r_k=lane_row(w['r_k']),
        lnx_g=lane_row(w['lnx_g']), lnx_b=lane_row(w['lnx_b']), seg=seg,
        w_out=w['w_out'].astype(bf16),
        router_wt=w['router_w'].T, router_b=w['router_b'].reshape(N_EXPERTS, 1),
        expert_wg=w['expert_wg'], expert_wu=w['expert_wu'], expert_wd=w['expert_wd'],
        shared_wg=w['shared_wg'].astype(bf16), shared_wu=w['shared_wu'].astype(bf16),
        shared_wd=w['shared_wd'].astype(bf16),
    )


def _rope_tables(seq_len):
    rows = seq_len // GRID_W
    row_idx = jnp.repeat(jnp.arange(rows, dtype=jnp.int32), GRID_W)
    col_idx = jnp.tile(jnp.arange(GRID_W, dtype=jnp.int32), rows)
    inv = ROPE_THETA ** (-jnp.arange(0, ROPE_HALF, 2, dtype=f32) / ROPE_HALF)
    ang = jnp.stack([row_idx.astype(f32)[:, None] * inv, col_idx.astype(f32)[:, None] * inv], axis=1)
    cos, sin = jnp.cos(ang), jnp.sin(ang)
    cos_l = jnp.stack([cos, cos], axis=2).reshape(seq_len, HEAD_DIM)
    sin_l = jnp.stack([-sin, sin], axis=2).reshape(seq_len, HEAD_DIM)
    pad = ((0, 0), (0, HEAD_PAD - HEAD_DIM))
    return jnp.pad(cos_l, pad), jnp.pad(sin_l, pad)


def _states_to_pairs(s):
    b = s.shape[0]
    s = s.reshape(b, 2, N_QUADS, QUAD_HEADS, RWKV_HEAD, RWKV_HEAD)
    eye = jnp.eye(QUAD_HEADS, dtype=s.dtype)
    return jnp.einsum('bdphvk,hg->bdphkgv', s, eye).reshape(b, 2, N_QUADS, QUAD, QUAD)


def _trunk(x, mod3, mod_row, p, rope_tabs, past_k, past_v, s0_pairs, tm_in, tq, tm_prep):
    b, t, _ = x.shape
    n = b * t
    x2 = x.reshape(n, D_MODEL)
    q_pad, k_own, v_own, k_pad, zr = _in_proj(
        x2, mod3, mod_row, p['g_pre_mix'], p['wq'], p['wk'], p['wv'], p['wr'], p['qg'], p['kg'],
        rope_tabs, t, tm_in)
    k_full = k_pad.reshape(b, t, KV_HEADS * HEAD_PAD)
    v_full = v_own.reshape(b, t, KV_HEADS * HEAD_DIM)
    if past_k is not None:
        k_past = _pad_heads(past_k.reshape(-1, KV_HEADS * HEAD_DIM), KV_HEADS).astype(bf16)
        k_full = jnp.concatenate([k_past.reshape(b, -1, KV_HEADS * HEAD_PAD), k_full], axis=1)
        v_full = jnp.concatenate([past_v.reshape(b, -1, KV_HEADS * HEAD_DIM), v_full], axis=1)
    vt_full = jnp.swapaxes(v_full, 1, 2).astype(bf16)
    attn = _attention(q_pad, k_full, vt_full, b, t, tq)
    r, k, v, kk, lw, a, g, bonus = _rwkv_prep(zr, p, t, tm_prep)
    o_fw, o_bw, s_fin = _rwkv_scan(r, k, v, kk, lw, a, p['k_a'], s0_pairs, b, t)
    x1, h2, slots, sw, cnt = _mix_out(attn, o_fw, o_bw, bonus, g, x2, mod3, mod_row, p, MIX_TILES * TM_MOE)
    return dict(h2=h2, slots=slots, sw=sw, cnt=cnt, x1=x1, mod_row=mod_row), k_own, v_own, s_fin


def kernel(x_prompt, x_sample, c, cache_k, cache_v, state_rwkv, c_ctx, w_ada, b_ada, g_pre_mix, g_post_mix,
           g_pre_ffn, g_post_ffn, w_in, w_out, q_gain, k_gain, mu_prev, mu_next, decay_w0, decay_w2, iclr_a0,
           iclr_a2, gate_g2, k_k, k_a, r_k, lnx_g, lnx_b, router_w, router_b, expert_wg, expert_wu, expert_wd,
           shared_wg, shared_wu, shared_wd):
    layer = 0
    names = ('g_pre_mix g_post_mix g_pre_ffn g_post_ffn w_in w_out q_gain k_gain mu_prev mu_next decay_w0 '
             'decay_w2 iclr_a0 iclr_a2 gate_g2 k_k k_a r_k lnx_g lnx_b router_w router_b expert_wg expert_wu '
             'expert_wd shared_wg shared_wu shared_wd').split()
    vals = (g_pre_mix, g_post_mix, g_pre_ffn, g_post_ffn, w_in, w_out, q_gain, k_gain, mu_prev, mu_next,
            decay_w0, decay_w2, iclr_a0, iclr_a2, gate_g2, k_k, k_a, r_k, lnx_g, lnx_b, router_w, router_b,
            expert_wg, expert_wu, expert_wd, shared_wg, shared_wu, shared_wd)
    p = _prepare_params({nm: v[layer] for nm, v in zip(names, vals)})

    nb, ts, _ = x_sample.shape
    npb, tp, _ = x_prompt.shape
    assert ts % (MIX_TILES * TM_MOE) == 0 and ts % LATENT_TILES.in_rows == 0 and ts % GRID_W == 0
    assert ts % LATENT_TILES.attn_queries == 0 and ts % LATENT_TILES.prep_rows == 0
    assert (npb * tp) % CONTEXT_TILES.in_rows == 0 and tp % CONTEXT_TILES.prep_rows == 0
    assert nb % SCAN_SEQS == 0 and npb % SCAN_SEQS == 0 and (npb * tp) % (MIX_TILES * TM_MOE) == 0
    mod_rows = 16
    assert nb < mod_rows
    cc = jnp.zeros((mod_rows, D_MODEL), f32).at[0].set(c_ctx).at[1:1 + nb].set(c)
    mod3 = _modulation(cc, w_ada[layer], b_ada[layer]).reshape(mod_rows, 1, 6 * D_MODEL)

    moe_p, kc, vc, st = _trunk(x_prompt, mod3, lambda i, tm: 0, p, None, None, None, None,
                               tm_in=CONTEXT_TILES.in_rows, tq=tp, tm_prep=CONTEXT_TILES.prep_rows)
    s0_lat = _states_to_pairs(state_rwkv[:, layer])
    moe_s, _, _, _ = _trunk(x_sample, mod3, lambda i, tm: 1 + (i * tm) // ts, p, _rope_tables(ts),
                            cache_k[:, layer], cache_v[:, layer], s0_lat,
                            tm_in=LATENT_TILES.in_rows, tq=LATENT_TILES.attn_queries,
                            tm_prep=LATENT_TILES.prep_rows)
    y_p, y_s = _sparse_moe([moe_p, moe_s], mod3, p)
    y_p = y_p.reshape(x_prompt.shape)
    y_s = y_s.reshape(x_sample.shape)

    new_cache_k = kc.reshape(npb, 1, tp, KV_HEADS, HEAD_DIM)
    new_cache_v = vc.reshape(npb, 1, tp, KV_HEADS, HEAD_DIM)
    new_state = st[:, None]
    return (y_p, y_s, new_cache_k, new_cache_v, new_state)
```

```python
import functools
from typing import NamedTuple

import numpy as np
import jax
import jax.numpy as jnp
from jax import lax
from jax.experimental import pallas as pl
from jax.experimental.pallas import tpu as pltpu

f32 = jnp.float32
bf16 = jnp.bfloat16

D_MODEL = 1024
GRID_W = 64
HEAD_DIM = 64
ATTN_HEADS = 8
KV_HEADS = 2
GQA_GROUP = ATTN_HEADS // KV_HEADS
ATTN_SCALE = HEAD_DIM ** -0.5
ROPE_THETA = 10000.0
ROPE_HALF = HEAD_DIM // 2
ROPE_FREQS = ROPE_HALF // 2
RWKV_HEAD = 64
RWKV_HEADS = 8
RWKV_WIDTH = RWKV_HEADS * RWKV_HEAD
DECAY_RANK = 64
ICLR_RANK = 64
GATE_RANK = 128
RWKV_IN = 3 * RWKV_WIDTH + 2 * DECAY_RANK + 2 * ICLR_RANK + GATE_RANK
DECAY_SCALE = 0.606531
GN_EPS = 64e-5
N_EXPERTS = 64
TOP_K = 6
N_GROUPS = 8
GROUP_SIZE = N_EXPERTS // N_GROUPS
TOPK_GROUPS = 4
EXPERT_FF = 256
ROUTED_SCALE = 2.5
EPS = 1e-6

LANE = 128
HEAD_PAD = LANE
CHUNK = 64
QUAD_HEADS = 4
QUAD = QUAD_HEADS * RWKV_HEAD
N_QUADS = RWKV_HEADS // QUAD_HEADS
SCAN_SEQS = 4
VMEM_LIMIT = 56 * 1024 * 1024


class PathTiles(NamedTuple):
    in_rows: int
    attn_queries: int
    prep_rows: int


CONTEXT_TILES = PathTiles(in_rows=512, attn_queries=0, prep_rows=256)
LATENT_TILES = PathTiles(in_rows=512, attn_queries=256, prep_rows=256)
TM_MOE = 256
MIX_TILES = 4
IN_SUB = 256
SEG_SHIFT = 4
SEG = 1 << SEG_SHIFT
SLOT_DATA_ROWS = 2560
CHUNK_GROUP_SHIFT = 3
CHUNK_GROUP = 1 << CHUNK_GROUP_SHIFT
COPY_ROWS = (2 * SEG, SEG)
MAX_COPIES = (SLOT_DATA_ROWS // (2 * SEG), N_EXPERTS)
TABLE_WIDTH = sum(MAX_COPIES)
TILE_COUNTS = 3
LOC_BITS = 9
PAD_LOCAL = (SLOT_DATA_ROWS, SLOT_DATA_ROWS + CHUNK_GROUP * 2 * SEG)
SLOT_ROWS = SLOT_DATA_ROWS + 512
SEL_BLOCK = 512
COMMON_BLOCKS = 4
FFN_ROWS = 1024


def _cparams(*sem):
    return pltpu.CompilerParams(dimension_semantics=sem, vmem_limit_bytes=VMEM_LIMIT)


def _dot(a, b, nt=False):
    dims = (((1,), (1,)), ((), ())) if nt else (((1,), (0,)), ((), ()))
    return lax.dot_general(a, b, dims, preferred_element_type=f32)


def _split2(x):
    hi = x.astype(bf16)
    lo = (x - hi.astype(f32)).astype(bf16)
    return hi, lo


def _split3(x):
    h1 = x.astype(bf16)
    r1 = x - h1.astype(f32)
    h2 = r1.astype(bf16)
    h3 = (r1 - h2.astype(f32)).astype(bf16)
    return h1, h2, h3


def _mm(a, b, nt=False, passes=3):
    if passes == 1:
        return _dot(a.astype(bf16), b.astype(bf16), nt)
    ah, al = _split2(a)
    bh, bl = _split2(b)
    return _dot(ah, bh, nt) + (_dot(al, bh, nt) + _dot(ah, bl, nt))


def _seg_sum(x, seg_bf16):
    return _dot(x.astype(bf16), seg_bf16)


def _rms(x, g):
    return x * lax.rsqrt(jnp.mean(x * x, axis=-1, keepdims=True) + EPS) * g


def _sigmoid(x):
    return jax.nn.sigmoid(x)


def _full_spec(a, grid_rank=1):
    zeros = (0,) * a.ndim
    if grid_rank == 1:
        return pl.BlockSpec(a.shape, lambda i: zeros)
    if grid_rank == 2:
        return pl.BlockSpec(a.shape, lambda i, j: zeros)
    return pl.BlockSpec(a.shape, lambda i, j, k: zeros)


def _mod_kernel(c_ref, w_ref, b_ref, o_ref):
    c = c_ref[...]
    s = c * _sigmoid(c)
    o_ref[...] = _mm(s, w_ref[...]) + b_ref[...]


def _modulation(cc, w_ada, b_ada):
    rows, n = cc.shape[0], w_ada.shape[1]
    tn = 512
    return pl.pallas_call(
        _mod_kernel,
        grid=(n // tn,),
        in_specs=[pl.BlockSpec((rows, D_MODEL), lambda j: (0, 0)),
                  pl.BlockSpec((D_MODEL, tn), lambda j: (0, j)),
                  pl.BlockSpec((1, tn), lambda j: (0, j))],
        out_specs=pl.BlockSpec((rows, tn), lambda j: (0, j)),
        out_shape=jax.ShapeDtypeStruct((rows, n), f32),
        compiler_params=_cparams("parallel"),
        name="modulation",
    )(cc, w_ada, b_ada.reshape(1, n))


def _in_kernel(*refs, rope):
    if rope:
        (x_ref, mod_ref, g_ref, wq_ref, wk_ref, wv_ref, wr_ref, qg_ref, kg_ref, cos_ref, sin_ref,
         q_ref, ko_ref, vo_ref, kp_ref, zr_ref) = refs
    else:
        (x_ref, mod_ref, g_ref, wq_ref, wk_ref, wv_ref, wr_ref, qg_ref, kg_ref,
         q_ref, ko_ref, vo_ref, kp_ref, zr_ref) = refs
    m = mod_ref[...]
    lane = lax.broadcasted_iota(jnp.int32, (1, LANE), 1)
    first_half = (lane & (ROPE_HALF - 1)) < ROPE_FREQS
    subs = [pl.ds(t * IN_SUB, IN_SUB) for t in range(x_ref.shape[0] // IN_SUB)]

    def front(rows):
        h = _rms(x_ref[rows, :], g_ref[...])
        return (h * (1.0 + m[:, D_MODEL:2 * D_MODEL]) + m[:, 0:D_MODEL]).astype(bf16)

    def head_norm(z, gain):
        ms = jnp.sum(z * z, axis=-1, keepdims=True) * (1.0 / HEAD_DIM)
        return z * lax.rsqrt(ms + EPS) * gain

    def rotate(y, rows):
        if not rope:
            return y
        partner = jnp.where(first_half, pltpu.roll(y, LANE - ROPE_FREQS, 1), pltpu.roll(y, ROPE_FREQS, 1))
        return y * cos_ref[rows, :] + partner * sin_ref[rows, :]

    def epilogue(rows, zq, zk):
        q = jnp.concatenate(
            [rotate(head_norm(zq[:, hh * HEAD_PAD:(hh + 1) * HEAD_PAD], qg_ref[...]), rows).astype(bf16)
             for hh in range(ATTN_HEADS)], axis=1)
        kn = [head_norm(zk[:, hh * HEAD_PAD:(hh + 1) * HEAD_PAD], kg_ref[...]) for hh in range(KV_HEADS)]
        k_own = kn[0] + pltpu.roll(kn[1], HEAD_DIM, 1)
        k_rot = jnp.concatenate([rotate(kn[hh], rows).astype(bf16) for hh in range(KV_HEADS)], axis=1)
        return q, k_own, k_rot

    hbs = [front(rows) for rows in subs]
    zs = [(_dot(hb, wr_ref[...]), _dot(hb, wq_ref[...]), _dot(hb, wk_ref[...]), _dot(hb, wv_ref[...])) for hb in hbs]
    outs = [epilogue(rows, zq, zk) for rows, (_, zq, zk, _) in zip(subs, zs)]
    for rows, (zr, _, _, zv), (q, k_own, k_rot) in zip(subs, zs, outs):
        zr_ref[rows, :] = zr
        q_ref[rows, :] = q
        ko_ref[rows, :] = k_own
        vo_ref[rows, :] = zv
        kp_ref[rows, :] = k_rot


def _in_proj(x2, mod3, mod_row, g_pre, wq, wk, wv, wr, qg, kg, rope_tabs, seq_len, tm):
    n = x2.shape[0]
    rope = rope_tabs is not None
    in_specs = [pl.BlockSpec((tm, D_MODEL), lambda i: (i, 0)),
                pl.BlockSpec((None, 1, 6 * D_MODEL), lambda i: (mod_row(i, tm), 0, 0)),
                _full_spec(g_pre), _full_spec(wq), _full_spec(wk), _full_spec(wv), _full_spec(wr),
                _full_spec(qg), _full_spec(kg)]
    args = [x2, mod3, g_pre, wq, wk, wv, wr, qg, kg]
    if rope:
        blocks_per_seq = seq_len // tm
        in_specs += [pl.BlockSpec((tm, LANE), lambda i: (i % blocks_per_seq, 0))] * 2
        args += list(rope_tabs)
    out_shape = [jax.ShapeDtypeStruct((n, ATTN_HEADS * HEAD_PAD), bf16),
                 jax.ShapeDtypeStruct((n, KV_HEADS * HEAD_DIM), f32),
                 jax.ShapeDtypeStruct((n, KV_HEADS * HEAD_DIM), f32),
                 jax.ShapeDtypeStruct((n, KV_HEADS * HEAD_PAD), bf16),
                 jax.ShapeDtypeStruct((n, RWKV_IN), f32)]
    out_specs = [pl.BlockSpec((tm, s.shape[1]), lambda i: (i, 0)) for s in out_shape]
    return pl.pallas_call(
        functools.partial(_in_kernel, rope=rope),
        grid=(n // tm,), in_specs=in_specs, out_specs=out_specs, out_shape=out_shape,
        compiler_params=_cparams("parallel"), name="in_proj",
    )(*args)


def _attn_kernel(q_ref, k_ref, vt_ref, o_ref):
    tq = q_ref.shape[0]
    def scores(kv):
        kh = k_ref[:, kv * HEAD_PAD:(kv + 1) * HEAD_PAD]
        q4 = jnp.concatenate(
            [q_ref[:, (GQA_GROUP * kv + g) * HEAD_PAD:(GQA_GROUP * kv + g + 1) * HEAD_PAD]
             for g in range(GQA_GROUP)], axis=0)
        return _dot(kh, q4, nt=True)

    def softmax(st):
        p = jnp.exp(st - jnp.max(st, axis=0, keepdims=True))
        return p.astype(bf16), jnp.sum(p, axis=0, keepdims=True)

    def values(kv, p, den):
        vt = vt_ref[kv * HEAD_DIM:(kv + 1) * HEAD_DIM, :]
        ot = _dot(vt, p) / den
        return [ot[:, g * tq:(g + 1) * tq] for g in range(GQA_GROUP)]

    st0 = scores(0)
    p0, den0 = softmax(st0)
    st1 = scores(1)
    heads_t = values(0, p0, den0)
    p1, den1 = softmax(st1)
    heads_t += values(1, p1, den1)
    o_ref[...] = jnp.concatenate(heads_t, axis=0).T.astype(bf16)


def _attention(q_pad, k_full, vt_full, batch, seq_len, tq):
    n = q_pad.shape[0]
    kv_len = k_full.shape[1]
    nq = seq_len // tq
    return pl.pallas_call(
        _attn_kernel,
        grid=(batch, nq),
        in_specs=[pl.BlockSpec((tq, ATTN_HEADS * HEAD_PAD), lambda b, j: (b * nq + j, 0)),
                  pl.BlockSpec((None, kv_len, KV_HEADS * HEAD_PAD), lambda b, j: (b, 0, 0)),
                  pl.BlockSpec((None, KV_HEADS * HEAD_DIM, kv_len), lambda b, j: (b, 0, 0))],
        out_specs=pl.BlockSpec((tq, ATTN_HEADS * HEAD_DIM), lambda b, j: (b * nq + j, 0)),
        out_shape=jax.ShapeDtypeStruct((n, ATTN_HEADS * HEAD_DIM), bf16),
        compiler_params=_cparams("parallel", "parallel"), name="attention",
    )(q_pad, k_full, vt_full)


def _prep_kernel(z_ref, zp_ref, zn_ref, mup_ref, mun_ref, w0_ref, w2_ref, a0_ref, a2_ref, g2_ref,
                 kkw_ref, ka_ref, rk_ref, seg_ref,
                 r_ref, k_ref, v_ref, kk_ref, lw_ref, a_ref, g_ref, bonus_ref, *, seq_len, tm):
    i = pl.program_id(0)
    z = z_ref[...]
    row = lax.broadcasted_iota(jnp.int32, (tm, 1), 0)
    seq_start = ((i * tm) % seq_len) == 0
    seq_end = (((i + 1) * tm) % seq_len) == 0
    prow = jnp.where(seq_start, 0.0, zp_ref[7:8, :])
    nrow = jnp.where(seq_end, 0.0, zn_ref[0:1, :])
    prev = jnp.where(row == 0, prow, pltpu.roll(z, 1, 0))
    nxt = jnp.where(row == tm - 1, nrow, pltpu.roll(z, tm - 1, 0))
    zs = z + mup_ref[...] * (prev - z) + mun_ref[...] * (nxt - z)
    w = RWKV_WIDTH
    r = zs[:, 0:w]
    k = zs[:, w:2 * w]
    v = zs[:, 2 * w:3 * w]
    dw = zs[:, 3 * w:3 * w + 2 * DECAY_RANK]
    da = zs[:, 3 * w + 2 * DECAY_RANK:3 * w + 2 * DECAY_RANK + 2 * ICLR_RANK]
    dg = zs[:, 3 * w + 2 * DECAY_RANK + 2 * ICLR_RANK:]
    lw = -DECAY_SCALE * _sigmoid(w0_ref[...] + _mm(jnp.tanh(dw), w2_ref[...], passes=1))
    a = _sigmoid(a0_ref[...] + _mm(da, a2_ref[...], passes=1))
    g = _mm(_sigmoid(dg), g2_ref[...], passes=1)
    kk = k * kkw_ref[...]
    seg = seg_ref[...]
    kk = kk * lax.rsqrt(_seg_sum(kk * kk, seg) + 1e-12)
    ka = ka_ref[...]
    ke_sum = k * (2.0 + (a[:, 0:w] + a[:, w:2 * w] - 2.0) * ka)
    bonus = _seg_sum(r * ke_sum * rk_ref[...], seg) * v
    r_ref[...] = r
    k_ref[...] = k
    v_ref[...] = v.astype(bf16)
    kk_ref[...] = kk
    lw_ref[...] = lw
    a_ref[...] = a
    g_ref[...] = g.astype(bf16)
    bonus_ref[...] = bonus.astype(bf16)


def _rwkv_prep(zr, p, seq_len, tm):
    n = zr.shape[0]
    w = RWKV_WIDTH
    nb8 = n // 8
    consts = [p['mu_prev'], p['mu_next'], p['w0'], p['w2bd'], p['a0'], p['a2bd'], p['g2'],
              p['k_k'], p['k_a'], p['r_k'], p['seg']]
    in_specs = [pl.BlockSpec((tm, RWKV_IN), lambda i: (i, 0)),
                pl.BlockSpec((8, RWKV_IN), lambda i: (jnp.maximum(i * (tm // 8) - 1, 0), 0)),
                pl.BlockSpec((8, RWKV_IN), lambda i: (jnp.minimum((i + 1) * (tm // 8), nb8 - 1), 0))]
    in_specs += [_full_spec(c) for c in consts]
    widths = [w, w, w, w, 2 * w, 2 * w, w, w]
    dtypes = [f32, f32, bf16, f32, f32, f32, bf16, bf16]
    out_shape = [jax.ShapeDtypeStruct((n, wd), dt) for wd, dt in zip(widths, dtypes)]
    out_specs = [pl.BlockSpec((tm, wd), lambda i: (i, 0)) for wd in widths]
    return pl.pallas_call(
        functools.partial(_prep_kernel, seq_len=seq_len, tm=tm),
        grid=(n // tm,), in_specs=in_specs, out_specs=out_specs, out_shape=out_shape,
        compiler_params=_cparams("parallel"), name="rwkv_prep",
    )(zr, zr, zr, *consts)


def _scan_kernel(rf_ref, kf_ref, vf_ref, kkf_ref, lwf_ref, af_ref,
                 rb_ref, kb_ref, vb_ref, kkb_ref, lwb_ref, ab_ref, ka_ref, *rest, has_init):
    s0_ref = rest[0] if has_init else None
    of_ref, ob_ref, sfin_ref, st_ref = rest[-4:]
    c = pl.program_id(1)

    @pl.when(c == 0)
    def _():
        st_ref[...] = s0_ref[...] if has_init else jnp.zeros_like(st_ref)

    n_seq = rf_ref.shape[0]
    row = lax.broadcasted_iota(jnp.int32, (CHUNK, CHUNK), 0)
    col = lax.broadcasted_iota(jnp.int32, (CHUNK, CHUNK), 1)
    qr = lax.broadcasted_iota(jnp.int32, (QUAD, QUAD), 0)
    qc = lax.broadcasted_iota(jnp.int32, (QUAD, QUAD), 1)
    head_mask = (qr & -RWKV_HEAD) == (qc & -RWKV_HEAD)
    head_mask_b = jnp.where(head_mask, 1.0, 0.0).astype(bf16)
    tr = lax.broadcasted_iota(jnp.int32, (CHUNK, QUAD), 0)
    tc = lax.broadcasted_iota(jnp.int32, (CHUNK, QUAD), 1) & (CHUNK - 1)
    eye = jnp.where(tr == tc, 1.0, 0.0).astype(f32)
    ka = ka_ref[...]

    def bd(x):
        xb = x.astype(bf16)
        return jnp.concatenate([xb] * QUAD_HEADS, axis=0) * head_mask_b

    def mm(a, b, nt=False):
        return _dot(a.astype(bf16), b.astype(bf16), nt)

    lane_lo = lax.broadcasted_iota(jnp.int32, (QUAD, LANE), 1) < RWKV_HEAD

    def first_stage(ch):
        ar = jnp.concatenate([ch['at'], ch['rt']], axis=0)
        ch['ar'] = ar.astype(bf16)
        bk_t = jnp.concatenate([ch['bh'], ch['kh']], axis=0).T
        swapped = pltpu.roll(bk_t, RWKV_HEAD, 1)
        b_t = jnp.where(lane_lo, bk_t, swapped).astype(bf16)
        k_t = jnp.where(lane_lo, swapped, bk_t).astype(bf16)
        w_bk = jnp.concatenate([jnp.concatenate([b_t, b_t], axis=1) * head_mask_b,
                                jnp.concatenate([k_t, k_t], axis=1) * head_mask_b], axis=1)
        amat = _dot(ch['ar'], w_bk)
        ch['n'] = jnp.where(ch['strict'], amat[0:CHUNK, 0:QUAD], 0.0)
        ch['a_kr'] = jnp.concatenate([jnp.where(ch['strict'], amat[0:CHUNK, QUAD:2 * QUAD], 0.0),
                                      jnp.where(ch['incl'], amat[CHUNK:2 * CHUNK, QUAD:2 * QUAD], 0.0)], axis=0)
        ch['a_rb'] = jnp.where(ch['incl'], amat[CHUNK:2 * CHUNK, 0:QUAD], 0.0)
        ch['tinv'] = eye + ch['n']

    chains = []
    for d, (r_ref, k_ref, v_ref, kk_ref, lw_ref, a_ref, o_ref) in enumerate(
            ((rf_ref, kf_ref, vf_ref, kkf_ref, lwf_ref, af_ref, of_ref),
             (rb_ref, kb_ref, vb_ref, kkb_ref, lwb_ref, ab_ref, ob_ref))):
        sgn = 1 - 2 * d
        tri = jnp.where((row - col) * sgn >= 0, 1.0, 0.0).astype(bf16)
        dist = (tr - tc) * sgn
        for s in range(n_seq):
            lw = lw_ref[s]
            l1, l2, l3 = _split3(lw)
            cum = _dot(tri, l1) + (_dot(tri, l2) + _dot(tri, l3))
            e_incl = jnp.exp(cum)
            e_excl = jnp.exp(cum - lw)
            e_neg = jnp.exp(-cum)
            wc = jnp.exp(jnp.sum(lw, axis=0, keepdims=True))
            kk = kk_ref[s]
            a = a_ref[s]
            bh = kk * a * e_neg
            kh = k_ref[s] * (1.0 + (a - 1.0) * ka) * e_neg
            ops = dict(at=-kk * e_excl, rt=r_ref[s] * e_incl, bh=bh, kh=kh, bw=bh * wc, kw=kh * wc,
                       v=v_ref[s], wc=wc)
            for g in range(N_QUADS):
                ch = {key: val[:, g * QUAD:(g + 1) * QUAD] for key, val in ops.items()}
                ch.update(s=s, d=d, g=g, o_ref=o_ref, strict=dist > 0, incl=dist >= 0, st=st_ref[s, d, g])
                chains.append(ch)

    for ch in chains:
        first_stage(ch)
    for ch in chains:
        ch['n'] = mm(ch['n'], bd(ch['n']))
    for _ in range(4):
        for ch in chains:
            both = mm(jnp.concatenate([ch['n'], ch['tinv']], axis=0), bd(ch['n']))
            ch['n'] = both[0:CHUNK]
            ch['tinv'] = ch['tinv'] + both[CHUNK:2 * CHUNK]
    for ch in chains:
        ch['tinv'] = ch['tinv'] + mm(ch['tinv'], bd(ch['n']))
        sa = _dot(ch['ar'], ch['st'].astype(bf16))
        av = mm(ch['a_kr'], bd(ch['v']))
        ch['rhs'] = sa[0:CHUNK] + av[0:CHUNK]
        ch['o'] = sa[CHUNK:2 * CHUNK] + av[CHUNK:2 * CHUNK]
    for ch in chains:
        ch['u'] = mm(ch['tinv'], bd(ch['rhs']))
    for ch in chains:
        ch['o'] = ch['o'] + mm(ch['a_rb'], bd(ch['u']))
        z_t = jnp.concatenate([ch['bw'], ch['kw'], jnp.broadcast_to(ch['wc'], (2 * CHUNK, QUAD))], axis=0).T
        upd = mm(z_t[:, 0:2 * CHUNK], jnp.concatenate([ch['u'].astype(bf16), ch['v']], axis=0))
        decay = jnp.concatenate([z_t[:, 2 * CHUNK:4 * CHUNK]] * 2, axis=1)
        ch['st_new'] = ch['st'] * decay + jnp.where(head_mask, upd, 0.0)
    for ch in chains:
        ch['o_ref'][ch['s'], :, ch['g'] * QUAD:(ch['g'] + 1) * QUAD] = ch['o']
        st_ref[ch['s'], ch['d'], ch['g']] = ch['st_new']

    @pl.when(c == pl.num_programs(1) - 1)
    def _():
        for s in range(n_seq):
            for d in range(2):
                for g in range(N_QUADS):
                    s_vk = st_ref[s, d, g].T
                    for h in range(QUAD_HEADS):
                        band = s_vk[h * RWKV_HEAD:(h + 1) * RWKV_HEAD, :]
                        if h:
                            band = pltpu.roll(band, QUAD - h * RWKV_HEAD, 1)
                        sfin_ref[s, d, g * QUAD_HEADS + h] = band[:, 0:RWKV_HEAD]


def _rwkv_scan(r, k, v, kk, lw, a, k_a, s0, batch, seq_len):
    n = r.shape[0]
    w = RWKV_WIDTH
    nc = seq_len // CHUNK
    per_seq = lambda x: x.reshape(batch, seq_len, x.shape[-1])
    r, k, v, kk, lw, a = (per_seq(x) for x in (r, k, v, kk, lw, a))
    fwd = lambda b, c: (b, c, 0)
    bwd = lambda b, c: (b, nc - 1 - c, 0)
    bwd_dir = lambda b, c: (b, nc - 1 - c, 1)
    blk = lambda index_map: pl.BlockSpec((SCAN_SEQS, CHUNK, w), index_map)
    state_spec = pl.BlockSpec((SCAN_SEQS, 2, N_QUADS, QUAD, QUAD), lambda b, c: (b, 0, 0, 0, 0))
    o_shape = jax.ShapeDtypeStruct((batch, seq_len, w), f32)
    init = [] if s0 is None else [s0]
    o_fw, o_bw, s_fin = pl.pallas_call(
        functools.partial(_scan_kernel, has_init=s0 is not None),
        grid=(batch // SCAN_SEQS, nc),
        in_specs=[blk(fwd)] * 6 + [blk(bwd)] * 4 + [blk(bwd_dir)] * 2
                 + [pl.BlockSpec((1, w), lambda b, c: (0, 0))] + [state_spec] * len(init),
        out_specs=[blk(fwd), blk(bwd),
                   pl.BlockSpec((SCAN_SEQS, 2, RWKV_HEADS, RWKV_HEAD, RWKV_HEAD), lambda b, c: (b, 0, 0, 0, 0))],
        out_shape=[o_shape, o_shape, jax.ShapeDtypeStruct((batch, 2, RWKV_HEADS, RWKV_HEAD, RWKV_HEAD), f32)],
        scratch_shapes=[pltpu.VMEM((SCAN_SEQS, 2, N_QUADS, QUAD, QUAD), f32)],
        compiler_params=_cparams("parallel", "arbitrary"), name="rwkv_scan",
    )(r, k, v, kk, lw, a, r, k, v, kk, lw, a, k_a, *init)
    return o_fw.reshape(n, w), o_bw.reshape(n, w), s_fin


def _out_kernel(attn_ref, of_ref, ob_ref, bonus_ref, g_ref, x_ref, mod_ref, lng_ref, lnb_ref, seg_ref,
                wo_ref, gpm_ref, gpf_ref, rwt_ref, rb_ref,
                x1_ref, h2_ref, slot_ref, sw_ref, cnt_ref):
    tiles = [pl.ds(t * TM_MOE, TM_MOE) for t in range(x_ref.shape[0] // TM_MOE)]
    fronts = [_mix_front(attn_ref.at[rows], of_ref.at[rows], ob_ref.at[rows], bonus_ref.at[rows],
                         g_ref.at[rows], x_ref.at[rows], mod_ref, lng_ref, lnb_ref, seg_ref, wo_ref,
                         gpm_ref, gpf_ref) for rows in tiles]
    picks = [_router_topk(h2, rwt_ref, rb_ref) for _, h2 in fronts]
    routes = [_router_slots(*pick) for pick in picks]
    for t, (rows, (x1, h2), (slots, sw, cnt)) in enumerate(zip(tiles, fronts, routes)):
        x1_ref[rows, :] = x1
        h2_ref[rows, :] = h2.astype(bf16)
        slot_ref[:, rows] = slots
        sw_ref[rows, :] = sw
        cnt_ref[t] = cnt


def _mix_front(attn_ref, of_ref, ob_ref, bonus_ref, g_ref, x_ref, mod_ref, lng_ref, lnb_ref, seg_ref,
               wo_ref, gpm_ref, gpf_ref):
    seg = seg_ref[...]
    inv = 1.0 / RWKV_HEAD
    o = of_ref[...] + ob_ref[...]
    mu = _seg_sum(o, seg) * inv
    dl = o - mu
    var = _seg_sum(dl * dl, seg) * inv
    on = dl * lax.rsqrt(var + GN_EPS) * lng_ref[...] + lnb_ref[...]
    rw = (on + bonus_ref[...]) * g_ref[...]
    half = ATTN_HEADS * HEAD_DIM
    y = _dot(attn_ref[...].astype(bf16), wo_ref[0:half, :]) + _dot(rw.astype(bf16), wo_ref[half:, :])
    m = mod_ref[...]
    gt_a = m[:, 2 * D_MODEL:3 * D_MODEL]
    sh_f = m[:, 3 * D_MODEL:4 * D_MODEL]
    sc_f = m[:, 4 * D_MODEL:5 * D_MODEL]
    x1 = x_ref[...] + gt_a * _rms(y, gpm_ref[...])
    h2 = _rms(x1, gpf_ref[...]) * (1.0 + sc_f) + sh_f
    return x1, h2


def _router_topk(h2, rwt_ref, rb_ref):
    tm = h2.shape[0]
    logits = _mm(rwt_ref[...], h2, nt=True)
    scores = _sigmoid(logits)
    biased = scores + rb_ref[...]
    neg = -jnp.inf
    shape3 = (N_GROUPS, GROUP_SIZE, tm)
    b3 = biased.reshape(shape3)
    s3 = scores.reshape(shape3)
    in_grp = lax.broadcasted_iota(jnp.int32, shape3, 1)
    grp = lax.broadcasted_iota(jnp.int32, shape3, 0)
    m1 = jnp.max(b3, axis=1, keepdims=True)
    i1 = jnp.min(jnp.where(b3 == m1, in_grp, GROUP_SIZE), axis=1, keepdims=True)
    m2 = jnp.max(jnp.where(in_grp == i1, neg, b3), axis=1, keepdims=True)
    gscore = m1 + m2
    gidx = lax.broadcasted_iota(jnp.int32, (N_GROUPS, 1, tm), 0)
    gsel = jnp.zeros((N_GROUPS, 1, tm), f32)
    cur = gscore
    for _ in range(TOPK_GROUPS):
        mx = jnp.max(cur, axis=0, keepdims=True)
        ii = jnp.min(jnp.where(cur == mx, gidx, N_GROUPS), axis=0, keepdims=True)
        hit = gidx == ii
        gsel = jnp.where(hit, 1.0, gsel)
        cur = jnp.where(hit, neg, cur)
    cand = jnp.where(jnp.broadcast_to(gsel, shape3) > 0.0, b3, neg)
    eidx = grp * GROUP_SIZE + in_grp
    wsel = jnp.zeros(shape3, f32)
    mask = jnp.zeros(shape3, f32)
    hits = []
    for _ in range(TOP_K):
        mx = jnp.max(jnp.max(cand, axis=1, keepdims=True), axis=0, keepdims=True)
        ii = jnp.min(jnp.min(jnp.where(cand == mx, eidx, N_EXPERTS), axis=1, keepdims=True),
                     axis=0, keepdims=True)
        hit = eidx == ii
        hits.append(hit)
        wsel = jnp.where(hit, s3, wsel)
        mask = jnp.where(hit, 1.0, mask)
        cand = jnp.where(hit, neg, cand)
    den = jnp.sum(jnp.sum(wsel, axis=1, keepdims=True), axis=0, keepdims=True)
    gates3 = wsel / den * ROUTED_SCALE
    return hits, gates3, mask


def _router_slots(hits, gates3, mask):
    tm = mask.shape[-1]
    shape3 = mask.shape
    mask2 = mask.reshape(N_EXPERTS, tm)
    cnt = jnp.sum(mask2, axis=1, keepdims=True)
    cnt_pad = jnp.floor((cnt + (SEG - 1)) * (1.0 / SEG)) * SEG
    er = lax.broadcasted_iota(jnp.int32, (N_EXPERTS, N_EXPERTS), 0)
    ec = lax.broadcasted_iota(jnp.int32, (N_EXPERTS, N_EXPERTS), 1)
    before = jnp.where(ec < er, 1.0, 0.0).astype(bf16)
    cnt_pad_l = jnp.broadcast_to(cnt_pad, (N_EXPERTS, LANE))
    seg_start = _dot(before, cnt_pad_l.astype(bf16))
    tr = lax.broadcasted_iota(jnp.int32, (tm, tm), 0)
    tc = lax.broadcasted_iota(jnp.int32, (tm, tm), 1)
    earlier = jnp.where(tr < tc, 1.0, 0.0).astype(bf16)
    rank = _dot(mask2.astype(bf16), earlier)
    slot3 = (rank + seg_start[:, 0:1]).reshape(shape3)

    def pick(hit, val3):
        return jnp.sum(jnp.sum(jnp.where(hit, val3, 0.0), axis=1, keepdims=True), axis=0)

    slots = [pick(h, slot3) for h in hits]
    wts = [pick(h, gates3) for h in hits]
    pad_rows = lambda k, val: [jnp.full((k, tm), val, f32)]
    slot_rows = jnp.concatenate(slots + pad_rows(8 - TOP_K, -1.0), axis=0).astype(jnp.int32)
    sw_t = jnp.concatenate(slots + pad_rows(8 - TOP_K, -1.0) + wts + pad_rows(LANE - 8 - TOP_K, 0.0), axis=0)
    return slot_rows, sw_t.T, cnt_pad_l


def _mix_out(attn, o_fw, o_bw, bonus, g, x2, mod3, mod_row, p, tm):
    n = x2.shape[0]
    w = RWKV_WIDTH
    consts = [p['lnx_g'], p['lnx_b'], p['seg'], p['w_out'], p['g_post_mix'], p['g_pre_ffn'],
              p['router_wt'], p['router_b']]
    row = lambda wd: pl.BlockSpec((tm, wd), lambda i: (i, 0))
    in_specs = [row(ATTN_HEADS * HEAD_DIM), row(w), row(w),
                row(w), row(w), row(D_MODEL),
                pl.BlockSpec((None, 1, 6 * D_MODEL), lambda i: (mod_row(i, tm), 0, 0))]
    in_specs += [_full_spec(c) for c in consts]
    out_shape = [jax.ShapeDtypeStruct((n, D_MODEL), f32),
                 jax.ShapeDtypeStruct((n, D_MODEL), bf16),
                 jax.ShapeDtypeStruct((8, n), jnp.int32),
                 jax.ShapeDtypeStruct((n, LANE), f32),
                 jax.ShapeDtypeStruct((n // TM_MOE, N_EXPERTS, LANE), f32)]
    out_specs = [row(D_MODEL), row(D_MODEL), pl.BlockSpec((8, tm), lambda i: (0, i)), row(LANE),
                 pl.BlockSpec((tm // TM_MOE, N_EXPERTS, LANE), lambda i: (i, 0, 0))]
    return pl.pallas_call(
        _out_kernel, grid=(n // tm,), in_specs=in_specs, out_specs=out_specs, out_shape=out_shape,
        compiler_params=_cparams("parallel"), name="mix_out",
    )(attn, o_fw, o_bw, bonus, g, x2, mod3, *consts)


def _repeat(lo, hi, body):
    def step(idx, carry):
        body(idx)
        return carry

    lax.fori_loop(lo, hi, step, 0)


def _copy_groups(tile, cnt_sm, kind):
    return lax.shift_right_logical(cnt_sm[TILE_COUNTS * tile + kind] + (CHUNK_GROUP - 1), CHUNK_GROUP_SHIFT)


def _start_chunks(tile, tab_sm, cnt_sm, make_copy):
    for kind, rows in enumerate(COPY_ROWS):
        base = tile * TABLE_WIDTH + sum(MAX_COPIES[:kind])

        def group(g, base=base, rows=rows):
            for u in range(CHUNK_GROUP):
                entry = tab_sm[base + g * CHUNK_GROUP + u]
                local_row = pl.multiple_of(jnp.bitwise_and(entry, (1 << LOC_BITS) - 1) * SEG, SEG)
                global_row = pl.multiple_of(lax.shift_right_logical(entry, LOC_BITS) * SEG, SEG)
                make_copy(local_row, global_row, rows).start()

        _repeat(0, _copy_groups(tile, cnt_sm, kind), group)


def _wait_chunks(tile, cnt_sm, make_copy):
    for kind, rows in enumerate(COPY_ROWS):
        def group(g, rows=rows):
            for _ in range(CHUNK_GROUP):
                make_copy(0, 0, rows).wait()

        _repeat(0, _copy_groups(tile, cnt_sm, kind), group)


def _dispatch_kernel(dst_sm, nch_sm, fill_sm, *refs, tile_starts):
    n_paths = len(tile_starts)
    h_refs, slot_refs = refs[:n_paths], refs[n_paths:2 * n_paths]
    xs_hbm, xc_ref, zero_ref, sems, zsem = refs[2 * n_paths:]
    i = pl.program_id(0)
    last = pl.num_programs(0) - 1
    buf = lax.rem(i, 2)

    def copy_from(b):
        def make(local_row, global_row, rows):
            return pltpu.make_async_copy(xc_ref.at[b, pl.ds(local_row, rows)],
                                         xs_hbm.at[pl.ds(global_row, rows)], sems.at[b])
        return make

    @pl.when(i == 0)
    def _():
        zero_ref[...] = jnp.zeros_like(zero_ref)
        xc_ref[...] = jnp.zeros_like(xc_ref)
        n_tiles = xs_hbm.shape[0] // FFN_ROWS

        def tail_copy(row):
            return pltpu.make_async_copy(zero_ref.at[pl.ds(0, SEG)], xs_hbm.at[pl.ds(row, SEG)], zsem)

        def tile_copy(t):
            return pltpu.make_async_copy(zero_ref, xs_hbm.at[pl.ds(pl.multiple_of(t * FFN_ROWS, FFN_ROWS), FFN_ROWS)],
                                         zsem)

        def per_expert(e):
            start = fill_sm[1 + e]
            chunks = lax.shift_right_logical(fill_sm[1 + N_EXPERTS + e], SEG_SHIFT)
            _repeat(0, chunks, lambda c: tail_copy(pl.multiple_of(start + c * SEG, SEG)).start())
            _repeat(0, chunks, lambda c: tail_copy(0).wait())

        _repeat(0, N_EXPERTS, per_expert)
        tile_copy(n_tiles - 1).start()
        tile_copy(0).wait()
        _repeat(fill_sm[0], n_tiles - 1, lambda t: tile_copy(t).start())

    @pl.when(i >= 2)
    def _():
        _wait_chunks(i - 2, nch_sm, copy_from(buf))

    slot, h = slot_refs[0][...], h_refs[0][...]
    for q in range(1, n_paths):
        mine = i >= tile_starts[q]
        slot = jnp.where(mine, slot_refs[q][...], slot)
        h = jnp.where(mine, h_refs[q][...], h)
    used = nch_sm[TILE_COUNTS * i + 2]

    def build(rb):
        rows = lax.broadcasted_iota(jnp.int32, (SEL_BLOCK, TM_MOE), 0) + rb * SEL_BLOCK
        sel = jnp.zeros((SEL_BLOCK, TM_MOE), f32)
        for j in range(TOP_K):
            sel = jnp.where(rows == slot[j:j + 1, :], 1.0, sel)
        xc_ref[buf, rb * SEL_BLOCK:(rb + 1) * SEL_BLOCK, :] = _dot(sel.astype(bf16), h).astype(bf16)

    for rb in range(COMMON_BLOCKS):
        build(rb)
    for rb in range(COMMON_BLOCKS, SLOT_ROWS // SEL_BLOCK):
        pl.when(rb * SEL_BLOCK < used)(functools.partial(build, rb))

    _start_chunks(i, dst_sm, nch_sm, copy_from(buf))

    @pl.when(i == last)
    def _():
        @pl.when(i >= 1)
        def _():
            _wait_chunks(i - 1, nch_sm, copy_from(1 - buf))
        _wait_chunks(i, nch_sm, copy_from(buf))
        n_tiles = xs_hbm.shape[0] // FFN_ROWS
        whole_tile = pltpu.make_async_copy(zero_ref, xs_hbm.at[pl.ds(0, FFN_ROWS)], zsem)
        _repeat(fill_sm[0], n_tiles - 1, lambda t: whole_tile.wait())


def _dispatch(h2s, slots, dst, nch, fill, total_rows):
    tiles = [h.shape[0] // TM_MOE for h in h2s]
    starts = [sum(tiles[:q]) for q in range(len(tiles))]

    def in_path(q):
        return lambda i: jnp.clip(i - starts[q], 0, tiles[q] - 1)

    h_specs = [pl.BlockSpec((TM_MOE, D_MODEL), lambda i, d, c, f, q=q: (in_path(q)(i), 0)) for q in range(len(tiles))]
    s_specs = [pl.BlockSpec((8, TM_MOE), lambda i, d, c, f, q=q: (0, in_path(q)(i))) for q in range(len(tiles))]
    grid_spec = pltpu.PrefetchScalarGridSpec(
        num_scalar_prefetch=3, grid=(sum(tiles),),
        in_specs=h_specs + s_specs,
        out_specs=pl.BlockSpec(memory_space=pl.ANY),
        scratch_shapes=[pltpu.VMEM((2, SLOT_ROWS, D_MODEL), bf16), pltpu.VMEM((FFN_ROWS, D_MODEL), bf16),
                        pltpu.SemaphoreType.DMA((2,)), pltpu.SemaphoreType.DMA(())])
    return pl.pallas_call(
        functools.partial(_dispatch_kernel, tile_starts=tuple(starts)), grid_spec=grid_spec,
        out_shape=jax.ShapeDtypeStruct((total_rows, D_MODEL), bf16),
        compiler_params=_cparams("arbitrary"), name="moe_dispatch",
    )(dst, nch, fill, *h2s, *slots)


def _ffn_kernel(te_sm, nv_sm, xs_ref, wg_ref, wu_ref, wd_ref, ys_ref, wg_b, wu_b, wd_b):
    r = pl.program_id(0)
    valid = r < nv_sm[0]

    @pl.when(jnp.logical_and(valid, jnp.logical_or(r == 0, te_sm[r] != te_sm[jnp.maximum(r - 1, 0)])))
    def _():
        wg_b[...] = wg_ref[...].astype(bf16)
        wu_b[...] = wu_ref[...].astype(bf16)
        wd_b[...] = wd_ref[...].astype(bf16)

    @pl.when(valid)
    def _():
        x = xs_ref[...]
        gg = _dot(x, wg_b[...])
        uu = _dot(x, wu_b[...])
        hm = gg * _sigmoid(gg) * uu
        ys_ref[...] = _dot(hm.astype(bf16), wd_b[...]).astype(bf16)

    @pl.when(r >= nv_sm[0])
    def _():
        ys_ref[...] = jnp.zeros_like(ys_ref)


def _expert_ffn(xs, tile_expert, n_valid, wg, wu, wd):
    total_rows = xs.shape[0]
    last = lambda r, nv: jnp.minimum(r, nv[0] - 1)
    grid_spec = pltpu.PrefetchScalarGridSpec(
        num_scalar_prefetch=2, grid=(total_rows // FFN_ROWS,),
        in_specs=[pl.BlockSpec((FFN_ROWS, D_MODEL), lambda r, te, nv: (last(r, nv), 0)),
                  pl.BlockSpec((None, D_MODEL, EXPERT_FF), lambda r, te, nv: (te[last(r, nv)], 0, 0)),
                  pl.BlockSpec((None, D_MODEL, EXPERT_FF), lambda r, te, nv: (te[last(r, nv)], 0, 0)),
                  pl.BlockSpec((None, EXPERT_FF, D_MODEL), lambda r, te, nv: (te[last(r, nv)], 0, 0))],
        out_specs=pl.BlockSpec((FFN_ROWS, D_MODEL), lambda r, te, nv: (r, 0)),
        scratch_shapes=[pltpu.VMEM((D_MODEL, EXPERT_FF), bf16), pltpu.VMEM((D_MODEL, EXPERT_FF), bf16),
                        pltpu.VMEM((EXPERT_FF, D_MODEL), bf16)])
    return pl.pallas_call(
        _ffn_kernel, grid_spec=grid_spec,
        out_shape=jax.ShapeDtypeStruct((total_rows, D_MODEL), bf16),
        compiler_params=_cparams("arbitrary"), name="moe_ffn",
    )(tile_expert, n_valid, xs, wg, wu, wd)


def _combine_kernel(dst_sm, nch_sm, sw_ref, h_ref, x1_ref, mod_ref, gpost_ref, swg_ref, swu_ref, swd_ref,
                    ys_hbm, y_ref, yc_ref, acc_ref, sems, *, tile_offset):
    step = pl.program_id(0)
    i = step + tile_offset
    buf = lax.rem(step, 2)

    def copy_into(b):
        def make(local_row, global_row, rows):
            return pltpu.make_async_copy(ys_hbm.at[pl.ds(global_row, rows)],
                                         yc_ref.at[b, pl.ds(local_row, rows)], sems.at[b])
        return make

    @pl.when(step == 0)
    def _():
        yc_ref[...] = jnp.zeros_like(yc_ref)
        _start_chunks(i, dst_sm, nch_sm, copy_into(buf))

    @pl.when(step + 1 < pl.num_programs(0))
    def _():
        _start_chunks(i + 1, dst_sm, nch_sm, copy_into(1 - buf))

    h = h_ref[...]
    gg = _dot(h, swg_ref[...])
    uu = _dot(h, swu_ref[...])
    acc_ref[...] = _dot((gg * _sigmoid(gg) * uu).astype(bf16), swd_ref[...])
    _wait_chunks(i, nch_sm, copy_into(buf))
    sw = sw_ref[...]
    used = nch_sm[TILE_COUNTS * i + 2]
    def weighted(rb):
        cols = (lax.broadcasted_iota(jnp.int32, (TM_MOE, SEL_BLOCK), 1) + rb * SEL_BLOCK).astype(f32)
        wmat = jnp.zeros((TM_MOE, SEL_BLOCK), f32)
        for j in range(TOP_K):
            wmat = jnp.where(cols == sw[:, j:j + 1], sw[:, 8 + j:9 + j], wmat)
        rows = lax.broadcasted_iota(jnp.int32, (SEL_BLOCK, 1), 0) + rb * SEL_BLOCK
        yc = jnp.where(rows < used, yc_ref[buf, rb * SEL_BLOCK:(rb + 1) * SEL_BLOCK, :], jnp.zeros((), bf16))
        return _dot(wmat.astype(bf16), yc)

    moe = acc_ref[...]
    for rb in range(COMMON_BLOCKS):
        moe = moe + weighted(rb)
    acc_ref[...] = moe
    for rb in range(COMMON_BLOCKS, SLOT_ROWS // SEL_BLOCK):
        @pl.when(rb * SEL_BLOCK < used)
        def _():
            acc_ref[...] += weighted(rb)
    gt_f = mod_ref[...][:, 5 * D_MODEL:6 * D_MODEL]
    y_ref[...] = x1_ref[...] + gt_f * _rms(acc_ref[...], gpost_ref[...])


def _combine(ys, dst, nch, sw, h2, x1, mod3, mod_row, g_post, swg, swu, swd, tile_offset):
    n = h2.shape[0]
    row = lambda wd: pl.BlockSpec((TM_MOE, wd), lambda i, d, c: (i, 0))
    const = lambda a: pl.BlockSpec(a.shape, lambda i, d, c: (0,) * a.ndim)
    grid_spec = pltpu.PrefetchScalarGridSpec(
        num_scalar_prefetch=2, grid=(n // TM_MOE,),
        in_specs=[row(LANE), row(D_MODEL), row(D_MODEL),
                  pl.BlockSpec((None, 1, 6 * D_MODEL), lambda i, d, c: (mod_row(i, TM_MOE), 0, 0)),
                  const(g_post), const(swg), const(swu), const(swd),
                  pl.BlockSpec(memory_space=pl.ANY)],
        out_specs=row(D_MODEL),
        scratch_shapes=[pltpu.VMEM((2, SLOT_ROWS, D_MODEL), bf16), pltpu.VMEM((TM_MOE, D_MODEL), f32),
                        pltpu.SemaphoreType.DMA((2,))])
    return pl.pallas_call(
        functools.partial(_combine_kernel, tile_offset=tile_offset), grid_spec=grid_spec,
        out_shape=jax.ShapeDtypeStruct((n, D_MODEL), f32),
        compiler_params=_cparams("arbitrary"), name="moe_combine",
    )(dst, nch, sw, h2, x1, mod3, g_post, swg, swu, swd, ys)


def _sparse_moe(paths, mod3, p):
    cnt = jnp.concatenate([q['cnt'][:, :, 0] for q in paths], axis=0).astype(jnp.int32)
    n_tiles = cnt.shape[0]
    before = jnp.cumsum(cnt, axis=0) - cnt
    rows_e = jnp.sum(cnt, axis=0)
    region = (rows_e + FFN_ROWS - 1) // FFN_ROWS * FFN_ROWS
    region_end = jnp.cumsum(region)
    region_start = region_end - region
    worst = n_tiles * (TOP_K * TM_MOE + N_EXPERTS * (SEG - 1)) + N_EXPERTS * (FFN_ROWS - SEG)
    total_rows = -(-worst // FFN_ROWS) * FFN_ROWS + FFN_ROWS
    spare_row = total_rows - FFN_ROWS
    first_row = jnp.arange(total_rows // FFN_ROWS, dtype=jnp.int32) * FFN_ROWS
    ended = jnp.sum((region_end[None, :] <= first_row[:, None]).astype(jnp.int32), axis=1)
    tile_expert = jnp.minimum(ended, N_EXPERTS - 1).astype(jnp.int32)
    n_valid = (region_end[-1:] // FFN_ROWS).astype(jnp.int32)

    seg_end = jnp.cumsum(cnt, axis=1)
    seg_start = seg_end - cnt
    seg_global = region_start[None, :] + before
    chunks = cnt // SEG
    pairs = chunks // 2
    tile_id = jnp.arange(n_tiles, dtype=jnp.int32)[:, None]
    experts = jnp.arange(N_EXPERTS, dtype=jnp.int32)[None, None, :]
    tables, counts = [], []
    spare_base = spare_row
    for kind, (rows, per_seg, first) in enumerate(((2 * SEG, pairs, jnp.zeros_like(pairs)),
                                                    (SEG, chunks - 2 * pairs, 2 * SEG * pairs))):
        ends = jnp.cumsum(per_seg, axis=1)
        idx = jnp.arange(MAX_COPIES[kind], dtype=jnp.int32)[None, :]
        seg_of = jnp.sum((ends[:, None, :] <= idx[:, :, None]).astype(jnp.int32), axis=2)
        own = seg_of[:, :, None] == experts
        pick = lambda a: jnp.sum(jnp.where(own, a[:, None, :], 0), axis=2)
        within = (idx - pick(ends - per_seg)) * rows
        local = pick(seg_start + first) + within
        dst = pick(seg_global + first) + within
        slot = (tile_id % 2) * CHUNK_GROUP + idx % CHUNK_GROUP
        is_copy = idx < ends[:, -1:]
        local = jnp.where(is_copy, local, PAD_LOCAL[kind] + (idx % CHUNK_GROUP) * rows)
        dst = jnp.where(is_copy, dst, spare_base + slot * rows)
        tables.append((dst // SEG) * (1 << LOC_BITS) + local // SEG)
        counts.append(ends[:, -1])
        spare_base = spare_base + 2 * CHUNK_GROUP * rows
    table = jnp.concatenate(tables, axis=1).astype(jnp.int32).reshape(-1)
    nch = jnp.stack(counts + [seg_end[:, -1]], axis=1).astype(jnp.int32).reshape(-1)
    dst = table

    fill = jnp.concatenate([n_valid, region_start + rows_e, region - rows_e]).astype(jnp.int32)
    xs = _dispatch([q['h2'] for q in paths], [q['slots'] for q in paths], dst, nch, fill, total_rows)
    ys = _expert_ffn(xs, tile_expert, n_valid, p['expert_wg'], p['expert_wu'], p['expert_wd'])
    outs, tile_offset = [], 0
    for q in paths:
        outs.append(_combine(ys, dst, nch, q['sw'], q['h2'], q['x1'], mod3, q['mod_row'],
                             p['g_post_ffn'], p['shared_wg'], p['shared_wu'], p['shared_wd'], tile_offset))
        tile_offset += q['h2'].shape[0] // TM_MOE
    return outs


def _pad_heads(w, heads):
    rows = w.shape[0]
    w = w.reshape(rows, heads, HEAD_DIM)
    return jnp.pad(w, ((0, 0), (0, 0), (0, HEAD_PAD - HEAD_DIM))).reshape(rows, heads * HEAD_PAD)


def _block_diag2(m):
    z = jnp.zeros_like(m[0])
    return jnp.concatenate([jnp.concatenate([m[0], z], axis=1), jnp.concatenate([z, m[1]], axis=1)], axis=0)


def _prepare_params(w):
    aw, kw = ATTN_HEADS * HEAD_DIM, KV_HEADS * HEAD_DIM
    w_in = w['w_in']
    lane_row = lambda v: v.reshape(1, -1).astype(f32)
    head_id = np.arange(RWKV_WIDTH) // RWKV_HEAD
    seg = jnp.asarray(head_id[:, None] == head_id[None, :], dtype=bf16)
    gain_pad = lambda gvec: jnp.pad(gvec, (0, HEAD_PAD - HEAD_DIM)).reshape(1, HEAD_PAD)
    return dict(
        g_pre_mix=lane_row(w['g_pre_mix']), g_post_mix=lane_row(w['g_post_mix']),
        g_pre_ffn=lane_row(w['g_pre_ffn']), g_post_ffn=lane_row(w['g_post_ffn']),
        wq=_pad_heads(w_in[:, :aw], ATTN_HEADS).astype(bf16),
        wk=_pad_heads(w_in[:, aw:aw + kw], KV_HEADS).astype(bf16),
        wv=w_in[:, aw + kw:aw + 2 * kw].astype(bf16),
        wr=w_in[:, aw + 2 * kw:].astype(bf16),
        qg=gain_pad(w['q_gain'] * ATTN_SCALE), kg=gain_pad(w['k_gain']),
        mu_prev=lane_row(w['mu_prev']), mu_next=lane_row(w['mu_next']),
        w0=lane_row(w['decay_w0']), w2bd=_block_diag2(w['decay_w2']),
        a0=lane_row(w['iclr_a0']), a2bd=_block_diag2(w['iclr_a2']),
        g2=w['gate_g2'], k_k=lane_row(w['k_k']), k_a=lane_row(w['k_a']), r_k=lane_row(w['r_k']),
        lnx_g=lane_row(w['lnx_g']), lnx_b=lane_row(w['lnx_b']), seg=seg,
        w_out=w['w_out'].astype(bf16),
        router_wt=w['router_w'].T, router_b=w['router_b'].reshape(N_EXPERTS, 1),
        expert_wg=w['expert_wg'], expert_wu=w['expert_wu'], expert_wd=w['expert_wd'],
        shared_wg=w['shared_wg'].astype(bf16), shared_wu=w['shared_wu'].astype(bf16),
        shared_wd=w['shared_wd'].astype(bf16),
    )


def _rope_tables(seq_len):
    rows = seq_len // GRID_W
    row_idx = jnp.repeat(jnp.arange(rows, dtype=jnp.int32), GRID_W)
    col_idx = jnp.tile(jnp.arange(GRID_W, dtype=jnp.int32), rows)
    inv = ROPE_THETA ** (-jnp.arange(0, ROPE_HALF, 2, dtype=f32) / ROPE_HALF)
    ang = jnp.stack([row_idx.astype(f32)[:, None] * inv, col_idx.astype(f32)[:, None] * inv], axis=1)
    cos, sin = jnp.cos(ang), jnp.sin(ang)
    cos_l = jnp.stack([cos, cos], axis=2).reshape(seq_len, HEAD_DIM)
    sin_l = jnp.stack([-sin, sin], axis=2).reshape(seq_len, HEAD_DIM)
    pad = ((0, 0), (0, HEAD_PAD - HEAD_DIM))
    return jnp.pad(cos_l, pad), jnp.pad(sin_l, pad)


def _states_to_pairs(s):
    b = s.shape[0]
    s = s.reshape(b, 2, N_QUADS, QUAD_HEADS, RWKV_HEAD, RWKV_HEAD)
    eye = jnp.eye(QUAD_HEADS, dtype=s.dtype)
    return jnp.einsum('bdphvk,hg->bdphkgv', s, eye).reshape(b, 2, N_QUADS, QUAD, QUAD)


def _trunk(x, mod3, mod_row, p, rope_tabs, past_k, past_v, s0_pairs, tm_in, tq, tm_prep):
    b, t, _ = x.shape
    n = b * t
    x2 = x.reshape(n, D_MODEL)
    q_pad, k_own, v_own, k_pad, zr = _in_proj(
        x2, mod3, mod_row, p['g_pre_mix'], p['wq'], p['wk'], p['wv'], p['wr'], p['qg'], p['kg'],
        rope_tabs, t, tm_in)
    k_full = k_pad.reshape(b, t, KV_HEADS * HEAD_PAD)
    v_full = v_own.reshape(b, t, KV_HEADS * HEAD_DIM)
    if past_k is not None:
        k_past = _pad_heads(past_k.reshape(-1, KV_HEADS * HEAD_DIM), KV_HEADS).astype(bf16)
        k_full = jnp.concatenate([k_past.reshape(b, -1, KV_HEADS * HEAD_PAD), k_full], axis=1)
        v_full = jnp.concatenate([past_v.reshape(b, -1, KV_HEADS * HEAD_DIM), v_full], axis=1)
    vt_full = jnp.swapaxes(v_full, 1, 2).astype(bf16)
    attn = _attention(q_pad, k_full, vt_full, b, t, tq)
    r, k, v, kk, lw, a, g, bonus = _rwkv_prep(zr, p, t, tm_prep)
    o_fw, o_bw, s_fin = _rwkv_scan(r, k, v, kk, lw, a, p['k_a'], s0_pairs, b, t)
    x1, h2, slots, sw, cnt = _mix_out(attn, o_fw, o_bw, bonus, g, x2, mod3, mod_row, p, MIX_TILES * TM_MOE)
    return dict(h2=h2, slots=slots, sw=sw, cnt=cnt, x1=x1, mod_row=mod_row), k_own, v_own, s_fin


def kernel(x_prompt, x_sample, c, cache_k, cache_v, state_rwkv, c_ctx, w_ada, b_ada, g_pre_mix, g_post_mix,
           g_pre_ffn, g_post_ffn, w_in, w_out, q_gain, k_gain, mu_prev, mu_next, decay_w0, decay_w2, iclr_a0,
           iclr_a2, gate_g2, k_k, k_a, r_k, lnx_g, lnx_b, router_w, router_b, expert_wg, expert_wu, expert_wd,
           shared_wg, shared_wu, shared_wd):
    layer = 0
    names = ('g_pre_mix g_post_mix g_pre_ffn g_post_ffn w_in w_out q_gain k_gain mu_prev mu_next decay_w0 '
             'decay_w2 iclr_a0 iclr_a2 gate_g2 k_k k_a r_k lnx_g lnx_b router_w router_b expert_wg expert_wu '
             'expert_wd shared_wg shared_wu shared_wd').split()
    vals = (g_pre_mix, g_post_mix, g_pre_ffn, g_post_ffn, w_in, w_out, q_gain, k_gain, mu_prev, mu_next,
            decay_w0, decay_w2, iclr_a0, iclr_a2, gate_g2, k_k, k_a, r_k, lnx_g, lnx_b, router_w, router_b,
            expert_wg, expert_wu, expert_wd, shared_wg, shared_wu, shared_wd)
    p = _prepare_params({nm: v[layer] for nm, v in zip(names, vals)})

    nb, ts, _ = x_sample.shape
    npb, tp, _ = x_prompt.shape
    assert ts % (MIX_TILES * TM_MOE) == 0 and ts % LATENT_TILES.in_rows == 0 and ts % GRID_W == 0
    assert ts % LATENT_TILES.attn_queries == 0 and ts % LATENT_TILES.prep_rows == 0
    assert (npb * tp) % CONTEXT_TILES.in_rows == 0 and tp % CONTEXT_TILES.prep_rows == 0
    assert nb % SCAN_SEQS == 0 and npb % SCAN_SEQS == 0 and (npb * tp) % (MIX_TILES * TM_MOE) == 0
    mod_rows = 16
    assert nb < mod_rows
    cc = jnp.zeros((mod_rows, D_MODEL), f32).at[0].set(c_ctx).at[1:1 + nb].set(c)
    mod3 = _modulation(cc, w_ada[layer], b_ada[layer]).reshape(mod_rows, 1, 6 * D_MODEL)

    moe_p, kc, vc, st = _trunk(x_prompt, mod3, lambda i, tm: 0, p, None, None, None, None,
                               tm_in=CONTEXT_TILES.in_rows, tq=tp, tm_prep=CONTEXT_TILES.prep_rows)
    s0_lat = _states_to_pairs(state_rwkv[:, layer])
    moe_s, _, _, _ = _trunk(x_sample, mod3, lambda i, tm: 1 + (i * tm) // ts, p, _rope_tables(ts),
                            cache_k[:, layer], cache_v[:, layer], s0_lat,
                            tm_in=LATENT_TILES.in_rows, tq=LATENT_TILES.attn_queries,
                            tm_prep=LATENT_TILES.prep_rows)
    y_p, y_s = _sparse_moe([moe_p, moe_s], mod3, p)
    y_p = y_p.reshape(x_prompt.shape)
    y_s = y_s.reshape(x_sample.shape)

    new_cache_k = kc.reshape(npb, 1, tp, KV_HEADS, HEAD_DIM)
    new_cache_v = vc.reshape(npb, 1, tp, KV_HEADS, HEAD_DIM)
    new_state = st[:, None]
    return (y_p, y_s, new_cache_k, new_cache_v, new_state)
```

```python
import functools
from typing import NamedTuple

import numpy as np
import jax
import jax.numpy as jnp
from jax import lax
from jax.experimental import pallas as pl
from jax.experimental.pallas import tpu as pltpu

f32 = jnp.float32
bf16 = jnp.bfloat16

D_MODEL = 1024
GRID_W = 64
HEAD_DIM = 64
ATTN_HEADS = 8
KV_HEADS = 2
GQA_GROUP = ATTN_HEADS // KV_HEADS
ATTN_SCALE = HEAD_DIM ** -0.5
ROPE_THETA = 10000.0
ROPE_HALF = HEAD_DIM // 2
ROPE_FREQS = ROPE_HALF // 2
RWKV_HEAD = 64
RWKV_HEADS = 8
RWKV_WIDTH = RWKV_HEADS * RWKV_HEAD
DECAY_RANK = 64
ICLR_RANK = 64
GATE_RANK = 128
RWKV_IN = 3 * RWKV_WIDTH + 2 * DECAY_RANK + 2 * ICLR_RANK + GATE_RANK
DECAY_SCALE = 0.606531
GN_EPS = 64e-5
N_EXPERTS = 64
TOP_K = 6
N_GROUPS = 8
GROUP_SIZE = N_EXPERTS // N_GROUPS
TOPK_GROUPS = 4
EXPERT_FF = 256
ROUTED_SCALE = 2.5
EPS = 1e-6

LANE = 128
HEAD_PAD = LANE
CHUNK = 64
QUAD_HEADS = 4
QUAD = QUAD_HEADS * RWKV_HEAD
N_QUADS = RWKV_HEADS // QUAD_HEADS
SCAN_SEQS = 4
VMEM_LIMIT = 56 * 1024 * 1024


class PathTiles(NamedTuple):
    in_rows: int
    attn_queries: int
    prep_rows: int


CONTEXT_TILES = PathTiles(in_rows=512, attn_queries=0, prep_rows=256)
LATENT_TILES = PathTiles(in_rows=512, attn_queries=256, prep_rows=512)
TM_MOE = 256
MIX_TILES = 4
IN_SUB = 256
SEG_SHIFT = 4
SEG = 1 << SEG_SHIFT
SLOT_DATA_ROWS = 2560
CHUNK_GROUP_SHIFT = 3
CHUNK_GROUP = 1 << CHUNK_GROUP_SHIFT
COPY_ROWS = (2 * SEG, SEG)
MAX_COPIES = (SLOT_DATA_ROWS // (2 * SEG), N_EXPERTS)
TABLE_WIDTH = sum(MAX_COPIES)
TILE_COUNTS = 3
LOC_BITS = 9
PAD_LOCAL = (SLOT_DATA_ROWS, SLOT_DATA_ROWS + CHUNK_GROUP * 2 * SEG)
SLOT_ROWS = SLOT_DATA_ROWS + 512
SEL_BLOCK = 512
COMMON_BLOCKS = 4
FFN_ROWS = 1024


def _cparams(*sem):
    return pltpu.CompilerParams(dimension_semantics=sem, vmem_limit_bytes=VMEM_LIMIT)


def _dot(a, b, nt=False):
    dims = (((1,), (1,)), ((), ())) if nt else (((1,), (0,)), ((), ()))
    return lax.dot_general(a, b, dims, preferred_element_type=f32)


def _split2(x):
    hi = x.astype(bf16)
    lo = (x - hi.astype(f32)).astype(bf16)
    return hi, lo


def _split3(x):
    h1 = x.astype(bf16)
    r1 = x - h1.astype(f32)
    h2 = r1.astype(bf16)
    h3 = (r1 - h2.astype(f32)).astype(bf16)
    return h1, h2, h3


def _mm(a, b, nt=False, passes=3):
    if passes == 1:
        return _dot(a.astype(bf16), b.astype(bf16), nt)
    ah, al = _split2(a)
    bh, bl = _split2(b)
    return _dot(ah, bh, nt) + (_dot(al, bh, nt) + _dot(ah, bl, nt))


def _seg_sum(x, seg_bf16):
    return _dot(x.astype(bf16), seg_bf16)


def _rms(x, g):
    return x * lax.rsqrt(jnp.mean(x * x, axis=-1, keepdims=True) + EPS) * g


def _sigmoid(x):
    return jax.nn.sigmoid(x)


def _full_spec(a, grid_rank=1):
    zeros = (0,) * a.ndim
    if grid_rank == 1:
        return pl.BlockSpec(a.shape, lambda i: zeros)
    if grid_rank == 2:
        return pl.BlockSpec(a.shape, lambda i, j: zeros)
    return pl.BlockSpec(a.shape, lambda i, j, k: zeros)


def _mod_kernel(c_ref, w_ref, b_ref, o_ref):
    c = c_ref[...]
    s = c * _sigmoid(c)
    o_ref[...] = _mm(s, w_ref[...]) + b_ref[...]


def _modulation(cc, w_ada, b_ada):
    rows, n = cc.shape[0], w_ada.shape[1]
    tn = 512
    return pl.pallas_call(
        _mod_kernel,
        grid=(n // tn,),
        in_specs=[pl.BlockSpec((rows, D_MODEL), lambda j: (0, 0)),
                  pl.BlockSpec((D_MODEL, tn), lambda j: (0, j)),
                  pl.BlockSpec((1, tn), lambda j: (0, j))],
        out_specs=pl.BlockSpec((rows, tn), lambda j: (0, j)),
        out_shape=jax.ShapeDtypeStruct((rows, n), f32),
        compiler_params=_cparams("parallel"),
        name="modulation",
    )(cc, w_ada, b_ada.reshape(1, n))


def _in_kernel(*refs, rope):
    if rope:
        (x_ref, mod_ref, g_ref, wq_ref, wk_ref, wv_ref, wr_ref, qg_ref, kg_ref, cos_ref, sin_ref,
         q_ref, ko_ref, vo_ref, kp_ref, zr_ref) = refs
    else:
        (x_ref, mod_ref, g_ref, wq_ref, wk_ref, wv_ref, wr_ref, qg_ref, kg_ref,
         q_ref, ko_ref, vo_ref, kp_ref, zr_ref) = refs
    m = mod_ref[...]
    lane = lax.broadcasted_iota(jnp.int32, (1, LANE), 1)
    first_half = (lane & (ROPE_HALF - 1)) < ROPE_FREQS
    subs = [pl.ds(t * IN_SUB, IN_SUB) for t in range(x_ref.shape[0] // IN_SUB)]

    def front(rows):
        h = _rms(x_ref[rows, :], g_ref[...])
        return (h * (1.0 + m[:, D_MODEL:2 * D_MODEL]) + m[:, 0:D_MODEL]).astype(bf16)

    def head_norm(z, gain):
        ms = jnp.sum(z * z, axis=-1, keepdims=True) * (1.0 / HEAD_DIM)
        return z * lax.rsqrt(ms + EPS) * gain

    def rotate(y, rows):
        if not rope:
            return y
        partner = jnp.where(first_half, pltpu.roll(y, LANE - ROPE_FREQS, 1), pltpu.roll(y, ROPE_FREQS, 1))
        return y * cos_ref[rows, :] + partner * sin_ref[rows, :]

    def epilogue(rows, zq, zk):
        q = jnp.concatenate(
            [rotate(head_norm(zq[:, hh * HEAD_PAD:(hh + 1) * HEAD_PAD], qg_ref[...]), rows).astype(bf16)
             for hh in range(ATTN_HEADS)], axis=1)
        kn = [head_norm(zk[:, hh * HEAD_PAD:(hh + 1) * HEAD_PAD], kg_ref[...]) for hh in range(KV_HEADS)]
        k_own = kn[0] + pltpu.roll(kn[1], HEAD_DIM, 1)
        k_rot = jnp.concatenate([rotate(kn[hh], rows).astype(bf16) for hh in range(KV_HEADS)], axis=1)
        return q, k_own, k_rot

    hbs = [front(rows) for rows in subs]
    zs = [(_dot(hb, wr_ref[...]), _dot(hb, wq_ref[...]), _dot(hb, wk_ref[...]), _dot(hb, wv_ref[...])) for hb in hbs]
    outs = [epilogue(rows, zq, zk) for rows, (_, zq, zk, _) in zip(subs, zs)]
    for rows, (zr, _, _, zv), (q, k_own, k_rot) in zip(subs, zs, outs):
        zr_ref[rows, :] = zr
        q_ref[rows, :] = q
        ko_ref[rows, :] = k_own
        vo_ref[rows, :] = zv
        kp_ref[rows, :] = k_rot


def _in_proj(x2, mod3, mod_row, g_pre, wq, wk, wv, wr, qg, kg, rope_tabs, seq_len, tm):
    n = x2.shape[0]
    rope = rope_tabs is not None
    in_specs = [pl.BlockSpec((tm, D_MODEL), lambda i: (i, 0)),
                pl.BlockSpec((None, 1, 6 * D_MODEL), lambda i: (mod_row(i, tm), 0, 0)),
                _full_spec(g_pre), _full_spec(wq), _full_spec(wk), _full_spec(wv), _full_spec(wr),
                _full_spec(qg), _full_spec(kg)]
    args = [x2, mod3, g_pre, wq, wk, wv, wr, qg, kg]
    if rope:
        blocks_per_seq = seq_len // tm
        in_specs += [pl.BlockSpec((tm, LANE), lambda i: (i % blocks_per_seq, 0))] * 2
        args += list(rope_tabs)
    out_shape = [jax.ShapeDtypeStruct((n, ATTN_HEADS * HEAD_PAD), bf16),
                 jax.ShapeDtypeStruct((n, KV_HEADS * HEAD_DIM), f32),
                 jax.ShapeDtypeStruct((n, KV_HEADS * HEAD_DIM), f32),
                 jax.ShapeDtypeStruct((n, KV_HEADS * HEAD_PAD), bf16),
                 jax.ShapeDtypeStruct((n, RWKV_IN), f32)]
    out_specs = [pl.BlockSpec((tm, s.shape[1]), lambda i: (i, 0)) for s in out_shape]
    return pl.pallas_call(
        functools.partial(_in_kernel, rope=rope),
        grid=(n // tm,), in_specs=in_specs, out_specs=out_specs, out_shape=out_shape,
        compiler_params=_cparams("parallel"), name="in_proj",
    )(*args)


def _attn_kernel(q_ref, k_ref, vt_ref, o_ref):
    tq = q_ref.shape[0]
    def scores(kv):
        kh = k_ref[:, kv * HEAD_PAD:(kv + 1) * HEAD_PAD]
        q4 = jnp.concatenate(
            [q_ref[:, (GQA_GROUP * kv + g) * HEAD_PAD:(GQA_GROUP * kv + g + 1) * HEAD_PAD]
             for g in range(GQA_GROUP)], axis=0)
        return _dot(kh, q4, nt=True)

    def softmax(st):
        p = jnp.exp(st - jnp.max(st, axis=0, keepdims=True))
        return p.astype(bf16), jnp.sum(p, axis=0, keepdims=True)

    def values(kv, p, den):
        vt = vt_ref[kv * HEAD_DIM:(kv + 1) * HEAD_DIM, :]
        ot = _dot(vt, p) / den
        return [ot[:, g * tq:(g + 1) * tq] for g in range(GQA_GROUP)]

    st0 = scores(0)
    p0, den0 = softmax(st0)
    st1 = scores(1)
    heads_t = values(0, p0, den0)
    p1, den1 = softmax(st1)
    heads_t += values(1, p1, den1)
    o_ref[...] = jnp.concatenate(heads_t, axis=0).T.astype(bf16)


def _attention(q_pad, k_full, vt_full, batch, seq_len, tq):
    n = q_pad.shape[0]
    kv_len = k_full.shape[1]
    nq = seq_len // tq
    return pl.pallas_call(
        _attn_kernel,
        grid=(batch, nq),
        in_specs=[pl.BlockSpec((tq, ATTN_HEADS * HEAD_PAD), lambda b, j: (b * nq + j, 0)),
                  pl.BlockSpec((None, kv_len, KV_HEADS * HEAD_PAD), lambda b, j: (b, 0, 0)),
                  pl.BlockSpec((None, KV_HEADS * HEAD_DIM, kv_len), lambda b, j: (b, 0, 0))],
        out_specs=pl.BlockSpec((tq, ATTN_HEADS * HEAD_DIM), lambda b, j: (b * nq + j, 0)),
        out_shape=jax.ShapeDtypeStruct((n, ATTN_HEADS * HEAD_DIM), bf16),
        compiler_params=_cparams("parallel", "parallel"), name="attention",
    )(q_pad, k_full, vt_full)


def _prep_kernel(z_ref, zp_ref, zn_ref, mup_ref, mun_ref, w0_ref, w2_ref, a0_ref, a2_ref, g2_ref,
                 kkw_ref, ka_ref, rk_ref, seg_ref,
                 r_ref, k_ref, v_ref, kk_ref, lw_ref, a_ref, g_ref, bonus_ref, *, seq_len, tm):
    i = pl.program_id(0)
    z = z_ref[...]
    row = lax.broadcasted_iota(jnp.int32, (tm, 1), 0)
    seq_start = ((i * tm) % seq_len) == 0
    seq_end = (((i + 1) * tm) % seq_len) == 0
    prow = jnp.where(seq_start, 0.0, zp_ref[7:8, :])
    nrow = jnp.where(seq_end, 0.0, zn_ref[0:1, :])
    prev = jnp.where(row == 0, prow, pltpu.roll(z, 1, 0))
    nxt = jnp.where(row == tm - 1, nrow, pltpu.roll(z, tm - 1, 0))
    zs = z + mup_ref[...] * (prev - z) + mun_ref[...] * (nxt - z)
    w = RWKV_WIDTH
    r = zs[:, 0:w]
    k = zs[:, w:2 * w]
    v = zs[:, 2 * w:3 * w]
    dw = zs[:, 3 * w:3 * w + 2 * DECAY_RANK]
    da = zs[:, 3 * w + 2 * DECAY_RANK:3 * w + 2 * DECAY_RANK + 2 * ICLR_RANK]
    dg = zs[:, 3 * w + 2 * DECAY_RANK + 2 * ICLR_RANK:]
    lw = -DECAY_SCALE * _sigmoid(w0_ref[...] + _mm(jnp.tanh(dw), w2_ref[...], passes=1))
    a = _sigmoid(a0_ref[...] + _mm(da, a2_ref[...], passes=1))
    g = _mm(_sigmoid(dg), g2_ref[...], passes=1)
    kk = k * kkw_ref[...]
    seg = seg_ref[...]
    kk = kk * lax.rsqrt(_seg_sum(kk * kk, seg) + 1e-12)
    ka = ka_ref[...]
    ke_sum = k * (2.0 + (a[:, 0:w] + a[:, w:2 * w] - 2.0) * ka)
    bonus = _seg_sum(r * ke_sum * rk_ref[...], seg) * v
    r_ref[...] = r
    k_ref[...] = k
    v_ref[...] = v.astype(bf16)
    kk_ref[...] = kk
    lw_ref[...] = lw
    a_ref[...] = a
    g_ref[...] = g.astype(bf16)
    bonus_ref[...] = bonus.astype(bf16)


def _rwkv_prep(zr, p, seq_len, tm):
    n = zr.shape[0]
    w = RWKV_WIDTH
    nb8 = n // 8
    consts = [p['mu_prev'], p['mu_next'], p['w0'], p['w2bd'], p['a0'], p['a2bd'], p['g2'],
              p['k_k'], p['k_a'], p['r_k'], p['seg']]
    in_specs = [pl.BlockSpec((tm, RWKV_IN), lambda i: (i, 0)),
                pl.BlockSpec((8, RWKV_IN), lambda i: (jnp.maximum(i * (tm // 8) - 1, 0), 0)),
                pl.BlockSpec((8, RWKV_IN), lambda i: (jnp.minimum((i + 1) * (tm // 8), nb8 - 1), 0))]
    in_specs += [_full_spec(c) for c in consts]
    widths = [w, w, w, w, 2 * w, 2 * w, w, w]
    dtypes = [f32, f32, bf16, f32, f32, f32, bf16, bf16]
    out_shape = [jax.ShapeDtypeStruct((n, wd), dt) for wd, dt in zip(widths, dtypes)]
    out_specs = [pl.BlockSpec((tm, wd), lambda i: (i, 0)) for wd in widths]
    return pl.pallas_call(
        functools.partial(_prep_kernel, seq_len=seq_len, tm=tm),
        grid=(n // tm,), in_specs=in_specs, out_specs=out_specs, out_shape=out_shape,
        compiler_params=_cparams("parallel"), name="rwkv_prep",
    )(zr, zr, zr, *consts)


def _scan_kernel(rf_ref, kf_ref, vf_ref, kkf_ref, lwf_ref, af_ref,
                 rb_ref, kb_ref, vb_ref, kkb_ref, lwb_ref, ab_ref, ka_ref, *rest, has_init):
    s0_ref = rest[0] if has_init else None
    of_ref, ob_ref, sfin_ref, st_ref = rest[-4:]
    c = pl.program_id(1)

    @pl.when(c == 0)
    def _():
        st_ref[...] = s0_ref[...] if has_init else jnp.zeros_like(st_ref)

    n_seq = rf_ref.shape[0]
    row = lax.broadcasted_iota(jnp.int32, (CHUNK, CHUNK), 0)
    col = lax.broadcasted_iota(jnp.int32, (CHUNK, CHUNK), 1)
    qr = lax.broadcasted_iota(jnp.int32, (QUAD, QUAD), 0)
    qc = lax.broadcasted_iota(jnp.int32, (QUAD, QUAD), 1)
    head_mask = (qr & -RWKV_HEAD) == (qc & -RWKV_HEAD)
    head_mask_b = jnp.where(head_mask, 1.0, 0.0).astype(bf16)
    tr = lax.broadcasted_iota(jnp.int32, (CHUNK, QUAD), 0)
    tc = lax.broadcasted_iota(jnp.int32, (CHUNK, QUAD), 1) & (CHUNK - 1)
    eye = jnp.where(tr == tc, 1.0, 0.0).astype(f32)
    ka = ka_ref[...]

    def bd(x):
        xb = x.astype(bf16)
        return jnp.concatenate([xb] * QUAD_HEADS, axis=0) * head_mask_b

    def mm(a, b, nt=False):
        return _dot(a.astype(bf16), b.astype(bf16), nt)

    lane_lo = lax.broadcasted_iota(jnp.int32, (QUAD, LANE), 1) < RWKV_HEAD

    def first_stage(ch):
        ar = jnp.concatenate([ch['at'], ch['rt']], axis=0)
        ch['ar'] = ar.astype(bf16)
        bk_t = jnp.concatenate([ch['bh'], ch['kh']], axis=0).T
        swapped = pltpu.roll(bk_t, RWKV_HEAD, 1)
        b_t = jnp.where(lane_lo, bk_t, swapped).astype(bf16)
        k_t = jnp.where(lane_lo, swapped, bk_t).astype(bf16)
        w_bk = jnp.concatenate([jnp.concatenate([b_t, b_t], axis=1) * head_mask_b,
                                jnp.concatenate([k_t, k_t], axis=1) * head_mask_b], axis=1)
        amat = _dot(ch['ar'], w_bk)
        ch['n'] = jnp.where(ch['strict'], amat[0:CHUNK, 0:QUAD], 0.0)
        ch['a_kr'] = jnp.concatenate([jnp.where(ch['strict'], amat[0:CHUNK, QUAD:2 * QUAD], 0.0),
                                      jnp.where(ch['incl'], amat[CHUNK:2 * CHUNK, QUAD:2 * QUAD], 0.0)], axis=0)
        ch['a_rb'] = jnp.where(ch['incl'], amat[CHUNK:2 * CHUNK, 0:QUAD], 0.0)
        ch['tinv'] = eye + ch['n']

    chains = []
    for d, (r_ref, k_ref, v_ref, kk_ref, lw_ref, a_ref, o_ref) in enumerate(
            ((rf_ref, kf_ref, vf_ref, kkf_ref, lwf_ref, af_ref, of_ref),
             (rb_ref, kb_ref, vb_ref, kkb_ref, lwb_ref, ab_ref, ob_ref))):
        sgn = 1 - 2 * d
        tri = jnp.where((row - col) * sgn >= 0, 1.0, 0.0).astype(bf16)
        dist = (tr - tc) * sgn
        for s in range(n_seq):
            lw = lw_ref[s]
            l1, l2, l3 = _split3(lw)
            cum = _dot(tri, l1) + (_dot(tri, l2) + _dot(tri, l3))
            e_incl = jnp.exp(cum)
            e_excl = jnp.exp(cum - lw)
            e_neg = jnp.exp(-cum)
            wc = jnp.exp(jnp.sum(lw, axis=0, keepdims=True))
            kk = kk_ref[s]
            a = a_ref[s]
            bh = kk * a * e_neg
            kh = k_ref[s] * (1.0 + (a - 1.0) * ka) * e_neg
            ops = dict(at=-kk * e_excl, rt=r_ref[s] * e_incl, bh=bh, kh=kh, bw=bh * wc, kw=kh * wc,
                       v=v_ref[s], wc=wc)
            for g in range(N_QUADS):
                ch = {key: val[:, g * QUAD:(g + 1) * QUAD] for key, val in ops.items()}
                ch.update(s=s, d=d, g=g, o_ref=o_ref, strict=dist > 0, incl=dist >= 0, st=st_ref[s, d, g])
                chains.append(ch)

    for ch in chains:
        first_stage(ch)
    for ch in chains:
        ch['n'] = mm(ch['n'], bd(ch['n']))
    for _ in range(4):
        for ch in chains:
            both = mm(jnp.concatenate([ch['n'], ch['tinv']], axis=0), bd(ch['n']))
            ch['n'] = both[0:CHUNK]
            ch['tinv'] = ch['tinv'] + both[CHUNK:2 * CHUNK]
    for ch in chains:
        ch['tinv'] = ch['tinv'] + mm(ch['tinv'], bd(ch['n']))
        sa = _dot(ch['ar'], ch['st'].astype(bf16))
        av = mm(ch['a_kr'], bd(ch['v']))
        ch['rhs'] = sa[0:CHUNK] + av[0:CHUNK]
        ch['o'] = sa[CHUNK:2 * CHUNK] + av[CHUNK:2 * CHUNK]
    for ch in chains:
        ch['u'] = mm(ch['tinv'], bd(ch['rhs']))
    for ch in chains:
        ch['o'] = ch['o'] + mm(ch['a_rb'], bd(ch['u']))
        z_t = jnp.concatenate([ch['bw'], ch['kw'], jnp.broadcast_to(ch['wc'], (2 * CHUNK, QUAD))], axis=0).T
        upd = mm(z_t[:, 0:2 * CHUNK], jnp.concatenate([ch['u'].astype(bf16), ch['v']], axis=0))
        decay = jnp.concatenate([z_t[:, 2 * CHUNK:4 * CHUNK]] * 2, axis=1)
        ch['st_new'] = ch['st'] * decay + jnp.where(head_mask, upd, 0.0)
    for ch in chains:
        ch['o_ref'][ch['s'], :, ch['g'] * QUAD:(ch['g'] + 1) * QUAD] = ch['o']
        st_ref[ch['s'], ch['d'], ch['g']] = ch['st_new']

    @pl.when(c == pl.num_programs(1) - 1)
    def _():
        for s in range(n_seq):
            for d in range(2):
                for g in range(N_QUADS):
                    s_vk = st_ref[s, d, g].T
                    for h in range(QUAD_HEADS):
                        band = s_vk[h * RWKV_HEAD:(h + 1) * RWKV_HEAD, :]
                        if h:
                            band = pltpu.roll(band, QUAD - h * RWKV_HEAD, 1)
                        sfin_ref[s, d, g * QUAD_HEADS + h] = band[:, 0:RWKV_HEAD]


def _rwkv_scan(r, k, v, kk, lw, a, k_a, s0, batch, seq_len):
    n = r.shape[0]
    w = RWKV_WIDTH
    nc = seq_len // CHUNK
    per_seq = lambda x: x.reshape(batch, seq_len, x.shape[-1])
    r, k, v, kk, lw, a = (per_seq(x) for x in (r, k, v, kk, lw, a))
    fwd = lambda b, c: (b, c, 0)
    bwd = lambda b, c: (b, nc - 1 - c, 0)
    bwd_dir = lambda b, c: (b, nc - 1 - c, 1)
    blk = lambda index_map: pl.BlockSpec((SCAN_SEQS, CHUNK, w), index_map)
    state_spec = pl.BlockSpec((SCAN_SEQS, 2, N_QUADS, QUAD, QUAD), lambda b, c: (b, 0, 0, 0, 0))
    o_shape = jax.ShapeDtypeStruct((batch, seq_len, w), f32)
    init = [] if s0 is None else [s0]
    o_fw, o_bw, s_fin = pl.pallas_call(
        functools.partial(_scan_kernel, has_init=s0 is not None),
        grid=(batch // SCAN_SEQS, nc),
        in_specs=[blk(fwd)] * 6 + [blk(bwd)] * 4 + [blk(bwd_dir)] * 2
                 + [pl.BlockSpec((1, w), lambda b, c: (0, 0))] + [state_spec] * len(init),
        out_specs=[blk(fwd), blk(bwd),
                   pl.BlockSpec((SCAN_SEQS, 2, RWKV_HEADS, RWKV_HEAD, RWKV_HEAD), lambda b, c: (b, 0, 0, 0, 0))],
        out_shape=[o_shape, o_shape, jax.ShapeDtypeStruct((batch, 2, RWKV_HEADS, RWKV_HEAD, RWKV_HEAD), f32)],
        scratch_shapes=[pltpu.VMEM((SCAN_SEQS, 2, N_QUADS, QUAD, QUAD), f32)],
        compiler_params=_cparams("parallel", "arbitrary"), name="rwkv_scan",
    )(r, k, v, kk, lw, a, r, k, v, kk, lw, a, k_a, *init)
    return o_fw.reshape(n, w), o_bw.reshape(n, w), s_fin


def _out_kernel(attn_ref, of_ref, ob_ref, bonus_ref, g_ref, x_ref, mod_ref, lng_ref, lnb_ref, seg_ref,
                wo_ref, gpm_ref, gpf_ref, rwt_ref, rb_ref,
                x1_ref, h2_ref, slot_ref, sw_ref, cnt_ref):
    tiles = [pl.ds(t * TM_MOE, TM_MOE) for t in range(x_ref.shape[0] // TM_MOE)]
    fronts = [_mix_front(attn_ref.at[rows], of_ref.at[rows], ob_ref.at[rows], bonus_ref.at[rows],
                         g_ref.at[rows], x_ref.at[rows], mod_ref, lng_ref, lnb_ref, seg_ref, wo_ref,
                         gpm_ref, gpf_ref) for rows in tiles]
    picks = [_router_topk(h2, rwt_ref, rb_ref) for _, h2 in fronts]
    routes = [_router_slots(*pick) for pick in picks]
    for t, (rows, (x1, h2), (slots, sw, cnt)) in enumerate(zip(tiles, fronts, routes)):
        x1_ref[rows, :] = x1
        h2_ref[rows, :] = h2.astype(bf16)
        slot_ref[:, rows] = slots
        sw_ref[rows, :] = sw
        cnt_ref[t] = cnt


def _mix_front(attn_ref, of_ref, ob_ref, bonus_ref, g_ref, x_ref, mod_ref, lng_ref, lnb_ref, seg_ref,
               wo_ref, gpm_ref, gpf_ref):
    seg = seg_ref[...]
    inv = 1.0 / RWKV_HEAD
    o = of_ref[...] + ob_ref[...]
    mu = _seg_sum(o, seg) * inv
    dl = o - mu
    var = _seg_sum(dl * dl, seg) * inv
    on = dl * lax.rsqrt(var + GN_EPS) * lng_ref[...] + lnb_ref[...]
    rw = (on + bonus_ref[...]) * g_ref[...]
    half = ATTN_HEADS * HEAD_DIM
    y = _dot(attn_ref[...].astype(bf16), wo_ref[0:half, :]) + _dot(rw.astype(bf16), wo_ref[half:, :])
    m = mod_ref[...]
    gt_a = m[:, 2 * D_MODEL:3 * D_MODEL]
    sh_f = m[:, 3 * D_MODEL:4 * D_MODEL]
    sc_f = m[:, 4 * D_MODEL:5 * D_MODEL]
    x1 = x_ref[...] + gt_a * _rms(y, gpm_ref[...])
    h2 = _rms(x1, gpf_ref[...]) * (1.0 + sc_f) + sh_f
    return x1, h2


def _router_topk(h2, rwt_ref, rb_ref):
    tm = h2.shape[0]
    logits = _mm(rwt_ref[...], h2, nt=True)
    scores = _sigmoid(logits)
    biased = scores + rb_ref[...]
    neg = -jnp.inf
    shape3 = (N_GROUPS, GROUP_SIZE, tm)
    b3 = biased.reshape(shape3)
    s3 = scores.reshape(shape3)
    in_grp = lax.broadcasted_iota(jnp.int32, shape3, 1)
    grp = lax.broadcasted_iota(jnp.int32, shape3, 0)
    m1 = jnp.max(b3, axis=1, keepdims=True)
    i1 = jnp.min(jnp.where(b3 == m1, in_grp, GROUP_SIZE), axis=1, keepdims=True)
    m2 = jnp.max(jnp.where(in_grp == i1, neg, b3), axis=1, keepdims=True)
    gscore = m1 + m2
    gidx = lax.broadcasted_iota(jnp.int32, (N_GROUPS, 1, tm), 0)
    gsel = jnp.zeros((N_GROUPS, 1, tm), f32)
    cur = gscore
    for _ in range(TOPK_GROUPS):
        mx = jnp.max(cur, axis=0, keepdims=True)
        ii = jnp.min(jnp.where(cur == mx, gidx, N_GROUPS), axis=0, keepdims=True)
        hit = gidx == ii
        gsel = jnp.where(hit, 1.0, gsel)
        cur = jnp.where(hit, neg, cur)
    cand = jnp.where(jnp.broadcast_to(gsel, shape3) > 0.0, b3, neg)
    eidx = grp * GROUP_SIZE + in_grp
    wsel = jnp.zeros(shape3, f32)
    mask = jnp.zeros(shape3, f32)
    hits = []
    for _ in range(TOP_K):
        mx = jnp.max(jnp.max(cand, axis=1, keepdims=True), axis=0, keepdims=True)
        ii = jnp.min(jnp.min(jnp.where(cand == mx, eidx, N_EXPERTS), axis=1, keepdims=True),
                     axis=0, keepdims=True)
        hit = eidx == ii
        hits.append(hit)
        wsel = jnp.where(hit, s3, wsel)
        mask = jnp.where(hit, 1.0, mask)
        cand = jnp.where(hit, neg, cand)
    den = jnp.sum(jnp.sum(wsel, axis=1, keepdims=True), axis=0, keepdims=True)
    gates3 = wsel / den * ROUTED_SCALE
    return hits, gates3, mask


def _router_slots(hits, gates3, mask):
    tm = mask.shape[-1]
    shape3 = mask.shape
    mask2 = mask.reshape(N_EXPERTS, tm)
    cnt = jnp.sum(mask2, axis=1, keepdims=True)
    cnt_pad = jnp.floor((cnt + (SEG - 1)) * (1.0 / SEG)) * SEG
    er = lax.broadcasted_iota(jnp.int32, (N_EXPERTS, N_EXPERTS), 0)
    ec = lax.broadcasted_iota(jnp.int32, (N_EXPERTS, N_EXPERTS), 1)
    before = jnp.where(ec < er, 1.0, 0.0).astype(bf16)
    cnt_pad_l = jnp.broadcast_to(cnt_pad, (N_EXPERTS, LANE))
    seg_start = _dot(before, cnt_pad_l.astype(bf16))
    tr = lax.broadcasted_iota(jnp.int32, (tm, tm), 0)
    tc = lax.broadcasted_iota(jnp.int32, (tm, tm), 1)
    earlier = jnp.where(tr < tc, 1.0, 0.0).astype(bf16)
    rank = _dot(mask2.astype(bf16), earlier)
    slot3 = (rank + seg_start[:, 0:1]).reshape(shape3)

    def pick(hit, val3):
        return jnp.sum(jnp.sum(jnp.where(hit, val3, 0.0), axis=1, keepdims=True), axis=0)

    slots = [pick(h, slot3) for h in hits]
    wts = [pick(h, gates3) for h in hits]
    pad_rows = lambda k, val: [jnp.full((k, tm), val, f32)]
    slot_rows = jnp.concatenate(slots + pad_rows(8 - TOP_K, -1.0), axis=0).astype(jnp.int32)
    sw_t = jnp.concatenate(slots + pad_rows(8 - TOP_K, -1.0) + wts + pad_rows(LANE - 8 - TOP_K, 0.0), axis=0)
    return slot_rows, sw_t.T, cnt_pad_l


def _mix_out(attn, o_fw, o_bw, bonus, g, x2, mod3, mod_row, p, tm):
    n = x2.shape[0]
    w = RWKV_WIDTH
    consts = [p['lnx_g'], p['lnx_b'], p['seg'], p['w_out'], p['g_post_mix'], p['g_pre_ffn'],
              p['router_wt'], p['router_b']]
    row = lambda wd: pl.BlockSpec((tm, wd), lambda i: (i, 0))
    in_specs = [row(ATTN_HEADS * HEAD_DIM), row(w), row(w),
                row(w), row(w), row(D_MODEL),
                pl.BlockSpec((None, 1, 6 * D_MODEL), lambda i: (mod_row(i, tm), 0, 0))]
    in_specs += [_full_spec(c) for c in consts]
    out_shape = [jax.ShapeDtypeStruct((n, D_MODEL), f32),
                 jax.ShapeDtypeStruct((n, D_MODEL), bf16),
                 jax.ShapeDtypeStruct((8, n), jnp.int32),
                 jax.ShapeDtypeStruct((n, LANE), f32),
                 jax.ShapeDtypeStruct((n // TM_MOE, N_EXPERTS, LANE), f32)]
    out_specs = [row(D_MODEL), row(D_MODEL), pl.BlockSpec((8, tm), lambda i: (0, i)), row(LANE),
                 pl.BlockSpec((tm // TM_MOE, N_EXPERTS, LANE), lambda i: (i, 0, 0))]
    return pl.pallas_call(
        _out_kernel, grid=(n // tm,), in_specs=in_specs, out_specs=out_specs, out_shape=out_shape,
        compiler_params=_cparams("parallel"), name="mix_out",
    )(attn, o_fw, o_bw, bonus, g, x2, mod3, *consts)


def _repeat(lo, hi, body):
    def step(idx, carry):
        body(idx)
        return carry

    lax.fori_loop(lo, hi, step, 0)


def _copy_groups(tile, cnt_sm, kind):
    return lax.shift_right_logical(cnt_sm[TILE_COUNTS * tile + kind] + (CHUNK_GROUP - 1), CHUNK_GROUP_SHIFT)


def _start_chunks(tile, tab_sm, cnt_sm, make_copy):
    for kind, rows in enumerate(COPY_ROWS):
        base = tile * TABLE_WIDTH + sum(MAX_COPIES[:kind])

        def group(g, base=base, rows=rows):
            for u in range(CHUNK_GROUP):
                entry = tab_sm[base + g * CHUNK_GROUP + u]
                local_row = pl.multiple_of(jnp.bitwise_and(entry, (1 << LOC_BITS) - 1) * SEG, SEG)
                global_row = pl.multiple_of(lax.shift_right_logical(entry, LOC_BITS) * SEG, SEG)
                make_copy(local_row, global_row, rows).start()

        _repeat(0, _copy_groups(tile, cnt_sm, kind), group)


def _wait_chunks(tile, cnt_sm, make_copy):
    for kind, rows in enumerate(COPY_ROWS):
        def group(g, rows=rows):
            for _ in range(CHUNK_GROUP):
                make_copy(0, 0, rows).wait()

        _repeat(0, _copy_groups(tile, cnt_sm, kind), group)


def _dispatch_kernel(dst_sm, nch_sm, fill_sm, *refs, tile_starts):
    n_paths = len(tile_starts)
    h_refs, slot_refs = refs[:n_paths], refs[n_paths:2 * n_paths]
    xs_hbm, xc_ref, zero_ref, sems, zsem = refs[2 * n_paths:]
    i = pl.program_id(0)
    last = pl.num_programs(0) - 1
    buf = lax.rem(i, 2)

    def copy_from(b):
        def make(local_row, global_row, rows):
            return pltpu.make_async_copy(xc_ref.at[b, pl.ds(local_row, rows)],
                                         xs_hbm.at[pl.ds(global_row, rows)], sems.at[b])
        return make

    @pl.when(i == 0)
    def _():
        zero_ref[...] = jnp.zeros_like(zero_ref)
        xc_ref[...] = jnp.zeros_like(xc_ref)
        n_tiles = xs_hbm.shape[0] // FFN_ROWS

        def tail_copy(row):
            return pltpu.make_async_copy(zero_ref.at[pl.ds(0, SEG)], xs_hbm.at[pl.ds(row, SEG)], zsem)

        def tile_copy(t):
            return pltpu.make_async_copy(zero_ref, xs_hbm.at[pl.ds(pl.multiple_of(t * FFN_ROWS, FFN_ROWS), FFN_ROWS)],
                                         zsem)

        def per_expert(e):
            start = fill_sm[1 + e]
            chunks = lax.shift_right_logical(fill_sm[1 + N_EXPERTS + e], SEG_SHIFT)
            _repeat(0, chunks, lambda c: tail_copy(pl.multiple_of(start + c * SEG, SEG)).start())
            _repeat(0, chunks, lambda c: tail_copy(0).wait())

        _repeat(0, N_EXPERTS, per_expert)
        tile_copy(n_tiles - 1).start()
        tile_copy(0).wait()
        _repeat(fill_sm[0], n_tiles - 1, lambda t: tile_copy(t).start())

    @pl.when(i >= 2)
    def _():
        _wait_chunks(i - 2, nch_sm, copy_from(buf))

    slot, h = slot_refs[0][...], h_refs[0][...]
    for q in range(1, n_paths):
        mine = i >= tile_starts[q]
        slot = jnp.where(mine, slot_refs[q][...], slot)
        h = jnp.where(mine, h_refs[q][...], h)
    used = nch_sm[TILE_COUNTS * i + 2]

    def build(rb):
        rows = lax.broadcasted_iota(jnp.int32, (SEL_BLOCK, TM_MOE), 0) + rb * SEL_BLOCK
        sel = jnp.zeros((SEL_BLOCK, TM_MOE), f32)
        for j in range(TOP_K):
            sel = jnp.where(rows == slot[j:j + 1, :], 1.0, sel)
        xc_ref[buf, rb * SEL_BLOCK:(rb + 1) * SEL_BLOCK, :] = _dot(sel.astype(bf16), h).astype(bf16)

    for rb in range(COMMON_BLOCKS):
        build(rb)
    for rb in range(COMMON_BLOCKS, SLOT_ROWS // SEL_BLOCK):
        pl.when(rb * SEL_BLOCK < used)(functools.partial(build, rb))

    _start_chunks(i, dst_sm, nch_sm, copy_from(buf))

    @pl.when(i == last)
    def _():
        @pl.when(i >= 1)
        def _():
            _wait_chunks(i - 1, nch_sm, copy_from(1 - buf))
        _wait_chunks(i, nch_sm, copy_from(buf))
        n_tiles = xs_hbm.shape[0] // FFN_ROWS
        whole_tile = pltpu.make_async_copy(zero_ref, xs_hbm.at[pl.ds(0, FFN_ROWS)], zsem)
        _repeat(fill_sm[0], n_tiles - 1, lambda t: whole_tile.wait())


def _dispatch(h2s, slots, dst, nch, fill, total_rows):
    tiles = [h.shape[0] // TM_MOE for h in h2s]
    starts = [sum(tiles[:q]) for q in range(len(tiles))]

    def in_path(q):
        return lambda i: jnp.clip(i - starts[q], 0, tiles[q] - 1)

    h_specs = [pl.BlockSpec((TM_MOE, D_MODEL), lambda i, d, c, f, q=q: (in_path(q)(i), 0)) for q in range(len(tiles))]
    s_specs = [pl.BlockSpec((8, TM_MOE), lambda i, d, c, f, q=q: (0, in_path(q)(i))) for q in range(len(tiles))]
    grid_spec = pltpu.PrefetchScalarGridSpec(
        num_scalar_prefetch=3, grid=(sum(tiles),),
        in_specs=h_specs + s_specs,
        out_specs=pl.BlockSpec(memory_space=pl.ANY),
        scratch_shapes=[pltpu.VMEM((2, SLOT_ROWS, D_MODEL), bf16), pltpu.VMEM((FFN_ROWS, D_MODEL), bf16),
                        pltpu.SemaphoreType.DMA((2,)), pltpu.SemaphoreType.DMA(())])
    return pl.pallas_call(
        functools.partial(_dispatch_kernel, tile_starts=tuple(starts)), grid_spec=grid_spec,
        out_shape=jax.ShapeDtypeStruct((total_rows, D_MODEL), bf16),
        compiler_params=_cparams("arbitrary"), name="moe_dispatch",
    )(dst, nch, fill, *h2s, *slots)


def _ffn_kernel(te_sm, nv_sm, xs_ref, wg_ref, wu_ref, wd_ref, ys_ref, wg_b, wu_b, wd_b):
    r = pl.program_id(0)
    valid = r < nv_sm[0]

    @pl.when(jnp.logical_and(valid, jnp.logical_or(r == 0, te_sm[r] != te_sm[jnp.maximum(r - 1, 0)])))
    def _():
        wg_b[...] = wg_ref[...].astype(bf16)
        wu_b[...] = wu_ref[...].astype(bf16)
        wd_b[...] = wd_ref[...].astype(bf16)

    @pl.when(valid)
    def _():
        x = xs_ref[...]
        gg = _dot(x, wg_b[...])
        uu = _dot(x, wu_b[...])
        hm = gg * _sigmoid(gg) * uu
        ys_ref[...] = _dot(hm.astype(bf16), wd_b[...]).astype(bf16)

    @pl.when(r >= nv_sm[0])
    def _():
        ys_ref[...] = jnp.zeros_like(ys_ref)


def _expert_ffn(xs, tile_expert, n_valid, wg, wu, wd):
    total_rows = xs.shape[0]
    last = lambda r, nv: jnp.minimum(r, nv[0] - 1)
    grid_spec = pltpu.PrefetchScalarGridSpec(
        num_scalar_prefetch=2, grid=(total_rows // FFN_ROWS,),
        in_specs=[pl.BlockSpec((FFN_ROWS, D_MODEL), lambda r, te, nv: (last(r, nv), 0)),
                  pl.BlockSpec((None, D_MODEL, EXPERT_FF), lambda r, te, nv: (te[last(r, nv)], 0, 0)),
                  pl.BlockSpec((None, D_MODEL, EXPERT_FF), lambda r, te, nv: (te[last(r, nv)], 0, 0)),
                  pl.BlockSpec((None, EXPERT_FF, D_MODEL), lambda r, te, nv: (te[last(r, nv)], 0, 0))],
        out_specs=pl.BlockSpec((FFN_ROWS, D_MODEL), lambda r, te, nv: (r, 0)),
        scratch_shapes=[pltpu.VMEM((D_MODEL, EXPERT_FF), bf16), pltpu.VMEM((D_MODEL, EXPERT_FF), bf16),
                        pltpu.VMEM((EXPERT_FF, D_MODEL), bf16)])
    return pl.pallas_call(
        _ffn_kernel, grid_spec=grid_spec,
        out_shape=jax.ShapeDtypeStruct((total_rows, D_MODEL), bf16),
        compiler_params=_cparams("arbitrary"), name="moe_ffn",
    )(tile_expert, n_valid, xs, wg, wu, wd)


def _combine_kernel(dst_sm, nch_sm, sw_ref, h_ref, x1_ref, mod_ref, gpost_ref, swg_ref, swu_ref, swd_ref,
                    ys_hbm, y_ref, yc_ref, acc_ref, sems, *, tile_offset):
    step = pl.program_id(0)
    i = step + tile_offset
    buf = lax.rem(step, 2)

    def copy_into(b):
        def make(local_row, global_row, rows):
            return pltpu.make_async_copy(ys_hbm.at[pl.ds(global_row, rows)],
                                         yc_ref.at[b, pl.ds(local_row, rows)], sems.at[b])
        return make

    @pl.when(step == 0)
    def _():
        yc_ref[...] = jnp.zeros_like(yc_ref)
        _start_chunks(i, dst_sm, nch_sm, copy_into(buf))

    @pl.when(step + 1 < pl.num_programs(0))
    def _():
        _start_chunks(i + 1, dst_sm, nch_sm, copy_into(1 - buf))

    h = h_ref[...]
    gg = _dot(h, swg_ref[...])
    uu = _dot(h, swu_ref[...])
    acc_ref[...] = _dot((gg * _sigmoid(gg) * uu).astype(bf16), swd_ref[...])
    _wait_chunks(i, nch_sm, copy_into(buf))
    sw = sw_ref[...]
    used = nch_sm[TILE_COUNTS * i + 2]
    def weighted(rb):
        cols = (lax.broadcasted_iota(jnp.int32, (TM_MOE, SEL_BLOCK), 1) + rb * SEL_BLOCK).astype(f32)
        wmat = jnp.zeros((TM_MOE, SEL_BLOCK), f32)
        for j in range(TOP_K):
            wmat = jnp.where(cols == sw[:, j:j + 1], sw[:, 8 + j:9 + j], wmat)
        rows = lax.broadcasted_iota(jnp.int32, (SEL_BLOCK, 1), 0) + rb * SEL_BLOCK
        yc = jnp.where(rows < used, yc_ref[buf, rb * SEL_BLOCK:(rb + 1) * SEL_BLOCK, :], jnp.zeros((), bf16))
        return _dot(wmat.astype(bf16), yc)

    moe = acc_ref[...]
    for rb in range(COMMON_BLOCKS):
        moe = moe + weighted(rb)
    acc_ref[...] = moe
    for rb in range(COMMON_BLOCKS, SLOT_ROWS // SEL_BLOCK):
        @pl.when(rb * SEL_BLOCK < used)
        def _():
            acc_ref[...] += weighted(rb)
    gt_f = mod_ref[...][:, 5 * D_MODEL:6 * D_MODEL]
    y_ref[...] = x1_ref[...] + gt_f * _rms(acc_ref[...], gpost_ref[...])


def _combine(ys, dst, nch, sw, h2, x1, mod3, mod_row, g_post, swg, swu, swd, tile_offset):
    n = h2.shape[0]
    row = lambda wd: pl.BlockSpec((TM_MOE, wd), lambda i, d, c: (i, 0))
    const = lambda a: pl.BlockSpec(a.shape, lambda i, d, c: (0,) * a.ndim)
    grid_spec = pltpu.PrefetchScalarGridSpec(
        num_scalar_prefetch=2, grid=(n // TM_MOE,),
        in_specs=[row(LANE), row(D_MODEL), row(D_MODEL),
                  pl.BlockSpec((None, 1, 6 * D_MODEL), lambda i, d, c: (mod_row(i, TM_MOE), 0, 0)),
                  const(g_post), const(swg), const(swu), const(swd),
                  pl.BlockSpec(memory_space=pl.ANY)],
        out_specs=row(D_MODEL),
        scratch_shapes=[pltpu.VMEM((2, SLOT_ROWS, D_MODEL), bf16), pltpu.VMEM((TM_MOE, D_MODEL), f32),
                        pltpu.SemaphoreType.DMA((2,))])
    return pl.pallas_call(
        functools.partial(_combine_kernel, tile_offset=tile_offset), grid_spec=grid_spec,
        out_shape=jax.ShapeDtypeStruct((n, D_MODEL), f32),
        compiler_params=_cparams("arbitrary"), name="moe_combine",
    )(dst, nch, sw, h2, x1, mod3, g_post, swg, swu, swd, ys)


def _sparse_moe(paths, mod3, p):
    cnt = jnp.concatenate([q['cnt'][:, :, 0] for q in paths], axis=0).astype(jnp.int32)
    n_tiles = cnt.shape[0]
    before = jnp.cumsum(cnt, axis=0) - cnt
    rows_e = jnp.sum(cnt, axis=0)
    region = (rows_e + FFN_ROWS - 1) // FFN_ROWS * FFN_ROWS
    region_end = jnp.cumsum(region)
    region_start = region_end - region
    worst = n_tiles * (TOP_K * TM_MOE + N_EXPERTS * (SEG - 1)) + N_EXPERTS * (FFN_ROWS - SEG)
    total_rows = -(-worst // FFN_ROWS) * FFN_ROWS + FFN_ROWS
    spare_row = total_rows - FFN_ROWS
    first_row = jnp.arange(total_rows // FFN_ROWS, dtype=jnp.int32) * FFN_ROWS
    ended = jnp.sum((region_end[None, :] <= first_row[:, None]).astype(jnp.int32), axis=1)
    tile_expert = jnp.minimum(ended, N_EXPERTS - 1).astype(jnp.int32)
    n_valid = (region_end[-1:] // FFN_ROWS).astype(jnp.int32)

    seg_end = jnp.cumsum(cnt, axis=1)
    seg_start = seg_end - cnt
    seg_global = region_start[None, :] + before
    chunks = cnt // SEG
    pairs = chunks // 2
    tile_id = jnp.arange(n_tiles, dtype=jnp.int32)[:, None]
    experts = jnp.arange(N_EXPERTS, dtype=jnp.int32)[None, None, :]
    tables, counts = [], []
    spare_base = spare_row
    for kind, (rows, per_seg, first) in enumerate(((2 * SEG, pairs, jnp.zeros_like(pairs)),
                                                    (SEG, chunks - 2 * pairs, 2 * SEG * pairs))):
        ends = jnp.cumsum(per_seg, axis=1)
        idx = jnp.arange(MAX_COPIES[kind], dtype=jnp.int32)[None, :]
        seg_of = jnp.sum((ends[:, None, :] <= idx[:, :, None]).astype(jnp.int32), axis=2)
        own = seg_of[:, :, None] == experts
        pick = lambda a: jnp.sum(jnp.where(own, a[:, None, :], 0), axis=2)
        within = (idx - pick(ends - per_seg)) * rows
        local = pick(seg_start + first) + within
        dst = pick(seg_global + first) + within
        slot = (tile_id % 2) * CHUNK_GROUP + idx % CHUNK_GROUP
        is_copy = idx < ends[:, -1:]
        local = jnp.where(is_copy, local, PAD_LOCAL[kind] + (idx % CHUNK_GROUP) * rows)
        dst = jnp.where(is_copy, dst, spare_base + slot * rows)
        tables.append((dst // SEG) * (1 << LOC_BITS) + local // SEG)
        counts.append(ends[:, -1])
        spare_base = spare_base + 2 * CHUNK_GROUP * rows
    table = jnp.concatenate(tables, axis=1).astype(jnp.int32).reshape(-1)
    nch = jnp.stack(counts + [seg_end[:, -1]], axis=1).astype(jnp.int32).reshape(-1)
    dst = table

    fill = jnp.concatenate([n_valid, region_start + rows_e, region - rows_e]).astype(jnp.int32)
    xs = _dispatch([q['h2'] for q in paths], [q['slots'] for q in paths], dst, nch, fill, total_rows)
    ys = _expert_ffn(xs, tile_expert, n_valid, p['expert_wg'], p['expert_wu'], p['expert_wd'])
    outs, tile_offset = [], 0
    for q in paths:
        outs.append(_combine(ys, dst, nch, q['sw'], q['h2'], q['x1'], mod3, q['mod_row'],
                             p['g_post_ffn'], p['shared_wg'], p['shared_wu'], p['shared_wd'], tile_offset))
        tile_offset += q['h2'].shape[0] // TM_MOE
    return outs


def _pad_heads(w, heads):
    rows = w.shape[0]
    w = w.reshape(rows, heads, HEAD_DIM)
    return jnp.pad(w, ((0, 0), (0, 0), (0, HEAD_PAD - HEAD_DIM))).reshape(rows, heads * HEAD_PAD)


def _block_diag2(m):
    z = jnp.zeros_like(m[0])
    return jnp.concatenate([jnp.concatenate([m[0], z], axis=1), jnp.concatenate([z, m[1]], axis=1)], axis=0)


def _prepare_params(w):
    aw, kw = ATTN_HEADS * HEAD_DIM, KV_HEADS * HEAD_DIM
    w_in = w['w_in']
    lane_row = lambda v: v.reshape(1, -1).astype(f32)
    head_id = np.arange(RWKV_WIDTH) // RWKV_HEAD
    seg = jnp.asarray(head_id[:, None] == head_id[None, :], dtype=bf16)
    gain_pad = lambda gvec: jnp.pad(gvec, (0, HEAD_PAD - HEAD_DIM)).reshape(1, HEAD_PAD)
    return dict(
        g_pre_mix=lane_row(w['g_pre_mix']), g_post_mix=lane_row(w['g_post_mix']),
        g_pre_ffn=lane_row(w['g_pre_ffn']), g_post_ffn=lane_row(w['g_post_ffn']),
        wq=_pad_heads(w_in[:, :aw], ATTN_HEADS).astype(bf16),
        wk=_pad_heads(w_in[:, aw:aw + kw], KV_HEADS).astype(bf16),
        wv=w_in[:, aw + kw:aw + 2 * kw].astype(bf16),
        wr=w_in[:, aw + 2 * kw:].astype(bf16),
        qg=gain_pad(w['q_gain'] * ATTN_SCALE), kg=gain_pad(w['k_gain']),
        mu_prev=lane_row(w['mu_prev']), mu_next=lane_row(w['mu_next']),
        w0=lane_row(w['decay_w0']), w2bd=_block_diag2(w['decay_w2']),
        a0=lane_row(w['iclr_a0']), a2bd=_block_diag2(w['iclr_a2']),
        g2=w['gate_g2'], k_k=lane_row(w['k_k']), k_a=lane_row(w['k_a']), r_k=lane_row(w['r_k']),
        lnx_g=lane_row(w['lnx_g']), lnx_b=lane_row(w['lnx_b']), seg=seg,
        w_out=w['w_out'].astype(bf16),
        router_wt=w['router_w'].T, router_b=w['router_b'].reshape(N_EXPERTS, 1),
        expert_wg=w['expert_wg'], expert_wu=w['expert_wu'], expert_wd=w['expert_wd'],
        shared_wg=w['shared_wg'].astype(bf16), shared_wu=w['shared_wu'].astype(bf16),
        shared_wd=w['shared_wd'].astype(bf16),
    )


def _rope_tables(seq_len):
    rows = seq_len // GRID_W
    row_idx = jnp.repeat(jnp.arange(rows, dtype=jnp.int32), GRID_W)
    col_idx = jnp.tile(jnp.arange(GRID_W, dtype=jnp.int32), rows)
    inv = ROPE_THETA ** (-jnp.arange(0, ROPE_HALF, 2, dtype=f32) / ROPE_HALF)
    ang = jnp.stack([row_idx.astype(f32)[:, None] * inv, col_idx.astype(f32)[:, None] * inv], axis=1)
    cos, sin = jnp.cos(ang), jnp.sin(ang)
    cos_l = jnp.stack([cos, cos], axis=2).reshape(seq_len, HEAD_DIM)
    sin_l = jnp.stack([-sin, sin], axis=2).reshape(seq_len, HEAD_DIM)
    pad = ((0, 0), (0, HEAD_PAD - HEAD_DIM))
    return jnp.pad(cos_l, pad), jnp.pad(sin_l, pad)


def _states_to_pairs(s):
    b = s.shape[0]
    s = s.reshape(b, 2, N_QUADS, QUAD_HEADS, RWKV_HEAD, RWKV_HEAD)
    eye = jnp.eye(QUAD_HEADS, dtype=s.dtype)
    return jnp.einsum('bdphvk,hg->bdphkgv', s, eye).reshape(b, 2, N_QUADS, QUAD, QUAD)


def _trunk(x, mod3, mod_row, p, rope_tabs, past_k, past_v, s0_pairs, tm_in, tq, tm_prep):
    b, t, _ = x.shape
    n = b * t
    x2 = x.reshape(n, D_MODEL)
    q_pad, k_own, v_own, k_pad, zr = _in_proj(
        x2, mod3, mod_row, p['g_pre_mix'], p['wq'], p['wk'], p['wv'], p['wr'], p['qg'], p['kg'],
        rope_tabs, t, tm_in)
    k_full = k_pad.reshape(b, t, KV_HEADS * HEAD_PAD)
    v_full = v_own.reshape(b, t, KV_HEADS * HEAD_DIM)
    if past_k is not None:
        k_past = _pad_heads(past_k.reshape(-1, KV_HEADS * HEAD_DIM), KV_HEADS).astype(bf16)
        k_full = jnp.concatenate([k_past.reshape(b, -1, KV_HEADS * HEAD_PAD), k_full], axis=1)
        v_full = jnp.concatenate([past_v.reshape(b, -1, KV_HEADS * HEAD_DIM), v_full], axis=1)
    vt_full = jnp.swapaxes(v_full, 1, 2).astype(bf16)
    attn = _attention(q_pad, k_full, vt_full, b, t, tq)
    r, k, v, kk, lw, a, g, bonus = _rwkv_prep(zr, p, t, tm_prep)
    o_fw, o_bw, s_fin = _rwkv_scan(r, k, v, kk, lw, a, p['k_a'], s0_pairs, b, t)
    x1, h2, slots, sw, cnt = _mix_out(attn, o_fw, o_bw, bonus, g, x2, mod3, mod_row, p, MIX_TILES * TM_MOE)
    return dict(h2=h2, slots=slots, sw=sw, cnt=cnt, x1=x1, mod_row=mod_row), k_own, v_own, s_fin


def kernel(x_prompt, x_sample, c, cache_k, cache_v, state_rwkv, c_ctx, w_ada, b_ada, g_pre_mix, g_post_mix,
           g_pre_ffn, g_post_ffn, w_in, w_out, q_gain, k_gain, mu_prev, mu_next, decay_w0, decay_w2, iclr_a0,
           iclr_a2, gate_g2, k_k, k_a, r_k, lnx_g, lnx_b, router_w, router_b, expert_wg, expert_wu, expert_wd,
           shared_wg, shared_wu, shared_wd):
    layer = 0
    names = ('g_pre_mix g_post_mix g_pre_ffn g_post_ffn w_in w_out q_gain k_gain mu_prev mu_next decay_w0 '
             'decay_w2 iclr_a0 iclr_a2 gate_g2 k_k k_a r_k lnx_g lnx_b router_w router_b expert_wg expert_wu '
             'expert_wd shared_wg shared_wu shared_wd').split()
    vals = (g_pre_mix, g_post_mix, g_pre_ffn, g_post_ffn, w_in, w_out, q_gain, k_gain, mu_prev, mu_next,
            decay_w0, decay_w2, iclr_a0, iclr_a2, gate_g2, k_k, k_a, r_k, lnx_g, lnx_b, router_w, router_b,
            expert_wg, expert_wu, expert_wd, shared_wg, shared_wu, shared_wd)
    p = _prepare_params({nm: v[layer] for nm, v in zip(names, vals)})

    nb, ts, _ = x_sample.shape
    npb, tp, _ = x_prompt.shape
    assert ts % (MIX_TILES * TM_MOE) == 0 and ts % LATENT_TILES.in_rows == 0 and ts % GRID_W == 0
    assert ts % LATENT_TILES.attn_queries == 0 and ts % LATENT_TILES.prep_rows == 0
    assert (npb * tp) % CONTEXT_TILES.in_rows == 0 and tp % CONTEXT_TILES.prep_rows == 0
    assert nb % SCAN_SEQS == 0 and npb % SCAN_SEQS == 0 and (npb * tp) % (MIX_TILES * TM_MOE) == 0
    mod_rows = 16
    assert nb < mod_rows
    cc = jnp.zeros((mod_rows, D_MODEL), f32).at[0].set(c_ctx).at[1:1 + nb].set(c)
    mod3 = _modulation(cc, w_ada[layer], b_ada[layer]).reshape(mod_rows, 1, 6 * D_MODEL)

    moe_p, kc, vc, st = _trunk(x_prompt, mod3, lambda i, tm: 0, p, None, None, None, None,
                               tm_in=CONTEXT_TILES.in_rows, tq=tp, tm_prep=CONTEXT_TILES.prep_rows)
    s0_lat = _states_to_pairs(state_rwkv[:, layer])
    moe_s, _, _, _ = _trunk(x_sample, mod3, lambda i, tm: 1 + (i * tm) // ts, p, _rope_tables(ts),
                            cache_k[:, layer], cache_v[:, layer], s0_lat,
                            tm_in=LATENT_TILES.in_rows, tq=LATENT_TILES.attn_queries,
                            tm_prep=LATENT_TILES.prep_rows)
    y_p, y_s = _sparse_moe([moe_p, moe_s], mod3, p)
    y_p = y_p.reshape(x_prompt.shape)
    y_s = y_s.reshape(x_sample.shape)

    new_cache_k = kc.reshape(npb, 1, tp, KV_HEADS, HEAD_DIM)
    new_cache_v = vc.reshape(npb, 1, tp, KV_HEADS, HEAD_DIM)
    new_state = st[:, None]
    return (y_p, y_s, new_cache_k, new_cache_v, new_state)
```

```python
import functools
from typing import NamedTuple

import numpy as np
import jax
import jax.numpy as jnp
from jax import lax
from jax.experimental import pallas as pl
from jax.experimental.pallas import tpu as pltpu

f32 = jnp.float32
bf16 = jnp.bfloat16

D_MODEL = 1024
GRID_W = 64
HEAD_DIM = 64
ATTN_HEADS = 8
KV_HEADS = 2
GQA_GROUP = ATTN_HEADS // KV_HEADS
ATTN_SCALE = HEAD_DIM ** -0.5
ROPE_THETA = 10000.0
ROPE_HALF = HEAD_DIM // 2
ROPE_FREQS = ROPE_HALF // 2
RWKV_HEAD = 64
RWKV_HEADS = 8
RWKV_WIDTH = RWKV_HEADS * RWKV_HEAD
DECAY_RANK = 64
ICLR_RANK = 64
GATE_RANK = 128
RWKV_IN = 3 * RWKV_WIDTH + 2 * DECAY_RANK + 2 * ICLR_RANK + GATE_RANK
DECAY_SCALE = 0.606531
GN_EPS = 64e-5
N_EXPERTS = 64
TOP_K = 6
N_GROUPS = 8
GROUP_SIZE = N_EXPERTS // N_GROUPS
TOPK_GROUPS = 4
EXPERT_FF = 256
ROUTED_SCALE = 2.5
EPS = 1e-6

LANE = 128
HEAD_PAD = LANE
CHUNK = 64
QUAD_HEADS = 4
QUAD = QUAD_HEADS * RWKV_HEAD
N_QUADS = RWKV_HEADS // QUAD_HEADS
SCAN_SEQS = 4
VMEM_LIMIT = 56 * 1024 * 1024


class PathTiles(NamedTuple):
    in_rows: int
    attn_queries: int
    prep_rows: int


CONTEXT_TILES = PathTiles(in_rows=512, attn_queries=0, prep_rows=256)
LATENT_TILES = PathTiles(in_rows=512, attn_queries=256, prep_rows=512)
TM_MOE = 256
MIX_TILES = 4
IN_SUB = 256
SEG_SHIFT = 4
SEG = 1 << SEG_SHIFT
SLOT_DATA_ROWS = 2560
CHUNK_GROUP_SHIFT = 3
CHUNK_GROUP = 1 << CHUNK_GROUP_SHIFT
COPY_ROWS = (2 * SEG, SEG)
MAX_COPIES = (SLOT_DATA_ROWS // (2 * SEG), N_EXPERTS)
TABLE_WIDTH = sum(MAX_COPIES)
TILE_COUNTS = 3
LOC_BITS = 9
PAD_LOCAL = (SLOT_DATA_ROWS, SLOT_DATA_ROWS + CHUNK_GROUP * 2 * SEG)
SLOT_ROWS = SLOT_DATA_ROWS + 512
SEL_BLOCK = 512
COMMON_BLOCKS = 4
FFN_ROWS = 1024


def _cparams(*sem):
    return pltpu.CompilerParams(dimension_semantics=sem, vmem_limit_bytes=VMEM_LIMIT)


def _dot(a, b, nt=False):
    dims = (((1,), (1,)), ((), ())) if nt else (((1,), (0,)), ((), ()))
    return lax.dot_general(a, b, dims, preferred_element_type=f32)


def _split2(x):
    hi = x.astype(bf16)
    lo = (x - hi.astype(f32)).astype(bf16)
    return hi, lo


def _split3(x):
    h1 = x.astype(bf16)
    r1 = x - h1.astype(f32)
    h2 = r1.astype(bf16)
    h3 = (r1 - h2.astype(f32)).astype(bf16)
    return h1, h2, h3


def _mm(a, b, nt=False, passes=3):
    if passes == 1:
        return _dot(a.astype(bf16), b.astype(bf16), nt)
    ah, al = _split2(a)
    bh, bl = _split2(b)
    return _dot(ah, bh, nt) + (_dot(al, bh, nt) + _dot(ah, bl, nt))


def _seg_sum(x, seg_bf16):
    return _dot(x.astype(bf16), seg_bf16)


def _rms(x, g):
    return x * lax.rsqrt(jnp.mean(x * x, axis=-1, keepdims=True) + EPS) * g


def _sigmoid(x):
    return jax.nn.sigmoid(x)


def _full_spec(a, grid_rank=1):
    zeros = (0,) * a.ndim
    if grid_rank == 1:
        return pl.BlockSpec(a.shape, lambda i: zeros)
    if grid_rank == 2:
        return pl.BlockSpec(a.shape, lambda i, j: zeros)
    return pl.BlockSpec(a.shape, lambda i, j, k: zeros)


def _mod_kernel(c_ref, w_ref, b_ref, o_ref):
    c = c_ref[...]
    s = c * _sigmoid(c)
    o_ref[...] = _mm(s, w_ref[...]) + b_ref[...]


def _modulation(cc, w_ada, b_ada):
    rows, n = cc.shape[0], w_ada.shape[1]
    tn = 512
    return pl.pallas_call(
        _mod_kernel,
        grid=(n // tn,),
        in_specs=[pl.BlockSpec((rows, D_MODEL), lambda j: (0, 0)),
                  pl.BlockSpec((D_MODEL, tn), lambda j: (0, j)),
                  pl.BlockSpec((1, tn), lambda j: (0, j))],
        out_specs=pl.BlockSpec((rows, tn), lambda j: (0, j)),
        out_shape=jax.ShapeDtypeStruct((rows, n), f32),
        compiler_params=_cparams("parallel"),
        name="modulation",
    )(cc, w_ada, b_ada.reshape(1, n))


def _in_kernel(*refs, rope):
    if rope:
        (x_ref, mod_ref, g_ref, wq_ref, wk_ref, wv_ref, wr_ref, qg_ref, kg_ref, cos_ref, sin_ref,
         q_ref, ko_ref, vo_ref, kp_ref, zr_ref) = refs
    else:
        (x_ref, mod_ref, g_ref, wq_ref, wk_ref, wv_ref, wr_ref, qg_ref, kg_ref,
         q_ref, ko_ref, vo_ref, kp_ref, zr_ref) = refs
    m = mod_ref[...]
    lane = lax.broadcasted_iota(jnp.int32, (1, LANE), 1)
    first_half = (lane & (ROPE_HALF - 1)) < ROPE_FREQS
    subs = [pl.ds(t * IN_SUB, IN_SUB) for t in range(x_ref.shape[0] // IN_SUB)]

    def front(rows):
        h = _rms(x_ref[rows, :], g_ref[...])
        return (h * (1.0 + m[:, D_MODEL:2 * D_MODEL]) + m[:, 0:D_MODEL]).astype(bf16)

    def head_norm(z, gain):
        ms = jnp.sum(z * z, axis=-1, keepdims=True) * (1.0 / HEAD_DIM)
        return z * lax.rsqrt(ms + EPS) * gain

    def rotate(y, rows):
        if not rope:
            return y
        partner = jnp.where(first_half, pltpu.roll(y, LANE - ROPE_FREQS, 1), pltpu.roll(y, ROPE_FREQS, 1))
        return y * cos_ref[rows, :] + partner * sin_ref[rows, :]

    def epilogue(rows, zq, zk):
        q = jnp.concatenate(
            [rotate(head_norm(zq[:, hh * HEAD_PAD:(hh + 1) * HEAD_PAD], qg_ref[...]), rows).astype(bf16)
             for hh in range(ATTN_HEADS)], axis=1)
        kn = [head_norm(zk[:, hh * HEAD_PAD:(hh + 1) * HEAD_PAD], kg_ref[...]) for hh in range(KV_HEADS)]
        k_own = kn[0] + pltpu.roll(kn[1], HEAD_DIM, 1)
        k_rot = jnp.concatenate([rotate(kn[hh], rows).astype(bf16) for hh in range(KV_HEADS)], axis=1)
        return q, k_own, k_rot

    hbs = [front(rows) for rows in subs]
    zs = [(_dot(hb, wr_ref[...]), _dot(hb, wq_ref[...]), _dot(hb, wk_ref[...]), _dot(hb, wv_ref[...])) for hb in hbs]
    outs = [epilogue(rows, zq, zk) for rows, (_, zq, zk, _) in zip(subs, zs)]
    for rows, (zr, _, _, zv), (q, k_own, k_rot) in zip(subs, zs, outs):
        zr_ref[rows, :] = zr
        q_ref[rows, :] = q
        ko_ref[rows, :] = k_own
        vo_ref[rows, :] = zv
        kp_ref[rows, :] = k_rot


def _in_proj(x2, mod3, mod_row, g_pre, wq, wk, wv, wr, qg, kg, rope_tabs, seq_len, tm):
    n = x2.shape[0]
    rope = rope_tabs is not None
    in_specs = [pl.BlockSpec((tm, D_MODEL), lambda i: (i, 0)),
                pl.BlockSpec((None, 1, 6 * D_MODEL), lambda i: (mod_row(i, tm), 0, 0)),
                _full_spec(g_pre), _full_spec(wq), _full_spec(wk), _full_spec(wv), _full_spec(wr),
                _full_spec(qg), _full_spec(kg)]
    args = [x2, mod3, g_pre, wq, wk, wv, wr, qg, kg]
    if rope:
        blocks_per_seq = seq_len // tm
        in_specs += [pl.BlockSpec((tm, LANE), lambda i: (i % blocks_per_seq, 0))] * 2
        args += list(rope_tabs)
    out_shape = [jax.ShapeDtypeStruct((n, ATTN_HEADS * HEAD_PAD), bf16),
                 jax.ShapeDtypeStruct((n, KV_HEADS * HEAD_DIM), f32),
                 jax.ShapeDtypeStruct((n, KV_HEADS * HEAD_DIM), f32),
                 jax.ShapeDtypeStruct((n, KV_HEADS * HEAD_PAD), bf16),
                 jax.ShapeDtypeStruct((n, RWKV_IN), f32)]
    out_specs = [pl.BlockSpec((tm, s.shape[1]), lambda i: (i, 0)) for s in out_shape]
    return pl.pallas_call(
        functools.partial(_in_kernel, rope=rope),
        grid=(n // tm,), in_specs=in_specs, out_specs=out_specs, out_shape=out_shape,
        compiler_params=_cparams("parallel"), name="in_proj",
    )(*args)


def _attn_kernel(q_ref, k_ref, vt_ref, o_ref):
    tq = q_ref.shape[0]
    def scores(kv):
        kh = k_ref[:, kv * HEAD_PAD:(kv + 1) * HEAD_PAD]
        q4 = jnp.concatenate(
            [q_ref[:, (GQA_GROUP * kv + g) * HEAD_PAD:(GQA_GROUP * kv + g + 1) * HEAD_PAD]
             for g in range(GQA_GROUP)], axis=0)
        return _dot(kh, q4, nt=True)

    def softmax(st):
        p = jnp.exp(st - jnp.max(st, axis=0, keepdims=True))
        return p.astype(bf16), jnp.sum(p, axis=0, keepdims=True)

    def values(kv, p, den):
        vt = vt_ref[kv * HEAD_DIM:(kv + 1) * HEAD_DIM, :]
        ot = _dot(vt, p) / den
        return [ot[:, g * tq:(g + 1) * tq] for g in range(GQA_GROUP)]

    st0 = scores(0)
    p0, den0 = softmax(st0)
    st1 = scores(1)
    heads_t = values(0, p0, den0)
    p1, den1 = softmax(st1)
    heads_t += values(1, p1, den1)
    o_ref[...] = jnp.concatenate(heads_t, axis=0).T.astype(bf16)


def _attention(q_pad, k_full, vt_full, batch, seq_len, tq):
    n = q_pad.shape[0]
    kv_len = k_full.shape[1]
    nq = seq_len // tq
    return pl.pallas_call(
        _attn_kernel,
        grid=(batch, nq),
        in_specs=[pl.BlockSpec((tq, ATTN_HEADS * HEAD_PAD), lambda b, j: (b * nq + j, 0)),
                  pl.BlockSpec((None, kv_len, KV_HEADS * HEAD_PAD), lambda b, j: (b, 0, 0)),
                  pl.BlockSpec((None, KV_HEADS * HEAD_DIM, kv_len), lambda b, j: (b, 0, 0))],
        out_specs=pl.BlockSpec((tq, ATTN_HEADS * HEAD_DIM), lambda b, j: (b * nq + j, 0)),
        out_shape=jax.ShapeDtypeStruct((n, ATTN_HEADS * HEAD_DIM), bf16),
        compiler_params=_cparams("parallel", "parallel"), name="attention",
    )(q_pad, k_full, vt_full)


def _prep_kernel(z_ref, zp_ref, zn_ref, mup_ref, mun_ref, w0_ref, w2_ref, a0_ref, a2_ref, g2_ref,
                 kkw_ref, ka_ref, rk_ref, seg_ref,
                 r_ref, k_ref, v_ref, kk_ref, lw_ref, a_ref, g_ref, bonus_ref, *, seq_len, tm):
    i = pl.program_id(0)
    z = z_ref[...]
    row = lax.broadcasted_iota(jnp.int32, (tm, 1), 0)
    seq_start = ((i * tm) % seq_len) == 0
    seq_end = (((i + 1) * tm) % seq_len) == 0
    prow = jnp.where(seq_start, 0.0, zp_ref[7:8, :])
    nrow = jnp.where(seq_end, 0.0, zn_ref[0:1, :])
    prev = jnp.where(row == 0, prow, pltpu.roll(z, 1, 0))
    nxt = jnp.where(row == tm - 1, nrow, pltpu.roll(z, tm - 1, 0))
    zs = z + mup_ref[...] * (prev - z) + mun_ref[...] * (nxt - z)
    w = RWKV_WIDTH
    r = zs[:, 0:w]
    k = zs[:, w:2 * w]
    v = zs[:, 2 * w:3 * w]
    dw = zs[:, 3 * w:3 * w + 2 * DECAY_RANK]
    da = zs[:, 3 * w + 2 * DECAY_RANK:3 * w + 2 * DECAY_RANK + 2 * ICLR_RANK]
    dg = zs[:, 3 * w + 2 * DECAY_RANK + 2 * ICLR_RANK:]
    lw = -DECAY_SCALE * _sigmoid(w0_ref[...] + _mm(jnp.tanh(dw), w2_ref[...], passes=1))
    a = _sigmoid(a0_ref[...] + _mm(da, a2_ref[...], passes=1))
    g = _mm(_sigmoid(dg), g2_ref[...], passes=1)
    kk = k * kkw_ref[...]
    seg = seg_ref[...]
    kk = kk * lax.rsqrt(_seg_sum(kk * kk, seg) + 1e-12)
    ka = ka_ref[...]
    ke_sum = k * (2.0 + (a[:, 0:w] + a[:, w:2 * w] - 2.0) * ka)
    bonus = _seg_sum(r * ke_sum * rk_ref[...], seg) * v
    r_ref[...] = r
    k_ref[...] = k
    v_ref[...] = v.astype(bf16)
    kk_ref[...] = kk
    lw_ref[...] = lw
    a_ref[...] = a
    g_ref[...] = g.astype(bf16)
    bonus_ref[...] = bonus.astype(bf16)


def _rwkv_prep(zr, p, seq_len, tm):
    n = zr.shape[0]
    w = RWKV_WIDTH
    nb8 = n // 8
    consts = [p['mu_prev'], p['mu_next'], p['w0'], p['w2bd'], p['a0'], p['a2bd'], p['g2'],
              p['k_k'], p['k_a'], p['r_k'], p['seg']]
    in_specs = [pl.BlockSpec((tm, RWKV_IN), lambda i: (i, 0)),
                pl.BlockSpec((8, RWKV_IN), lambda i: (jnp.maximum(i * (tm // 8) - 1, 0), 0)),
                pl.BlockSpec((8, RWKV_IN), lambda i: (jnp.minimum((i + 1) * (tm // 8), nb8 - 1), 0))]
    in_specs += [_full_spec(c) for c in consts]
    widths = [w, w, w, w, 2 * w, 2 * w, w, w]
    dtypes = [f32, f32, bf16, f32, f32, f32, bf16, bf16]
    out_shape = [jax.ShapeDtypeStruct((n, wd), dt) for wd, dt in zip(widths, dtypes)]
    out_specs = [pl.BlockSpec((tm, wd), lambda i: (i, 0)) for wd in widths]
    return pl.pallas_call(
        functools.partial(_prep_kernel, seq_len=seq_len, tm=tm),
        grid=(n // tm,), in_specs=in_specs, out_specs=out_specs, out_shape=out_shape,
        compiler_params=_cparams("parallel"), name="rwkv_prep",
    )(zr, zr, zr, *consts)


def _scan_kernel(rf_ref, kf_ref, vf_ref, kkf_ref, lwf_ref, af_ref,
                 rb_ref, kb_ref, vb_ref, kkb_ref, lwb_ref, ab_ref, ka_ref, *rest, has_init):
    s0_ref = rest[0] if has_init else None
    of_ref, ob_ref, sfin_ref, st_ref = rest[-4:]
    c = pl.program_id(1)

    @pl.when(c == 0)
    def _():
        st_ref[...] = s0_ref[...] if has_init else jnp.zeros_like(st_ref)

    n_seq = rf_ref.shape[0]
    row = lax.broadcasted_iota(jnp.int32, (CHUNK, CHUNK), 0)
    col = lax.broadcasted_iota(jnp.int32, (CHUNK, CHUNK), 1)
    qr = lax.broadcasted_iota(jnp.int32, (QUAD, QUAD), 0)
    qc = lax.broadcasted_iota(jnp.int32, (QUAD, QUAD), 1)
    head_mask = (qr & -RWKV_HEAD) == (qc & -RWKV_HEAD)
    head_mask_b = jnp.where(head_mask, 1.0, 0.0).astype(bf16)
    tr = lax.broadcasted_iota(jnp.int32, (CHUNK, QUAD), 0)
    tc = lax.broadcasted_iota(jnp.int32, (CHUNK, QUAD), 1) & (CHUNK - 1)
    eye = jnp.where(tr == tc, 1.0, 0.0).astype(f32)
    ka = ka_ref[...]

    def bd(x):
        xb = x.astype(bf16)
        return jnp.concatenate([xb] * QUAD_HEADS, axis=0) * head_mask_b

    def mm(a, b, nt=False):
        return _dot(a.astype(bf16), b.astype(bf16), nt)

    lane_lo = lax.broadcasted_iota(jnp.int32, (QUAD, LANE), 1) < RWKV_HEAD

    def first_stage(ch):
        ar = jnp.concatenate([ch['at'], ch['rt']], axis=0)
        ch['ar'] = ar.astype(bf16)
        bk_t = jnp.concatenate([ch['bh'], ch['kh']], axis=0).T
        swapped = pltpu.roll(bk_t, RWKV_HEAD, 1)
        b_t = jnp.where(lane_lo, bk_t, swapped).astype(bf16)
        k_t = jnp.where(lane_lo, swapped, bk_t).astype(bf16)
        w_bk = jnp.concatenate([jnp.concatenate([b_t, b_t], axis=1) * head_mask_b,
                                jnp.concatenate([k_t, k_t], axis=1) * head_mask_b], axis=1)
        amat = _dot(ch['ar'], w_bk)
        ch['n'] = jnp.where(ch['strict'], amat[0:CHUNK, 0:QUAD], 0.0)
        ch['a_kr'] = jnp.concatenate([jnp.where(ch['strict'], amat[0:CHUNK, QUAD:2 * QUAD], 0.0),
                                      jnp.where(ch['incl'], amat[CHUNK:2 * CHUNK, QUAD:2 * QUAD], 0.0)], axis=0)
        ch['a_rb'] = jnp.where(ch['incl'], amat[CHUNK:2 * CHUNK, 0:QUAD], 0.0)
        ch['tinv'] = eye + ch['n']

    chains = []
    for d, (r_ref, k_ref, v_ref, kk_ref, lw_ref, a_ref, o_ref) in enumerate(
            ((rf_ref, kf_ref, vf_ref, kkf_ref, lwf_ref, af_ref, of_ref),
             (rb_ref, kb_ref, vb_ref, kkb_ref, lwb_ref, ab_ref, ob_ref))):
        sgn = 1 - 2 * d
        tri = jnp.where((row - col) * sgn >= 0, 1.0, 0.0).astype(bf16)
        dist = (tr - tc) * sgn
        for s in range(n_seq):
            lw = lw_ref[s]
            l1, l2, l3 = _split3(lw)
            cum = _dot(tri, l1) + (_dot(tri, l2) + _dot(tri, l3))
            e_incl = jnp.exp(cum)
            e_excl = jnp.exp(cum - lw)
            e_neg = jnp.exp(-cum)
            wc = jnp.exp(jnp.sum(lw, axis=0, keepdims=True))
            kk = kk_ref[s]
            a = a_ref[s]
            bh = kk * a * e_neg
            kh = k_ref[s] * (1.0 + (a - 1.0) * ka) * e_neg
            ops = dict(at=-kk * e_excl, rt=r_ref[s] * e_incl, bh=bh, kh=kh, bw=bh * wc, kw=kh * wc,
                       v=v_ref[s], wc=wc)
            for g in range(N_QUADS):
                ch = {key: val[:, g * QUAD:(g + 1) * QUAD] for key, val in ops.items()}
                ch.update(s=s, d=d, g=g, o_ref=o_ref, strict=dist > 0, incl=dist >= 0, st=st_ref[s, d, g])
                chains.append(ch)

    for ch in chains:
        first_stage(ch)
    for ch in chains:
        ch['n'] = mm(ch['n'], bd(ch['n']))
    for _ in range(4):
        for ch in chains:
            both = mm(jnp.concatenate([ch['n'], ch['tinv']], axis=0), bd(ch['n']))
            ch['n'] = both[0:CHUNK]
            ch['tinv'] = ch['tinv'] + both[CHUNK:2 * CHUNK]
    for ch in chains:
        ch['tinv'] = ch['tinv'] + mm(ch['tinv'], bd(ch['n']))
        sa = _dot(ch['ar'], ch['st'].astype(bf16))
        av = mm(ch['a_kr'], bd(ch['v']))
        ch['rhs'] = sa[0:CHUNK] + av[0:CHUNK]
        ch['o'] = sa[CHUNK:2 * CHUNK] + av[CHUNK:2 * CHUNK]
    for ch in chains:
        ch['u'] = mm(ch['tinv'], bd(ch['rhs']))
    for ch in chains:
        ch['o'] = ch['o'] + mm(ch['a_rb'], bd(ch['u']))
        z_t = jnp.concatenate([ch['bw'], ch['kw'], jnp.broadcast_to(ch['wc'], (2 * CHUNK, QUAD))], axis=0).T
        upd = mm(z_t[:, 0:2 * CHUNK], jnp.concatenate([ch['u'].astype(bf16), ch['v']], axis=0))
        decay = jnp.concatenate([z_t[:, 2 * CHUNK:4 * CHUNK]] * 2, axis=1)
        ch['st_new'] = ch['st'] * decay + jnp.where(head_mask, upd, 0.0)
    for ch in chains:
        ch['o_ref'][ch['s'], :, ch['g'] * QUAD:(ch['g'] + 1) * QUAD] = ch['o']
        st_ref[ch['s'], ch['d'], ch['g']] = ch['st_new']

    @pl.when(c == pl.num_programs(1) - 1)
    def _():
        for s in range(n_seq):
            for d in range(2):
                for g in range(N_QUADS):
                    s_vk = st_ref[s, d, g].T
                    for h in range(QUAD_HEADS):
                        band = s_vk[h * RWKV_HEAD:(h + 1) * RWKV_HEAD, :]
                        if h:
                            band = pltpu.roll(band, QUAD - h * RWKV_HEAD, 1)
                        sfin_ref[s, d, g * QUAD_HEADS + h] = band[:, 0:RWKV_HEAD]


def _rwkv_scan(r, k, v, kk, lw, a, k_a, s0, batch, seq_len):
    n = r.shape[0]
    w = RWKV_WIDTH
    nc = seq_len // CHUNK
    per_seq = lambda x: x.reshape(batch, seq_len, x.shape[-1])
    r, k, v, kk, lw, a = (per_seq(x) for x in (r, k, v, kk, lw, a))
    fwd = lambda b, c: (b, c, 0)
    bwd = lambda b, c: (b, nc - 1 - c, 0)
    bwd_dir = lambda b, c: (b, nc - 1 - c, 1)
    blk = lambda index_map: pl.BlockSpec((SCAN_SEQS, CHUNK, w), index_map)
    state_spec = pl.BlockSpec((SCAN_SEQS, 2, N_QUADS, QUAD, QUAD), lambda b, c: (b, 0, 0, 0, 0))
    o_shape = jax.ShapeDtypeStruct((batch, seq_len, w), f32)
    init = [] if s0 is None else [s0]
    o_fw, o_bw, s_fin = pl.pallas_call(
        functools.partial(_scan_kernel, has_init=s0 is not None),
        grid=(batch // SCAN_SEQS, nc),
        in_specs=[blk(fwd)] * 6 + [blk(bwd)] * 4 + [blk(bwd_dir)] * 2
                 + [pl.BlockSpec((1, w), lambda b, c: (0, 0))] + [state_spec] * len(init),
        out_specs=[blk(fwd), blk(bwd),
                   pl.BlockSpec((SCAN_SEQS, 2, RWKV_HEADS, RWKV_HEAD, RWKV_HEAD), lambda b, c: (b, 0, 0, 0, 0))],
        out_shape=[o_shape, o_shape, jax.ShapeDtypeStruct((batch, 2, RWKV_HEADS, RWKV_HEAD, RWKV_HEAD), f32)],
        scratch_shapes=[pltpu.VMEM((SCAN_SEQS, 2, N_QUADS, QUAD, QUAD), f32)],
        compiler_params=_cparams("parallel", "arbitrary"), name="rwkv_scan",
    )(r, k, v, kk, lw, a, r, k, v, kk, lw, a, k_a, *init)
    return o_fw.reshape(n, w), o_bw.reshape(n, w), s_fin


def _out_kernel(attn_ref, of_ref, ob_ref, bonus_ref, g_ref, x_ref, mod_ref, lng_ref, lnb_ref, seg_ref,
                wo_ref, gpm_ref, gpf_ref, rwt_ref, rb_ref,
                x1_ref, h2_ref, slot_ref, sw_ref, cnt_ref):
    tiles = [pl.ds(t * TM_MOE, TM_MOE) for t in range(x_ref.shape[0] // TM_MOE)]
    fronts = [_mix_front(attn_ref.at[rows], of_ref.at[rows], ob_ref.at[rows], bonus_ref.at[rows],
                         g_ref.at[rows], x_ref.at[rows], mod_ref, lng_ref, lnb_ref, seg_ref, wo_ref,
                         gpm_ref, gpf_ref) for rows in tiles]
    picks = [_router_topk(h2, rwt_ref, rb_ref) for _, h2 in fronts]
    routes = [_router_slots(*pick) for pick in picks]
    for t, (rows, (x1, h2), (slots, sw, cnt)) in enumerate(zip(tiles, fronts, routes)):
        x1_ref[rows, :] = x1
        h2_ref[rows, :] = h2.astype(bf16)
        slot_ref[:, rows] = slots
        sw_ref[rows, :] = sw
        cnt_ref[t] = cnt


def _mix_front(attn_ref, of_ref, ob_ref, bonus_ref, g_ref, x_ref, mod_ref, lng_ref, lnb_ref, seg_ref,
               wo_ref, gpm_ref, gpf_ref):
    seg = seg_ref[...]
    inv = 1.0 / RWKV_HEAD
    o = of_ref[...] + ob_ref[...]
    mu = _seg_sum(o, seg) * inv
    dl = o - mu
    var = _seg_sum(dl * dl, seg) * inv
    on = dl * lax.rsqrt(var + GN_EPS) * lng_ref[...] + lnb_ref[...]
    rw = (on + bonus_ref[...]) * g_ref[...]
    half = ATTN_HEADS * HEAD_DIM
    y = _dot(attn_ref[...].astype(bf16), wo_ref[0:half, :]) + _dot(rw.astype(bf16), wo_ref[half:, :])
    m = mod_ref[...]
    gt_a = m[:, 2 * D_MODEL:3 * D_MODEL]
    sh_f = m[:, 3 * D_MODEL:4 * D_MODEL]
    sc_f = m[:, 4 * D_MODEL:5 * D_MODEL]
    x1 = x_ref[...] + gt_a * _rms(y, gpm_ref[...])
    h2 = _rms(x1, gpf_ref[...]) * (1.0 + sc_f) + sh_f
    return x1, h2


def _router_topk(h2, rwt_ref, rb_ref):
    tm = h2.shape[0]
    logits = _mm(rwt_ref[...], h2, nt=True)
    scores = _sigmoid(logits)
    biased = scores + rb_ref[...]
    neg = -jnp.inf
    shape3 = (N_GROUPS, GROUP_SIZE, tm)
    b3 = biased.reshape(shape3)
    s3 = scores.reshape(shape3)
    in_grp = lax.broadcasted_iota(jnp.int32, shape3, 1)
    grp = lax.broadcasted_iota(jnp.int32, shape3, 0)
    m1 = jnp.max(b3, axis=1, keepdims=True)
    i1 = jnp.min(jnp.where(b3 == m1, in_grp, GROUP_SIZE), axis=1, keepdims=True)
    m2 = jnp.max(jnp.where(in_grp == i1, neg, b3), axis=1, keepdims=True)
    gscore = m1 + m2
    gidx = lax.broadcasted_iota(jnp.int32, (N_GROUPS, 1, tm), 0)
    gsel = jnp.zeros((N_GROUPS, 1, tm), f32)
    cur = gscore
    for _ in range(TOPK_GROUPS):
        mx = jnp.max(cur, axis=0, keepdims=True)
        ii = jnp.min(jnp.where(cur == mx, gidx, N_GROUPS), axis=0, keepdims=True)
        hit = gidx == ii
        gsel = jnp.where(hit, 1.0, gsel)
        cur = jnp.where(hit, neg, cur)
    cand = jnp.where(jnp.broadcast_to(gsel, shape3) > 0.0, b3, neg)
    eidx = grp * GROUP_SIZE + in_grp
    wsel = jnp.zeros(shape3, f32)
    mask = jnp.zeros(shape3, f32)
    hits = []
    for _ in range(TOP_K):
        mx = jnp.max(jnp.max(cand, axis=1, keepdims=True), axis=0, keepdims=True)
        ii = jnp.min(jnp.min(jnp.where(cand == mx, eidx, N_EXPERTS), axis=1, keepdims=True),
                     axis=0, keepdims=True)
        hit = eidx == ii
        hits.append(hit)
        wsel = jnp.where(hit, s3, wsel)
        mask = jnp.where(hit, 1.0, mask)
        cand = jnp.where(hit, neg, cand)
    den = jnp.sum(jnp.sum(wsel, axis=1, keepdims=True), axis=0, keepdims=True)
    gates3 = wsel / den * ROUTED_SCALE
    return hits, gates3, mask


def _router_slots(hits, gates3, mask):
    tm = mask.shape[-1]
    shape3 = mask.shape
    mask2 = mask.reshape(N_EXPERTS, tm)
    cnt = jnp.sum(mask2, axis=1, keepdims=True)
    cnt_pad = jnp.floor((cnt + (SEG - 1)) * (1.0 / SEG)) * SEG
    er = lax.broadcasted_iota(jnp.int32, (N_EXPERTS, N_EXPERTS), 0)
    ec = lax.broadcasted_iota(jnp.int32, (N_EXPERTS, N_EXPERTS), 1)
    before = jnp.where(ec < er, 1.0, 0.0).astype(bf16)
    cnt_pad_l = jnp.broadcast_to(cnt_pad, (N_EXPERTS, LANE))
    seg_start = _dot(before, cnt_pad_l.astype(bf16))
    tr = lax.broadcasted_iota(jnp.int32, (tm, tm), 0)
    tc = lax.broadcasted_iota(jnp.int32, (tm, tm), 1)
    earlier = jnp.where(tr < tc, 1.0, 0.0).astype(bf16)
    rank = _dot(mask2.astype(bf16), earlier)
    slot3 = (rank + seg_start[:, 0:1]).reshape(shape3)

    def pick(hit, val3):
        return jnp.sum(jnp.sum(jnp.where(hit, val3, 0.0), axis=1, keepdims=True), axis=0)

    slots = [pick(h, slot3) for h in hits]
    wts = [pick(h, gates3) for h in hits]
    pad_rows = lambda k, val: [jnp.full((k, tm), val, f32)]
    slot_rows = jnp.concatenate(slots + pad_rows(8 - TOP_K, -1.0), axis=0).astype(jnp.int32)
    sw_t = jnp.concatenate(slots + pad_rows(8 - TOP_K, -1.0) + wts + pad_rows(LANE - 8 - TOP_K, 0.0), axis=0)
    return slot_rows, sw_t.T, cnt_pad_l


def _mix_out(attn, o_fw, o_bw, bonus, g, x2, mod3, mod_row, p, tm):
    n = x2.shape[0]
    w = RWKV_WIDTH
    consts = [p['lnx_g'], p['lnx_b'], p['seg'], p['w_out'], p['g_post_mix'], p['g_pre_ffn'],
              p['router_wt'], p['router_b']]
    row = lambda wd: pl.BlockSpec((tm, wd), lambda i: (i, 0))
    in_specs = [row(ATTN_HEADS * HEAD_DIM), row(w), row(w),
                row(w), row(w), row(D_MODEL),
                pl.BlockSpec((None, 1, 6 * D_MODEL), lambda i: (mod_row(i, tm), 0, 0))]
    in_specs += [_full_spec(c) for c in consts]
    out_shape = [jax.ShapeDtypeStruct((n, D_MODEL), f32),
                 jax.ShapeDtypeStruct((n, D_MODEL), bf16),
                 jax.ShapeDtypeStruct((8, n), jnp.int32),
                 jax.ShapeDtypeStruct((n, LANE), f32),
                 jax.ShapeDtypeStruct((n // TM_MOE, N_EXPERTS, LANE), f32)]
    out_specs = [row(D_MODEL), row(D_MODEL), pl.BlockSpec((8, tm), lambda i: (0, i)), row(LANE),
                 pl.BlockSpec((tm // TM_MOE, N_EXPERTS, LANE), lambda i: (i, 0, 0))]
    return pl.pallas_call(
        _out_kernel, grid=(n // tm,), in_specs=in_specs, out_specs=out_specs, out_shape=out_shape,
        compiler_params=_cparams("parallel"), name="mix_out",
    )(attn, o_fw, o_bw, bonus, g, x2, mod3, *consts)


def _repeat(lo, hi, body):
    def step(idx, carry):
        body(idx)
        return carry

    lax.fori_loop(lo, hi, step, 0)


def _copy_groups(tile, cnt_sm, kind):
    return lax.shift_right_logical(cnt_sm[TILE_COUNTS * tile + kind] + (CHUNK_GROUP - 1), CHUNK_GROUP_SHIFT)


def _start_chunks(tile, tab_sm, cnt_sm, make_copy):
    for kind, rows in enumerate(COPY_ROWS):
        base = tile * TABLE_WIDTH + sum(MAX_COPIES[:kind])

        def group(g, base=base, rows=rows):
            for u in range(CHUNK_GROUP):
                entry = tab_sm[base + g * CHUNK_GROUP + u]
                local_row = pl.multiple_of(jnp.bitwise_and(entry, (1 << LOC_BITS) - 1) * SEG, SEG)
                global_row = pl.multiple_of(lax.shift_right_logical(entry, LOC_BITS) * SEG, SEG)
                make_copy(local_row, global_row, rows).start(priority=u % 2)

        _repeat(0, _copy_groups(tile, cnt_sm, kind), group)


def _wait_chunks(tile, cnt_sm, make_copy):
    for kind, rows in enumerate(COPY_ROWS):
        def group(g, rows=rows):
            for _ in range(CHUNK_GROUP):
                make_copy(0, 0, rows).wait()

        _repeat(0, _copy_groups(tile, cnt_sm, kind), group)


def _dispatch_kernel(dst_sm, nch_sm, fill_sm, *refs, tile_starts):
    n_paths = len(tile_starts)
    h_refs, slot_refs = refs[:n_paths], refs[n_paths:2 * n_paths]
    xs_hbm, xc_ref, zero_ref, sems, zsem = refs[2 * n_paths:]
    i = pl.program_id(0)
    last = pl.num_programs(0) - 1
    buf = lax.rem(i, 2)

    def copy_from(b):
        def make(local_row, global_row, rows):
            return pltpu.make_async_copy(xc_ref.at[b, pl.ds(local_row, rows)],
                                         xs_hbm.at[pl.ds(global_row, rows)], sems.at[b])
        return make

    @pl.when(i == 0)
    def _():
        zero_ref[...] = jnp.zeros_like(zero_ref)
        xc_ref[...] = jnp.zeros_like(xc_ref)
        n_tiles = xs_hbm.shape[0] // FFN_ROWS

        def tail_copy(row):
            return pltpu.make_async_copy(zero_ref.at[pl.ds(0, SEG)], xs_hbm.at[pl.ds(row, SEG)], zsem)

        def tile_copy(t):
            return pltpu.make_async_copy(zero_ref, xs_hbm.at[pl.ds(pl.multiple_of(t * FFN_ROWS, FFN_ROWS), FFN_ROWS)],
                                         zsem)

        def per_expert(e):
            start = fill_sm[1 + e]
            chunks = lax.shift_right_logical(fill_sm[1 + N_EXPERTS + e], SEG_SHIFT)
            _repeat(0, chunks, lambda c: tail_copy(pl.multiple_of(start + c * SEG, SEG)).start())
            _repeat(0, chunks, lambda c: tail_copy(0).wait())

        _repeat(0, N_EXPERTS, per_expert)
        tile_copy(n_tiles - 1).start()
        tile_copy(0).wait()
        _repeat(fill_sm[0], n_tiles - 1, lambda t: tile_copy(t).start())

    @pl.when(i >= 2)
    def _():
        _wait_chunks(i - 2, nch_sm, copy_from(buf))

    slot, h = slot_refs[0][...], h_refs[0][...]
    for q in range(1, n_paths):
        mine = i >= tile_starts[q]
        slot = jnp.where(mine, slot_refs[q][...], slot)
        h = jnp.where(mine, h_refs[q][...], h)
    used = nch_sm[TILE_COUNTS * i + 2]

    def build(rb):
        rows = lax.broadcasted_iota(jnp.int32, (SEL_BLOCK, TM_MOE), 0) + rb * SEL_BLOCK
        sel = jnp.zeros((SEL_BLOCK, TM_MOE), f32)
        for j in range(TOP_K):
            sel = jnp.where(rows == slot[j:j + 1, :], 1.0, sel)
        xc_ref[buf, rb * SEL_BLOCK:(rb + 1) * SEL_BLOCK, :] = _dot(sel.astype(bf16), h).astype(bf16)

    for rb in range(COMMON_BLOCKS):
        build(rb)
    for rb in range(COMMON_BLOCKS, SLOT_ROWS // SEL_BLOCK):
        pl.when(rb * SEL_BLOCK < used)(functools.partial(build, rb))

    _start_chunks(i, dst_sm, nch_sm, copy_from(buf))

    @pl.when(i == last)
    def _():
        @pl.when(i >= 1)
        def _():
            _wait_chunks(i - 1, nch_sm, copy_from(1 - buf))
        _wait_chunks(i, nch_sm, copy_from(buf))
        n_tiles = xs_hbm.shape[0] // FFN_ROWS
        whole_tile = pltpu.make_async_copy(zero_ref, xs_hbm.at[pl.ds(0, FFN_ROWS)], zsem)
        _repeat(fill_sm[0], n_tiles - 1, lambda t: whole_tile.wait())


def _dispatch(h2s, slots, dst, nch, fill, total_rows):
    tiles = [h.shape[0] // TM_MOE for h in h2s]
    starts = [sum(tiles[:q]) for q in range(len(tiles))]

    def in_path(q):
        return lambda i: jnp.clip(i - starts[q], 0, tiles[q] - 1)

    h_specs = [pl.BlockSpec((TM_MOE, D_MODEL), lambda i, d, c, f, q=q: (in_path(q)(i), 0)) for q in range(len(tiles))]
    s_specs = [pl.BlockSpec((8, TM_MOE), lambda i, d, c, f, q=q: (0, in_path(q)(i))) for q in range(len(tiles))]
    grid_spec = pltpu.PrefetchScalarGridSpec(
        num_scalar_prefetch=3, grid=(sum(tiles),),
        in_specs=h_specs + s_specs,
        out_specs=pl.BlockSpec(memory_space=pl.ANY),
        scratch_shapes=[pltpu.VMEM((2, SLOT_ROWS, D_MODEL), bf16), pltpu.VMEM((FFN_ROWS, D_MODEL), bf16),
                        pltpu.SemaphoreType.DMA((2,)), pltpu.SemaphoreType.DMA(())])
    return pl.pallas_call(
        functools.partial(_dispatch_kernel, tile_starts=tuple(starts)), grid_spec=grid_spec,
        out_shape=jax.ShapeDtypeStruct((total_rows, D_MODEL), bf16),
        compiler_params=_cparams("arbitrary"), name="moe_dispatch",
    )(dst, nch, fill, *h2s, *slots)


def _ffn_kernel(te_sm, nv_sm, xs_ref, wg_ref, wu_ref, wd_ref, ys_ref, wg_b, wu_b, wd_b):
    r = pl.program_id(0)
    valid = r < nv_sm[0]

    @pl.when(jnp.logical_and(valid, jnp.logical_or(r == 0, te_sm[r] != te_sm[jnp.maximum(r - 1, 0)])))
    def _():
        wg_b[...] = wg_ref[...].astype(bf16)
        wu_b[...] = wu_ref[...].astype(bf16)
        wd_b[...] = wd_ref[...].astype(bf16)

    @pl.when(valid)
    def _():
        x = xs_ref[...]
        gg = _dot(x, wg_b[...])
        uu = _dot(x, wu_b[...])
        hm = gg * _sigmoid(gg) * uu
        ys_ref[...] = _dot(hm.astype(bf16), wd_b[...]).astype(bf16)

    @pl.when(r >= nv_sm[0])
    def _():
        ys_ref[...] = jnp.zeros_like(ys_ref)


def _expert_ffn(xs, tile_expert, n_valid, wg, wu, wd):
    total_rows = xs.shape[0]
    last = lambda r, nv: jnp.minimum(r, nv[0] - 1)
    grid_spec = pltpu.PrefetchScalarGridSpec(
        num_scalar_prefetch=2, grid=(total_rows // FFN_ROWS,),
        in_specs=[pl.BlockSpec((FFN_ROWS, D_MODEL), lambda r, te, nv: (last(r, nv), 0)),
                  pl.BlockSpec((None, D_MODEL, EXPERT_FF), lambda r, te, nv: (te[last(r, nv)], 0, 0)),
                  pl.BlockSpec((None, D_MODEL, EXPERT_FF), lambda r, te, nv: (te[last(r, nv)], 0, 0)),
                  pl.BlockSpec((None, EXPERT_FF, D_MODEL), lambda r, te, nv: (te[last(r, nv)], 0, 0))],
        out_specs=pl.BlockSpec((FFN_ROWS, D_MODEL), lambda r, te, nv: (r, 0)),
        scratch_shapes=[pltpu.VMEM((D_MODEL, EXPERT_FF), bf16), pltpu.VMEM((D_MODEL, EXPERT_FF), bf16),
                        pltpu.VMEM((EXPERT_FF, D_MODEL), bf16)])
    return pl.pallas_call(
        _ffn_kernel, grid_spec=grid_spec,
        out_shape=jax.ShapeDtypeStruct((total_rows, D_MODEL), bf16),
        compiler_params=_cparams("arbitrary"), name="moe_ffn",
    )(tile_expert, n_valid, xs, wg, wu, wd)


def _combine_kernel(dst_sm, nch_sm, sw_ref, h_ref, x1_ref, mod_ref, gpost_ref, swg_ref, swu_ref, swd_ref,
                    ys_hbm, y_ref, yc_ref, acc_ref, sems, *, tile_offset):
    step = pl.program_id(0)
    i = step + tile_offset
    buf = lax.rem(step, 2)

    def copy_into(b):
        def make(local_row, global_row, rows):
            return pltpu.make_async_copy(ys_hbm.at[pl.ds(global_row, rows)],
                                         yc_ref.at[b, pl.ds(local_row, rows)], sems.at[b])
        return make

    @pl.when(step == 0)
    def _():
        yc_ref[...] = jnp.zeros_like(yc_ref)
        _start_chunks(i, dst_sm, nch_sm, copy_into(buf))

    @pl.when(step + 1 < pl.num_programs(0))
    def _():
        _start_chunks(i + 1, dst_sm, nch_sm, copy_into(1 - buf))

    h = h_ref[...]
    gg = _dot(h, swg_ref[...])
    uu = _dot(h, swu_ref[...])
    acc_ref[...] = _dot((gg * _sigmoid(gg) * uu).astype(bf16), swd_ref[...])
    _wait_chunks(i, nch_sm, copy_into(buf))
    sw = sw_ref[...]
    used = nch_sm[TILE_COUNTS * i + 2]
    def weighted(rb):
        cols = (lax.broadcasted_iota(jnp.int32, (TM_MOE, SEL_BLOCK), 1) + rb * SEL_BLOCK).astype(f32)
        wmat = jnp.zeros((TM_MOE, SEL_BLOCK), f32)
        for j in range(TOP_K):
            wmat = jnp.where(cols == sw[:, j:j + 1], sw[:, 8 + j:9 + j], wmat)
        rows = lax.broadcasted_iota(jnp.int32, (SEL_BLOCK, 1), 0) + rb * SEL_BLOCK
        yc = jnp.where(rows < used, yc_ref[buf, rb * SEL_BLOCK:(rb + 1) * SEL_BLOCK, :], jnp.zeros((), bf16))
        return _dot(wmat.astype(bf16), yc)

    moe = acc_ref[...]
    for rb in range(COMMON_BLOCKS):
        moe = moe + weighted(rb)
    acc_ref[...] = moe
    for rb in range(COMMON_BLOCKS, SLOT_ROWS // SEL_BLOCK):
        @pl.when(rb * SEL_BLOCK < used)
        def _():
            acc_ref[...] += weighted(rb)
    gt_f = mod_ref[...][:, 5 * D_MODEL:6 * D_MODEL]
    y_ref[...] = x1_ref[...] + gt_f * _rms(acc_ref[...], gpost_ref[...])


def _combine(ys, dst, nch, sw, h2, x1, mod3, mod_row, g_post, swg, swu, swd, tile_offset):
    n = h2.shape[0]
    row = lambda wd: pl.BlockSpec((TM_MOE, wd), lambda i, d, c: (i, 0))
    const = lambda a: pl.BlockSpec(a.shape, lambda i, d, c: (0,) * a.ndim)
    grid_spec = pltpu.PrefetchScalarGridSpec(
        num_scalar_prefetch=2, grid=(n // TM_MOE,),
        in_specs=[row(LANE), row(D_MODEL), row(D_MODEL),
                  pl.BlockSpec((None, 1, 6 * D_MODEL), lambda i, d, c: (mod_row(i, TM_MOE), 0, 0)),
                  const(g_post), const(swg), const(swu), const(swd),
                  pl.BlockSpec(memory_space=pl.ANY)],
        out_specs=row(D_MODEL),
        scratch_shapes=[pltpu.VMEM((2, SLOT_ROWS, D_MODEL), bf16), pltpu.VMEM((TM_MOE, D_MODEL), f32),
                        pltpu.SemaphoreType.DMA((2,))])
    return pl.pallas_call(
        functools.partial(_combine_kernel, tile_offset=tile_offset), grid_spec=grid_spec,
        out_shape=jax.ShapeDtypeStruct((n, D_MODEL), f32),
        compiler_params=_cparams("arbitrary"), name="moe_combine",
    )(dst, nch, sw, h2, x1, mod3, g_post, swg, swu, swd, ys)


def _sparse_moe(paths, mod3, p):
    cnt = jnp.concatenate([q['cnt'][:, :, 0] for q in paths], axis=0).astype(jnp.int32)
    n_tiles = cnt.shape[0]
    before = jnp.cumsum(cnt, axis=0) - cnt
    rows_e = jnp.sum(cnt, axis=0)
    region = (rows_e + FFN_ROWS - 1) // FFN_ROWS * FFN_ROWS
    region_end = jnp.cumsum(region)
    region_start = region_end - region
    worst = n_tiles * (TOP_K * TM_MOE + N_EXPERTS * (SEG - 1)) + N_EXPERTS * (FFN_ROWS - SEG)
    total_rows = -(-worst // FFN_ROWS) * FFN_ROWS + FFN_ROWS
    spare_row = total_rows - FFN_ROWS
    first_row = jnp.arange(total_rows // FFN_ROWS, dtype=jnp.int32) * FFN_ROWS
    ended = jnp.sum((region_end[None, :] <= first_row[:, None]).astype(jnp.int32), axis=1)
    tile_expert = jnp.minimum(ended, N_EXPERTS - 1).astype(jnp.int32)
    n_valid = (region_end[-1:] // FFN_ROWS).astype(jnp.int32)

    seg_end = jnp.cumsum(cnt, axis=1)
    seg_start = seg_end - cnt
    seg_global = region_start[None, :] + before
    chunks = cnt // SEG
    pairs = chunks // 2
    tile_id = jnp.arange(n_tiles, dtype=jnp.int32)[:, None]
    experts = jnp.arange(N_EXPERTS, dtype=jnp.int32)[None, None, :]
    tables, counts = [], []
    spare_base = spare_row
    for kind, (rows, per_seg, first) in enumerate(((2 * SEG, pairs, jnp.zeros_like(pairs)),
                                                    (SEG, chunks - 2 * pairs, 2 * SEG * pairs))):
        ends = jnp.cumsum(per_seg, axis=1)
        idx = jnp.arange(MAX_COPIES[kind], dtype=jnp.int32)[None, :]
        seg_of = jnp.sum((ends[:, None, :] <= idx[:, :, None]).astype(jnp.int32), axis=2)
        own = seg_of[:, :, None] == experts
        pick = lambda a: jnp.sum(jnp.where(own, a[:, None, :], 0), axis=2)
        within = (idx - pick(ends - per_seg)) * rows
        local = pick(seg_start + first) + within
        dst = pick(seg_global + first) + within
        slot = (tile_id % 2) * CHUNK_GROUP + idx % CHUNK_GROUP
        is_copy = idx < ends[:, -1:]
        local = jnp.where(is_copy, local, PAD_LOCAL[kind] + (idx % CHUNK_GROUP) * rows)
        dst = jnp.where(is_copy, dst, spare_base + slot * rows)
        tables.append((dst // SEG) * (1 << LOC_BITS) + local // SEG)
        counts.append(ends[:, -1])
        spare_base = spare_base + 2 * CHUNK_GROUP * rows
    table = jnp.concatenate(tables, axis=1).astype(jnp.int32).reshape(-1)
    nch = jnp.stack(counts + [seg_end[:, -1]], axis=1).astype(jnp.int32).reshape(-1)
    dst = table

    fill = jnp.concatenate([n_valid, region_start + rows_e, region - rows_e]).astype(jnp.int32)
    xs = _dispatch([q['h2'] for q in paths], [q['slots'] for q in paths], dst, nch, fill, total_rows)
    ys = _expert_ffn(xs, tile_expert, n_valid, p['expert_wg'], p['expert_wu'], p['expert_wd'])
    outs, tile_offset = [], 0
    for q in paths:
        outs.append(_combine(ys, dst, nch, q['sw'], q['h2'], q['x1'], mod3, q['mod_row'],
                             p['g_post_ffn'], p['shared_wg'], p['shared_wu'], p['shared_wd'], tile_offset))
        tile_offset += q['h2'].shape[0] // TM_MOE
    return outs


def _pad_heads(w, heads):
    rows = w.shape[0]
    w = w.reshape(rows, heads, HEAD_DIM)
    return jnp.pad(w, ((0, 0), (0, 0), (0, HEAD_PAD - HEAD_DIM))).reshape(rows, heads * HEAD_PAD)


def _block_diag2(m):
    z = jnp.zeros_like(m[0])
    return jnp.concatenate([jnp.concatenate([m[0], z], axis=1), jnp.concatenate([z, m[1]], axis=1)], axis=0)


def _prepare_params(w):
    aw, kw = ATTN_HEADS * HEAD_DIM, KV_HEADS * HEAD_DIM
    w_in = w['w_in']
    lane_row = lambda v: v.reshape(1, -1).astype(f32)
    head_id = np.arange(RWKV_WIDTH) // RWKV_HEAD
    seg = jnp.asarray(head_id[:, None] == head_id[None, :], dtype=bf16)
    gain_pad = lambda gvec: jnp.pad(gvec, (0, HEAD_PAD - HEAD_DIM)).reshape(1, HEAD_PAD)
    return dict(
        g_pre_mix=lane_row(w['g_pre_mix']), g_post_mix=lane_row(w['g_post_mix']),
        g_pre_ffn=lane_row(w['g_pre_ffn']), g_post_ffn=lane_row(w['g_post_ffn']),
        wq=_pad_heads(w_in[:, :aw], ATTN_HEADS).astype(bf16),
        wk=_pad_heads(w_in[:, aw:aw + kw], KV_HEADS).astype(bf16),
        wv=w_in[:, aw + kw:aw + 2 * kw].astype(bf16),
        wr=w_in[:, aw + 2 * kw:].astype(bf16),
        qg=gain_pad(w['q_gain'] * ATTN_SCALE), kg=gain_pad(w['k_gain']),
        mu_prev=lane_row(w['mu_prev']), mu_next=lane_row(w['mu_next']),
        w0=lane_row(w['decay_w0']), w2bd=_block_diag2(w['decay_w2']),
        a0=lane_row(w['iclr_a0']), a2bd=_block_diag2(w['iclr_a2']),
        g2=w['gate_g2'], k_k=lane_row(w['k_k']), k_a=lane_row(w['k_a']), r_k=lane_row(w['r_k']),
        lnx_g=lane_row(w['lnx_g']), lnx_b=lane_row(w['lnx_b']), seg=seg,
        w_out=w['w_out'].astype(bf16),
        router_wt=w['router_w'].T, router_b=w['router_b'].reshape(N_EXPERTS, 1),
        expert_wg=w['expert_wg'], expert_wu=w['expert_wu'], expert_wd=w['expert_wd'],
        shared_wg=w['shared_wg'].astype(bf16), shared_wu=w['shared_wu'].astype(bf16),
        shared_wd=w['shared_wd'].astype(bf16),
    )


def _rope_tables(seq_len):
    rows = seq_len // GRID_W
    row_idx = jnp.repeat(jnp.arange(rows, dtype=jnp.int32), GRID_W)
    col_idx = jnp.tile(jnp.arange(GRID_W, dtype=jnp.int32), rows)
    inv = ROPE_THETA ** (-jnp.arange(0, ROPE_HALF, 2, dtype=f32) / ROPE_HALF)
    ang = jnp.stack([row_idx.astype(f32)[:, None] * inv, col_idx.astype(f32)[:, None] * inv], axis=1)
    cos, sin = jnp.cos(ang), jnp.sin(ang)
    cos_l = jnp.stack([cos, cos], axis=2).reshape(seq_len, HEAD_DIM)
    sin_l = jnp.stack([-sin, sin], axis=2).reshape(seq_len, HEAD_DIM)
    pad = ((0, 0), (0, HEAD_PAD - HEAD_DIM))
    return jnp.pad(cos_l, pad), jnp.pad(sin_l, pad)


def _states_to_pairs(s):
    b = s.shape[0]
    s = s.reshape(b, 2, N_QUADS, QUAD_HEADS, RWKV_HEAD, RWKV_HEAD)
    eye = jnp.eye(QUAD_HEADS, dtype=s.dtype)
    return jnp.einsum('bdphvk,hg->bdphkgv', s, eye).reshape(b, 2, N_QUADS, QUAD, QUAD)


def _trunk(x, mod3, mod_row, p, rope_tabs, past_k, past_v, s0_pairs, tm_in, tq, tm_prep):
    b, t, _ = x.shape
    n = b * t
    x2 = x.reshape(n, D_MODEL)
    q_pad, k_own, v_own, k_pad, zr = _in_proj(
        x2, mod3, mod_row, p['g_pre_mix'], p['wq'], p['wk'], p['wv'], p['wr'], p['qg'], p['kg'],
        rope_tabs, t, tm_in)
    k_full = k_pad.reshape(b, t, KV_HEADS * HEAD_PAD)
    v_full = v_own.reshape(b, t, KV_HEADS * HEAD_DIM)
    if past_k is not None:
        k_past = _pad_heads(past_k.reshape(-1, KV_HEADS * HEAD_DIM), KV_HEADS).astype(bf16)
        k_full = jnp.concatenate([k_past.reshape(b, -1, KV_HEADS * HEAD_PAD), k_full], axis=1)
        v_full = jnp.concatenate([past_v.reshape(b, -1, KV_HEADS * HEAD_DIM), v_full], axis=1)
    vt_full = jnp.swapaxes(v_full, 1, 2).astype(bf16)
    attn = _attention(q_pad, k_full, vt_full, b, t, tq)
    r, k, v, kk, lw, a, g, bonus = _rwkv_prep(zr, p, t, tm_prep)
    o_fw, o_bw, s_fin = _rwkv_scan(r, k, v, kk, lw, a, p['k_a'], s0_pairs, b, t)
    x1, h2, slots, sw, cnt = _mix_out(attn, o_fw, o_bw, bonus, g, x2, mod3, mod_row, p, MIX_TILES * TM_MOE)
    return dict(h2=h2, slots=slots, sw=sw, cnt=cnt, x1=x1, mod_row=mod_row), k_own, v_own, s_fin


def kernel(x_prompt, x_sample, c, cache_k, cache_v, state_rwkv, c_ctx, w_ada, b_ada, g_pre_mix, g_post_mix,
           g_pre_ffn, g_post_ffn, w_in, w_out, q_gain, k_gain, mu_prev, mu_next, decay_w0, decay_w2, iclr_a0,
           iclr_a2, gate_g2, k_k, k_a, r_k, lnx_g, lnx_b, router_w, router_b, expert_wg, expert_wu, expert_wd,
           shared_wg, shared_wu, shared_wd):
    layer = 0
    names = ('g_pre_mix g_post_mix g_pre_ffn g_post_ffn w_in w_out q_gain k_gain mu_prev mu_next decay_w0 '
             'decay_w2 iclr_a0 iclr_a2 gate_g2 k_k k_a r_k lnx_g lnx_b router_w router_b expert_wg expert_wu '
             'expert_wd shared_wg shared_wu shared_wd').split()
    vals = (g_pre_mix, g_post_mix, g_pre_ffn, g_post_ffn, w_in, w_out, q_gain, k_gain, mu_prev, mu_next,
            decay_w0, decay_w2, iclr_a0, iclr_a2, gate_g2, k_k, k_a, r_k, lnx_g, lnx_b, router_w, router_b,
            expert_wg, expert_wu, expert_wd, shared_wg, shared_wu, shared_wd)
    p = _prepare_params({nm: v[layer] for nm, v in zip(names, vals)})

    nb, ts, _ = x_sample.shape
    npb, tp, _ = x_prompt.shape
    assert ts % (MIX_TILES * TM_MOE) == 0 and ts % LATENT_TILES.in_rows == 0 and ts % GRID_W == 0
    assert ts % LATENT_TILES.attn_queries == 0 and ts % LATENT_TILES.prep_rows == 0
    assert (npb * tp) % CONTEXT_TILES.in_rows == 0 and tp % CONTEXT_TILES.prep_rows == 0
    assert nb % SCAN_SEQS == 0 and npb % SCAN_SEQS == 0 and (npb * tp) % (MIX_TILES * TM_MOE) == 0
    mod_rows = 16
    assert nb < mod_rows
    cc = jnp.zeros((mod_rows, D_MODEL), f32).at[0].set(c_ctx).at[1:1 + nb].set(c)
    mod3 = _modulation(cc, w_ada[layer], b_ada[layer]).reshape(mod_rows, 1, 6 * D_MODEL)

    moe_p, kc, vc, st = _trunk(x_prompt, mod3, lambda i, tm: 0, p, None, None, None, None,
                               tm_in=CONTEXT_TILES.in_rows, tq=tp, tm_prep=CONTEXT_TILES.prep_rows)
    s0_lat = _states_to_pairs(state_rwkv[:, layer])
    moe_s, _, _, _ = _trunk(x_sample, mod3, lambda i, tm: 1 + (i * tm) // ts, p, _rope_tables(ts),
                            cache_k[:, layer], cache_v[:, layer], s0_lat,
                            tm_in=LATENT_TILES.in_rows, tq=LATENT_TILES.attn_queries,
                            tm_prep=LATENT_TILES.prep_rows)
    y_p, y_s = _sparse_moe([moe_p, moe_s], mod3, p)
    y_p = y_p.reshape(x_prompt.shape)
    y_s = y_s.reshape(x_sample.shape)

    new_cache_k = kc.reshape(npb, 1, tp, KV_HEADS, HEAD_DIM)
    new_cache_v = vc.reshape(npb, 1, tp, KV_HEADS, HEAD_DIM)
    new_state = st[:, None]
    return (y_p, y_s, new_cache_k, new_cache_v, new_state)
```
